```python
import math
import jax, jax.numpy as jnp
from jax import lax
import numpy as np

D_MODEL = 1024
BATCH = 2
SEQ = 8192
DEPTH = 2

N_EVEN = (DEPTH + 1) // 2
N_ODD = DEPTH // 2
ALPHA = (2.0 * DEPTH) ** 0.25
BETA = (8.0 * DEPTH) ** -0.25
LN_EPS = 1e-5

POOL_WINDOWS = (2, 4, 8, 16)
POOL_GROUPS = len(POOL_WINDOWS)
POOL_CH = D_MODEL // 8
POOL_WIDTH = POOL_GROUPS * POOL_CH
HEAD_DIM = 64
N_Q_HEADS = D_MODEL // 128
N_KV_HEADS = 2
Q_GROUP = N_Q_HEADS // N_KV_HEADS
ATTN_WIDTH = N_Q_HEADS * HEAD_DIM
KV_WIDTH = N_KV_HEADS * HEAD_DIM
WINDOW = 128
BLOCK = 128
EVEN_IN = POOL_WIDTH + ATTN_WIDTH + 2 * KV_WIDTH
EVEN_MIX = POOL_WIDTH + ATTN_WIDTH

CHUNK = 128
SG_WIDTH = D_MODEL
SG_GROUPS = 8
SG_CH = SG_WIDTH // SG_GROUPS

D_FF = ((8 * D_MODEL // 3 + 127) // 128) * 128
N_EXPERTS = 8
TOP_K = 2
D_FF_EXPERT = 7 * D_MODEL // 2

kernel_name = "hybrid_pool_swa_gmlp_moe_deepnorm_adaln"


def layer_norm(x, g, b):
    xf = x.astype(jnp.float32)
    mu = jnp.mean(xf, axis=-1, keepdims=True)
    var = jnp.mean(jnp.square(xf - mu), axis=-1, keepdims=True)
    return ((xf - mu) * lax.rsqrt(var + LN_EPS) * g + b).astype(x.dtype)


def pool_mixer(p, w_pool, pool_scale):
    B_, S, _ = p.shape
    pf = p.astype(jnp.float32)
    cs = jnp.concatenate([jnp.zeros((B_, 1, POOL_WIDTH), jnp.float32), jnp.cumsum(pf, axis=1)], axis=1)
    t = jnp.arange(S)
    outs = []
    for g, w in enumerate(POOL_WINDOWS):
        r = w // 2
        lo = jnp.maximum(t - r, 0)
        hi = jnp.minimum(t + r + 1, S)
        sl = slice(g * POOL_CH, (g + 1) * POOL_CH)
        csg = cs[..., sl]
        cnt = (hi - lo).astype(jnp.float32)[None, :, None]
        outs.append((csg[:, hi] - csg[:, lo]) / cnt - pf[..., sl])
    pooled = jnp.stack(outs, axis=2).astype(p.dtype)
    y = jnp.einsum('bsgc,gcd->bsgd', pooled, w_pool)
    return y.reshape(B_, S, POOL_WIDTH) * pool_scale


def windowed_gqa(q, k, v, sink):
    B_, S, _, _ = q.shape
    nb = S // BLOCK
    qb = q.reshape(B_, nb, BLOCK, N_KV_HEADS, Q_GROUP, HEAD_DIM)
    pad = ((0, 0), (BLOCK, BLOCK), (0, 0), (0, 0))
    kp = jnp.pad(k, pad).reshape(B_, nb + 2, BLOCK, N_KV_HEADS, HEAD_DIM)
    vp = jnp.pad(v, pad).reshape(B_, nb + 2, BLOCK, N_KV_HEADS, HEAD_DIM)
    kw = jnp.concatenate([kp[:, :-2], kp[:, 1:-1], kp[:, 2:]], axis=2)
    vw = jnp.concatenate([vp[:, :-2], vp[:, 1:-1], vp[:, 2:]], axis=2)
    scores = jnp.einsum('bnqhgd,bnshd->bnhgqs', qb, kw).astype(jnp.float32) * (HEAD_DIM ** -0.5)
    i = jnp.arange(BLOCK)[:, None]
    j = jnp.arange(3 * BLOCK)[None, :]
    dist = jnp.abs(j - BLOCK - i)
    key_pos = jnp.arange(nb)[:, None] * BLOCK - BLOCK + jnp.arange(3 * BLOCK)[None, :]
    valid = (key_pos >= 0) & (key_pos < S)
    mask = (dist <= WINDOW)[None] & valid[:, None, :]
    slopes = 2.0 ** (-8.0 * jnp.arange(1, N_Q_HEADS + 1, dtype=jnp.float32) / N_Q_HEADS)
    alibi = -slopes.reshape(N_KV_HEADS, Q_GROUP)[:, :, None, None] * dist.astype(jnp.float32)
    scores = jnp.where(mask[None, :, None, None], scores + alibi, -1e30)
    sink_b = sink.astype(jnp.float32).reshape(N_KV_HEADS, Q_GROUP)[None, None, :, :, None, None]
    m = jnp.maximum(jnp.max(scores, axis=-1, keepdims=True), sink_b)
    pexp = jnp.exp(scores - m)
    probs = pexp / (jnp.sum(pexp, axis=-1, keepdims=True) + jnp.exp(sink_b - m))
    out = jnp.einsum('bnhgqs,bnshd->bnqhgd', probs.astype(v.dtype), vw)
    return out.reshape(B_, S, ATTN_WIDTH)


def even_mixer(h, w_in, w_pool, pool_scale, sink, w_out):
    B_, S, _ = h.shape
    z = h @ w_in
    p, q, k, v = jnp.split(z, [POOL_WIDTH, POOL_WIDTH + ATTN_WIDTH, POOL_WIDTH + ATTN_WIDTH + KV_WIDTH], axis=-1)
    ya = pool_mixer(p, w_pool, pool_scale)
    yb = windowed_gqa(q.reshape(B_, S, N_Q_HEADS, HEAD_DIM),
                      k.reshape(B_, S, N_KV_HEADS, HEAD_DIM),
                      v.reshape(B_, S, N_KV_HEADS, HEAD_DIM), sink)
    return jnp.concatenate([ya, yb], axis=-1) @ w_out


def spatial_gating_mixer(h, w_in, sg_ln_g, sg_ln_b, w_s, b_s, w_out):
    B_, S, _ = h.shape
    z = jax.nn.gelu(h @ w_in)
    u, v = jnp.split(z, 2, axis=-1)
    v = layer_norm(v, sg_ln_g, sg_ln_b)
    vc = v.reshape(B_, S // CHUNK, CHUNK, SG_GROUPS, SG_CH)
    sv = jnp.einsum('gts,bnsgc->bntgc', w_s, vc) + b_s.T[None, None, :, :, None]
    return (u * sv.reshape(B_, S, SG_WIDTH)) @ w_out


def swiglu(h, w_gate, w_up, w_down):
    return (jax.nn.silu(h @ w_gate) * (h @ w_up)) @ w_down


def moe_swiglu(h, w_router, w_gate, w_up, w_down):
    logits = (h @ w_router).astype(jnp.float32)
    top_v, top_i = lax.top_k(logits, TOP_K)
    gates = jax.nn.softmax(top_v, axis=-1)
    dense_gate = jnp.sum(jax.nn.one_hot(top_i, N_EXPERTS, dtype=jnp.float32) * gates[..., None], axis=-2)
    y = jnp.zeros_like(h)
    for e in range(N_EXPERTS):
        y = y + dense_gate[..., e:e + 1].astype(h.dtype) * swiglu(h, w_gate[e], w_up[e], w_down[e])
    return y


def setup_inputs(seed: int = 0) -> dict:
    key = jax.random.key(seed)
    ks = jax.random.split(key, 32)
    f32 = jnp.float32
    D = D_MODEL

    def nrm(k, shape, scale):
        return jax.random.normal(k, shape, f32) * scale

    x = nrm(ks[0], (BATCH, SEQ, D), 1.0)
    c = nrm(ks[1], (BATCH, D), 1.0)
    ada_w = nrm(ks[2], (DEPTH, D, 6 * D), 0.1 * D ** -0.5)
    ada_b = nrm(ks[3], (DEPTH, 6 * D), 0.01)
    ln_g = 1.0 + nrm(ks[4], (DEPTH, 2, D), 0.02)
    ln_b = nrm(ks[5], (DEPTH, 2, D), 0.01)

    ev_w_in = nrm(ks[6], (N_EVEN, D, EVEN_IN), D ** -0.5)
    col_scale = jnp.concatenate([jnp.full((EVEN_IN - KV_WIDTH,), 1.0, f32), jnp.full((KV_WIDTH,), BETA, f32)])
    ev_w_in = ev_w_in * col_scale
    ev_pool_w = nrm(ks[7], (N_EVEN, POOL_GROUPS, POOL_CH, POOL_CH), POOL_CH ** -0.5)
    ev_pool_scale = 1.0 + nrm(ks[8], (N_EVEN, POOL_WIDTH), 0.02)
    ev_sink = nrm(ks[9], (N_EVEN, N_Q_HEADS), 0.5)
    ev_w_out = nrm(ks[10], (N_EVEN, EVEN_MIX, D), BETA * EVEN_MIX ** -0.5)

    od_w_in = nrm(ks[11], (N_ODD, D, 2 * SG_WIDTH), D ** -0.5)
    od_sg_ln_g = 1.0 + nrm(ks[12], (N_ODD, SG_WIDTH), 0.02)
    od_sg_ln_b = nrm(ks[13], (N_ODD, SG_WIDTH), 0.01)
    od_w_s = nrm(ks[14], (N_ODD, SG_GROUPS, CHUNK, CHUNK), CHUNK ** -0.5)
    od_b_s = 1.0 + nrm(ks[15], (N_ODD, SG_GROUPS, CHUNK), 0.02)
    od_w_out = nrm(ks[16], (N_ODD, SG_WIDTH, D), BETA * SG_WIDTH ** -0.5)

    ffn_w_gate = nrm(ks[17], (N_EVEN, D, D_FF), BETA * D ** -0.5)
    ffn_w_up = nrm(ks[18], (N_EVEN, D, D_FF), BETA * D ** -0.5)
    ffn_w_down = nrm(ks[19], (N_EVEN, D_FF, D), BETA * D_FF ** -0.5)

    moe_w_router = nrm(ks[20], (N_ODD, D, N_EXPERTS), D ** -0.5)
    moe_w_gate = nrm(ks[21], (N_ODD, N_EXPERTS, D, D_FF_EXPERT), BETA * D ** -0.5)
    moe_w_up = nrm(ks[22], (N_ODD, N_EXPERTS, D, D_FF_EXPERT), BETA * D ** -0.5)
    moe_w_down = nrm(ks[23], (N_ODD, N_EXPERTS, D_FF_EXPERT, D), BETA * D_FF_EXPERT ** -0.5)

    return {"x": x, "c": c, "ada_w": ada_w, "ada_b": ada_b, "ln_g": ln_g, "ln_b": ln_b,
            "ev_w_in": ev_w_in, "ev_pool_w": ev_pool_w, "ev_pool_scale": ev_pool_scale,
            "ev_sink": ev_sink, "ev_w_out": ev_w_out,
            "od_w_in": od_w_in, "od_sg_ln_g": od_sg_ln_g, "od_sg_ln_b": od_sg_ln_b,
            "od_w_s": od_w_s, "od_b_s": od_b_s, "od_w_out": od_w_out,
            "ffn_w_gate": ffn_w_gate, "ffn_w_up": ffn_w_up, "ffn_w_down": ffn_w_down,
            "moe_w_router": moe_w_router, "moe_w_gate": moe_w_gate, "moe_w_up": moe_w_up,
            "moe_w_down": moe_w_down}


def reference(x, c, ada_w, ada_b, ln_g, ln_b,
              ev_w_in, ev_pool_w, ev_pool_scale, ev_sink, ev_w_out,
              od_w_in, od_sg_ln_g, od_sg_ln_b, od_w_s, od_b_s, od_w_out,
              ffn_w_gate, ffn_w_up, ffn_w_down,
              moe_w_router, moe_w_gate, moe_w_up, moe_w_down):
    cond = jax.nn.silu(c)
    for l in range(DEPTH):
        mod = (cond @ ada_w[l] + ada_b[l])[:, None, :]
        sh_m, sc_m, g_m, sh_f, sc_f, g_f = jnp.split(mod, 6, axis=-1)
        e = l // 2
        h = x * (1.0 + sc_m) + sh_m
        if l % 2 == 0:
            y = even_mixer(h, ev_w_in[e], ev_pool_w[e], ev_pool_scale[e], ev_sink[e], ev_w_out[e])
        else:
            y = spatial_gating_mixer(h, od_w_in[e], od_sg_ln_g[e], od_sg_ln_b[e], od_w_s[e], od_b_s[e], od_w_out[e])
        x = layer_norm(ALPHA * x + (1.0 + g_m) * y, ln_g[l, 0], ln_b[l, 0])
        h = x * (1.0 + sc_f) + sh_f
        if l % 2 == 0:
            y = swiglu(h, ffn_w_gate[e], ffn_w_up[e], ffn_w_down[e])
        else:
            y = moe_swiglu(h, moe_w_router[e], moe_w_gate[e], moe_w_up[e], moe_w_down[e])
        x = layer_norm(ALPHA * x + (1.0 + g_f) * y, ln_g[l, 1], ln_b[l, 1])
    return x
```

```python
import functools
import math

import jax
import jax.numpy as jnp
from jax import lax
from jax.experimental import pallas as pl
from jax.experimental.pallas import tpu as pltpu

D = 1024
DEPTH = 2
ALPHA = (2.0 * DEPTH) ** 0.25
LN_EPS = 1e-5

POOL_WINDOWS = (2, 4, 8, 16)
POOL_CH = 128
POOL_WIDTH = 512
HEAD_DIM = 64
N_Q_HEADS = 8
N_KV_HEADS = 2
Q_GROUP = 4
ATTN_WIDTH = 512
KV_WIDTH = 128
BLOCK = 128
EVEN_IN = 1280
POOL_HALO = 8

CHUNK = 128
SG_GROUPS = 8
SG_CH = 128

N_EXPERTS = 8

ROUTE_W = 512
GRAN = 16
CHUNK_SLOTS = 2 * ROUTE_W + N_EXPERTS * GRAN
CHUNK_GRANS = CHUNK_SLOTS // GRAN
MOE_TM = 512
TILE_GRANS = MOE_TM // GRAN
MOE_TF = 512

VMEM_LIMIT = 48 * 1024 * 1024

bf16 = jnp.bfloat16
f32 = jnp.float32


def _dot(a, b):
    return jnp.dot(a, b, preferred_element_type=f32)


def _split_bf16(a):
    hi = a.astype(bf16)
    lo = (a - hi.astype(f32)).astype(bf16)
    return hi, lo


def _layer_norm(x, g, b):
    mu = jnp.mean(x, axis=-1, keepdims=True)
    xc = x - mu
    var = jnp.mean(xc * xc, axis=-1, keepdims=True)
    return xc * lax.rsqrt(var + LN_EPS) * g + b


def _silu(x):
    return x * jax.nn.sigmoid(x)


def _gelu_tanh(x):
    c = math.sqrt(2.0 / math.pi)
    return x * (0.5 * (1.0 + jnp.tanh(c * (x + 0.044715 * (x * x * x)))))


def _adaln_kernel(c_ref, w_ref, b_ref, o_ref):
    cond = _silu(c_ref[...])
    c_hi, c_lo = _split_bf16(cond)
    w_hi, w_lo = _split_bf16(w_ref[...])
    acc = _dot(c_hi, w_hi) + (_dot(c_lo, w_hi) + _dot(c_hi, w_lo))
    o_ref[...] = acc + b_ref[...]


def _adaln(c, ada_w, ada_b):
    B = c.shape[0]
    tn = 1024
    c_pad = jnp.zeros((8, D), f32).at[:B].set(c)
    out = pl.pallas_call(
        _adaln_kernel,
        grid=(DEPTH, 6 * D // tn),
        in_specs=[
            pl.BlockSpec((8, D), lambda l, j: (0, 0)),
            pl.BlockSpec((None, D, tn), lambda l, j: (l, 0, j)),
            pl.BlockSpec((None, 1, tn), lambda l, j: (l, 0, j)),
        ],
        out_specs=pl.BlockSpec((None, 8, tn), lambda l, j: (l, 0, j)),
        out_shape=jax.ShapeDtypeStruct((DEPTH, 8, 6 * D), f32),
        compiler_params=pltpu.CompilerParams(vmem_limit_bytes=VMEM_LIMIT),
        name="adaln",
    )(c_pad, ada_w, ada_b.reshape(DEPTH, 1, 6 * D))
    return out[:, :B].reshape(DEPTH, B, 6, D)


def _ev_in_kernel(x_ref, mod_ref, w_ref, p_ref, q_ref, k_ref, v_ref):
    h = x_ref[...] * (1.0 + mod_ref[1:2, :]) + mod_ref[0:1, :]
    z = _dot(h.astype(bf16), w_ref[...])
    p_ref[...] = z[:, :POOL_WIDTH]
    q_ref[...] = z[:, POOL_WIDTH:POOL_WIDTH + ATTN_WIDTH].astype(bf16)
    k_ref[...] = z[:, POOL_WIDTH + ATTN_WIDTH:POOL_WIDTH + ATTN_WIDTH + KV_WIDTH].astype(bf16)
    v_ref[...] = z[:, POOL_WIDTH + ATTN_WIDTH + KV_WIDTH:].astype(bf16)


def _ev_in(x2d, mod_l, w_in, S):
    T = x2d.shape[0]
    tm = 512
    tpb = S // tm
    return pl.pallas_call(
        _ev_in_kernel,
        grid=(T // tm,),
        in_specs=[
            pl.BlockSpec((tm, D), lambda i: (i, 0)),
            pl.BlockSpec((None, 6, D), lambda i: (i // tpb, 0, 0)),
            pl.BlockSpec((D, EVEN_IN), lambda i: (0, 0)),
        ],
        out_specs=[
            pl.BlockSpec((tm, POOL_WIDTH), lambda i: (i, 0)),
            pl.BlockSpec((tm, ATTN_WIDTH), lambda i: (i, 0)),
            pl.BlockSpec((tm, KV_WIDTH), lambda i: (i, 0)),
            pl.BlockSpec((tm, KV_WIDTH), lambda i: (i, 0)),
        ],
        out_shape=[
            jax.ShapeDtypeStruct((T, POOL_WIDTH), f32),
            jax.ShapeDtypeStruct((T, ATTN_WIDTH), bf16),
            jax.ShapeDtypeStruct((T, KV_WIDTH), bf16),
            jax.ShapeDtypeStruct((T, KV_WIDTH), bf16),
        ],
        compiler_params=pltpu.CompilerParams(vmem_limit_bytes=VMEM_LIMIT),
        name="ev_in",
    )(x2d, mod_l, w_in)


def _ev_mix_kernel(sink_ref, x_ref, mod_ref, p_ref, pp_ref, pn_ref, q_ref,
                   k_ref, kp_ref, kn_ref, v_ref, vp_ref, vn_ref,
                   wpool_ref, pscale_ref, wout_ref, lng_ref, lnb_ref,
                   o_ref, pext_ref, kext_ref, vext_ref, mix_ref, *, S, tq):
    i = pl.program_id(1)
    n_tiles = S // tq
    is_first = i == 0
    is_last = i == n_tiles - 1

    p = p_ref[...]
    pext_ref[0:POOL_HALO, :] = jnp.where(is_first, 0.0, pp_ref[...])
    pext_ref[POOL_HALO:POOL_HALO + tq, :] = p
    pext_ref[POOL_HALO + tq:, :] = jnp.where(is_last, 0.0, pn_ref[...])
    pos = i * tq + lax.broadcasted_iota(jnp.int32, (tq, 1), 0)
    for g, w in enumerate(POOL_WINDOWS):
        r = w // 2
        cs = slice(g * POOL_CH, (g + 1) * POOL_CH)
        acc = pext_ref[POOL_HALO - r:POOL_HALO - r + tq, cs]
        for d in range(-r + 1, r + 1):
            acc = acc + pext_ref[POOL_HALO + d:POOL_HALO + d + tq, cs]
        cnt = (jnp.minimum(pos + r + 1, S) - jnp.maximum(pos - r, 0)).astype(f32)
        pooled = acc / cnt - p[:, cs]
        ya = _dot(pooled.astype(bf16), wpool_ref[g])
        mix_ref[:, cs] = (ya * pscale_ref[:, cs]).astype(bf16)

    kext_ref[0:BLOCK, :] = kp_ref[...]
    kext_ref[BLOCK:BLOCK + tq, :] = k_ref[...]
    kext_ref[BLOCK + tq:, :] = kn_ref[...]
    vext_ref[0:BLOCK, :] = vp_ref[...]
    vext_ref[BLOCK:BLOCK + tq, :] = v_ref[...]
    vext_ref[BLOCK + tq:, :] = vn_ref[...]

    qi = lax.broadcasted_iota(jnp.int32, (BLOCK, 3 * BLOCK), 0)
    kj = lax.broadcasted_iota(jnp.int32, (BLOCK, 3 * BLOCK), 1)
    dist_i = jnp.abs(kj - BLOCK - qi)
    dist = dist_i.astype(f32)
    n_blocks = S // BLOCK
    scale = HEAD_DIM ** -0.5
    for n in range(tq // BLOCK):
        gb = i * (tq // BLOCK) + n
        lo = jnp.where(gb == 0, BLOCK, 0)
        hi = jnp.where(gb == n_blocks - 1, 2 * BLOCK, 3 * BLOCK)
        mask = (dist_i <= BLOCK) & (kj >= lo) & (kj < hi)
        rows = slice(n * BLOCK, (n + 1) * BLOCK)
        for kvh in range(N_KV_HEADS):
            kw = kext_ref[n * BLOCK:n * BLOCK + 3 * BLOCK, kvh * HEAD_DIM:(kvh + 1) * HEAD_DIM]
            vw = vext_ref[n * BLOCK:n * BLOCK + 3 * BLOCK, kvh * HEAD_DIM:(kvh + 1) * HEAD_DIM]
            for gq in range(Q_GROUP):
                hq = kvh * Q_GROUP + gq
                slope = 2.0 ** (-8.0 * (hq + 1) / N_Q_HEADS)
                qh = q_ref[rows, hq * HEAD_DIM:(hq + 1) * HEAD_DIM]
                s = lax.dot_general(qh, kw, (((1,), (1,)), ((), ())), preferred_element_type=f32)
                s = jnp.where(mask, s * scale + (-slope) * dist, -1e30)
                sink = sink_ref[hq]
                m = jnp.maximum(jnp.max(s, axis=-1, keepdims=True), sink)
                e = jnp.exp(s - m)
                denom = jnp.sum(e, axis=-1, keepdims=True) + jnp.exp(sink - m)
                probs = e / denom
                oh = _dot(probs.astype(bf16), vw)
                mix_ref[rows, POOL_WIDTH + hq * HEAD_DIM:POOL_WIDTH + (hq + 1) * HEAD_DIM] = oh.astype(bf16)

    y = _dot(mix_ref[...], wout_ref[...])
    xr = ALPHA * x_ref[...] + (1.0 + mod_ref[2:3, :]) * y
    o_ref[...] = _layer_norm(xr, lng_ref[...], lnb_ref[...])


def _ev_mix(x2d, mod_l, p, q, k, v, w_pool, pool_scale, sink, w_out, ln_g, ln_b, B, S):
    T = x2d.shape[0]
    tq = 512
    nt = S // tq
    kb = tq // BLOCK
    pb = tq // POOL_HALO
    n_kblocks = T // BLOCK
    n_pblocks = T // POOL_HALO

    def main(b, i, s): return (b * nt + i, 0)
    def kprev(b, i, s): return (jnp.maximum((b * nt + i) * kb - 1, 0), 0)
    def knext(b, i, s): return (jnp.minimum((b * nt + i + 1) * kb, n_kblocks - 1), 0)
    def pprev(b, i, s): return (jnp.maximum((b * nt + i) * pb - 1, 0), 0)
    def pnext(b, i, s): return (jnp.minimum((b * nt + i + 1) * pb, n_pblocks - 1), 0)
    def const2(b, i, s): return (0, 0)

    kernel = functools.partial(_ev_mix_kernel, S=S, tq=tq)
    grid_spec = pltpu.PrefetchScalarGridSpec(
        num_scalar_prefetch=1,
        grid=(B, nt),
        in_specs=[
            pl.BlockSpec((tq, D), main),
            pl.BlockSpec((None, 6, D), lambda b, i, s: (b, 0, 0)),
            pl.BlockSpec((tq, POOL_WIDTH), main),
            pl.BlockSpec((POOL_HALO, POOL_WIDTH), pprev),
            pl.BlockSpec((POOL_HALO, POOL_WIDTH), pnext),
            pl.BlockSpec((tq, ATTN_WIDTH), main),
            pl.BlockSpec((tq, KV_WIDTH), main),
            pl.BlockSpec((BLOCK, KV_WIDTH), kprev),
            pl.BlockSpec((BLOCK, KV_WIDTH), knext),
            pl.BlockSpec((tq, KV_WIDTH), main),
            pl.BlockSpec((BLOCK, KV_WIDTH), kprev),
            pl.BlockSpec((BLOCK, KV_WIDTH), knext),
            pl.BlockSpec((len(POOL_WINDOWS), POOL_CH, POOL_CH), lambda b, i, s: (0, 0, 0)),
            pl.BlockSpec((1, POOL_WIDTH), const2),
            pl.BlockSpec((D, D), const2),
            pl.BlockSpec((1, D), const2),
            pl.BlockSpec((1, D), const2),
        ],
        out_specs=pl.BlockSpec((tq, D), main),
        scratch_shapes=[
            pltpu.VMEM((tq + 2 * POOL_HALO, POOL_WIDTH), f32),
            pltpu.VMEM((tq + 2 * BLOCK, KV_WIDTH), bf16),
            pltpu.VMEM((tq + 2 * BLOCK, KV_WIDTH), bf16),
            pltpu.VMEM((tq, D), bf16),
        ],
    )
    return pl.pallas_call(
        kernel,
        grid_spec=grid_spec,
        out_shape=jax.ShapeDtypeStruct((T, D), f32),
        compiler_params=pltpu.CompilerParams(vmem_limit_bytes=VMEM_LIMIT),
        name="ev_mix",
    )(sink, x2d, mod_l, p, p, p, q, k, k, k, v, v, v, w_pool, pool_scale, w_out, ln_g, ln_b)


def _ffn_kernel(x_ref, mod_ref, wg_ref, wu_ref, wd_ref, lng_ref, lnb_ref, o_ref, h_ref, acc_ref):
    j = pl.program_id(1)

    @pl.when(j == 0)
    def _():
        h = x_ref[...] * (1.0 + mod_ref[4:5, :]) + mod_ref[3:4, :]
        h_ref[...] = h.astype(bf16)
        acc_ref[...] = jnp.zeros_like(acc_ref)

    h = h_ref[...]
    a = _silu(_dot(h, wg_ref[...])) * _dot(h, wu_ref[...])
    acc_ref[...] += _dot(a.astype(bf16), wd_ref[...])

    @pl.when(j == pl.num_programs(1) - 1)
    def _():
        xr = ALPHA * x_ref[...] + (1.0 + mod_ref[5:6, :]) * acc_ref[...]
        o_ref[...] = _layer_norm(xr, lng_ref[...], lnb_ref[...])


def _ffn(x2d, mod_l, w_gate, w_up, w_down, ln_g, ln_b, S):
    T = x2d.shape[0]
    tm = 1024
    tf = 256
    dff = w_gate.shape[1]
    tpb = S // tm
    return pl.pallas_call(
        _ffn_kernel,
        grid=(T // tm, dff // tf),
        in_specs=[
            pl.BlockSpec((tm, D), lambda i, j: (i, 0)),
            pl.BlockSpec((None, 6, D), lambda i, j: (i // tpb, 0, 0)),
            pl.BlockSpec((D, tf), lambda i, j: (0, j)),
            pl.BlockSpec((D, tf), lambda i, j: (0, j)),
            pl.BlockSpec((tf, D), lambda i, j: (j, 0)),
            pl.BlockSpec((1, D), lambda i, j: (0, 0)),
            pl.BlockSpec((1, D), lambda i, j: (0, 0)),
        ],
        out_specs=pl.BlockSpec((tm, D), lambda i, j: (i, 0)),
        out_shape=jax.ShapeDtypeStruct((T, D), f32),
        scratch_shapes=[pltpu.VMEM((tm, D), bf16), pltpu.VMEM((tm, D), f32)],
        compiler_params=pltpu.CompilerParams(vmem_limit_bytes=VMEM_LIMIT),
        name="ffn",
    )(x2d, mod_l, w_gate, w_up, w_down, ln_g, ln_b)


def _sg_kernel(x_ref, mod_ref, win_ref, sgg_ref, sgb_ref, ws_ref, bst_ref, wout_ref, lng_ref, lnb_ref,
               o_ref, gate_ref, *, tm):
    x = x_ref[...]
    h = x * (1.0 + mod_ref[1:2, :]) + mod_ref[0:1, :]
    z = _gelu_tanh(_dot(h.astype(bf16), win_ref[...]))
    u = z[:, :D]
    v = _layer_norm(z[:, D:], sgg_ref[...], sgb_ref[...]).astype(bf16)
    for n in range(tm // CHUNK):
        rows = slice(n * CHUNK, (n + 1) * CHUNK)
        for g in range(SG_GROUPS):
            cols = slice(g * SG_CH, (g + 1) * SG_CH)
            sv = _dot(ws_ref[g], v[rows, cols]) + bst_ref[:, g:g + 1]
            gate_ref[rows, cols] = (u[rows, cols] * sv).astype(bf16)
    y = _dot(gate_ref[...], wout_ref[...])
    xr = ALPHA * x + (1.0 + mod_ref[2:3, :]) * y
    o_ref[...] = _layer_norm(xr, lng_ref[...], lnb_ref[...])


def _sg_mix(x2d, mod_l, w_in, sg_g, sg_b, w_s, b_s_t, w_out, ln_g, ln_b, S):
    T = x2d.shape[0]
    tm = 512
    tpb = S // tm
    c2 = lambda i: (0, 0)
    return pl.pallas_call(
        functools.partial(_sg_kernel, tm=tm),
        grid=(T // tm,),
        in_specs=[
            pl.BlockSpec((tm, D), lambda i: (i, 0)),
            pl.BlockSpec((None, 6, D), lambda i: (i // tpb, 0, 0)),
            pl.BlockSpec((D, 2 * D), c2),
            pl.BlockSpec((1, D), c2),
            pl.BlockSpec((1, D), c2),
            pl.BlockSpec((SG_GROUPS, CHUNK, CHUNK), lambda i: (0, 0, 0)),
            pl.BlockSpec((CHUNK, SG_GROUPS), c2),
            pl.BlockSpec((D, D), c2),
            pl.BlockSpec((1, D), c2),
            pl.BlockSpec((1, D), c2),
        ],
        out_specs=pl.BlockSpec((tm, D), lambda i: (i, 0)),
        out_shape=jax.ShapeDtypeStruct((T, D), f32),
        scratch_shapes=[pltpu.VMEM((tm, D), bf16)],
        compiler_params=pltpu.CompilerParams(vmem_limit_bytes=VMEM_LIMIT),
        name="sg_mix",
    )(x2d, mod_l, w_in, sg_g, sg_b, w_s, b_s_t, w_out, ln_g, ln_b)


def _route_kernel(x_ref, mod_ref, wrt_ref, hs_ref, route_ref, cnt_ref):
    W = ROUTE_W
    h = x_ref[...] * (1.0 + mod_ref[4:5, :]) + mod_ref[3:4, :]
    h_hi, h_lo = _split_bf16(h)
    w_hi, w_lo = _split_bf16(wrt_ref[...])
    nt = (((1,), (1,)), ((), ()))
    logits = (lax.dot_general(w_hi, h_hi, nt, preferred_element_type=f32)
              + (lax.dot_general(w_hi, h_lo, nt, preferred_element_type=f32)
                 + lax.dot_general(w_lo, h_hi, nt, preferred_element_type=f32)))

    eidx = lax.broadcasted_iota(jnp.int32, (N_EXPERTS, W), 0)
    m1 = jnp.max(logits, axis=0, keepdims=True)
    i1 = jnp.min(jnp.where(logits == m1, eidx, N_EXPERTS), axis=0, keepdims=True)
    sel1 = eidx == i1
    rest = jnp.where(sel1, -jnp.inf, logits)
    m2 = jnp.max(rest, axis=0, keepdims=True)
    i2 = jnp.min(jnp.where(rest == m2, eidx, N_EXPERTS), axis=0, keepdims=True)
    sel2 = eidx == i2
    e2 = jnp.exp(m2 - m1)
    g1 = 1.0 / (1.0 + e2)
    g2 = e2 / (1.0 + e2)

    a1 = sel1.astype(f32)
    a2 = sel2.astype(f32)
    assign = a1 + a2
    counts = jnp.sum(assign, axis=1, keepdims=True)
    grans = jnp.ceil(counts * (1.0 / GRAN))
    sub = lax.broadcasted_iota(jnp.int32, (N_EXPERTS, 1), 0)
    seg = jnp.zeros((N_EXPERTS, 1), f32)
    for e in range(N_EXPERTS - 1):
        seg = seg + jnp.where(sub > e, grans[e:e + 1, :] * GRAN, 0.0)
    tr = lax.broadcasted_iota(jnp.int32, (W, W), 0)
    tc = lax.broadcasted_iota(jnp.int32, (W, W), 1)
    upper = (tr < tc).astype(bf16)
    rank = _dot(assign.astype(bf16), upper)
    slot = seg + rank
    pos1 = jnp.sum(a1 * slot, axis=0, keepdims=True)
    pos2 = jnp.sum(a2 * slot, axis=0, keepdims=True)

    srow = lax.broadcasted_iota(jnp.int32, (CHUNK_SLOTS, W), 0)
    perm = ((srow == pos1.astype(jnp.int32)) | (srow == pos2.astype(jnp.int32)))
    hs_ref[...] = _dot(perm.astype(f32).astype(bf16), h_hi).astype(bf16)

    ridx = lax.broadcasted_iota(jnp.int32, (8, W), 0)
    route = jnp.where(ridx == 0, pos1, jnp.where(ridx == 1, pos2, jnp.where(ridx == 2, g1, jnp.where(ridx == 3, g2, 0.0))))
    route_ref[...] = route
    cnt_ref[...] = jnp.broadcast_to(counts, (N_EXPERTS, 128)).astype(jnp.int32)


def _route(x2d, mod_l, w_router_t, S):
    T = x2d.shape[0]
    W = ROUTE_W
    nc = T // W
    tpb = S // W
    return pl.pallas_call(
        _route_kernel,
        grid=(nc,),
        in_specs=[
            pl.BlockSpec((W, D), lambda c: (c, 0)),
            pl.BlockSpec((None, 6, D), lambda c: (c // tpb, 0, 0)),
            pl.BlockSpec((N_EXPERTS, D), lambda c: (0, 0)),
        ],
        out_specs=[
            pl.BlockSpec((None, CHUNK_SLOTS, D), lambda c: (c, 0, 0)),
            pl.BlockSpec((None, 8, W), lambda c: (c, 0, 0)),
            pl.BlockSpec((None, N_EXPERTS, 128), lambda c: (c, 0, 0)),
        ],
        out_shape=[
            jax.ShapeDtypeStruct((nc, CHUNK_SLOTS, D), bf16),
            jax.ShapeDtypeStruct((nc, 8, W), f32),
            jax.ShapeDtypeStruct((nc, N_EXPERTS, 128), jnp.int32),
        ],
        compiler_params=pltpu.CompilerParams(vmem_limit_bytes=VMEM_LIMIT),
        name="route",
    )(x2d, mod_l, w_router_t)


_COPY_WINDOW = 16


def _granule_copy_kernel(idx_ref, src_ref, dst_ref, sem, *, n):
    def copy(g):
        return pltpu.make_async_copy(src_ref.at[idx_ref[g]], dst_ref.at[g], sem)

    def body(g, carry):
        copy(g).start()

        @pl.when(g >= _COPY_WINDOW)
        def _():
            copy(g - _COPY_WINDOW).wait()
        return carry

    lax.fori_loop(0, n, body, 0)

    def drain(g, carry):
        copy(g).wait()
        return carry

    lax.fori_loop(n - _COPY_WINDOW, n, drain, 0)


def _granule_copy(idx, src, name):
    n = idx.shape[0]
    grid_spec = pltpu.PrefetchScalarGridSpec(
        num_scalar_prefetch=1,
        grid=(1,),
        in_specs=[pl.BlockSpec(memory_space=pl.ANY)],
        out_specs=pl.BlockSpec(memory_space=pl.ANY),
        scratch_shapes=[pltpu.SemaphoreType.DMA(())],
    )
    return pl.pallas_call(
        functools.partial(_granule_copy_kernel, n=n),
        grid_spec=grid_spec,
        out_shape=jax.ShapeDtypeStruct((n, GRAN, D), src.dtype),
        name=name,
    )(idx, src)


def _expert_kernel(te_ref, tv_ref, x_ref, wg_ref, wu_ref, wd_ref, o_ref, acc_ref):
    i = pl.program_id(0)
    j = pl.program_id(1)
    last = pl.num_programs(1) - 1
    valid = tv_ref[i] > 0

    @pl.when(valid)
    def _():
        @pl.when(j == 0)
        def _():
            acc_ref[...] = jnp.zeros_like(acc_ref)

        x = x_ref[...]
        a = _silu(_dot(x, wg_ref[...])) * _dot(x, wu_ref[...])
        acc_ref[...] += _dot(a.astype(bf16), wd_ref[...])

        @pl.when(j == last)
        def _():
            o_ref[...] = acc_ref[...].astype(o_ref.dtype)

    @pl.when(jnp.logical_not(valid) & (j == last))
    def _():
        o_ref[...] = jnp.zeros_like(o_ref)


def _experts(tile_expert, tile_valid, xs, w_gate, w_up, w_down):
    n_tiles = tile_expert.shape[0]
    dff = w_gate.shape[2]
    nff = dff // MOE_TF

    def jj(j, tv, i):
        return jnp.where(tv[i] > 0, j, nff - 1)

    grid_spec = pltpu.PrefetchScalarGridSpec(
        num_scalar_prefetch=2,
        grid=(n_tiles, nff),
        in_specs=[
            pl.BlockSpec((MOE_TM, D), lambda i, j, te, tv: (i, 0)),
            pl.BlockSpec((None, D, MOE_TF), lambda i, j, te, tv: (te[i], 0, jj(j, tv, i))),
            pl.BlockSpec((None, D, MOE_TF), lambda i, j, te, tv: (te[i], 0, jj(j, tv, i))),
            pl.BlockSpec((None, MOE_TF, D), lambda i, j, te, tv: (te[i], jj(j, tv, i), 0)),
        ],
        out_specs=pl.BlockSpec((MOE_TM, D), lambda i, j, te, tv: (i, 0)),
        scratch_shapes=[pltpu.VMEM((MOE_TM, D), f32)],
    )
    return pl.pallas_call(
        _expert_kernel,
        grid_spec=grid_spec,
        out_shape=jax.ShapeDtypeStruct((n_tiles * MOE_TM, D), bf16),
        compiler_params=pltpu.CompilerParams(vmem_limit_bytes=VMEM_LIMIT),
        name="experts",
    )(tile_expert, tile_valid, xs, w_gate, w_up, w_down)


def _combine_kernel(x_ref, mod_ref, os_ref, rt_ref, lng_ref, lnb_ref, o_ref):
    W = ROUTE_W
    rt = rt_ref[...]
    scol = lax.broadcasted_iota(jnp.int32, (W, CHUNK_SLOTS), 1)
    p1 = (scol == rt[:, 0:1].astype(jnp.int32)).astype(f32).astype(bf16)
    p2 = (scol == rt[:, 1:2].astype(jnp.int32)).astype(f32).astype(bf16)
    osv = os_ref[...]
    y = rt[:, 2:3] * _dot(p1, osv) + rt[:, 3:4] * _dot(p2, osv)
    xr = ALPHA * x_ref[...] + (1.0 + mod_ref[5:6, :]) * y
    o_ref[...] = _layer_norm(xr, lng_ref[...], lnb_ref[...])


def _combine(x2d, mod_l, os_, route_t, ln_g, ln_b, S):
    T = x2d.shape[0]
    W = ROUTE_W
    tpb = S // W
    return pl.pallas_call(
        _combine_kernel,
        grid=(T // W,),
        in_specs=[
            pl.BlockSpec((W, D), lambda c: (c, 0)),
            pl.BlockSpec((None, 6, D), lambda c: (c // tpb, 0, 0)),
            pl.BlockSpec((None, CHUNK_SLOTS, D), lambda c: (c, 0, 0)),
            pl.BlockSpec((None, W, 8), lambda c: (c, 0, 0)),
            pl.BlockSpec((1, D), lambda c: (0, 0)),
            pl.BlockSpec((1, D), lambda c: (0, 0)),
        ],
        out_specs=pl.BlockSpec((W, D), lambda c: (c, 0)),
        out_shape=jax.ShapeDtypeStruct((T, D), f32),
        compiler_params=pltpu.CompilerParams(vmem_limit_bytes=VMEM_LIMIT),
        name="combine",
    )(x2d, mod_l, os_, route_t, ln_g, ln_b)


def _routing_tables(counts, n_tiles):
    nc = counts.shape[0]
    gr = (counts + GRAN - 1) // GRAN
    seg_start = jnp.cumsum(gr, axis=1) - gr
    chunk_total = jnp.sum(gr, axis=1)
    prefix = jnp.cumsum(gr, axis=0) - gr
    g_e = jnp.sum(gr, axis=0)
    tiles_e = (g_e + TILE_GRANS - 1) // TILE_GRANS
    tile_end = jnp.cumsum(tiles_e)
    tile_start = tile_end - tiles_e
    total_tiles = tile_end[-1]

    t = jnp.arange(n_tiles, dtype=jnp.int32)
    te = jnp.sum((t[:, None] >= tile_end[None, :]).astype(jnp.int32), axis=1)
    tile_valid = (t < total_tiles).astype(jnp.int32)
    last_e = jnp.sum((total_tiles - 1 >= tile_end).astype(jnp.int32))
    tile_expert = jnp.where(tile_valid > 0, jnp.minimum(te, N_EXPERTS - 1), last_e).astype(jnp.int32)

    k = jnp.arange(CHUNK_GRANS, dtype=jnp.int32)
    seg_end = seg_start + gr
    e_of = jnp.sum((k[None, :, None] >= seg_end[:, None, :]).astype(jnp.int32), axis=2)
    e_cl = jnp.minimum(e_of, N_EXPERTS - 1)
    j_in = k[None, :] - jnp.take_along_axis(seg_start, e_cl, axis=1)
    dst = (tile_start[e_cl] * TILE_GRANS + jnp.take_along_axis(prefix, e_cl, axis=1) + j_in)
    valid_src = k[None, :] < chunk_total[:, None]
    dst_of_src = jnp.where(valid_src, dst, 0).astype(jnp.int32).reshape(-1)

    d = jnp.arange(n_tiles * TILE_GRANS, dtype=jnp.int32)
    e_d = tile_expert[d // TILE_GRANS]
    q = d - tile_start[e_d] * TILE_GRANS
    incl = (prefix + gr).T
    c_d = jnp.sum((q[:, None] >= incl[e_d]).astype(jnp.int32), axis=1)
    c_cl = jnp.minimum(c_d, nc - 1)
    src = c_cl * CHUNK_GRANS + seg_start[c_cl, e_d] + (q - prefix[c_cl, e_d])
    valid_dst = (tile_valid[d // TILE_GRANS] > 0) & (q >= 0) & (q < g_e[e_d])
    src_of_dst = jnp.where(valid_dst, src, 0).astype(jnp.int32)
    return tile_expert, tile_valid, src_of_dst, dst_of_src


def kernel(x, c, ada_w, ada_b, ln_g, ln_b, ev_w_in, ev_pool_w, ev_pool_scale, ev_sink, ev_w_out, od_w_in, od_sg_ln_g, od_sg_ln_b, od_w_s, od_b_s, od_w_out, ffn_w_gate, ffn_w_up, ffn_w_down, moe_w_router, moe_w_gate, moe_w_up, moe_w_down):
    B, S, _ = x.shape
    T = B * S
    x2d = x.reshape(T, D)
    mod = _adaln(c, ada_w, ada_b)

    p, q, k, v = _ev_in(x2d, mod[0], ev_w_in[0].astype(bf16), S)
    x2d = _ev_mix(x2d, mod[0], p, q, k, v, ev_pool_w[0].astype(bf16), ev_pool_scale[0][None, :],
                  ev_sink[0], ev_w_out[0].astype(bf16), ln_g[0, 0][None, :], ln_b[0, 0][None, :], B, S)
    x2d = _ffn(x2d, mod[0], ffn_w_gate[0].astype(bf16), ffn_w_up[0].astype(bf16),
               ffn_w_down[0].astype(bf16), ln_g[0, 1][None, :], ln_b[0, 1][None, :], S)

    x2d = _sg_mix(x2d, mod[1], od_w_in[0].astype(bf16), od_sg_ln_g[0][None, :], od_sg_ln_b[0][None, :],
                  od_w_s[0].astype(bf16), od_b_s[0].T, od_w_out[0].astype(bf16),
                  ln_g[1, 0][None, :], ln_b[1, 0][None, :], S)

    hs, route, cnt = _route(x2d, mod[1], moe_w_router[0].T, S)
    nc = T // ROUTE_W
    n_tiles = (nc * CHUNK_GRANS) // TILE_GRANS + N_EXPERTS
    tile_expert, tile_valid, src_of_dst, dst_of_src = _routing_tables(cnt[:, :, 0], n_tiles)
    xs = _granule_copy(src_of_dst, hs.reshape(nc * CHUNK_GRANS, GRAN, D), "regroup_in")
    o = _experts(tile_expert, tile_valid, xs.reshape(n_tiles * MOE_TM, D),
                 moe_w_gate[0].astype(bf16), moe_w_up[0].astype(bf16), moe_w_down[0].astype(bf16))
    os_ = _granule_copy(dst_of_src, o.reshape(n_tiles * TILE_GRANS, GRAN, D), "regroup_out")
    x2d = _combine(x2d, mod[1], os_.reshape(nc, CHUNK_SLOTS, D), jnp.swapaxes(route, 1, 2),
                   ln_g[1, 1][None, :], ln_b[1, 1][None, :], S)
    return x2d.reshape(B, S, D)
```

```python
import functools
import math

import jax
import jax.numpy as jnp
from jax import lax
from jax.experimental import pallas as pl
from jax.experimental.pallas import tpu as pltpu

D = 1024
DEPTH = 2
ALPHA = (2.0 * DEPTH) ** 0.25
LN_EPS = 1e-5

POOL_WINDOWS = (2, 4, 8, 16)
POOL_CH = 128
POOL_WIDTH = 512
HEAD_DIM = 64
N_Q_HEADS = 8
N_KV_HEADS = 2
Q_GROUP = 4
ATTN_WIDTH = 512
KV_WIDTH = 128
BLOCK = 128
EVEN_IN = 1280
POOL_HALO = 8

CHUNK = 128
SG_GROUPS = 8
SG_CH = 128

N_EXPERTS = 8

ROUTE_W = 512
GRAN = 16
CHUNK_SLOTS = 2 * ROUTE_W + N_EXPERTS * GRAN
CHUNK_GRANS = CHUNK_SLOTS // GRAN
MOE_TM = 512
TILE_GRANS = MOE_TM // GRAN
MOE_TF = 512

VMEM_LIMIT = 48 * 1024 * 1024

bf16 = jnp.bfloat16
f32 = jnp.float32


def _dot(a, b):
    return jnp.dot(a, b, preferred_element_type=f32)


def _split_bf16(a):
    hi = a.astype(bf16)
    lo = (a - hi.astype(f32)).astype(bf16)
    return hi, lo


def _layer_norm(x, g, b):
    mu = jnp.mean(x, axis=-1, keepdims=True)
    xc = x - mu
    var = jnp.mean(xc * xc, axis=-1, keepdims=True)
    return xc * lax.rsqrt(var + LN_EPS) * g + b


def _silu(x):
    return x * jax.nn.sigmoid(x)


def _gelu_tanh(x):
    c = math.sqrt(2.0 / math.pi)
    return x * (0.5 * (1.0 + jnp.tanh(c * (x + 0.044715 * (x * x * x)))))


def _adaln_kernel(c_ref, w_ref, b_ref, o_ref):
    cond = _silu(c_ref[...])
    c_hi, c_lo = _split_bf16(cond)
    w_hi, w_lo = _split_bf16(w_ref[...])
    acc = _dot(c_hi, w_hi) + (_dot(c_lo, w_hi) + _dot(c_hi, w_lo))
    o_ref[...] = acc + b_ref[...]


def _adaln(c, ada_w, ada_b):
    B = c.shape[0]
    tn = 1024
    c_pad = jnp.zeros((8, D), f32).at[:B].set(c)
    out = pl.pallas_call(
        _adaln_kernel,
        grid=(DEPTH, 6 * D // tn),
        in_specs=[
            pl.BlockSpec((8, D), lambda l, j: (0, 0)),
            pl.BlockSpec((None, D, tn), lambda l, j: (l, 0, j)),
            pl.BlockSpec((None, 1, tn), lambda l, j: (l, 0, j)),
        ],
        out_specs=pl.BlockSpec((None, 8, tn), lambda l, j: (l, 0, j)),
        out_shape=jax.ShapeDtypeStruct((DEPTH, 8, 6 * D), f32),
        compiler_params=pltpu.CompilerParams(vmem_limit_bytes=VMEM_LIMIT),
        name="adaln",
    )(c_pad, ada_w, ada_b.reshape(DEPTH, 1, 6 * D))
    return out[:, :B].reshape(DEPTH, B, 6, D)


def _ev_in_kernel(x_ref, mod_ref, w_ref, p_ref, q_ref, k_ref, v_ref):
    h = x_ref[...] * (1.0 + mod_ref[1:2, :]) + mod_ref[0:1, :]
    z = _dot(h.astype(bf16), w_ref[...])
    p_ref[...] = z[:, :POOL_WIDTH]
    q_ref[...] = z[:, POOL_WIDTH:POOL_WIDTH + ATTN_WIDTH].astype(bf16)
    k_ref[...] = z[:, POOL_WIDTH + ATTN_WIDTH:POOL_WIDTH + ATTN_WIDTH + KV_WIDTH].astype(bf16)
    v_ref[...] = z[:, POOL_WIDTH + ATTN_WIDTH + KV_WIDTH:].astype(bf16)


def _ev_in(x2d, mod_l, w_in, S):
    T = x2d.shape[0]
    tm = 512
    tpb = S // tm
    return pl.pallas_call(
        _ev_in_kernel,
        grid=(T // tm,),
        in_specs=[
            pl.BlockSpec((tm, D), lambda i: (i, 0)),
            pl.BlockSpec((None, 6, D), lambda i: (i // tpb, 0, 0)),
            pl.BlockSpec((D, EVEN_IN), lambda i: (0, 0)),
        ],
        out_specs=[
            pl.BlockSpec((tm, POOL_WIDTH), lambda i: (i, 0)),
            pl.BlockSpec((tm, ATTN_WIDTH), lambda i: (i, 0)),
            pl.BlockSpec((tm, KV_WIDTH), lambda i: (i, 0)),
            pl.BlockSpec((tm, KV_WIDTH), lambda i: (i, 0)),
        ],
        out_shape=[
            jax.ShapeDtypeStruct((T, POOL_WIDTH), f32),
            jax.ShapeDtypeStruct((T, ATTN_WIDTH), bf16),
            jax.ShapeDtypeStruct((T, KV_WIDTH), bf16),
            jax.ShapeDtypeStruct((T, KV_WIDTH), bf16),
        ],
        compiler_params=pltpu.CompilerParams(vmem_limit_bytes=VMEM_LIMIT),
        name="ev_in",
    )(x2d, mod_l, w_in)


def _ev_mix_kernel(sink_ref, x_ref, mod_ref, p_ref, pp_ref, pn_ref, q_ref,
                   k_ref, kp_ref, kn_ref, v_ref, vp_ref, vn_ref,
                   wpool_ref, pscale_ref, wout_ref, lng_ref, lnb_ref,
                   o_ref, pext_ref, kext_ref, vext_ref, mix_ref, *, S, tq):
    i = pl.program_id(1)
    n_tiles = S // tq
    is_first = i == 0
    is_last = i == n_tiles - 1

    p = p_ref[...]
    pext_ref[0:POOL_HALO, :] = jnp.where(is_first, 0.0, pp_ref[...])
    pext_ref[POOL_HALO:POOL_HALO + tq, :] = p
    pext_ref[POOL_HALO + tq:, :] = jnp.where(is_last, 0.0, pn_ref[...])
    pos = i * tq + lax.broadcasted_iota(jnp.int32, (tq, 1), 0)
    for g, w in enumerate(POOL_WINDOWS):
        r = w // 2
        cs = slice(g * POOL_CH, (g + 1) * POOL_CH)
        acc = pext_ref[POOL_HALO - r:POOL_HALO - r + tq, cs]
        for d in range(-r + 1, r + 1):
            acc = acc + pext_ref[POOL_HALO + d:POOL_HALO + d + tq, cs]
        cnt = (jnp.minimum(pos + r + 1, S) - jnp.maximum(pos - r, 0)).astype(f32)
        pooled = acc / cnt - p[:, cs]
        ya = _dot(pooled.astype(bf16), wpool_ref[g])
        mix_ref[:, cs] = (ya * pscale_ref[:, cs]).astype(bf16)

    kext_ref[0:BLOCK, :] = kp_ref[...]
    kext_ref[BLOCK:BLOCK + tq, :] = k_ref[...]
    kext_ref[BLOCK + tq:, :] = kn_ref[...]
    vext_ref[0:BLOCK, :] = vp_ref[...]
    vext_ref[BLOCK:BLOCK + tq, :] = v_ref[...]
    vext_ref[BLOCK + tq:, :] = vn_ref[...]

    qi = lax.broadcasted_iota(jnp.int32, (BLOCK, 3 * BLOCK), 0)
    kj = lax.broadcasted_iota(jnp.int32, (BLOCK, 3 * BLOCK), 1)
    dist_i = jnp.abs(kj - BLOCK - qi)
    dist = dist_i.astype(f32)
    n_blocks = S // BLOCK
    scale = HEAD_DIM ** -0.5
    for n in range(tq // BLOCK):
        gb = i * (tq // BLOCK) + n
        lo = jnp.where(gb == 0, BLOCK, 0)
        hi = jnp.where(gb == n_blocks - 1, 2 * BLOCK, 3 * BLOCK)
        mask = (dist_i <= BLOCK) & (kj >= lo) & (kj < hi)
        rows = slice(n * BLOCK, (n + 1) * BLOCK)
        for kvh in range(N_KV_HEADS):
            kw = kext_ref[n * BLOCK:n * BLOCK + 3 * BLOCK, kvh * HEAD_DIM:(kvh + 1) * HEAD_DIM]
            vw = vext_ref[n * BLOCK:n * BLOCK + 3 * BLOCK, kvh * HEAD_DIM:(kvh + 1) * HEAD_DIM]
            for gq in range(Q_GROUP):
                hq = kvh * Q_GROUP + gq
                slope = 2.0 ** (-8.0 * (hq + 1) / N_Q_HEADS)
                qh = q_ref[rows, hq * HEAD_DIM:(hq + 1) * HEAD_DIM]
                s = lax.dot_general(qh, kw, (((1,), (1,)), ((), ())), preferred_element_type=f32)
                s = jnp.where(mask, s * scale + (-slope) * dist, -1e30)
                sink = sink_ref[hq]
                m = jnp.maximum(jnp.max(s, axis=-1, keepdims=True), sink)
                e = jnp.exp(s - m)
                denom = jnp.sum(e, axis=-1, keepdims=True) + jnp.exp(sink - m)
                probs = e / denom
                oh = _dot(probs.astype(bf16), vw)
                mix_ref[rows, POOL_WIDTH + hq * HEAD_DIM:POOL_WIDTH + (hq + 1) * HEAD_DIM] = oh.astype(bf16)

    y = _dot(mix_ref[...], wout_ref[...])
    xr = ALPHA * x_ref[...] + (1.0 + mod_ref[2:3, :]) * y
    o_ref[...] = _layer_norm(xr, lng_ref[...], lnb_ref[...])


def _ev_mix(x2d, mod_l, p, q, k, v, w_pool, pool_scale, sink, w_out, ln_g, ln_b, B, S):
    T = x2d.shape[0]
    tq = 512
    nt = S // tq
    kb = tq // BLOCK
    pb = tq // POOL_HALO
    n_kblocks = T // BLOCK
    n_pblocks = T // POOL_HALO

    def main(b, i, s): return (b * nt + i, 0)
    def kprev(b, i, s): return (jnp.maximum((b * nt + i) * kb - 1, 0), 0)
    def knext(b, i, s): return (jnp.minimum((b * nt + i + 1) * kb, n_kblocks - 1), 0)
    def pprev(b, i, s): return (jnp.maximum((b * nt + i) * pb - 1, 0), 0)
    def pnext(b, i, s): return (jnp.minimum((b * nt + i + 1) * pb, n_pblocks - 1), 0)
    def const2(b, i, s): return (0, 0)

    kernel = functools.partial(_ev_mix_kernel, S=S, tq=tq)
    grid_spec = pltpu.PrefetchScalarGridSpec(
        num_scalar_prefetch=1,
        grid=(B, nt),
        in_specs=[
            pl.BlockSpec((tq, D), main),
            pl.BlockSpec((None, 6, D), lambda b, i, s: (b, 0, 0)),
            pl.BlockSpec((tq, POOL_WIDTH), main),
            pl.BlockSpec((POOL_HALO, POOL_WIDTH), pprev),
            pl.BlockSpec((POOL_HALO, POOL_WIDTH), pnext),
            pl.BlockSpec((tq, ATTN_WIDTH), main),
            pl.BlockSpec((tq, KV_WIDTH), main),
            pl.BlockSpec((BLOCK, KV_WIDTH), kprev),
            pl.BlockSpec((BLOCK, KV_WIDTH), knext),
            pl.BlockSpec((tq, KV_WIDTH), main),
            pl.BlockSpec((BLOCK, KV_WIDTH), kprev),
            pl.BlockSpec((BLOCK, KV_WIDTH), knext),
            pl.BlockSpec((len(POOL_WINDOWS), POOL_CH, POOL_CH), lambda b, i, s: (0, 0, 0)),
            pl.BlockSpec((1, POOL_WIDTH), const2),
            pl.BlockSpec((D, D), const2),
            pl.BlockSpec((1, D), const2),
            pl.BlockSpec((1, D), const2),
        ],
        out_specs=pl.BlockSpec((tq, D), main),
        scratch_shapes=[
            pltpu.VMEM((tq + 2 * POOL_HALO, POOL_WIDTH), f32),
            pltpu.VMEM((tq + 2 * BLOCK, KV_WIDTH), bf16),
            pltpu.VMEM((tq + 2 * BLOCK, KV_WIDTH), bf16),
            pltpu.VMEM((tq, D), bf16),
        ],
    )
    return pl.pallas_call(
        kernel,
        grid_spec=grid_spec,
        out_shape=jax.ShapeDtypeStruct((T, D), f32),
        compiler_params=pltpu.CompilerParams(vmem_limit_bytes=VMEM_LIMIT),
        name="ev_mix",
    )(sink, x2d, mod_l, p, p, p, q, k, k, k, v, v, v, w_pool, pool_scale, w_out, ln_g, ln_b)


def _ffn_kernel(x_ref, mod_ref, wg_ref, wu_ref, wd_ref, lng_ref, lnb_ref, o_ref, h_ref, acc_ref):
    j = pl.program_id(1)

    @pl.when(j == 0)
    def _():
        h = x_ref[...] * (1.0 + mod_ref[4:5, :]) + mod_ref[3:4, :]
        h_ref[...] = h.astype(bf16)
        acc_ref[...] = jnp.zeros_like(acc_ref)

    h = h_ref[...]
    a = _silu(_dot(h, wg_ref[...])) * _dot(h, wu_ref[...])
    acc_ref[...] += _dot(a.astype(bf16), wd_ref[...])

    @pl.when(j == pl.num_programs(1) - 1)
    def _():
        xr = ALPHA * x_ref[...] + (1.0 + mod_ref[5:6, :]) * acc_ref[...]
        o_ref[...] = _layer_norm(xr, lng_ref[...], lnb_ref[...])


def _ffn(x2d, mod_l, w_gate, w_up, w_down, ln_g, ln_b, S):
    T = x2d.shape[0]
    tm = 1024
    tf = 256
    dff = w_gate.shape[1]
    tpb = S // tm
    return pl.pallas_call(
        _ffn_kernel,
        grid=(T // tm, dff // tf),
        in_specs=[
            pl.BlockSpec((tm, D), lambda i, j: (i, 0)),
            pl.BlockSpec((None, 6, D), lambda i, j: (i // tpb, 0, 0)),
            pl.BlockSpec((D, tf), lambda i, j: (0, j)),
            pl.BlockSpec((D, tf), lambda i, j: (0, j)),
            pl.BlockSpec((tf, D), lambda i, j: (j, 0)),
            pl.BlockSpec((1, D), lambda i, j: (0, 0)),
            pl.BlockSpec((1, D), lambda i, j: (0, 0)),
        ],
        out_specs=pl.BlockSpec((tm, D), lambda i, j: (i, 0)),
        out_shape=jax.ShapeDtypeStruct((T, D), f32),
        scratch_shapes=[pltpu.VMEM((tm, D), bf16), pltpu.VMEM((tm, D), f32)],
        compiler_params=pltpu.CompilerParams(vmem_limit_bytes=VMEM_LIMIT),
        name="ffn",
    )(x2d, mod_l, w_gate, w_up, w_down, ln_g, ln_b)


def _sg_kernel(x_ref, mod_ref, win_ref, sgg_ref, sgb_ref, ws_ref, bst_ref, wout_ref, lng_ref, lnb_ref,
               o_ref, gate_ref, *, tm):
    x = x_ref[...]
    h = x * (1.0 + mod_ref[1:2, :]) + mod_ref[0:1, :]
    z = _gelu_tanh(_dot(h.astype(bf16), win_ref[...]))
    u = z[:, :D]
    v = _layer_norm(z[:, D:], sgg_ref[...], sgb_ref[...]).astype(bf16)
    for n in range(tm // CHUNK):
        rows = slice(n * CHUNK, (n + 1) * CHUNK)
        for g in range(SG_GROUPS):
            cols = slice(g * SG_CH, (g + 1) * SG_CH)
            sv = _dot(ws_ref[g], v[rows, cols]) + bst_ref[:, g:g + 1]
            gate_ref[rows, cols] = (u[rows, cols] * sv).astype(bf16)
    y = _dot(gate_ref[...], wout_ref[...])
    xr = ALPHA * x + (1.0 + mod_ref[2:3, :]) * y
    o_ref[...] = _layer_norm(xr, lng_ref[...], lnb_ref[...])


def _sg_mix(x2d, mod_l, w_in, sg_g, sg_b, w_s, b_s_t, w_out, ln_g, ln_b, S):
    T = x2d.shape[0]
    tm = 512
    tpb = S // tm
    c2 = lambda i: (0, 0)
    return pl.pallas_call(
        functools.partial(_sg_kernel, tm=tm),
        grid=(T // tm,),
        in_specs=[
            pl.BlockSpec((tm, D), lambda i: (i, 0)),
            pl.BlockSpec((None, 6, D), lambda i: (i // tpb, 0, 0)),
            pl.BlockSpec((D, 2 * D), c2),
            pl.BlockSpec((1, D), c2),
            pl.BlockSpec((1, D), c2),
            pl.BlockSpec((SG_GROUPS, CHUNK, CHUNK), lambda i: (0, 0, 0)),
            pl.BlockSpec((CHUNK, SG_GROUPS), c2),
            pl.BlockSpec((D, D), c2),
            pl.BlockSpec((1, D), c2),
            pl.BlockSpec((1, D), c2),
        ],
        out_specs=pl.BlockSpec((tm, D), lambda i: (i, 0)),
        out_shape=jax.ShapeDtypeStruct((T, D), f32),
        scratch_shapes=[pltpu.VMEM((tm, D), bf16)],
        compiler_params=pltpu.CompilerParams(vmem_limit_bytes=VMEM_LIMIT),
        name="sg_mix",
    )(x2d, mod_l, w_in, sg_g, sg_b, w_s, b_s_t, w_out, ln_g, ln_b)


def _route_kernel(x_ref, mod_ref, wrt_ref, hs_ref, route_ref, cnt_ref):
    W = ROUTE_W
    h = x_ref[...] * (1.0 + mod_ref[4:5, :]) + mod_ref[3:4, :]
    h_hi, h_lo = _split_bf16(h)
    w_hi, w_lo = _split_bf16(wrt_ref[...])
    nt = (((1,), (1,)), ((), ()))
    logits = (lax.dot_general(w_hi, h_hi, nt, preferred_element_type=f32)
              + (lax.dot_general(w_hi, h_lo, nt, preferred_element_type=f32)
                 + lax.dot_general(w_lo, h_hi, nt, preferred_element_type=f32)))

    eidx = lax.broadcasted_iota(jnp.int32, (N_EXPERTS, W), 0)
    m1 = jnp.max(logits, axis=0, keepdims=True)
    i1 = jnp.min(jnp.where(logits == m1, eidx, N_EXPERTS), axis=0, keepdims=True)
    sel1 = eidx == i1
    rest = jnp.where(sel1, -jnp.inf, logits)
    m2 = jnp.max(rest, axis=0, keepdims=True)
    i2 = jnp.min(jnp.where(rest == m2, eidx, N_EXPERTS), axis=0, keepdims=True)
    sel2 = eidx == i2
    e2 = jnp.exp(m2 - m1)
    g1 = 1.0 / (1.0 + e2)
    g2 = e2 / (1.0 + e2)

    a1 = sel1.astype(f32)
    a2 = sel2.astype(f32)
    assign = a1 + a2
    counts = jnp.sum(assign, axis=1, keepdims=True)
    grans = jnp.ceil(counts * (1.0 / GRAN))
    sub = lax.broadcasted_iota(jnp.int32, (N_EXPERTS, 1), 0)
    seg = jnp.zeros((N_EXPERTS, 1), f32)
    for e in range(N_EXPERTS - 1):
        seg = seg + jnp.where(sub > e, grans[e:e + 1, :] * GRAN, 0.0)
    tr = lax.broadcasted_iota(jnp.int32, (W, W), 0)
    tc = lax.broadcasted_iota(jnp.int32, (W, W), 1)
    upper = (tr < tc).astype(bf16)
    rank = _dot(assign.astype(bf16), upper)
    slot = seg + rank
    pos1 = jnp.sum(a1 * slot, axis=0, keepdims=True)
    pos2 = jnp.sum(a2 * slot, axis=0, keepdims=True)

    srow = lax.broadcasted_iota(jnp.int32, (CHUNK_SLOTS, W), 0)
    perm = ((srow == pos1.astype(jnp.int32)) | (srow == pos2.astype(jnp.int32)))
    hs_ref[...] = _dot(perm.astype(f32).astype(bf16), h_hi).astype(bf16)

    ridx = lax.broadcasted_iota(jnp.int32, (8, W), 0)
    route = jnp.where(ridx == 0, pos1, jnp.where(ridx == 1, pos2, jnp.where(ridx == 2, g1, jnp.where(ridx == 3, g2, 0.0))))
    route_ref[...] = route
    cnt_ref[...] = jnp.broadcast_to(counts, (N_EXPERTS, 128)).astype(jnp.int32)


def _route(x2d, mod_l, w_router_t, S):
    T = x2d.shape[0]
    W = ROUTE_W
    nc = T // W
    tpb = S // W
    return pl.pallas_call(
        _route_kernel,
        grid=(nc,),
        in_specs=[
            pl.BlockSpec((W, D), lambda c: (c, 0)),
            pl.BlockSpec((None, 6, D), lambda c: (c // tpb, 0, 0)),
            pl.BlockSpec((N_EXPERTS, D), lambda c: (0, 0)),
        ],
        out_specs=[
            pl.BlockSpec((None, CHUNK_SLOTS, D), lambda c: (c, 0, 0)),
            pl.BlockSpec((None, 8, W), lambda c: (c, 0, 0)),
            pl.BlockSpec((None, N_EXPERTS, 128), lambda c: (c, 0, 0)),
        ],
        out_shape=[
            jax.ShapeDtypeStruct((nc, CHUNK_SLOTS, D), bf16),
            jax.ShapeDtypeStruct((nc, 8, W), f32),
            jax.ShapeDtypeStruct((nc, N_EXPERTS, 128), jnp.int32),
        ],
        compiler_params=pltpu.CompilerParams(vmem_limit_bytes=VMEM_LIMIT),
        name="route",
    )(x2d, mod_l, w_router_t)


def _granule_copy(src_ref, buf_ref, sem, idx_ref, base, g):
    row = pl.multiple_of(idx_ref[base + g] * GRAN, GRAN)
    return pltpu.make_async_copy(src_ref.at[pl.ds(row, GRAN), :], buf_ref.at[pl.ds(g * GRAN, GRAN), :], sem)


def _gather_start(src_ref, buf_ref, sem, idx_ref, base, n):
    for g in range(n):
        _granule_copy(src_ref, buf_ref, sem, idx_ref, base, g).start()


def _gather_wait(src_ref, buf_ref, sem, idx_ref, base, n):
    for g in range(n):
        _granule_copy(src_ref, buf_ref, sem, idx_ref, base, g).wait()


def _expert_kernel(te_ref, tv_ref, src_ref, hs_ref, wg_ref, wu_ref, wd_ref, o_ref, xbuf_ref, sem, acc_ref):
    i = pl.program_id(0)
    j = pl.program_id(1)
    n_tiles = pl.num_programs(0)
    last = pl.num_programs(1) - 1
    valid = tv_ref[i] > 0
    slot = i % 2
    nxt = jnp.minimum(i + 1, n_tiles - 1)

    @pl.when(j == 0)
    def _():
        @pl.when(i == 0)
        def _():
            _gather_start(hs_ref, xbuf_ref.at[0], sem.at[0], src_ref, 0, TILE_GRANS)

        @pl.when(valid)
        def _():
            _gather_wait(hs_ref, xbuf_ref.at[slot], sem.at[slot], src_ref, i * TILE_GRANS, TILE_GRANS)

        @pl.when((i + 1 < n_tiles) & (tv_ref[nxt] > 0))
        def _():
            _gather_start(hs_ref, xbuf_ref.at[1 - slot], sem.at[1 - slot], src_ref, nxt * TILE_GRANS, TILE_GRANS)

    @pl.when(valid)
    def _():
        @pl.when(j == 0)
        def _():
            acc_ref[...] = jnp.zeros_like(acc_ref)

        x = xbuf_ref[slot]
        a = _silu(_dot(x, wg_ref[...])) * _dot(x, wu_ref[...])
        acc_ref[...] += _dot(a.astype(bf16), wd_ref[...])

        @pl.when(j == last)
        def _():
            o_ref[...] = acc_ref[...].astype(o_ref.dtype)

    @pl.when(jnp.logical_not(valid) & (j == last))
    def _():
        o_ref[...] = jnp.zeros_like(o_ref)


def _experts(tile_expert, tile_valid, src_of_dst, hs2d, w_gate, w_up, w_down):
    n_tiles = tile_expert.shape[0]
    dff = w_gate.shape[2]
    nff = dff // MOE_TF

    def jj(j, tv, i):
        return jnp.where(tv[i] > 0, j, nff - 1)

    grid_spec = pltpu.PrefetchScalarGridSpec(
        num_scalar_prefetch=3,
        grid=(n_tiles, nff),
        in_specs=[
            pl.BlockSpec(memory_space=pl.ANY),
            pl.BlockSpec((None, D, MOE_TF), lambda i, j, te, tv, sd: (te[i], 0, jj(j, tv, i))),
            pl.BlockSpec((None, D, MOE_TF), lambda i, j, te, tv, sd: (te[i], 0, jj(j, tv, i))),
            pl.BlockSpec((None, MOE_TF, D), lambda i, j, te, tv, sd: (te[i], jj(j, tv, i), 0)),
        ],
        out_specs=pl.BlockSpec((MOE_TM, D), lambda i, j, te, tv, sd: (i, 0)),
        scratch_shapes=[
            pltpu.VMEM((2, MOE_TM, D), bf16),
            pltpu.SemaphoreType.DMA((2,)),
            pltpu.VMEM((MOE_TM, D), f32),
        ],
    )
    return pl.pallas_call(
        _expert_kernel,
        grid_spec=grid_spec,
        out_shape=jax.ShapeDtypeStruct((n_tiles * MOE_TM, D), bf16),
        compiler_params=pltpu.CompilerParams(vmem_limit_bytes=VMEM_LIMIT),
        name="experts",
    )(tile_expert, tile_valid, src_of_dst, hs2d, w_gate, w_up, w_down)


def _combine_kernel(ds_ref, x_ref, mod_ref, o_hbm_ref, rt_ref, lng_ref, lnb_ref, out_ref, obuf_ref, sem):
    W = ROUTE_W
    c = pl.program_id(0)
    nc = pl.num_programs(0)
    slot = c % 2
    nxt = jnp.minimum(c + 1, nc - 1)

    @pl.when(c == 0)
    def _():
        _gather_start(o_hbm_ref, obuf_ref.at[0], sem.at[0], ds_ref, 0, CHUNK_GRANS)

    _gather_wait(o_hbm_ref, obuf_ref.at[slot], sem.at[slot], ds_ref, c * CHUNK_GRANS, CHUNK_GRANS)

    @pl.when(c + 1 < nc)
    def _():
        _gather_start(o_hbm_ref, obuf_ref.at[1 - slot], sem.at[1 - slot], ds_ref, nxt * CHUNK_GRANS, CHUNK_GRANS)

    rt = rt_ref[...]
    scol = lax.broadcasted_iota(jnp.int32, (W, CHUNK_SLOTS), 1)
    p1 = (scol == rt[:, 0:1].astype(jnp.int32)).astype(f32).astype(bf16)
    p2 = (scol == rt[:, 1:2].astype(jnp.int32)).astype(f32).astype(bf16)
    osv = obuf_ref[slot]
    y = rt[:, 2:3] * _dot(p1, osv) + rt[:, 3:4] * _dot(p2, osv)
    xr = ALPHA * x_ref[...] + (1.0 + mod_ref[5:6, :]) * y
    out_ref[...] = _layer_norm(xr, lng_ref[...], lnb_ref[...])


def _combine(dst_of_src, x2d, mod_l, o2d, route_t, ln_g, ln_b, S):
    T = x2d.shape[0]
    W = ROUTE_W
    tpb = S // W
    grid_spec = pltpu.PrefetchScalarGridSpec(
        num_scalar_prefetch=1,
        grid=(T // W,),
        in_specs=[
            pl.BlockSpec((W, D), lambda c, ds: (c, 0)),
            pl.BlockSpec((None, 6, D), lambda c, ds: (c // tpb, 0, 0)),
            pl.BlockSpec(memory_space=pl.ANY),
            pl.BlockSpec((None, W, 8), lambda c, ds: (c, 0, 0)),
            pl.BlockSpec((1, D), lambda c, ds: (0, 0)),
            pl.BlockSpec((1, D), lambda c, ds: (0, 0)),
        ],
        out_specs=pl.BlockSpec((W, D), lambda c, ds: (c, 0)),
        scratch_shapes=[
            pltpu.VMEM((2, CHUNK_SLOTS, D), bf16),
            pltpu.SemaphoreType.DMA((2,)),
        ],
    )
    return pl.pallas_call(
        _combine_kernel,
        grid_spec=grid_spec,
        out_shape=jax.ShapeDtypeStruct((T, D), f32),
        compiler_params=pltpu.CompilerParams(vmem_limit_bytes=VMEM_LIMIT),
        name="combine",
    )(dst_of_src, x2d, mod_l, o2d, route_t, ln_g, ln_b)


def _routing_tables(counts, n_tiles):
    nc = counts.shape[0]
    gr = (counts + GRAN - 1) // GRAN
    seg_start = jnp.cumsum(gr, axis=1) - gr
    chunk_total = jnp.sum(gr, axis=1)
    prefix = jnp.cumsum(gr, axis=0) - gr
    g_e = jnp.sum(gr, axis=0)
    tiles_e = (g_e + TILE_GRANS - 1) // TILE_GRANS
    tile_end = jnp.cumsum(tiles_e)
    tile_start = tile_end - tiles_e
    total_tiles = tile_end[-1]

    i32 = jnp.int32
    er = jnp.arange(N_EXPERTS, dtype=i32)
    t = jnp.arange(n_tiles, dtype=i32)
    te = jnp.sum((t[:, None] >= tile_end[None, :]).astype(i32), axis=1)
    tile_valid = (t < total_tiles).astype(i32)
    last_e = jnp.sum((total_tiles - 1 >= tile_end).astype(i32))
    tile_expert = jnp.where(tile_valid > 0, jnp.minimum(te, N_EXPERTS - 1), last_e).astype(i32)

    k = jnp.arange(CHUNK_GRANS, dtype=i32)
    seg_end = seg_start + gr
    e_of = jnp.sum((k[None, :, None] >= seg_end[:, None, :]).astype(i32), axis=2)
    oh_e = (jnp.minimum(e_of, N_EXPERTS - 1)[:, :, None] == er).astype(i32)
    base = tile_start[None, :] * TILE_GRANS + prefix - seg_start
    dst = jnp.sum(oh_e * base[:, None, :], axis=2) + k[None, :]
    valid_src = k[None, :] < chunk_total[:, None]
    dst_of_src = jnp.where(valid_src, dst, 0).astype(i32).reshape(-1)

    d = jnp.arange(n_tiles * TILE_GRANS, dtype=i32)
    oh_d = (jnp.repeat(tile_expert, TILE_GRANS)[:, None] == er).astype(i32)
    q = d - jnp.sum(oh_d * tile_start[None, :], axis=1) * TILE_GRANS
    incl_d = jnp.sum(oh_d[:, :, None] * (prefix + gr).T[None], axis=1)
    c_d = jnp.sum((q[:, None] >= incl_d).astype(i32), axis=1)
    oh_c = (jnp.minimum(c_d, nc - 1)[:, None] == jnp.arange(nc, dtype=i32)).astype(i32)
    cbase = jnp.arange(nc, dtype=i32)[:, None] * CHUNK_GRANS + seg_start - prefix
    sel = jnp.sum(oh_c[:, :, None] * oh_d[:, None, :] * cbase[None], axis=(1, 2))
    valid_dst = (jnp.repeat(tile_valid, TILE_GRANS) > 0) & (q >= 0) & (q < jnp.sum(oh_d * g_e[None, :], axis=1))
    src_of_dst = jnp.where(valid_dst, sel + q, 0).astype(i32)
    return tile_expert, tile_valid, src_of_dst, dst_of_src


def kernel(x, c, ada_w, ada_b, ln_g, ln_b, ev_w_in, ev_pool_w, ev_pool_scale, ev_sink, ev_w_out, od_w_in, od_sg_ln_g, od_sg_ln_b, od_w_s, od_b_s, od_w_out, ffn_w_gate, ffn_w_up, ffn_w_down, moe_w_router, moe_w_gate, moe_w_up, moe_w_down):
    B, S, _ = x.shape
    T = B * S
    x2d = x.reshape(T, D)
    mod = _adaln(c, ada_w, ada_b)

    p, q, k, v = _ev_in(x2d, mod[0], ev_w_in[0].astype(bf16), S)
    x2d = _ev_mix(x2d, mod[0], p, q, k, v, ev_pool_w[0].astype(bf16), ev_pool_scale[0][None, :],
                  ev_sink[0], ev_w_out[0].astype(bf16), ln_g[0, 0][None, :], ln_b[0, 0][None, :], B, S)
    x2d = _ffn(x2d, mod[0], ffn_w_gate[0].astype(bf16), ffn_w_up[0].astype(bf16),
               ffn_w_down[0].astype(bf16), ln_g[0, 1][None, :], ln_b[0, 1][None, :], S)

    x2d = _sg_mix(x2d, mod[1], od_w_in[0].astype(bf16), od_sg_ln_g[0][None, :], od_sg_ln_b[0][None, :],
                  od_w_s[0].astype(bf16), od_b_s[0].T, od_w_out[0].astype(bf16),
                  ln_g[1, 0][None, :], ln_b[1, 0][None, :], S)

    hs, route, cnt = _route(x2d, mod[1], moe_w_router[0].T, S)
    nc = T // ROUTE_W
    n_tiles = (nc * CHUNK_GRANS) // TILE_GRANS + N_EXPERTS
    tile_expert, tile_valid, src_of_dst, dst_of_src = _routing_tables(cnt[:, :, 0], n_tiles)
    o = _experts(tile_expert, tile_valid, src_of_dst, hs.reshape(nc * CHUNK_SLOTS, D),
                 moe_w_gate[0].astype(bf16), moe_w_up[0].astype(bf16), moe_w_down[0].astype(bf16))
    x2d = _combine(dst_of_src, x2d, mod[1], o, jnp.swapaxes(route, 1, 2),
                   ln_g[1, 1][None, :], ln_b[1, 1][None, :], S)
    return x2d.reshape(B, S, D)
```

```python
import functools
import math

import jax
import jax.numpy as jnp
from jax import lax
from jax.experimental import pallas as pl
from jax.experimental.pallas import tpu as pltpu

D = 1024
DEPTH = 2
ALPHA = (2.0 * DEPTH) ** 0.25
LN_EPS = 1e-5

POOL_WINDOWS = (2, 4, 8, 16)
POOL_CH = 128
POOL_WIDTH = 512
HEAD_DIM = 64
N_Q_HEADS = 8
N_KV_HEADS = 2
Q_GROUP = 4
ATTN_WIDTH = 512
KV_WIDTH = 128
BLOCK = 128
EVEN_IN = 1280
POOL_HALO = 8

CHUNK = 128
SG_GROUPS = 8
SG_CH = 128

N_EXPERTS = 8

ROUTE_W = 512
GRAN = 16
CHUNK_SLOTS = 2 * ROUTE_W + N_EXPERTS * GRAN
CHUNK_GRANS = CHUNK_SLOTS // GRAN
MOE_TM = 512
TILE_GRANS = MOE_TM // GRAN
MOE_TF = 512

VMEM_LIMIT = 48 * 1024 * 1024

bf16 = jnp.bfloat16
f32 = jnp.float32


def _dot(a, b):
    return jnp.dot(a, b, preferred_element_type=f32)


def _split_bf16(a):
    hi = a.astype(bf16)
    lo = (a - hi.astype(f32)).astype(bf16)
    return hi, lo


def _layer_norm(x, g, b):
    mu = jnp.mean(x, axis=-1, keepdims=True)
    xc = x - mu
    var = jnp.mean(xc * xc, axis=-1, keepdims=True)
    return xc * lax.rsqrt(var + LN_EPS) * g + b


def _silu(x):
    return x * jax.nn.sigmoid(x)


def _gelu_tanh(x):
    c = math.sqrt(2.0 / math.pi)
    return x * (0.5 * (1.0 + jnp.tanh(c * (x + 0.044715 * (x * x * x)))))


def _adaln_kernel(c_ref, w_ref, b_ref, o_ref):
    cond = _silu(c_ref[...])
    c_hi, c_lo = _split_bf16(cond)
    w_hi, w_lo = _split_bf16(w_ref[...])
    acc = _dot(c_hi, w_hi) + (_dot(c_lo, w_hi) + _dot(c_hi, w_lo))
    o_ref[...] = acc + b_ref[...]


def _adaln(c, ada_w, ada_b):
    B = c.shape[0]
    tn = 1024
    c_pad = jnp.zeros((8, D), f32).at[:B].set(c)
    out = pl.pallas_call(
        _adaln_kernel,
        grid=(DEPTH, 6 * D // tn),
        in_specs=[
            pl.BlockSpec((8, D), lambda l, j: (0, 0)),
            pl.BlockSpec((None, D, tn), lambda l, j: (l, 0, j)),
            pl.BlockSpec((None, 1, tn), lambda l, j: (l, 0, j)),
        ],
        out_specs=pl.BlockSpec((None, 8, tn), lambda l, j: (l, 0, j)),
        out_shape=jax.ShapeDtypeStruct((DEPTH, 8, 6 * D), f32),
        compiler_params=pltpu.CompilerParams(vmem_limit_bytes=VMEM_LIMIT),
        name="adaln",
    )(c_pad, ada_w, ada_b.reshape(DEPTH, 1, 6 * D))
    return out[:, :B].reshape(DEPTH, B, 6, D)


def _ev_in_kernel(x_ref, mod_ref, wpk_ref, wqvt_ref, p_ref, k_ref, qt_ref, vt_ref):
    h = (x_ref[...] * (1.0 + mod_ref[1:2, :]) + mod_ref[0:1, :]).astype(bf16)
    zpk = _dot(h, wpk_ref[...])
    p_ref[...] = zpk[:, :POOL_WIDTH]
    k_ref[...] = zpk[:, POOL_WIDTH:].astype(bf16)
    zt = lax.dot_general(wqvt_ref[...], h, (((1,), (1,)), ((), ())), preferred_element_type=f32)
    qt_ref[...] = (zt[:ATTN_WIDTH] * (HEAD_DIM ** -0.5)).astype(bf16)
    vt_ref[...] = zt[ATTN_WIDTH:].astype(bf16)


def _ev_in(x2d, mod_l, w_pk, w_qv_t, S):
    T = x2d.shape[0]
    tm = 512
    tpb = S // tm
    return pl.pallas_call(
        _ev_in_kernel,
        grid=(T // tm,),
        in_specs=[
            pl.BlockSpec((tm, D), lambda i: (i, 0)),
            pl.BlockSpec((None, 6, D), lambda i: (i // tpb, 0, 0)),
            pl.BlockSpec((D, POOL_WIDTH + KV_WIDTH), lambda i: (0, 0)),
            pl.BlockSpec((ATTN_WIDTH + KV_WIDTH, D), lambda i: (0, 0)),
        ],
        out_specs=[
            pl.BlockSpec((tm, POOL_WIDTH), lambda i: (i, 0)),
            pl.BlockSpec((tm, KV_WIDTH), lambda i: (i, 0)),
            pl.BlockSpec((ATTN_WIDTH, tm), lambda i: (0, i)),
            pl.BlockSpec((KV_WIDTH, tm), lambda i: (0, i)),
        ],
        out_shape=[
            jax.ShapeDtypeStruct((T, POOL_WIDTH), f32),
            jax.ShapeDtypeStruct((T, KV_WIDTH), bf16),
            jax.ShapeDtypeStruct((ATTN_WIDTH, T), bf16),
            jax.ShapeDtypeStruct((KV_WIDTH, T), bf16),
        ],
        compiler_params=pltpu.CompilerParams(vmem_limit_bytes=VMEM_LIMIT),
        name="ev_in",
    )(x2d, mod_l, w_pk, w_qv_t)


def _ev_mix_kernel(x_ref, mod_ref, p_ref, pp_ref, pn_ref, qt_ref,
                   k_ref, kp_ref, kn_ref, vt_ref, vtp_ref, vtn_ref,
                   bias_ref, sink_ref, wpool_ref, pscale_ref, wout_ref, lng_ref, lnb_ref,
                   o_ref, pext_ref, kext_ref, vext_ref, ybt_ref, mix_ref, *, S, tq):
    i = pl.program_id(1)
    n_tiles = S // tq
    is_first = i == 0
    is_last = i == n_tiles - 1

    p = p_ref[...]
    pext_ref[0:POOL_HALO, :] = jnp.where(is_first, 0.0, pp_ref[...])
    pext_ref[POOL_HALO:POOL_HALO + tq, :] = p
    pext_ref[POOL_HALO + tq:, :] = jnp.where(is_last, 0.0, pn_ref[...])
    pos = i * tq + lax.broadcasted_iota(jnp.int32, (tq, 1), 0)
    for g, w in enumerate(POOL_WINDOWS):
        r = w // 2
        cs = slice(g * POOL_CH, (g + 1) * POOL_CH)
        acc = pext_ref[POOL_HALO - r:POOL_HALO - r + tq, cs]
        for d in range(-r + 1, r + 1):
            acc = acc + pext_ref[POOL_HALO + d:POOL_HALO + d + tq, cs]
        cnt = (jnp.minimum(pos + r + 1, S) - jnp.maximum(pos - r, 0)).astype(f32)
        pooled = acc / cnt - p[:, cs]
        ya = _dot(pooled.astype(bf16), wpool_ref[g])
        mix_ref[:, cs] = (ya * pscale_ref[:, cs]).astype(bf16)

    kext_ref[0:BLOCK, :] = kp_ref[...]
    kext_ref[BLOCK:BLOCK + tq, :] = k_ref[...]
    kext_ref[BLOCK + tq:, :] = kn_ref[...]
    vext_ref[:, 0:BLOCK] = vtp_ref[...]
    vext_ref[:, BLOCK:BLOCK + tq] = vt_ref[...]
    vext_ref[:, BLOCK + tq:] = vtn_ref[...]

    n_blocks = S // BLOCK
    zeros_q = jnp.zeros((HEAD_DIM, Q_GROUP * BLOCK), bf16)
    for n in range(tq // BLOCK):
        gb = i * (tq // BLOCK) + n
        variant = jnp.where(gb == 0, 1, jnp.where(gb == n_blocks - 1, 2, 0))
        cols = slice(n * BLOCK, (n + 1) * BLOCK)
        kw = kext_ref[n * BLOCK:n * BLOCK + 3 * BLOCK, :]
        for kvh in range(N_KV_HEADS):
            qst = jnp.concatenate(
                [qt_ref[(kvh * Q_GROUP + gq) * HEAD_DIM:(kvh * Q_GROUP + gq + 1) * HEAD_DIM, cols]
                 for gq in range(Q_GROUP)], axis=1)
            qst = jnp.concatenate([qst, zeros_q] if kvh == 0 else [zeros_q, qst], axis=0)
            s = _dot(kw, qst) + bias_ref[variant, kvh]
            sink = sink_ref[kvh]
            m = jnp.maximum(jnp.max(s, axis=0, keepdims=True), sink)
            e = jnp.exp(s - m)
            denom = jnp.sum(e, axis=0, keepdims=True) + jnp.exp(sink - m)
            vwt = vext_ref[kvh * HEAD_DIM:(kvh + 1) * HEAD_DIM, n * BLOCK:n * BLOCK + 3 * BLOCK]
            out = _dot(vwt, e.astype(bf16)) / denom
            for gq in range(Q_GROUP):
                hq = kvh * Q_GROUP + gq
                ybt_ref[hq * HEAD_DIM:(hq + 1) * HEAD_DIM, cols] = out[:, gq * BLOCK:(gq + 1) * BLOCK]
    mix_ref[:, POOL_WIDTH:] = ybt_ref[...].T.astype(bf16)

    y = _dot(mix_ref[...], wout_ref[...])
    xr = ALPHA * x_ref[...] + (1.0 + mod_ref[2:3, :]) * y
    o_ref[...] = _layer_norm(xr, lng_ref[...], lnb_ref[...])


def _ev_mix(x2d, mod_l, p, qt, k, vt, w_pool, pool_scale, sink, w_out, ln_g, ln_b, B, S):
    T = x2d.shape[0]
    tq = 512
    nt = S // tq
    kb = tq // BLOCK
    pb = tq // POOL_HALO
    n_kblocks = T // BLOCK
    n_pblocks = T // POOL_HALO

    def main(b, i): return (b * nt + i, 0)
    def kprev(b, i): return (jnp.maximum((b * nt + i) * kb - 1, 0), 0)
    def knext(b, i): return (jnp.minimum((b * nt + i + 1) * kb, n_kblocks - 1), 0)
    def pprev(b, i): return (jnp.maximum((b * nt + i) * pb - 1, 0), 0)
    def pnext(b, i): return (jnp.minimum((b * nt + i + 1) * pb, n_pblocks - 1), 0)
    def const2(b, i): return (0, 0)

    def tmain(b, i): return (0, b * nt + i)
    def tprev(b, i): return (0, jnp.maximum((b * nt + i) * kb - 1, 0))
    def tnext(b, i): return (0, jnp.minimum((b * nt + i + 1) * kb, n_kblocks - 1))

    assert S // BLOCK >= 2
    kj = jnp.arange(3 * BLOCK)[:, None]
    qi = jnp.arange(BLOCK)[None, :]
    dist = jnp.abs(kj - BLOCK - qi)
    slopes = 2.0 ** (-8.0 * jnp.arange(1, N_Q_HEADS + 1, dtype=f32) / N_Q_HEADS)
    alibi = -slopes[:, None, None] * dist.astype(f32)[None]
    in_window = dist <= BLOCK
    key_ok = jnp.stack([kj >= 0, kj >= BLOCK, kj < 2 * BLOCK])
    bias = jnp.where((in_window[None] & key_ok)[:, None], alibi[None], -1e30)
    bias = bias.reshape(3, N_KV_HEADS, Q_GROUP, 3 * BLOCK, BLOCK).transpose(0, 1, 3, 2, 4)
    bias = bias.reshape(3, N_KV_HEADS, 3 * BLOCK, Q_GROUP * BLOCK)
    sink_row = jnp.repeat(sink.astype(f32).reshape(N_KV_HEADS, Q_GROUP), BLOCK, axis=1)[:, None, :]

    kernel = functools.partial(_ev_mix_kernel, S=S, tq=tq)
    return pl.pallas_call(
        kernel,
        grid=(B, nt),
        in_specs=[
            pl.BlockSpec((tq, D), main),
            pl.BlockSpec((None, 6, D), lambda b, i: (b, 0, 0)),
            pl.BlockSpec((tq, POOL_WIDTH), main),
            pl.BlockSpec((POOL_HALO, POOL_WIDTH), pprev),
            pl.BlockSpec((POOL_HALO, POOL_WIDTH), pnext),
            pl.BlockSpec((ATTN_WIDTH, tq), tmain),
            pl.BlockSpec((tq, KV_WIDTH), main),
            pl.BlockSpec((BLOCK, KV_WIDTH), kprev),
            pl.BlockSpec((BLOCK, KV_WIDTH), knext),
            pl.BlockSpec((KV_WIDTH, tq), tmain),
            pl.BlockSpec((KV_WIDTH, BLOCK), tprev),
            pl.BlockSpec((KV_WIDTH, BLOCK), tnext),
            pl.BlockSpec((3, N_KV_HEADS, 3 * BLOCK, Q_GROUP * BLOCK), lambda b, i: (0, 0, 0, 0)),
            pl.BlockSpec((N_KV_HEADS, 1, Q_GROUP * BLOCK), lambda b, i: (0, 0, 0)),
            pl.BlockSpec((len(POOL_WINDOWS), POOL_CH, POOL_CH), lambda b, i: (0, 0, 0)),
            pl.BlockSpec((1, POOL_WIDTH), const2),
            pl.BlockSpec((D, D), const2),
            pl.BlockSpec((1, D), const2),
            pl.BlockSpec((1, D), const2),
        ],
        out_specs=pl.BlockSpec((tq, D), main),
        out_shape=jax.ShapeDtypeStruct((T, D), f32),
        scratch_shapes=[
            pltpu.VMEM((tq + 2 * POOL_HALO, POOL_WIDTH), f32),
            pltpu.VMEM((tq + 2 * BLOCK, KV_WIDTH), bf16),
            pltpu.VMEM((KV_WIDTH, tq + 2 * BLOCK), bf16),
            pltpu.VMEM((ATTN_WIDTH, tq), f32),
            pltpu.VMEM((tq, D), bf16),
        ],
        compiler_params=pltpu.CompilerParams(vmem_limit_bytes=VMEM_LIMIT),
        name="ev_mix",
    )(x2d, mod_l, p, p, p, qt, k, k, k, vt, vt, vt, bias, sink_row, w_pool, pool_scale, w_out, ln_g, ln_b)


def _ffn_kernel(x_ref, mod_ref, wg_ref, wu_ref, wd_ref, lng_ref, lnb_ref, o_ref, h_ref, acc_ref):
    j = pl.program_id(1)

    @pl.when(j == 0)
    def _():
        h = x_ref[...] * (1.0 + mod_ref[4:5, :]) + mod_ref[3:4, :]
        h_ref[...] = h.astype(bf16)
        acc_ref[...] = jnp.zeros_like(acc_ref)

    h = h_ref[...]
    a = _silu(_dot(h, wg_ref[...])) * _dot(h, wu_ref[...])
    acc_ref[...] += _dot(a.astype(bf16), wd_ref[...])

    @pl.when(j == pl.num_programs(1) - 1)
    def _():
        xr = ALPHA * x_ref[...] + (1.0 + mod_ref[5:6, :]) * acc_ref[...]
        o_ref[...] = _layer_norm(xr, lng_ref[...], lnb_ref[...])


def _ffn(x2d, mod_l, w_gate, w_up, w_down, ln_g, ln_b, S):
    T = x2d.shape[0]
    tm = 1024
    tf = 256
    dff = w_gate.shape[1]
    tpb = S // tm
    return pl.pallas_call(
        _ffn_kernel,
        grid=(T // tm, dff // tf),
        in_specs=[
            pl.BlockSpec((tm, D), lambda i, j: (i, 0)),
            pl.BlockSpec((None, 6, D), lambda i, j: (i // tpb, 0, 0)),
            pl.BlockSpec((D, tf), lambda i, j: (0, j)),
            pl.BlockSpec((D, tf), lambda i, j: (0, j)),
            pl.BlockSpec((tf, D), lambda i, j: (j, 0)),
            pl.BlockSpec((1, D), lambda i, j: (0, 0)),
            pl.BlockSpec((1, D), lambda i, j: (0, 0)),
        ],
        out_specs=pl.BlockSpec((tm, D), lambda i, j: (i, 0)),
        out_shape=jax.ShapeDtypeStruct((T, D), f32),
        scratch_shapes=[pltpu.VMEM((tm, D), bf16), pltpu.VMEM((tm, D), f32)],
        compiler_params=pltpu.CompilerParams(vmem_limit_bytes=VMEM_LIMIT),
        name="ffn",
    )(x2d, mod_l, w_gate, w_up, w_down, ln_g, ln_b)


def _sg_kernel(x_ref, mod_ref, win_ref, sgg_ref, sgb_ref, ws_ref, bst_ref, wout_ref, lng_ref, lnb_ref,
               o_ref, gate_ref, *, tm):
    x = x_ref[...]
    h = x * (1.0 + mod_ref[1:2, :]) + mod_ref[0:1, :]
    z = _gelu_tanh(_dot(h.astype(bf16), win_ref[...]))
    u = z[:, :D]
    v = _layer_norm(z[:, D:], sgg_ref[...], sgb_ref[...]).astype(bf16)
    for n in range(tm // CHUNK):
        rows = slice(n * CHUNK, (n + 1) * CHUNK)
        for g in range(SG_GROUPS):
            cols = slice(g * SG_CH, (g + 1) * SG_CH)
            sv = _dot(ws_ref[g], v[rows, cols]) + bst_ref[:, g:g + 1]
            gate_ref[rows, cols] = (u[rows, cols] * sv).astype(bf16)
    y = _dot(gate_ref[...], wout_ref[...])
    xr = ALPHA * x + (1.0 + mod_ref[2:3, :]) * y
    o_ref[...] = _layer_norm(xr, lng_ref[...], lnb_ref[...])


def _sg_mix(x2d, mod_l, w_in, sg_g, sg_b, w_s, b_s_t, w_out, ln_g, ln_b, S):
    T = x2d.shape[0]
    tm = 512
    tpb = S // tm
    c2 = lambda i: (0, 0)
    return pl.pallas_call(
        functools.partial(_sg_kernel, tm=tm),
        grid=(T // tm,),
        in_specs=[
            pl.BlockSpec((tm, D), lambda i: (i, 0)),
            pl.BlockSpec((None, 6, D), lambda i: (i // tpb, 0, 0)),
            pl.BlockSpec((D, 2 * D), c2),
            pl.BlockSpec((1, D), c2),
            pl.BlockSpec((1, D), c2),
            pl.BlockSpec((SG_GROUPS, CHUNK, CHUNK), lambda i: (0, 0, 0)),
            pl.BlockSpec((CHUNK, SG_GROUPS), c2),
            pl.BlockSpec((D, D), c2),
            pl.BlockSpec((1, D), c2),
            pl.BlockSpec((1, D), c2),
        ],
        out_specs=pl.BlockSpec((tm, D), lambda i: (i, 0)),
        out_shape=jax.ShapeDtypeStruct((T, D), f32),
        scratch_shapes=[pltpu.VMEM((tm, D), bf16)],
        compiler_params=pltpu.CompilerParams(vmem_limit_bytes=VMEM_LIMIT),
        name="sg_mix",
    )(x2d, mod_l, w_in, sg_g, sg_b, w_s, b_s_t, w_out, ln_g, ln_b)


def _route_kernel(x_ref, mod_ref, wrt_ref, hs_ref, route_ref, cnt_ref):
    W = ROUTE_W
    h = x_ref[...] * (1.0 + mod_ref[4:5, :]) + mod_ref[3:4, :]
    h_hi, h_lo = _split_bf16(h)
    w_hi, w_lo = _split_bf16(wrt_ref[...])
    nt = (((1,), (1,)), ((), ()))
    logits = (lax.dot_general(w_hi, h_hi, nt, preferred_element_type=f32)
              + (lax.dot_general(w_hi, h_lo, nt, preferred_element_type=f32)
                 + lax.dot_general(w_lo, h_hi, nt, preferred_element_type=f32)))

    eidx = lax.broadcasted_iota(jnp.int32, (N_EXPERTS, W), 0)
    m1 = jnp.max(logits, axis=0, keepdims=True)
    i1 = jnp.min(jnp.where(logits == m1, eidx, N_EXPERTS), axis=0, keepdims=True)
    sel1 = eidx == i1
    rest = jnp.where(sel1, -jnp.inf, logits)
    m2 = jnp.max(rest, axis=0, keepdims=True)
    i2 = jnp.min(jnp.where(rest == m2, eidx, N_EXPERTS), axis=0, keepdims=True)
    sel2 = eidx == i2
    e2 = jnp.exp(m2 - m1)
    g1 = 1.0 / (1.0 + e2)
    g2 = e2 / (1.0 + e2)

    a1 = sel1.astype(f32)
    a2 = sel2.astype(f32)
    assign = a1 + a2
    counts = jnp.sum(assign, axis=1, keepdims=True)
    grans = jnp.ceil(counts * (1.0 / GRAN))
    sub = lax.broadcasted_iota(jnp.int32, (N_EXPERTS, 1), 0)
    seg = jnp.zeros((N_EXPERTS, 1), f32)
    for e in range(N_EXPERTS - 1):
        seg = seg + jnp.where(sub > e, grans[e:e + 1, :] * GRAN, 0.0)
    tr = lax.broadcasted_iota(jnp.int32, (W, W), 0)
    tc = lax.broadcasted_iota(jnp.int32, (W, W), 1)
    upper = (tr < tc).astype(bf16)
    rank = _dot(assign.astype(bf16), upper)
    slot = seg + rank
    pos1 = jnp.sum(a1 * slot, axis=0, keepdims=True)
    pos2 = jnp.sum(a2 * slot, axis=0, keepdims=True)

    srow = lax.broadcasted_iota(jnp.int32, (CHUNK_SLOTS, W), 0)
    perm = ((srow == pos1.astype(jnp.int32)) | (srow == pos2.astype(jnp.int32)))
    hs_ref[...] = _dot(perm.astype(f32).astype(bf16), h_hi).astype(bf16)

    ridx = lax.broadcasted_iota(jnp.int32, (8, W), 0)
    route = jnp.where(ridx == 0, pos1, jnp.where(ridx == 1, pos2, jnp.where(ridx == 2, g1, jnp.where(ridx == 3, g2, 0.0))))
    route_ref[...] = route
    cnt_ref[...] = jnp.broadcast_to(counts, (N_EXPERTS, 128)).astype(jnp.int32)


def _route(x2d, mod_l, w_router_t, S):
    T = x2d.shape[0]
    W = ROUTE_W
    nc = T // W
    tpb = S // W
    return pl.pallas_call(
        _route_kernel,
        grid=(nc,),
        in_specs=[
            pl.BlockSpec((W, D), lambda c: (c, 0)),
            pl.BlockSpec((None, 6, D), lambda c: (c // tpb, 0, 0)),
            pl.BlockSpec((N_EXPERTS, D), lambda c: (0, 0)),
        ],
        out_specs=[
            pl.BlockSpec((None, CHUNK_SLOTS, D), lambda c: (c, 0, 0)),
            pl.BlockSpec((None, 8, W), lambda c: (c, 0, 0)),
            pl.BlockSpec((None, N_EXPERTS, 128), lambda c: (c, 0, 0)),
        ],
        out_shape=[
            jax.ShapeDtypeStruct((nc, CHUNK_SLOTS, D), bf16),
            jax.ShapeDtypeStruct((nc, 8, W), f32),
            jax.ShapeDtypeStruct((nc, N_EXPERTS, 128), jnp.int32),
        ],
        compiler_params=pltpu.CompilerParams(vmem_limit_bytes=VMEM_LIMIT),
        name="route",
    )(x2d, mod_l, w_router_t)


def _granule_copy(src_ref, buf_ref, sem, idx_ref, base, g):
    row = pl.multiple_of(idx_ref[base + g] * GRAN, GRAN)
    return pltpu.make_async_copy(src_ref.at[pl.ds(row, GRAN), :], buf_ref.at[pl.ds(g * GRAN, GRAN), :], sem)


def _gather_start(src_ref, buf_ref, sem, idx_ref, base, n):
    for g in range(n):
        _granule_copy(src_ref, buf_ref, sem, idx_ref, base, g).start()


def _gather_wait(src_ref, buf_ref, sem, idx_ref, base, n):
    for g in range(n):
        _granule_copy(src_ref, buf_ref, sem, idx_ref, base, g).wait()


def _expert_kernel(te_ref, tv_ref, src_ref, hs_ref, wg_ref, wu_ref, wd_ref, o_ref, xbuf_ref, sem, acc_ref):
    i = pl.program_id(0)
    j = pl.program_id(1)
    n_tiles = pl.num_programs(0)
    last = pl.num_programs(1) - 1
    valid = tv_ref[i] > 0
    slot = i % 2
    nxt = jnp.minimum(i + 1, n_tiles - 1)

    @pl.when(j == 0)
    def _():
        @pl.when(i == 0)
        def _():
            _gather_start(hs_ref, xbuf_ref.at[0], sem.at[0], src_ref, 0, TILE_GRANS)

        @pl.when(valid)
        def _():
            _gather_wait(hs_ref, xbuf_ref.at[slot], sem.at[slot], src_ref, i * TILE_GRANS, TILE_GRANS)

        @pl.when((i + 1 < n_tiles) & (tv_ref[nxt] > 0))
        def _():
            _gather_start(hs_ref, xbuf_ref.at[1 - slot], sem.at[1 - slot], src_ref, nxt * TILE_GRANS, TILE_GRANS)

    @pl.when(valid)
    def _():
        @pl.when(j == 0)
        def _():
            acc_ref[...] = jnp.zeros_like(acc_ref)

        x = xbuf_ref[slot]
        a = _silu(_dot(x, wg_ref[...])) * _dot(x, wu_ref[...])
        acc_ref[...] += _dot(a.astype(bf16), wd_ref[...])

        @pl.when(j == last)
        def _():
            o_ref[...] = acc_ref[...].astype(o_ref.dtype)

    @pl.when(jnp.logical_not(valid) & (j == last))
    def _():
        o_ref[...] = jnp.zeros_like(o_ref)


def _experts(tile_expert, tile_valid, src_of_dst, hs2d, w_gate, w_up, w_down):
    n_tiles = tile_expert.shape[0]
    dff = w_gate.shape[2]
    nff = dff // MOE_TF

    def jj(j, tv, i):
        return jnp.where(tv[i] > 0, j, nff - 1)

    grid_spec = pltpu.PrefetchScalarGridSpec(
        num_scalar_prefetch=3,
        grid=(n_tiles, nff),
        in_specs=[
            pl.BlockSpec(memory_space=pl.ANY),
            pl.BlockSpec((None, D, MOE_TF), lambda i, j, te, tv, sd: (te[i], 0, jj(j, tv, i))),
            pl.BlockSpec((None, D, MOE_TF), lambda i, j, te, tv, sd: (te[i], 0, jj(j, tv, i))),
            pl.BlockSpec((None, MOE_TF, D), lambda i, j, te, tv, sd: (te[i], jj(j, tv, i), 0)),
        ],
        out_specs=pl.BlockSpec((MOE_TM, D), lambda i, j, te, tv, sd: (i, 0)),
        scratch_shapes=[
            pltpu.VMEM((2, MOE_TM, D), bf16),
            pltpu.SemaphoreType.DMA((2,)),
            pltpu.VMEM((MOE_TM, D), f32),
        ],
    )
    return pl.pallas_call(
        _expert_kernel,
        grid_spec=grid_spec,
        out_shape=jax.ShapeDtypeStruct((n_tiles * MOE_TM, D), bf16),
        compiler_params=pltpu.CompilerParams(vmem_limit_bytes=VMEM_LIMIT),
        name="experts",
    )(tile_expert, tile_valid, src_of_dst, hs2d, w_gate, w_up, w_down)


def _combine_kernel(ds_ref, x_ref, mod_ref, o_hbm_ref, rt_ref, lng_ref, lnb_ref, out_ref, obuf_ref, sem):
    W = ROUTE_W
    c = pl.program_id(0)
    nc = pl.num_programs(0)
    slot = c % 2
    nxt = jnp.minimum(c + 1, nc - 1)

    @pl.when(c == 0)
    def _():
        _gather_start(o_hbm_ref, obuf_ref.at[0], sem.at[0], ds_ref, 0, CHUNK_GRANS)

    _gather_wait(o_hbm_ref, obuf_ref.at[slot], sem.at[slot], ds_ref, c * CHUNK_GRANS, CHUNK_GRANS)

    @pl.when(c + 1 < nc)
    def _():
        _gather_start(o_hbm_ref, obuf_ref.at[1 - slot], sem.at[1 - slot], ds_ref, nxt * CHUNK_GRANS, CHUNK_GRANS)

    rt = rt_ref[...]
    scol = lax.broadcasted_iota(jnp.int32, (W, CHUNK_SLOTS), 1)
    p1 = (scol == rt[:, 0:1].astype(jnp.int32)).astype(f32).astype(bf16)
    p2 = (scol == rt[:, 1:2].astype(jnp.int32)).astype(f32).astype(bf16)
    osv = obuf_ref[slot]
    y = rt[:, 2:3] * _dot(p1, osv) + rt[:, 3:4] * _dot(p2, osv)
    xr = ALPHA * x_ref[...] + (1.0 + mod_ref[5:6, :]) * y
    out_ref[...] = _layer_norm(xr, lng_ref[...], lnb_ref[...])


def _combine(dst_of_src, x2d, mod_l, o2d, route_t, ln_g, ln_b, S):
    T = x2d.shape[0]
    W = ROUTE_W
    tpb = S // W
    grid_spec = pltpu.PrefetchScalarGridSpec(
        num_scalar_prefetch=1,
        grid=(T // W,),
        in_specs=[
            pl.BlockSpec((W, D), lambda c, ds: (c, 0)),
            pl.BlockSpec((None, 6, D), lambda c, ds: (c // tpb, 0, 0)),
            pl.BlockSpec(memory_space=pl.ANY),
            pl.BlockSpec((None, W, 8), lambda c, ds: (c, 0, 0)),
            pl.BlockSpec((1, D), lambda c, ds: (0, 0)),
            pl.BlockSpec((1, D), lambda c, ds: (0, 0)),
        ],
        out_specs=pl.BlockSpec((W, D), lambda c, ds: (c, 0)),
        scratch_shapes=[
            pltpu.VMEM((2, CHUNK_SLOTS, D), bf16),
            pltpu.SemaphoreType.DMA((2,)),
        ],
    )
    return pl.pallas_call(
        _combine_kernel,
        grid_spec=grid_spec,
        out_shape=jax.ShapeDtypeStruct((T, D), f32),
        compiler_params=pltpu.CompilerParams(vmem_limit_bytes=VMEM_LIMIT),
        name="combine",
    )(dst_of_src, x2d, mod_l, o2d, route_t, ln_g, ln_b)


def _routing_tables(counts, n_tiles):
    nc = counts.shape[0]
    gr = (counts + GRAN - 1) // GRAN
    seg_start = jnp.cumsum(gr, axis=1) - gr
    chunk_total = jnp.sum(gr, axis=1)
    prefix = jnp.cumsum(gr, axis=0) - gr
    g_e = jnp.sum(gr, axis=0)
    tiles_e = (g_e + TILE_GRANS - 1) // TILE_GRANS
    tile_end = jnp.cumsum(tiles_e)
    tile_start = tile_end - tiles_e
    total_tiles = tile_end[-1]

    i32 = jnp.int32
    er = jnp.arange(N_EXPERTS, dtype=i32)
    t = jnp.arange(n_tiles, dtype=i32)
    te = jnp.sum((t[:, None] >= tile_end[None, :]).astype(i32), axis=1)
    tile_valid = (t < total_tiles).astype(i32)
    last_e = jnp.sum((total_tiles - 1 >= tile_end).astype(i32))
    tile_expert = jnp.where(tile_valid > 0, jnp.minimum(te, N_EXPERTS - 1), last_e).astype(i32)

    k = jnp.arange(CHUNK_GRANS, dtype=i32)
    seg_end = seg_start + gr
    e_of = jnp.sum((k[None, :, None] >= seg_end[:, None, :]).astype(i32), axis=2)
    oh_e = (jnp.minimum(e_of, N_EXPERTS - 1)[:, :, None] == er).astype(i32)
    base = tile_start[None, :] * TILE_GRANS + prefix - seg_start
    dst = jnp.sum(oh_e * base[:, None, :], axis=2) + k[None, :]
    valid_src = k[None, :] < chunk_total[:, None]
    dst_of_src = jnp.where(valid_src, dst, 0).astype(i32).reshape(-1)

    d = jnp.arange(n_tiles * TILE_GRANS, dtype=i32)
    oh_d = (jnp.repeat(tile_expert, TILE_GRANS)[:, None] == er).astype(i32)
    q = d - jnp.sum(oh_d * tile_start[None, :], axis=1) * TILE_GRANS
    incl_d = jnp.sum(oh_d[:, :, None] * (prefix + gr).T[None], axis=1)
    c_d = jnp.sum((q[:, None] >= incl_d).astype(i32), axis=1)
    oh_c = (jnp.minimum(c_d, nc - 1)[:, None] == jnp.arange(nc, dtype=i32)).astype(i32)
    cbase = jnp.arange(nc, dtype=i32)[:, None] * CHUNK_GRANS + seg_start - prefix
    sel = jnp.sum(oh_c[:, :, None] * oh_d[:, None, :] * cbase[None], axis=(1, 2))
    valid_dst = (jnp.repeat(tile_valid, TILE_GRANS) > 0) & (q >= 0) & (q < jnp.sum(oh_d * g_e[None, :], axis=1))
    src_of_dst = jnp.where(valid_dst, sel + q, 0).astype(i32)
    return tile_expert, tile_valid, src_of_dst, dst_of_src


def kernel(x, c, ada_w, ada_b, ln_g, ln_b, ev_w_in, ev_pool_w, ev_pool_scale, ev_sink, ev_w_out, od_w_in, od_sg_ln_g, od_sg_ln_b, od_w_s, od_b_s, od_w_out, ffn_w_gate, ffn_w_up, ffn_w_down, moe_w_router, moe_w_gate, moe_w_up, moe_w_down):
    B, S, _ = x.shape
    T = B * S
    x2d = x.reshape(T, D)
    mod = _adaln(c, ada_w, ada_b)

    w_in = ev_w_in[0].astype(bf16)
    q0, k0, v0 = POOL_WIDTH, POOL_WIDTH + ATTN_WIDTH, POOL_WIDTH + ATTN_WIDTH + KV_WIDTH
    w_pk = jnp.concatenate([w_in[:, :q0], w_in[:, k0:v0]], axis=1)
    w_qv_t = jnp.concatenate([w_in[:, q0:k0], w_in[:, v0:]], axis=1).T
    p, k, qt, vt = _ev_in(x2d, mod[0], w_pk, w_qv_t, S)
    x2d = _ev_mix(x2d, mod[0], p, qt, k, vt, ev_pool_w[0].astype(bf16), ev_pool_scale[0][None, :],
                  ev_sink[0], ev_w_out[0].astype(bf16), ln_g[0, 0][None, :], ln_b[0, 0][None, :], B, S)
    x2d = _ffn(x2d, mod[0], ffn_w_gate[0].astype(bf16), ffn_w_up[0].astype(bf16),
               ffn_w_down[0].astype(bf16), ln_g[0, 1][None, :], ln_b[0, 1][None, :], S)

    x2d = _sg_mix(x2d, mod[1], od_w_in[0].astype(bf16), od_sg_ln_g[0][None, :], od_sg_ln_b[0][None, :],
                  od_w_s[0].astype(bf16), od_b_s[0].T, od_w_out[0].astype(bf16),
                  ln_g[1, 0][None, :], ln_b[1, 0][None, :], S)

    hs, route, cnt = _route(x2d, mod[1], moe_w_router[0].T, S)
    nc = T // ROUTE_W
    n_tiles = (nc * CHUNK_GRANS) // TILE_GRANS + N_EXPERTS
    tile_expert, tile_valid, src_of_dst, dst_of_src = _routing_tables(cnt[:, :, 0], n_tiles)
    o = _experts(tile_expert, tile_valid, src_of_dst, hs.reshape(nc * CHUNK_SLOTS, D),
                 moe_w_gate[0].astype(bf16), moe_w_up[0].astype(bf16), moe_w_down[0].astype(bf16))
    x2d = _combine(dst_of_src, x2d, mod[1], o, jnp.swapaxes(route, 1, 2),
                   ln_g[1, 1][None, :], ln_b[1, 1][None, :], S)
    return x2d.reshape(B, S, D)
```

```python
import functools
import math

import jax
import jax.numpy as jnp
from jax import lax
from jax.experimental import pallas as pl
from jax.experimental.pallas import tpu as pltpu

D = 1024
DEPTH = 2
ALPHA = (2.0 * DEPTH) ** 0.25
LN_EPS = 1e-5

POOL_WINDOWS = (2, 4, 8, 16)
POOL_CH = 128
POOL_WIDTH = 512
HEAD_DIM = 64
N_Q_HEADS = 8
N_KV_HEADS = 2
Q_GROUP = 4
ATTN_WIDTH = 512
KV_WIDTH = 128
BLOCK = 128
EVEN_IN = 1280
POOL_HALO = 8

CHUNK = 128
SG_GROUPS = 8
SG_CH = 128

N_EXPERTS = 8

ROUTE_W = 512
GRAN = 16
CHUNK_SLOTS = 2 * ROUTE_W + N_EXPERTS * GRAN
CHUNK_GRANS = CHUNK_SLOTS // GRAN
MOE_TM = 2048
MOE_SUB = 512
TILE_GRANS = MOE_TM // GRAN
SUB_GRANS = MOE_SUB // GRAN
MOE_TF = 512

VMEM_LIMIT = 48 * 1024 * 1024
VMEM_LIMIT_EXPERTS = 56 * 1024 * 1024

bf16 = jnp.bfloat16
f32 = jnp.float32


def _dot(a, b):
    return jnp.dot(a, b, preferred_element_type=f32)


def _split_bf16(a):
    hi = a.astype(bf16)
    lo = (a - hi.astype(f32)).astype(bf16)
    return hi, lo


def _layer_norm(x, g, b):
    mu = jnp.mean(x, axis=-1, keepdims=True)
    xc = x - mu
    var = jnp.mean(xc * xc, axis=-1, keepdims=True)
    return xc * lax.rsqrt(var + LN_EPS) * g + b


def _silu(x):
    return x * jax.nn.sigmoid(x)


def _gelu_tanh(x):
    c = math.sqrt(2.0 / math.pi)
    return x * (0.5 * (1.0 + jnp.tanh(c * (x + 0.044715 * (x * x * x)))))


def _adaln_kernel(c_ref, w_ref, b_ref, o_ref):
    cond = _silu(c_ref[...])
    c_hi, c_lo = _split_bf16(cond)
    w_hi, w_lo = _split_bf16(w_ref[...])
    acc = _dot(c_hi, w_hi) + (_dot(c_lo, w_hi) + _dot(c_hi, w_lo))
    o_ref[...] = acc + b_ref[...]


def _adaln(c, ada_w, ada_b):
    B = c.shape[0]
    tn = 1024
    c_pad = jnp.zeros((8, D), f32).at[:B].set(c)
    out = pl.pallas_call(
        _adaln_kernel,
        grid=(DEPTH, 6 * D // tn),
        in_specs=[
            pl.BlockSpec((8, D), lambda l, j: (0, 0)),
            pl.BlockSpec((None, D, tn), lambda l, j: (l, 0, j)),
            pl.BlockSpec((None, 1, tn), lambda l, j: (l, 0, j)),
        ],
        out_specs=pl.BlockSpec((None, 8, tn), lambda l, j: (l, 0, j)),
        out_shape=jax.ShapeDtypeStruct((DEPTH, 8, 6 * D), f32),
        compiler_params=pltpu.CompilerParams(vmem_limit_bytes=VMEM_LIMIT),
        name="adaln",
    )(c_pad, ada_w, ada_b.reshape(DEPTH, 1, 6 * D))
    return out[:, :B].reshape(DEPTH, B, 6, D)


def _ev_in_kernel(x_ref, mod_ref, wpk_ref, wqvt_ref, p_ref, k_ref, qt_ref, vt_ref):
    h = (x_ref[...] * (1.0 + mod_ref[1:2, :]) + mod_ref[0:1, :]).astype(bf16)
    zpk = _dot(h, wpk_ref[...])
    p_ref[...] = zpk[:, :POOL_WIDTH]
    k_ref[...] = zpk[:, POOL_WIDTH:].astype(bf16)
    zt = lax.dot_general(wqvt_ref[...], h, (((1,), (1,)), ((), ())), preferred_element_type=f32)
    qt_ref[...] = (zt[:ATTN_WIDTH] * (HEAD_DIM ** -0.5)).astype(bf16)
    vt_ref[...] = zt[ATTN_WIDTH:].astype(bf16)


def _ev_in(x2d, mod_l, w_pk, w_qv_t, S):
    T = x2d.shape[0]
    tm = 512
    tpb = S // tm
    return pl.pallas_call(
        _ev_in_kernel,
        grid=(T // tm,),
        in_specs=[
            pl.BlockSpec((tm, D), lambda i: (i, 0)),
            pl.BlockSpec((None, 6, D), lambda i: (i // tpb, 0, 0)),
            pl.BlockSpec((D, POOL_WIDTH + KV_WIDTH), lambda i: (0, 0)),
            pl.BlockSpec((ATTN_WIDTH + KV_WIDTH, D), lambda i: (0, 0)),
        ],
        out_specs=[
            pl.BlockSpec((tm, POOL_WIDTH), lambda i: (i, 0)),
            pl.BlockSpec((tm, KV_WIDTH), lambda i: (i, 0)),
            pl.BlockSpec((ATTN_WIDTH, tm), lambda i: (0, i)),
            pl.BlockSpec((KV_WIDTH, tm), lambda i: (0, i)),
        ],
        out_shape=[
            jax.ShapeDtypeStruct((T, POOL_WIDTH), f32),
            jax.ShapeDtypeStruct((T, KV_WIDTH), bf16),
            jax.ShapeDtypeStruct((ATTN_WIDTH, T), bf16),
            jax.ShapeDtypeStruct((KV_WIDTH, T), bf16),
        ],
        compiler_params=pltpu.CompilerParams(vmem_limit_bytes=VMEM_LIMIT),
        name="ev_in",
    )(x2d, mod_l, w_pk, w_qv_t)


def _ev_mix_kernel(x_ref, mod_ref, p_ref, pp_ref, pn_ref, qt_ref,
                   k_ref, kp_ref, kn_ref, vt_ref, vtp_ref, vtn_ref,
                   bias_ref, sink_ref, wpool_ref, pscale_ref, wout_ref, lng_ref, lnb_ref,
                   o_ref, pext_ref, kext_ref, vext_ref, ybt_ref, mix_ref, *, S, tq):
    i = pl.program_id(1)
    n_tiles = S // tq
    is_first = i == 0
    is_last = i == n_tiles - 1

    p = p_ref[...]
    pext_ref[0:POOL_HALO, :] = jnp.where(is_first, 0.0, pp_ref[...])
    pext_ref[POOL_HALO:POOL_HALO + tq, :] = p
    pext_ref[POOL_HALO + tq:, :] = jnp.where(is_last, 0.0, pn_ref[...])
    pos = i * tq + lax.broadcasted_iota(jnp.int32, (tq, 1), 0)
    for g, w in enumerate(POOL_WINDOWS):
        r = w // 2
        cs = slice(g * POOL_CH, (g + 1) * POOL_CH)
        acc = pext_ref[POOL_HALO - r:POOL_HALO - r + tq, cs]
        for d in range(-r + 1, r + 1):
            acc = acc + pext_ref[POOL_HALO + d:POOL_HALO + d + tq, cs]
        cnt = (jnp.minimum(pos + r + 1, S) - jnp.maximum(pos - r, 0)).astype(f32)
        pooled = acc / cnt - p[:, cs]
        ya = _dot(pooled.astype(bf16), wpool_ref[g])
        mix_ref[:, cs] = (ya * pscale_ref[:, cs]).astype(bf16)

    kext_ref[0:BLOCK, :] = kp_ref[...]
    kext_ref[BLOCK:BLOCK + tq, :] = k_ref[...]
    kext_ref[BLOCK + tq:, :] = kn_ref[...]
    vext_ref[:, 0:BLOCK] = vtp_ref[...]
    vext_ref[:, BLOCK:BLOCK + tq] = vt_ref[...]
    vext_ref[:, BLOCK + tq:] = vtn_ref[...]

    n_blocks = S // BLOCK
    zeros_q = jnp.zeros((HEAD_DIM, Q_GROUP * BLOCK), bf16)
    for n in range(tq // BLOCK):
        gb = i * (tq // BLOCK) + n
        variant = jnp.where(gb == 0, 1, jnp.where(gb == n_blocks - 1, 2, 0))
        cols = slice(n * BLOCK, (n + 1) * BLOCK)
        kw = kext_ref[n * BLOCK:n * BLOCK + 3 * BLOCK, :]
        for kvh in range(N_KV_HEADS):
            qst = jnp.concatenate(
                [qt_ref[(kvh * Q_GROUP + gq) * HEAD_DIM:(kvh * Q_GROUP + gq + 1) * HEAD_DIM, cols]
                 for gq in range(Q_GROUP)], axis=1)
            qst = jnp.concatenate([qst, zeros_q] if kvh == 0 else [zeros_q, qst], axis=0)
            s = _dot(kw, qst) + bias_ref[variant, kvh]
            sink = sink_ref[kvh]
            m = jnp.maximum(jnp.max(s, axis=0, keepdims=True), sink)
            e = jnp.exp(s - m)
            denom = jnp.sum(e, axis=0, keepdims=True) + jnp.exp(sink - m)
            vwt = vext_ref[kvh * HEAD_DIM:(kvh + 1) * HEAD_DIM, n * BLOCK:n * BLOCK + 3 * BLOCK]
            out = _dot(vwt, e.astype(bf16)) / denom
            for gq in range(Q_GROUP):
                hq = kvh * Q_GROUP + gq
                ybt_ref[hq * HEAD_DIM:(hq + 1) * HEAD_DIM, cols] = out[:, gq * BLOCK:(gq + 1) * BLOCK]
    mix_ref[:, POOL_WIDTH:] = ybt_ref[...].T.astype(bf16)

    y = _dot(mix_ref[...], wout_ref[...])
    xr = ALPHA * x_ref[...] + (1.0 + mod_ref[2:3, :]) * y
    o_ref[...] = _layer_norm(xr, lng_ref[...], lnb_ref[...])


def _ev_mix(x2d, mod_l, p, qt, k, vt, w_pool, pool_scale, sink, w_out, ln_g, ln_b, B, S):
    T = x2d.shape[0]
    tq = 512
    nt = S // tq
    kb = tq // BLOCK
    pb = tq // POOL_HALO
    n_kblocks = T // BLOCK
    n_pblocks = T // POOL_HALO

    def main(b, i): return (b * nt + i, 0)
    def kprev(b, i): return (jnp.maximum((b * nt + i) * kb - 1, 0), 0)
    def knext(b, i): return (jnp.minimum((b * nt + i + 1) * kb, n_kblocks - 1), 0)
    def pprev(b, i): return (jnp.maximum((b * nt + i) * pb - 1, 0), 0)
    def pnext(b, i): return (jnp.minimum((b * nt + i + 1) * pb, n_pblocks - 1), 0)
    def const2(b, i): return (0, 0)

    def tmain(b, i): return (0, b * nt + i)
    def tprev(b, i): return (0, jnp.maximum((b * nt + i) * kb - 1, 0))
    def tnext(b, i): return (0, jnp.minimum((b * nt + i + 1) * kb, n_kblocks - 1))

    assert S // BLOCK >= 2
    kj = jnp.arange(3 * BLOCK)[:, None]
    qi = jnp.arange(BLOCK)[None, :]
    dist = jnp.abs(kj - BLOCK - qi)
    slopes = 2.0 ** (-8.0 * jnp.arange(1, N_Q_HEADS + 1, dtype=f32) / N_Q_HEADS)
    alibi = -slopes[:, None, None] * dist.astype(f32)[None]
    in_window = dist <= BLOCK
    key_ok = jnp.stack([kj >= 0, kj >= BLOCK, kj < 2 * BLOCK])
    bias = jnp.where((in_window[None] & key_ok)[:, None], alibi[None], -1e30)
    bias = bias.reshape(3, N_KV_HEADS, Q_GROUP, 3 * BLOCK, BLOCK).transpose(0, 1, 3, 2, 4)
    bias = bias.reshape(3, N_KV_HEADS, 3 * BLOCK, Q_GROUP * BLOCK)
    sink_row = jnp.repeat(sink.astype(f32).reshape(N_KV_HEADS, Q_GROUP), BLOCK, axis=1)[:, None, :]

    kernel = functools.partial(_ev_mix_kernel, S=S, tq=tq)
    return pl.pallas_call(
        kernel,
        grid=(B, nt),
        in_specs=[
            pl.BlockSpec((tq, D), main),
            pl.BlockSpec((None, 6, D), lambda b, i: (b, 0, 0)),
            pl.BlockSpec((tq, POOL_WIDTH), main),
            pl.BlockSpec((POOL_HALO, POOL_WIDTH), pprev),
            pl.BlockSpec((POOL_HALO, POOL_WIDTH), pnext),
            pl.BlockSpec((ATTN_WIDTH, tq), tmain),
            pl.BlockSpec((tq, KV_WIDTH), main),
            pl.BlockSpec((BLOCK, KV_WIDTH), kprev),
            pl.BlockSpec((BLOCK, KV_WIDTH), knext),
            pl.BlockSpec((KV_WIDTH, tq), tmain),
            pl.BlockSpec((KV_WIDTH, BLOCK), tprev),
            pl.BlockSpec((KV_WIDTH, BLOCK), tnext),
            pl.BlockSpec((3, N_KV_HEADS, 3 * BLOCK, Q_GROUP * BLOCK), lambda b, i: (0, 0, 0, 0)),
            pl.BlockSpec((N_KV_HEADS, 1, Q_GROUP * BLOCK), lambda b, i: (0, 0, 0)),
            pl.BlockSpec((len(POOL_WINDOWS), POOL_CH, POOL_CH), lambda b, i: (0, 0, 0)),
            pl.BlockSpec((1, POOL_WIDTH), const2),
            pl.BlockSpec((D, D), const2),
            pl.BlockSpec((1, D), const2),
            pl.BlockSpec((1, D), const2),
        ],
        out_specs=pl.BlockSpec((tq, D), main),
        out_shape=jax.ShapeDtypeStruct((T, D), f32),
        scratch_shapes=[
            pltpu.VMEM((tq + 2 * POOL_HALO, POOL_WIDTH), f32),
            pltpu.VMEM((tq + 2 * BLOCK, KV_WIDTH), bf16),
            pltpu.VMEM((KV_WIDTH, tq + 2 * BLOCK), bf16),
            pltpu.VMEM((ATTN_WIDTH, tq), f32),
            pltpu.VMEM((tq, D), bf16),
        ],
        compiler_params=pltpu.CompilerParams(vmem_limit_bytes=VMEM_LIMIT),
        name="ev_mix",
    )(x2d, mod_l, p, p, p, qt, k, k, k, vt, vt, vt, bias, sink_row, w_pool, pool_scale, w_out, ln_g, ln_b)


def _ffn_kernel(x_ref, mod_ref, wg_ref, wu_ref, wd_ref, lng_ref, lnb_ref, o_ref, h_ref, acc_ref):
    j = pl.program_id(1)

    @pl.when(j == 0)
    def _():
        h = x_ref[...] * (1.0 + mod_ref[4:5, :]) + mod_ref[3:4, :]
        h_ref[...] = h.astype(bf16)
        acc_ref[...] = jnp.zeros_like(acc_ref)

    h = h_ref[...]
    a = _silu(_dot(h, wg_ref[...])) * _dot(h, wu_ref[...])
    acc_ref[...] += _dot(a.astype(bf16), wd_ref[...])

    @pl.when(j == pl.num_programs(1) - 1)
    def _():
        xr = ALPHA * x_ref[...] + (1.0 + mod_ref[5:6, :]) * acc_ref[...]
        o_ref[...] = _layer_norm(xr, lng_ref[...], lnb_ref[...])


def _ffn(x2d, mod_l, w_gate, w_up, w_down, ln_g, ln_b, S):
    T = x2d.shape[0]
    tm = 1024
    tf = 256
    dff = w_gate.shape[1]
    tpb = S // tm
    return pl.pallas_call(
        _ffn_kernel,
        grid=(T // tm, dff // tf),
        in_specs=[
            pl.BlockSpec((tm, D), lambda i, j: (i, 0)),
            pl.BlockSpec((None, 6, D), lambda i, j: (i // tpb, 0, 0)),
            pl.BlockSpec((D, tf), lambda i, j: (0, j)),
            pl.BlockSpec((D, tf), lambda i, j: (0, j)),
            pl.BlockSpec((tf, D), lambda i, j: (j, 0)),
            pl.BlockSpec((1, D), lambda i, j: (0, 0)),
            pl.BlockSpec((1, D), lambda i, j: (0, 0)),
        ],
        out_specs=pl.BlockSpec((tm, D), lambda i, j: (i, 0)),
        out_shape=jax.ShapeDtypeStruct((T, D), f32),
        scratch_shapes=[pltpu.VMEM((tm, D), bf16), pltpu.VMEM((tm, D), f32)],
        compiler_params=pltpu.CompilerParams(vmem_limit_bytes=VMEM_LIMIT),
        name="ffn",
    )(x2d, mod_l, w_gate, w_up, w_down, ln_g, ln_b)


def _sg_kernel(x_ref, mod_ref, win_ref, sgg_ref, sgb_ref, ws_ref, bst_ref, wout_ref, lng_ref, lnb_ref,
               o_ref, gate_ref, *, tm):
    x = x_ref[...]
    h = x * (1.0 + mod_ref[1:2, :]) + mod_ref[0:1, :]
    z = _gelu_tanh(_dot(h.astype(bf16), win_ref[...]))
    u = z[:, :D]
    v = _layer_norm(z[:, D:], sgg_ref[...], sgb_ref[...]).astype(bf16)
    for n in range(tm // CHUNK):
        rows = slice(n * CHUNK, (n + 1) * CHUNK)
        for g in range(SG_GROUPS):
            cols = slice(g * SG_CH, (g + 1) * SG_CH)
            sv = _dot(ws_ref[g], v[rows, cols]) + bst_ref[:, g:g + 1]
            gate_ref[rows, cols] = (u[rows, cols] * sv).astype(bf16)
    y = _dot(gate_ref[...], wout_ref[...])
    xr = ALPHA * x + (1.0 + mod_ref[2:3, :]) * y
    o_ref[...] = _layer_norm(xr, lng_ref[...], lnb_ref[...])


def _sg_mix(x2d, mod_l, w_in, sg_g, sg_b, w_s, b_s_t, w_out, ln_g, ln_b, S):
    T = x2d.shape[0]
    tm = 512
    tpb = S // tm
    c2 = lambda i: (0, 0)
    return pl.pallas_call(
        functools.partial(_sg_kernel, tm=tm),
        grid=(T // tm,),
        in_specs=[
            pl.BlockSpec((tm, D), lambda i: (i, 0)),
            pl.BlockSpec((None, 6, D), lambda i: (i // tpb, 0, 0)),
            pl.BlockSpec((D, 2 * D), c2),
            pl.BlockSpec((1, D), c2),
            pl.BlockSpec((1, D), c2),
            pl.BlockSpec((SG_GROUPS, CHUNK, CHUNK), lambda i: (0, 0, 0)),
            pl.BlockSpec((CHUNK, SG_GROUPS), c2),
            pl.BlockSpec((D, D), c2),
            pl.BlockSpec((1, D), c2),
            pl.BlockSpec((1, D), c2),
        ],
        out_specs=pl.BlockSpec((tm, D), lambda i: (i, 0)),
        out_shape=jax.ShapeDtypeStruct((T, D), f32),
        scratch_shapes=[pltpu.VMEM((tm, D), bf16)],
        compiler_params=pltpu.CompilerParams(vmem_limit_bytes=VMEM_LIMIT),
        name="sg_mix",
    )(x2d, mod_l, w_in, sg_g, sg_b, w_s, b_s_t, w_out, ln_g, ln_b)


def _route_kernel(x_ref, mod_ref, wrt_ref, hs_ref, route_ref, cnt_ref):
    W = ROUTE_W
    h = x_ref[...] * (1.0 + mod_ref[4:5, :]) + mod_ref[3:4, :]
    h_hi, h_lo = _split_bf16(h)
    w_hi, w_lo = _split_bf16(wrt_ref[...])
    nt = (((1,), (1,)), ((), ()))
    logits = (lax.dot_general(w_hi, h_hi, nt, preferred_element_type=f32)
              + (lax.dot_general(w_hi, h_lo, nt, preferred_element_type=f32)
                 + lax.dot_general(w_lo, h_hi, nt, preferred_element_type=f32)))

    eidx = lax.broadcasted_iota(jnp.int32, (N_EXPERTS, W), 0)
    m1 = jnp.max(logits, axis=0, keepdims=True)
    i1 = jnp.min(jnp.where(logits == m1, eidx, N_EXPERTS), axis=0, keepdims=True)
    sel1 = eidx == i1
    rest = jnp.where(sel1, -jnp.inf, logits)
    m2 = jnp.max(rest, axis=0, keepdims=True)
    i2 = jnp.min(jnp.where(rest == m2, eidx, N_EXPERTS), axis=0, keepdims=True)
    sel2 = eidx == i2
    e2 = jnp.exp(m2 - m1)
    g1 = 1.0 / (1.0 + e2)
    g2 = e2 / (1.0 + e2)

    a1 = sel1.astype(f32)
    a2 = sel2.astype(f32)
    assign = a1 + a2
    counts = jnp.sum(assign, axis=1, keepdims=True)
    grans = jnp.ceil(counts * (1.0 / GRAN))
    sub = lax.broadcasted_iota(jnp.int32, (N_EXPERTS, 1), 0)
    seg = jnp.zeros((N_EXPERTS, 1), f32)
    for e in range(N_EXPERTS - 1):
        seg = seg + jnp.where(sub > e, grans[e:e + 1, :] * GRAN, 0.0)
    tr = lax.broadcasted_iota(jnp.int32, (W, W), 0)
    tc = lax.broadcasted_iota(jnp.int32, (W, W), 1)
    upper = (tr < tc).astype(bf16)
    rank = _dot(assign.astype(bf16), upper)
    slot = seg + rank
    pos1 = jnp.sum(a1 * slot, axis=0, keepdims=True)
    pos2 = jnp.sum(a2 * slot, axis=0, keepdims=True)

    srow = lax.broadcasted_iota(jnp.int32, (CHUNK_SLOTS, W), 0)
    perm = ((srow == pos1.astype(jnp.int32)) | (srow == pos2.astype(jnp.int32)))
    hs_ref[...] = _dot(perm.astype(f32).astype(bf16), h_hi).astype(bf16)

    ridx = lax.broadcasted_iota(jnp.int32, (8, W), 0)
    route = jnp.where(ridx == 0, pos1, jnp.where(ridx == 1, pos2, jnp.where(ridx == 2, g1, jnp.where(ridx == 3, g2, 0.0))))
    route_ref[...] = route
    cnt_ref[...] = jnp.broadcast_to(counts, (N_EXPERTS, 128)).astype(jnp.int32)


def _route(x2d, mod_l, w_router_t, S):
    T = x2d.shape[0]
    W = ROUTE_W
    nc = T // W
    tpb = S // W
    return pl.pallas_call(
        _route_kernel,
        grid=(nc,),
        in_specs=[
            pl.BlockSpec((W, D), lambda c: (c, 0)),
            pl.BlockSpec((None, 6, D), lambda c: (c // tpb, 0, 0)),
            pl.BlockSpec((N_EXPERTS, D), lambda c: (0, 0)),
        ],
        out_specs=[
            pl.BlockSpec((None, CHUNK_SLOTS, D), lambda c: (c, 0, 0)),
            pl.BlockSpec((None, 8, W), lambda c: (c, 0, 0)),
            pl.BlockSpec((None, N_EXPERTS, 128), lambda c: (c, 0, 0)),
        ],
        out_shape=[
            jax.ShapeDtypeStruct((nc, CHUNK_SLOTS, D), bf16),
            jax.ShapeDtypeStruct((nc, 8, W), f32),
            jax.ShapeDtypeStruct((nc, N_EXPERTS, 128), jnp.int32),
        ],
        compiler_params=pltpu.CompilerParams(vmem_limit_bytes=VMEM_LIMIT),
        name="route",
    )(x2d, mod_l, w_router_t)


def _granule_copy(src_ref, buf_ref, sem, idx_ref, base, g):
    row = pl.multiple_of(idx_ref[base + g] * GRAN, GRAN)
    return pltpu.make_async_copy(src_ref.at[pl.ds(row, GRAN), :], buf_ref.at[pl.ds(g * GRAN, GRAN), :], sem)


def _gather_start(src_ref, buf_ref, sem, idx_ref, base, n):
    for g in range(n):
        _granule_copy(src_ref, buf_ref, sem, idx_ref, base, g).start()


def _gather_wait(src_ref, buf_ref, sem, idx_ref, base, n):
    for g in range(n):
        _granule_copy(src_ref, buf_ref, sem, idx_ref, base, g).wait()


def _expert_kernel(te_ref, tv_ref, src_ref, hs_ref, wg_ref, wu_ref, wd_ref, o_ref,
                   xbuf_ref, sem, acc_ref, wgb_ref, wub_ref, wdb_ref):
    i = pl.program_id(0)
    j = pl.program_id(1)
    n_tiles = pl.num_programs(0)
    last = pl.num_programs(1) - 1
    valid = tv_ref[i] > 0
    slot = i % 2
    nxt = jnp.minimum(i + 1, n_tiles - 1)

    @pl.when(j == 0)
    def _():
        @pl.when(i == 0)
        def _():
            _gather_start(hs_ref, xbuf_ref.at[0], sem.at[0], src_ref, 0, TILE_GRANS)

        @pl.when(valid)
        def _():
            _gather_wait(hs_ref, xbuf_ref.at[slot], sem.at[slot], src_ref, i * TILE_GRANS, TILE_GRANS)

        @pl.when((i + 1 < n_tiles) & (tv_ref[nxt] > 0))
        def _():
            _gather_start(hs_ref, xbuf_ref.at[1 - slot], sem.at[1 - slot], src_ref, nxt * TILE_GRANS, TILE_GRANS)

    @pl.when(valid)
    def _():
        wgb_ref[...] = wg_ref[...].astype(bf16)
        wub_ref[...] = wu_ref[...].astype(bf16)
        wdb_ref[...] = wd_ref[...].astype(bf16)

    for sb in range(MOE_TM // MOE_SUB):
        rows = pl.ds(sb * MOE_SUB, MOE_SUB)

        @pl.when(sb < tv_ref[i])
        def _():
            x = xbuf_ref[slot, rows, :]
            a = _silu(_dot(x, wgb_ref[...])) * _dot(x, wub_ref[...])
            y = _dot(a.astype(bf16), wdb_ref[...])

            @pl.when(j == 0)
            def _():
                acc_ref[rows, :] = y

            @pl.when((j > 0) & (j < last))
            def _():
                acc_ref[rows, :] += y

            @pl.when(j == last)
            def _():
                o_ref[rows, :] = (acc_ref[rows, :] + y).astype(o_ref.dtype)

        @pl.when((sb >= tv_ref[i]) & (j == last))
        def _():
            o_ref[rows, :] = jnp.zeros((MOE_SUB, D), o_ref.dtype)


def _experts(tile_expert, tile_valid, src_of_dst, hs2d, w_gate, w_up, w_down):
    n_tiles = tile_expert.shape[0]
    dff = w_gate.shape[2]
    nff = dff // MOE_TF
    assert nff >= 2

    def jj(j, tv, i):
        return jnp.where(tv[i] > 0, j, nff - 1)

    grid_spec = pltpu.PrefetchScalarGridSpec(
        num_scalar_prefetch=3,
        grid=(n_tiles, nff),
        in_specs=[
            pl.BlockSpec(memory_space=pl.ANY),
            pl.BlockSpec((None, D, MOE_TF), lambda i, j, te, tv, sd: (te[i], 0, jj(j, tv, i))),
            pl.BlockSpec((None, D, MOE_TF), lambda i, j, te, tv, sd: (te[i], 0, jj(j, tv, i))),
            pl.BlockSpec((None, MOE_TF, D), lambda i, j, te, tv, sd: (te[i], jj(j, tv, i), 0)),
        ],
        out_specs=pl.BlockSpec((MOE_TM, D), lambda i, j, te, tv, sd: (i, 0)),
        scratch_shapes=[
            pltpu.VMEM((2, MOE_TM, D), bf16),
            pltpu.SemaphoreType.DMA((2,)),
            pltpu.VMEM((MOE_TM, D), f32),
            pltpu.VMEM((D, MOE_TF), bf16),
            pltpu.VMEM((D, MOE_TF), bf16),
            pltpu.VMEM((MOE_TF, D), bf16),
        ],
    )
    return pl.pallas_call(
        _expert_kernel,
        grid_spec=grid_spec,
        out_shape=jax.ShapeDtypeStruct((n_tiles * MOE_TM, D), bf16),
        compiler_params=pltpu.CompilerParams(vmem_limit_bytes=VMEM_LIMIT_EXPERTS),
        name="experts",
    )(tile_expert, tile_valid, src_of_dst, hs2d, w_gate, w_up, w_down)


def _combine_kernel(ds_ref, x_ref, mod_ref, o_hbm_ref, rt_ref, lng_ref, lnb_ref, out_ref, obuf_ref, sem):
    W = ROUTE_W
    c = pl.program_id(0)
    nc = pl.num_programs(0)
    slot = c % 2
    nxt = jnp.minimum(c + 1, nc - 1)

    @pl.when(c == 0)
    def _():
        _gather_start(o_hbm_ref, obuf_ref.at[0], sem.at[0], ds_ref, 0, CHUNK_GRANS)

    _gather_wait(o_hbm_ref, obuf_ref.at[slot], sem.at[slot], ds_ref, c * CHUNK_GRANS, CHUNK_GRANS)

    @pl.when(c + 1 < nc)
    def _():
        _gather_start(o_hbm_ref, obuf_ref.at[1 - slot], sem.at[1 - slot], ds_ref, nxt * CHUNK_GRANS, CHUNK_GRANS)

    rt = rt_ref[...]
    scol = lax.broadcasted_iota(jnp.int32, (W, CHUNK_SLOTS), 1)
    p1 = (scol == rt[:, 0:1].astype(jnp.int32)).astype(f32).astype(bf16)
    p2 = (scol == rt[:, 1:2].astype(jnp.int32)).astype(f32).astype(bf16)
    osv = obuf_ref[slot]
    y = rt[:, 2:3] * _dot(p1, osv) + rt[:, 3:4] * _dot(p2, osv)
    xr = ALPHA * x_ref[...] + (1.0 + mod_ref[5:6, :]) * y
    out_ref[...] = _layer_norm(xr, lng_ref[...], lnb_ref[...])


def _combine(dst_of_src, x2d, mod_l, o2d, route_t, ln_g, ln_b, S):
    T = x2d.shape[0]
    W = ROUTE_W
    tpb = S // W
    grid_spec = pltpu.PrefetchScalarGridSpec(
        num_scalar_prefetch=1,
        grid=(T // W,),
        in_specs=[
            pl.BlockSpec((W, D), lambda c, ds: (c, 0)),
            pl.BlockSpec((None, 6, D), lambda c, ds: (c // tpb, 0, 0)),
            pl.BlockSpec(memory_space=pl.ANY),
            pl.BlockSpec((None, W, 8), lambda c, ds: (c, 0, 0)),
            pl.BlockSpec((1, D), lambda c, ds: (0, 0)),
            pl.BlockSpec((1, D), lambda c, ds: (0, 0)),
        ],
        out_specs=pl.BlockSpec((W, D), lambda c, ds: (c, 0)),
        scratch_shapes=[
            pltpu.VMEM((2, CHUNK_SLOTS, D), bf16),
            pltpu.SemaphoreType.DMA((2,)),
        ],
    )
    return pl.pallas_call(
        _combine_kernel,
        grid_spec=grid_spec,
        out_shape=jax.ShapeDtypeStruct((T, D), f32),
        compiler_params=pltpu.CompilerParams(vmem_limit_bytes=VMEM_LIMIT),
        name="combine",
    )(dst_of_src, x2d, mod_l, o2d, route_t, ln_g, ln_b)


def _routing_tables(counts, n_tiles):
    nc = counts.shape[0]
    gr = (counts + GRAN - 1) // GRAN
    seg_start = jnp.cumsum(gr, axis=1) - gr
    chunk_total = jnp.sum(gr, axis=1)
    prefix = jnp.cumsum(gr, axis=0) - gr
    g_e = jnp.sum(gr, axis=0)
    tiles_e = (g_e + TILE_GRANS - 1) // TILE_GRANS
    tile_end = jnp.cumsum(tiles_e)
    tile_start = tile_end - tiles_e
    total_tiles = tile_end[-1]

    i32 = jnp.int32
    er = jnp.arange(N_EXPERTS, dtype=i32)
    t = jnp.arange(n_tiles, dtype=i32)
    te = jnp.sum((t[:, None] >= tile_end[None, :]).astype(i32), axis=1)
    tile_valid = (t < total_tiles).astype(i32)
    last_e = jnp.sum((total_tiles - 1 >= tile_end).astype(i32))
    tile_expert = jnp.where(tile_valid > 0, jnp.minimum(te, N_EXPERTS - 1), last_e).astype(i32)
    oh_t = (tile_expert[:, None] == er).astype(i32)
    grans_left = jnp.sum(oh_t * (g_e - (t[:, None] - tile_start[None, :]) * TILE_GRANS), axis=1)
    tile_subs = tile_valid * jnp.clip((grans_left + SUB_GRANS - 1) // SUB_GRANS, 0, TILE_GRANS // SUB_GRANS)

    k = jnp.arange(CHUNK_GRANS, dtype=i32)
    seg_end = seg_start + gr
    e_of = jnp.sum((k[None, :, None] >= seg_end[:, None, :]).astype(i32), axis=2)
    oh_e = (jnp.minimum(e_of, N_EXPERTS - 1)[:, :, None] == er).astype(i32)
    base = tile_start[None, :] * TILE_GRANS + prefix - seg_start
    dst = jnp.sum(oh_e * base[:, None, :], axis=2) + k[None, :]
    valid_src = k[None, :] < chunk_total[:, None]
    dst_of_src = jnp.where(valid_src, dst, 0).astype(i32).reshape(-1)

    d = jnp.arange(n_tiles * TILE_GRANS, dtype=i32)
    oh_d = (jnp.repeat(tile_expert, TILE_GRANS)[:, None] == er).astype(i32)
    q = d - jnp.sum(oh_d * tile_start[None, :], axis=1) * TILE_GRANS
    incl_d = jnp.sum(oh_d[:, :, None] * (prefix + gr).T[None], axis=1)
    c_d = jnp.sum((q[:, None] >= incl_d).astype(i32), axis=1)
    oh_c = (jnp.minimum(c_d, nc - 1)[:, None] == jnp.arange(nc, dtype=i32)).astype(i32)
    cbase = jnp.arange(nc, dtype=i32)[:, None] * CHUNK_GRANS + seg_start - prefix
    sel = jnp.sum(oh_c[:, :, None] * oh_d[:, None, :] * cbase[None], axis=(1, 2))
    valid_dst = (jnp.repeat(tile_valid, TILE_GRANS) > 0) & (q >= 0) & (q < jnp.sum(oh_d * g_e[None, :], axis=1))
    src_of_dst = jnp.where(valid_dst, sel + q, 0).astype(i32)
    return tile_expert, tile_subs.astype(i32), src_of_dst, dst_of_src


def kernel(x, c, ada_w, ada_b, ln_g, ln_b, ev_w_in, ev_pool_w, ev_pool_scale, ev_sink, ev_w_out, od_w_in, od_sg_ln_g, od_sg_ln_b, od_w_s, od_b_s, od_w_out, ffn_w_gate, ffn_w_up, ffn_w_down, moe_w_router, moe_w_gate, moe_w_up, moe_w_down):
    B, S, _ = x.shape
    T = B * S
    x2d = x.reshape(T, D)
    mod = _adaln(c, ada_w, ada_b)

    w_in = ev_w_in[0].astype(bf16)
    q0, k0, v0 = POOL_WIDTH, POOL_WIDTH + ATTN_WIDTH, POOL_WIDTH + ATTN_WIDTH + KV_WIDTH
    w_pk = jnp.concatenate([w_in[:, :q0], w_in[:, k0:v0]], axis=1)
    w_qv_t = jnp.concatenate([w_in[:, q0:k0], w_in[:, v0:]], axis=1).T
    p, k, qt, vt = _ev_in(x2d, mod[0], w_pk, w_qv_t, S)
    x2d = _ev_mix(x2d, mod[0], p, qt, k, vt, ev_pool_w[0].astype(bf16), ev_pool_scale[0][None, :],
                  ev_sink[0], ev_w_out[0].astype(bf16), ln_g[0, 0][None, :], ln_b[0, 0][None, :], B, S)
    x2d = _ffn(x2d, mod[0], ffn_w_gate[0].astype(bf16), ffn_w_up[0].astype(bf16),
               ffn_w_down[0].astype(bf16), ln_g[0, 1][None, :], ln_b[0, 1][None, :], S)

    x2d = _sg_mix(x2d, mod[1], od_w_in[0].astype(bf16), od_sg_ln_g[0][None, :], od_sg_ln_b[0][None, :],
                  od_w_s[0].astype(bf16), od_b_s[0].T, od_w_out[0].astype(bf16),
                  ln_g[1, 0][None, :], ln_b[1, 0][None, :], S)

    hs, route, cnt = _route(x2d, mod[1], moe_w_router[0].T, S)
    nc = T // ROUTE_W
    n_tiles = (nc * CHUNK_GRANS) // TILE_GRANS + N_EXPERTS
    tile_expert, tile_subs, src_of_dst, dst_of_src = _routing_tables(cnt[:, :, 0], n_tiles)
    o = _experts(tile_expert, tile_subs, src_of_dst, hs.reshape(nc * CHUNK_SLOTS, D),
                 moe_w_gate[0], moe_w_up[0], moe_w_down[0])
    x2d = _combine(dst_of_src, x2d, mod[1], o, jnp.swapaxes(route, 1, 2),
                   ln_g[1, 1][None, :], ln_b[1, 1][None, :], S)
    return x2d.reshape(B, S, D)
```

```python
import functools
import math

import jax
import jax.numpy as jnp
from jax import lax
from jax.experimental import pallas as pl
from jax.experimental.pallas import tpu as pltpu

D = 1024
DEPTH = 2
ALPHA = (2.0 * DEPTH) ** 0.25
LN_EPS = 1e-5

POOL_WINDOWS = (2, 4, 8, 16)
POOL_CH = 128
POOL_WIDTH = 512
HEAD_DIM = 64
N_Q_HEADS = 8
N_KV_HEADS = 2
Q_GROUP = 4
ATTN_WIDTH = 512
KV_WIDTH = 128
BLOCK = 128
EVEN_IN = 1280
POOL_HALO = 8

CHUNK = 128
SG_GROUPS = 8
SG_CH = 128

N_EXPERTS = 8

ROUTE_W = 512
GRAN = 16
CHUNK_SLOTS = 2 * ROUTE_W + N_EXPERTS * GRAN
CHUNK_GRANS = CHUNK_SLOTS // GRAN
MOE_TM = 2048
MOE_SUB = 256
TILE_GRANS = MOE_TM // GRAN
SUB_GRANS = MOE_SUB // GRAN
MOE_TF = 512

VMEM_LIMIT = 48 * 1024 * 1024
VMEM_LIMIT_EXPERTS = 56 * 1024 * 1024

bf16 = jnp.bfloat16
f32 = jnp.float32


def _dot(a, b):
    return jnp.dot(a, b, preferred_element_type=f32)


def _split_bf16(a):
    hi = a.astype(bf16)
    lo = (a - hi.astype(f32)).astype(bf16)
    return hi, lo


def _layer_norm(x, g, b):
    mu = jnp.mean(x, axis=-1, keepdims=True)
    xc = x - mu
    var = jnp.mean(xc * xc, axis=-1, keepdims=True)
    return xc * lax.rsqrt(var + LN_EPS) * g + b


def _silu(x):
    return x * jax.nn.sigmoid(x)


def _gelu_tanh(x):
    c = math.sqrt(2.0 / math.pi)
    return x * (0.5 * (1.0 + jnp.tanh(c * (x + 0.044715 * (x * x * x)))))


def _adaln_kernel(c_ref, w_ref, b_ref, o_ref):
    cond = _silu(c_ref[...])
    c_hi, c_lo = _split_bf16(cond)
    w_hi, w_lo = _split_bf16(w_ref[...])
    acc = _dot(c_hi, w_hi) + (_dot(c_lo, w_hi) + _dot(c_hi, w_lo))
    o_ref[...] = acc + b_ref[...]


def _adaln(c, ada_w, ada_b):
    B = c.shape[0]
    tn = 1024
    c_pad = jnp.zeros((8, D), f32).at[:B].set(c)
    out = pl.pallas_call(
        _adaln_kernel,
        grid=(DEPTH, 6 * D // tn),
        in_specs=[
            pl.BlockSpec((8, D), lambda l, j: (0, 0)),
            pl.BlockSpec((None, D, tn), lambda l, j: (l, 0, j)),
            pl.BlockSpec((None, 1, tn), lambda l, j: (l, 0, j)),
        ],
        out_specs=pl.BlockSpec((None, 8, tn), lambda l, j: (l, 0, j)),
        out_shape=jax.ShapeDtypeStruct((DEPTH, 8, 6 * D), f32),
        compiler_params=pltpu.CompilerParams(vmem_limit_bytes=VMEM_LIMIT),
        name="adaln",
    )(c_pad, ada_w, ada_b.reshape(DEPTH, 1, 6 * D))
    return out[:, :B].reshape(DEPTH, B, 6, D)


def _ev_in_kernel(x_ref, mod_ref, wpk_ref, wqvt_ref, p_ref, k_ref, qt_ref, vt_ref):
    h = (x_ref[...] * (1.0 + mod_ref[1:2, :]) + mod_ref[0:1, :]).astype(bf16)
    zpk = _dot(h, wpk_ref[...])
    p_ref[...] = zpk[:, :POOL_WIDTH]
    k_ref[...] = zpk[:, POOL_WIDTH:].astype(bf16)
    zt = lax.dot_general(wqvt_ref[...], h, (((1,), (1,)), ((), ())), preferred_element_type=f32)
    qt_ref[...] = (zt[:ATTN_WIDTH] * (HEAD_DIM ** -0.5)).astype(bf16)
    vt_ref[...] = zt[ATTN_WIDTH:].astype(bf16)


def _ev_in(x2d, mod_l, w_pk, w_qv_t, S):
    T = x2d.shape[0]
    tm = 512
    tpb = S // tm
    return pl.pallas_call(
        _ev_in_kernel,
        grid=(T // tm,),
        in_specs=[
            pl.BlockSpec((tm, D), lambda i: (i, 0)),
            pl.BlockSpec((None, 6, D), lambda i: (i // tpb, 0, 0)),
            pl.BlockSpec((D, POOL_WIDTH + KV_WIDTH), lambda i: (0, 0)),
            pl.BlockSpec((ATTN_WIDTH + KV_WIDTH, D), lambda i: (0, 0)),
        ],
        out_specs=[
            pl.BlockSpec((tm, POOL_WIDTH), lambda i: (i, 0)),
            pl.BlockSpec((tm, KV_WIDTH), lambda i: (i, 0)),
            pl.BlockSpec((ATTN_WIDTH, tm), lambda i: (0, i)),
            pl.BlockSpec((KV_WIDTH, tm), lambda i: (0, i)),
        ],
        out_shape=[
            jax.ShapeDtypeStruct((T, POOL_WIDTH), f32),
            jax.ShapeDtypeStruct((T, KV_WIDTH), bf16),
            jax.ShapeDtypeStruct((ATTN_WIDTH, T), bf16),
            jax.ShapeDtypeStruct((KV_WIDTH, T), bf16),
        ],
        compiler_params=pltpu.CompilerParams(vmem_limit_bytes=VMEM_LIMIT),
        name="ev_in",
    )(x2d, mod_l, w_pk, w_qv_t)


def _ev_mix_kernel(x_ref, mod_ref, p_ref, pp_ref, pn_ref, qt_ref,
                   k_ref, kp_ref, kn_ref, vt_ref, vtp_ref, vtn_ref,
                   bias_ref, sink_ref, wpool_ref, pscale_ref, wout_ref, lng_ref, lnb_ref,
                   o_ref, pext_ref, kext_ref, vext_ref, ybt_ref, mix_ref, *, S, tq):
    i = pl.program_id(1)
    n_tiles = S // tq
    is_first = i == 0
    is_last = i == n_tiles - 1

    p = p_ref[...]
    pext_ref[0:POOL_HALO, :] = jnp.where(is_first, 0.0, pp_ref[...])
    pext_ref[POOL_HALO:POOL_HALO + tq, :] = p
    pext_ref[POOL_HALO + tq:, :] = jnp.where(is_last, 0.0, pn_ref[...])
    pos = i * tq + lax.broadcasted_iota(jnp.int32, (tq, 1), 0)
    for g, w in enumerate(POOL_WINDOWS):
        r = w // 2
        cs = slice(g * POOL_CH, (g + 1) * POOL_CH)
        acc = pext_ref[POOL_HALO - r:POOL_HALO - r + tq, cs]
        for d in range(-r + 1, r + 1):
            acc = acc + pext_ref[POOL_HALO + d:POOL_HALO + d + tq, cs]
        cnt = (jnp.minimum(pos + r + 1, S) - jnp.maximum(pos - r, 0)).astype(f32)
        pooled = acc / cnt - p[:, cs]
        ya = _dot(pooled.astype(bf16), wpool_ref[g])
        mix_ref[:, cs] = (ya * pscale_ref[:, cs]).astype(bf16)

    kext_ref[0:BLOCK, :] = kp_ref[...]
    kext_ref[BLOCK:BLOCK + tq, :] = k_ref[...]
    kext_ref[BLOCK + tq:, :] = kn_ref[...]
    vext_ref[:, 0:BLOCK] = vtp_ref[...]
    vext_ref[:, BLOCK:BLOCK + tq] = vt_ref[...]
    vext_ref[:, BLOCK + tq:] = vtn_ref[...]

    n_blocks = S // BLOCK
    zeros_q = jnp.zeros((HEAD_DIM, Q_GROUP * BLOCK), bf16)
    for n in range(tq // BLOCK):
        gb = i * (tq // BLOCK) + n
        variant = jnp.where(gb == 0, 1, jnp.where(gb == n_blocks - 1, 2, 0))
        cols = slice(n * BLOCK, (n + 1) * BLOCK)
        kw = kext_ref[n * BLOCK:n * BLOCK + 3 * BLOCK, :]
        for kvh in range(N_KV_HEADS):
            qst = jnp.concatenate(
                [qt_ref[(kvh * Q_GROUP + gq) * HEAD_DIM:(kvh * Q_GROUP + gq + 1) * HEAD_DIM, cols]
                 for gq in range(Q_GROUP)], axis=1)
            qst = jnp.concatenate([qst, zeros_q] if kvh == 0 else [zeros_q, qst], axis=0)
            s = _dot(kw, qst) + bias_ref[variant, kvh]
            sink = sink_ref[kvh]
            m = jnp.maximum(jnp.max(s, axis=0, keepdims=True), sink)
            e = jnp.exp(s - m)
            denom = jnp.sum(e, axis=0, keepdims=True) + jnp.exp(sink - m)
            vwt = vext_ref[kvh * HEAD_DIM:(kvh + 1) * HEAD_DIM, n * BLOCK:n * BLOCK + 3 * BLOCK]
            out = _dot(vwt, e.astype(bf16)) / denom
            for gq in range(Q_GROUP):
                hq = kvh * Q_GROUP + gq
                ybt_ref[hq * HEAD_DIM:(hq + 1) * HEAD_DIM, cols] = out[:, gq * BLOCK:(gq + 1) * BLOCK]
    mix_ref[:, POOL_WIDTH:] = ybt_ref[...].T.astype(bf16)

    y = _dot(mix_ref[...], wout_ref[...])
    xr = ALPHA * x_ref[...] + (1.0 + mod_ref[2:3, :]) * y
    o_ref[...] = _layer_norm(xr, lng_ref[...], lnb_ref[...])


def _ev_mix(x2d, mod_l, p, qt, k, vt, w_pool, pool_scale, sink, w_out, ln_g, ln_b, B, S):
    T = x2d.shape[0]
    tq = 512
    nt = S // tq
    kb = tq // BLOCK
    pb = tq // POOL_HALO
    n_kblocks = T // BLOCK
    n_pblocks = T // POOL_HALO

    def main(b, i): return (b * nt + i, 0)
    def kprev(b, i): return (jnp.maximum((b * nt + i) * kb - 1, 0), 0)
    def knext(b, i): return (jnp.minimum((b * nt + i + 1) * kb, n_kblocks - 1), 0)
    def pprev(b, i): return (jnp.maximum((b * nt + i) * pb - 1, 0), 0)
    def pnext(b, i): return (jnp.minimum((b * nt + i + 1) * pb, n_pblocks - 1), 0)
    def const2(b, i): return (0, 0)

    def tmain(b, i): return (0, b * nt + i)
    def tprev(b, i): return (0, jnp.maximum((b * nt + i) * kb - 1, 0))
    def tnext(b, i): return (0, jnp.minimum((b * nt + i + 1) * kb, n_kblocks - 1))

    assert S // BLOCK >= 2
    kj = jnp.arange(3 * BLOCK)[:, None]
    qi = jnp.arange(BLOCK)[None, :]
    dist = jnp.abs(kj - BLOCK - qi)
    slopes = 2.0 ** (-8.0 * jnp.arange(1, N_Q_HEADS + 1, dtype=f32) / N_Q_HEADS)
    alibi = -slopes[:, None, None] * dist.astype(f32)[None]
    in_window = dist <= BLOCK
    key_ok = jnp.stack([kj >= 0, kj >= BLOCK, kj < 2 * BLOCK])
    bias = jnp.where((in_window[None] & key_ok)[:, None], alibi[None], -1e30)
    bias = bias.reshape(3, N_KV_HEADS, Q_GROUP, 3 * BLOCK, BLOCK).transpose(0, 1, 3, 2, 4)
    bias = bias.reshape(3, N_KV_HEADS, 3 * BLOCK, Q_GROUP * BLOCK)
    sink_row = jnp.repeat(sink.astype(f32).reshape(N_KV_HEADS, Q_GROUP), BLOCK, axis=1)[:, None, :]

    kernel = functools.partial(_ev_mix_kernel, S=S, tq=tq)
    return pl.pallas_call(
        kernel,
        grid=(B, nt),
        in_specs=[
            pl.BlockSpec((tq, D), main),
            pl.BlockSpec((None, 6, D), lambda b, i: (b, 0, 0)),
            pl.BlockSpec((tq, POOL_WIDTH), main),
            pl.BlockSpec((POOL_HALO, POOL_WIDTH), pprev),
            pl.BlockSpec((POOL_HALO, POOL_WIDTH), pnext),
            pl.BlockSpec((ATTN_WIDTH, tq), tmain),
            pl.BlockSpec((tq, KV_WIDTH), main),
            pl.BlockSpec((BLOCK, KV_WIDTH), kprev),
            pl.BlockSpec((BLOCK, KV_WIDTH), knext),
            pl.BlockSpec((KV_WIDTH, tq), tmain),
            pl.BlockSpec((KV_WIDTH, BLOCK), tprev),
            pl.BlockSpec((KV_WIDTH, BLOCK), tnext),
            pl.BlockSpec((3, N_KV_HEADS, 3 * BLOCK, Q_GROUP * BLOCK), lambda b, i: (0, 0, 0, 0)),
            pl.BlockSpec((N_KV_HEADS, 1, Q_GROUP * BLOCK), lambda b, i: (0, 0, 0)),
            pl.BlockSpec((len(POOL_WINDOWS), POOL_CH, POOL_CH), lambda b, i: (0, 0, 0)),
            pl.BlockSpec((1, POOL_WIDTH), const2),
            pl.BlockSpec((D, D), const2),
            pl.BlockSpec((1, D), const2),
            pl.BlockSpec((1, D), const2),
        ],
        out_specs=pl.BlockSpec((tq, D), main),
        out_shape=jax.ShapeDtypeStruct((T, D), f32),
        scratch_shapes=[
            pltpu.VMEM((tq + 2 * POOL_HALO, POOL_WIDTH), f32),
            pltpu.VMEM((tq + 2 * BLOCK, KV_WIDTH), bf16),
            pltpu.VMEM((KV_WIDTH, tq + 2 * BLOCK), bf16),
            pltpu.VMEM((ATTN_WIDTH, tq), f32),
            pltpu.VMEM((tq, D), bf16),
        ],
        compiler_params=pltpu.CompilerParams(vmem_limit_bytes=VMEM_LIMIT),
        name="ev_mix",
    )(x2d, mod_l, p, p, p, qt, k, k, k, vt, vt, vt, bias, sink_row, w_pool, pool_scale, w_out, ln_g, ln_b)


def _ffn_kernel(x_ref, mod_ref, wg_ref, wu_ref, wd_ref, lng_ref, lnb_ref, o_ref, h_ref, acc_ref):
    j = pl.program_id(1)

    @pl.when(j == 0)
    def _():
        h = x_ref[...] * (1.0 + mod_ref[4:5, :]) + mod_ref[3:4, :]
        h_ref[...] = h.astype(bf16)
        acc_ref[...] = jnp.zeros_like(acc_ref)

    h = h_ref[...]
    a = _silu(_dot(h, wg_ref[...])) * _dot(h, wu_ref[...])
    acc_ref[...] += _dot(a.astype(bf16), wd_ref[...])

    @pl.when(j == pl.num_programs(1) - 1)
    def _():
        xr = ALPHA * x_ref[...] + (1.0 + mod_ref[5:6, :]) * acc_ref[...]
        o_ref[...] = _layer_norm(xr, lng_ref[...], lnb_ref[...])


def _ffn(x2d, mod_l, w_gate, w_up, w_down, ln_g, ln_b, S):
    T = x2d.shape[0]
    tm = 1024
    tf = 256
    dff = w_gate.shape[1]
    tpb = S // tm
    return pl.pallas_call(
        _ffn_kernel,
        grid=(T // tm, dff // tf),
        in_specs=[
            pl.BlockSpec((tm, D), lambda i, j: (i, 0)),
            pl.BlockSpec((None, 6, D), lambda i, j: (i // tpb, 0, 0)),
            pl.BlockSpec((D, tf), lambda i, j: (0, j)),
            pl.BlockSpec((D, tf), lambda i, j: (0, j)),
            pl.BlockSpec((tf, D), lambda i, j: (j, 0)),
            pl.BlockSpec((1, D), lambda i, j: (0, 0)),
            pl.BlockSpec((1, D), lambda i, j: (0, 0)),
        ],
        out_specs=pl.BlockSpec((tm, D), lambda i, j: (i, 0)),
        out_shape=jax.ShapeDtypeStruct((T, D), f32),
        scratch_shapes=[pltpu.VMEM((tm, D), bf16), pltpu.VMEM((tm, D), f32)],
        compiler_params=pltpu.CompilerParams(vmem_limit_bytes=VMEM_LIMIT),
        name="ffn",
    )(x2d, mod_l, w_gate, w_up, w_down, ln_g, ln_b)


def _sg_kernel(x_ref, mod_ref, win_ref, sgg_ref, sgb_ref, ws_ref, bst_ref, wout_ref, lng_ref, lnb_ref,
               o_ref, gate_ref, *, tm):
    x = x_ref[...]
    h = x * (1.0 + mod_ref[1:2, :]) + mod_ref[0:1, :]
    z = _gelu_tanh(_dot(h.astype(bf16), win_ref[...]))
    u = z[:, :D]
    v = _layer_norm(z[:, D:], sgg_ref[...], sgb_ref[...]).astype(bf16)
    for n in range(tm // CHUNK):
        rows = slice(n * CHUNK, (n + 1) * CHUNK)
        for g in range(SG_GROUPS):
            cols = slice(g * SG_CH, (g + 1) * SG_CH)
            sv = _dot(ws_ref[g], v[rows, cols]) + bst_ref[:, g:g + 1]
            gate_ref[rows, cols] = (u[rows, cols] * sv).astype(bf16)
    y = _dot(gate_ref[...], wout_ref[...])
    xr = ALPHA * x + (1.0 + mod_ref[2:3, :]) * y
    o_ref[...] = _layer_norm(xr, lng_ref[...], lnb_ref[...])


def _sg_mix(x2d, mod_l, w_in, sg_g, sg_b, w_s, b_s_t, w_out, ln_g, ln_b, S):
    T = x2d.shape[0]
    tm = 512
    tpb = S // tm
    c2 = lambda i: (0, 0)
    return pl.pallas_call(
        functools.partial(_sg_kernel, tm=tm),
        grid=(T // tm,),
        in_specs=[
            pl.BlockSpec((tm, D), lambda i: (i, 0)),
            pl.BlockSpec((None, 6, D), lambda i: (i // tpb, 0, 0)),
            pl.BlockSpec((D, 2 * D), c2),
            pl.BlockSpec((1, D), c2),
            pl.BlockSpec((1, D), c2),
            pl.BlockSpec((SG_GROUPS, CHUNK, CHUNK), lambda i: (0, 0, 0)),
            pl.BlockSpec((CHUNK, SG_GROUPS), c2),
            pl.BlockSpec((D, D), c2),
            pl.BlockSpec((1, D), c2),
            pl.BlockSpec((1, D), c2),
        ],
        out_specs=pl.BlockSpec((tm, D), lambda i: (i, 0)),
        out_shape=jax.ShapeDtypeStruct((T, D), f32),
        scratch_shapes=[pltpu.VMEM((tm, D), bf16)],
        compiler_params=pltpu.CompilerParams(vmem_limit_bytes=VMEM_LIMIT),
        name="sg_mix",
    )(x2d, mod_l, w_in, sg_g, sg_b, w_s, b_s_t, w_out, ln_g, ln_b)


def _route_kernel(x_ref, mod_ref, wrt_ref, hs_ref, route_ref, cnt_ref):
    W = ROUTE_W
    h = x_ref[...] * (1.0 + mod_ref[4:5, :]) + mod_ref[3:4, :]
    h_hi, h_lo = _split_bf16(h)
    w_hi, w_lo = _split_bf16(wrt_ref[...])
    nt = (((1,), (1,)), ((), ()))
    logits = (lax.dot_general(w_hi, h_hi, nt, preferred_element_type=f32)
              + (lax.dot_general(w_hi, h_lo, nt, preferred_element_type=f32)
                 + lax.dot_general(w_lo, h_hi, nt, preferred_element_type=f32)))

    eidx = lax.broadcasted_iota(jnp.int32, (N_EXPERTS, W), 0)
    m1 = jnp.max(logits, axis=0, keepdims=True)
    i1 = jnp.min(jnp.where(logits == m1, eidx, N_EXPERTS), axis=0, keepdims=True)
    sel1 = eidx == i1
    rest = jnp.where(sel1, -jnp.inf, logits)
    m2 = jnp.max(rest, axis=0, keepdims=True)
    i2 = jnp.min(jnp.where(rest == m2, eidx, N_EXPERTS), axis=0, keepdims=True)
    sel2 = eidx == i2
    e2 = jnp.exp(m2 - m1)
    g1 = 1.0 / (1.0 + e2)
    g2 = e2 / (1.0 + e2)

    a1 = sel1.astype(f32)
    a2 = sel2.astype(f32)
    assign = a1 + a2
    counts = jnp.sum(assign, axis=1, keepdims=True)
    grans = jnp.ceil(counts * (1.0 / GRAN))
    sub = lax.broadcasted_iota(jnp.int32, (N_EXPERTS, 1), 0)
    seg = jnp.zeros((N_EXPERTS, 1), f32)
    for e in range(N_EXPERTS - 1):
        seg = seg + jnp.where(sub > e, grans[e:e + 1, :] * GRAN, 0.0)
    tr = lax.broadcasted_iota(jnp.int32, (W, W), 0)
    tc = lax.broadcasted_iota(jnp.int32, (W, W), 1)
    upper = (tr < tc).astype(bf16)
    rank = _dot(assign.astype(bf16), upper)
    slot = seg + rank
    pos1 = jnp.sum(a1 * slot, axis=0, keepdims=True)
    pos2 = jnp.sum(a2 * slot, axis=0, keepdims=True)

    srow = lax.broadcasted_iota(jnp.int32, (CHUNK_SLOTS, W), 0)
    perm = ((srow == pos1.astype(jnp.int32)) | (srow == pos2.astype(jnp.int32)))
    hs_ref[...] = _dot(perm.astype(f32).astype(bf16), h_hi).astype(bf16)

    ridx = lax.broadcasted_iota(jnp.int32, (8, W), 0)
    route = jnp.where(ridx == 0, pos1, jnp.where(ridx == 1, pos2, jnp.where(ridx == 2, g1, jnp.where(ridx == 3, g2, 0.0))))
    route_ref[...] = route
    cnt_ref[...] = jnp.broadcast_to(counts, (N_EXPERTS, 128)).astype(jnp.int32)


def _route(x2d, mod_l, w_router_t, S):
    T = x2d.shape[0]
    W = ROUTE_W
    nc = T // W
    tpb = S // W
    return pl.pallas_call(
        _route_kernel,
        grid=(nc,),
        in_specs=[
            pl.BlockSpec((W, D), lambda c: (c, 0)),
            pl.BlockSpec((None, 6, D), lambda c: (c // tpb, 0, 0)),
            pl.BlockSpec((N_EXPERTS, D), lambda c: (0, 0)),
        ],
        out_specs=[
            pl.BlockSpec((None, CHUNK_SLOTS, D), lambda c: (c, 0, 0)),
            pl.BlockSpec((None, 8, W), lambda c: (c, 0, 0)),
            pl.BlockSpec((None, N_EXPERTS, 128), lambda c: (c, 0, 0)),
        ],
        out_shape=[
            jax.ShapeDtypeStruct((nc, CHUNK_SLOTS, D), bf16),
            jax.ShapeDtypeStruct((nc, 8, W), f32),
            jax.ShapeDtypeStruct((nc, N_EXPERTS, 128), jnp.int32),
        ],
        compiler_params=pltpu.CompilerParams(vmem_limit_bytes=VMEM_LIMIT),
        name="route",
    )(x2d, mod_l, w_router_t)


def _granule_copy(src_ref, buf_ref, sem, idx_ref, base, g):
    row = pl.multiple_of(idx_ref[base + g] * GRAN, GRAN)
    return pltpu.make_async_copy(src_ref.at[pl.ds(row, GRAN), :], buf_ref.at[pl.ds(g * GRAN, GRAN), :], sem)


def _gather_start(src_ref, buf_ref, sem, idx_ref, base, n):
    for g in range(n):
        _granule_copy(src_ref, buf_ref, sem, idx_ref, base, g).start()


def _gather_wait(src_ref, buf_ref, sem, idx_ref, base, n):
    for g in range(n):
        _granule_copy(src_ref, buf_ref, sem, idx_ref, base, g).wait()


def _expert_kernel(te_ref, tv_ref, src_ref, hs_ref, wg_ref, wu_ref, wd_ref, o_ref,
                   xbuf_ref, sem, acc_ref, wgb_ref, wub_ref, wdb_ref):
    i = pl.program_id(0)
    j = pl.program_id(1)
    n_tiles = pl.num_programs(0)
    last = pl.num_programs(1) - 1
    valid = tv_ref[i] > 0
    slot = i % 2
    nxt = jnp.minimum(i + 1, n_tiles - 1)

    @pl.when(j == 0)
    def _():
        @pl.when(i == 0)
        def _():
            _gather_start(hs_ref, xbuf_ref.at[0], sem.at[0], src_ref, 0, TILE_GRANS)

        @pl.when(valid)
        def _():
            _gather_wait(hs_ref, xbuf_ref.at[slot], sem.at[slot], src_ref, i * TILE_GRANS, TILE_GRANS)

        @pl.when((i + 1 < n_tiles) & (tv_ref[nxt] > 0))
        def _():
            _gather_start(hs_ref, xbuf_ref.at[1 - slot], sem.at[1 - slot], src_ref, nxt * TILE_GRANS, TILE_GRANS)

    n_sub = tv_ref[i]
    full = n_sub == MOE_TM // MOE_SUB

    def round_weights():
        wgb_ref[...] = wg_ref[...].astype(bf16)
        wub_ref[...] = wu_ref[...].astype(bf16)
        wdb_ref[...] = wd_ref[...].astype(bf16)

    def swiglu(x):
        a = _silu(_dot(x, wgb_ref[...])) * _dot(x, wub_ref[...])
        return _dot(a.astype(bf16), wdb_ref[...])

    @pl.when(valid & (j == 0))
    def _():
        acc_ref[...] = jnp.zeros_like(acc_ref)

    @pl.when(full)
    def _():
        round_weights()
        acc_ref[...] += swiglu(xbuf_ref[slot])

    @pl.when(valid & jnp.logical_not(full))
    def _():
        round_weights()

        def body(sb, carry):
            rows = pl.ds(pl.multiple_of(sb * MOE_SUB, MOE_SUB), MOE_SUB)
            acc_ref[rows, :] += swiglu(xbuf_ref[slot, rows, :])
            return carry

        lax.fori_loop(0, n_sub, body, 0)

    @pl.when(j == last)
    def _():
        @pl.when(valid)
        def _():
            o_ref[...] = acc_ref[...].astype(o_ref.dtype)

        @pl.when(jnp.logical_not(valid))
        def _():
            o_ref[...] = jnp.zeros_like(o_ref)


def _experts(tile_expert, tile_valid, src_of_dst, hs2d, w_gate, w_up, w_down):
    n_tiles = tile_expert.shape[0]
    dff = w_gate.shape[2]
    nff = dff // MOE_TF

    def jj(j, tv, i):
        return jnp.where(tv[i] > 0, j, nff - 1)

    grid_spec = pltpu.PrefetchScalarGridSpec(
        num_scalar_prefetch=3,
        grid=(n_tiles, nff),
        in_specs=[
            pl.BlockSpec(memory_space=pl.ANY),
            pl.BlockSpec((None, D, MOE_TF), lambda i, j, te, tv, sd: (te[i], 0, jj(j, tv, i))),
            pl.BlockSpec((None, D, MOE_TF), lambda i, j, te, tv, sd: (te[i], 0, jj(j, tv, i))),
            pl.BlockSpec((None, MOE_TF, D), lambda i, j, te, tv, sd: (te[i], jj(j, tv, i), 0)),
        ],
        out_specs=pl.BlockSpec((MOE_TM, D), lambda i, j, te, tv, sd: (i, 0)),
        scratch_shapes=[
            pltpu.VMEM((2, MOE_TM, D), bf16),
            pltpu.SemaphoreType.DMA((2,)),
            pltpu.VMEM((MOE_TM, D), f32),
            pltpu.VMEM((D, MOE_TF), bf16),
            pltpu.VMEM((D, MOE_TF), bf16),
            pltpu.VMEM((MOE_TF, D), bf16),
        ],
    )
    return pl.pallas_call(
        _expert_kernel,
        grid_spec=grid_spec,
        out_shape=jax.ShapeDtypeStruct((n_tiles * MOE_TM, D), bf16),
        compiler_params=pltpu.CompilerParams(vmem_limit_bytes=VMEM_LIMIT_EXPERTS),
        name="experts",
    )(tile_expert, tile_valid, src_of_dst, hs2d, w_gate, w_up, w_down)


def _combine_kernel(ds_ref, x_ref, mod_ref, o_hbm_ref, rt_ref, lng_ref, lnb_ref, out_ref, obuf_ref, sem):
    W = ROUTE_W
    c = pl.program_id(0)
    nc = pl.num_programs(0)
    slot = c % 2
    nxt = jnp.minimum(c + 1, nc - 1)

    @pl.when(c == 0)
    def _():
        _gather_start(o_hbm_ref, obuf_ref.at[0], sem.at[0], ds_ref, 0, CHUNK_GRANS)

    _gather_wait(o_hbm_ref, obuf_ref.at[slot], sem.at[slot], ds_ref, c * CHUNK_GRANS, CHUNK_GRANS)

    @pl.when(c + 1 < nc)
    def _():
        _gather_start(o_hbm_ref, obuf_ref.at[1 - slot], sem.at[1 - slot], ds_ref, nxt * CHUNK_GRANS, CHUNK_GRANS)

    rt = rt_ref[...]
    scol = lax.broadcasted_iota(jnp.int32, (W, CHUNK_SLOTS), 1)
    p1 = (scol == rt[:, 0:1].astype(jnp.int32)).astype(f32).astype(bf16)
    p2 = (scol == rt[:, 1:2].astype(jnp.int32)).astype(f32).astype(bf16)
    osv = obuf_ref[slot]
    y = rt[:, 2:3] * _dot(p1, osv) + rt[:, 3:4] * _dot(p2, osv)
    xr = ALPHA * x_ref[...] + (1.0 + mod_ref[5:6, :]) * y
    out_ref[...] = _layer_norm(xr, lng_ref[...], lnb_ref[...])


def _combine(dst_of_src, x2d, mod_l, o2d, route_t, ln_g, ln_b, S):
    T = x2d.shape[0]
    W = ROUTE_W
    tpb = S // W
    grid_spec = pltpu.PrefetchScalarGridSpec(
        num_scalar_prefetch=1,
        grid=(T // W,),
        in_specs=[
            pl.BlockSpec((W, D), lambda c, ds: (c, 0)),
            pl.BlockSpec((None, 6, D), lambda c, ds: (c // tpb, 0, 0)),
            pl.BlockSpec(memory_space=pl.ANY),
            pl.BlockSpec((None, W, 8), lambda c, ds: (c, 0, 0)),
            pl.BlockSpec((1, D), lambda c, ds: (0, 0)),
            pl.BlockSpec((1, D), lambda c, ds: (0, 0)),
        ],
        out_specs=pl.BlockSpec((W, D), lambda c, ds: (c, 0)),
        scratch_shapes=[
            pltpu.VMEM((2, CHUNK_SLOTS, D), bf16),
            pltpu.SemaphoreType.DMA((2,)),
        ],
    )
    return pl.pallas_call(
        _combine_kernel,
        grid_spec=grid_spec,
        out_shape=jax.ShapeDtypeStruct((T, D), f32),
        compiler_params=pltpu.CompilerParams(vmem_limit_bytes=VMEM_LIMIT),
        name="combine",
    )(dst_of_src, x2d, mod_l, o2d, route_t, ln_g, ln_b)


def _routing_tables(counts, n_tiles):
    nc = counts.shape[0]
    gr = (counts + GRAN - 1) // GRAN
    seg_start = jnp.cumsum(gr, axis=1) - gr
    chunk_total = jnp.sum(gr, axis=1)
    prefix = jnp.cumsum(gr, axis=0) - gr
    g_e = jnp.sum(gr, axis=0)
    tiles_e = (g_e + TILE_GRANS - 1) // TILE_GRANS
    tile_end = jnp.cumsum(tiles_e)
    tile_start = tile_end - tiles_e
    total_tiles = tile_end[-1]

    i32 = jnp.int32
    er = jnp.arange(N_EXPERTS, dtype=i32)
    t = jnp.arange(n_tiles, dtype=i32)
    te = jnp.sum((t[:, None] >= tile_end[None, :]).astype(i32), axis=1)
    tile_valid = (t < total_tiles).astype(i32)
    last_e = jnp.sum((total_tiles - 1 >= tile_end).astype(i32))
    tile_expert = jnp.where(tile_valid > 0, jnp.minimum(te, N_EXPERTS - 1), last_e).astype(i32)
    oh_t = (tile_expert[:, None] == er).astype(i32)
    grans_left = jnp.sum(oh_t * (g_e - (t[:, None] - tile_start[None, :]) * TILE_GRANS), axis=1)
    tile_subs = tile_valid * jnp.clip((grans_left + SUB_GRANS - 1) // SUB_GRANS, 0, TILE_GRANS // SUB_GRANS)

    k = jnp.arange(CHUNK_GRANS, dtype=i32)
    seg_end = seg_start + gr
    e_of = jnp.sum((k[None, :, None] >= seg_end[:, None, :]).astype(i32), axis=2)
    oh_e = (jnp.minimum(e_of, N_EXPERTS - 1)[:, :, None] == er).astype(i32)
    base = tile_start[None, :] * TILE_GRANS + prefix - seg_start
    dst = jnp.sum(oh_e * base[:, None, :], axis=2) + k[None, :]
    valid_src = k[None, :] < chunk_total[:, None]
    dst_of_src = jnp.where(valid_src, dst, 0).astype(i32).reshape(-1)

    d = jnp.arange(n_tiles * TILE_GRANS, dtype=i32)
    oh_d = (jnp.repeat(tile_expert, TILE_GRANS)[:, None] == er).astype(i32)
    q = d - jnp.sum(oh_d * tile_start[None, :], axis=1) * TILE_GRANS
    incl_d = jnp.sum(oh_d[:, :, None] * (prefix + gr).T[None], axis=1)
    c_d = jnp.sum((q[:, None] >= incl_d).astype(i32), axis=1)
    oh_c = (jnp.minimum(c_d, nc - 1)[:, None] == jnp.arange(nc, dtype=i32)).astype(i32)
    cbase = jnp.arange(nc, dtype=i32)[:, None] * CHUNK_GRANS + seg_start - prefix
    sel = jnp.sum(oh_c[:, :, None] * oh_d[:, None, :] * cbase[None], axis=(1, 2))
    valid_dst = (jnp.repeat(tile_valid, TILE_GRANS) > 0) & (q >= 0) & (q < jnp.sum(oh_d * g_e[None, :], axis=1))
    src_of_dst = jnp.where(valid_dst, sel + q, 0).astype(i32)
    return tile_expert, tile_subs.astype(i32), src_of_dst, dst_of_src


def kernel(x, c, ada_w, ada_b, ln_g, ln_b, ev_w_in, ev_pool_w, ev_pool_scale, ev_sink, ev_w_out, od_w_in, od_sg_ln_g, od_sg_ln_b, od_w_s, od_b_s, od_w_out, ffn_w_gate, ffn_w_up, ffn_w_down, moe_w_router, moe_w_gate, moe_w_up, moe_w_down):
    B, S, _ = x.shape
    T = B * S
    x2d = x.reshape(T, D)
    mod = _adaln(c, ada_w, ada_b)

    w_in = ev_w_in[0].astype(bf16)
    q0, k0, v0 = POOL_WIDTH, POOL_WIDTH + ATTN_WIDTH, POOL_WIDTH + ATTN_WIDTH + KV_WIDTH
    w_pk = jnp.concatenate([w_in[:, :q0], w_in[:, k0:v0]], axis=1)
    w_qv_t = jnp.concatenate([w_in[:, q0:k0], w_in[:, v0:]], axis=1).T
    p, k, qt, vt = _ev_in(x2d, mod[0], w_pk, w_qv_t, S)
    x2d = _ev_mix(x2d, mod[0], p, qt, k, vt, ev_pool_w[0].astype(bf16), ev_pool_scale[0][None, :],
                  ev_sink[0], ev_w_out[0].astype(bf16), ln_g[0, 0][None, :], ln_b[0, 0][None, :], B, S)
    x2d = _ffn(x2d, mod[0], ffn_w_gate[0].astype(bf16), ffn_w_up[0].astype(bf16),
               ffn_w_down[0].astype(bf16), ln_g[0, 1][None, :], ln_b[0, 1][None, :], S)

    x2d = _sg_mix(x2d, mod[1], od_w_in[0].astype(bf16), od_sg_ln_g[0][None, :], od_sg_ln_b[0][None, :],
                  od_w_s[0].astype(bf16), od_b_s[0].T, od_w_out[0].astype(bf16),
                  ln_g[1, 0][None, :], ln_b[1, 0][None, :], S)

    hs, route, cnt = _route(x2d, mod[1], moe_w_router[0].T, S)
    nc = T // ROUTE_W
    n_tiles = (nc * CHUNK_GRANS) // TILE_GRANS + N_EXPERTS
    tile_expert, tile_subs, src_of_dst, dst_of_src = _routing_tables(cnt[:, :, 0], n_tiles)
    o = _experts(tile_expert, tile_subs, src_of_dst, hs.reshape(nc * CHUNK_SLOTS, D),
                 moe_w_gate[0], moe_w_up[0], moe_w_down[0])
    x2d = _combine(dst_of_src, x2d, mod[1], o, jnp.swapaxes(route, 1, 2),
                   ln_g[1, 1][None, :], ln_b[1, 1][None, :], S)
    return x2d.reshape(B, S, D)
```

```python
import functools
import math

import jax
import jax.numpy as jnp
from jax import lax
from jax.experimental import pallas as pl
from jax.experimental.pallas import tpu as pltpu

D = 1024
DEPTH = 2
ALPHA = (2.0 * DEPTH) ** 0.25
LN_EPS = 1e-5

POOL_WINDOWS = (2, 4, 8, 16)
POOL_CH = 128
POOL_WIDTH = 512
HEAD_DIM = 64
N_Q_HEADS = 8
N_KV_HEADS = 2
Q_GROUP = 4
ATTN_WIDTH = 512
KV_WIDTH = 128
BLOCK = 128
EVEN_IN = 1280
POOL_HALO = 8

CHUNK = 128
SG_GROUPS = 8
SG_CH = 128

N_EXPERTS = 8

ROUTE_W = 512
GRAN = 16
CHUNK_SLOTS = 2 * ROUTE_W + N_EXPERTS * GRAN
CHUNK_GRANS = CHUNK_SLOTS // GRAN
MOE_TM = 2048
MOE_SUB = 256
TILE_GRANS = MOE_TM // GRAN
SUB_GRANS = MOE_SUB // GRAN
MOE_TF = 512

VMEM_LIMIT = 48 * 1024 * 1024
VMEM_LIMIT_BIG = 56 * 1024 * 1024

bf16 = jnp.bfloat16
f32 = jnp.float32


def _dot(a, b):
    return jnp.dot(a, b, preferred_element_type=f32)


def _split_bf16(a):
    hi = a.astype(bf16)
    lo = (a - hi.astype(f32)).astype(bf16)
    return hi, lo


def _layer_norm(x, g, b):
    mu = jnp.mean(x, axis=-1, keepdims=True)
    xc = x - mu
    var = jnp.mean(xc * xc, axis=-1, keepdims=True)
    return xc * lax.rsqrt(var + LN_EPS) * g + b


def _silu(x):
    return x * jax.nn.sigmoid(x)


def _gelu_tanh(x):
    c = math.sqrt(2.0 / math.pi)
    return x * (0.5 * (1.0 + jnp.tanh(c * (x + 0.044715 * (x * x * x)))))


def _adaln_kernel(c_ref, w_ref, b_ref, o_ref):
    cond = _silu(c_ref[...])
    c_hi, c_lo = _split_bf16(cond)
    w_hi, w_lo = _split_bf16(w_ref[...])
    acc = _dot(c_hi, w_hi) + (_dot(c_lo, w_hi) + _dot(c_hi, w_lo))
    o_ref[...] = acc + b_ref[...]


def _adaln(c, ada_w, ada_b):
    B = c.shape[0]
    tn = 1024
    c_pad = jnp.zeros((8, D), f32).at[:B].set(c)
    out = pl.pallas_call(
        _adaln_kernel,
        grid=(DEPTH, 6 * D // tn),
        in_specs=[
            pl.BlockSpec((8, D), lambda l, j: (0, 0)),
            pl.BlockSpec((None, D, tn), lambda l, j: (l, 0, j)),
            pl.BlockSpec((None, 1, tn), lambda l, j: (l, 0, j)),
        ],
        out_specs=pl.BlockSpec((None, 8, tn), lambda l, j: (l, 0, j)),
        out_shape=jax.ShapeDtypeStruct((DEPTH, 8, 6 * D), f32),
        compiler_params=pltpu.CompilerParams(vmem_limit_bytes=VMEM_LIMIT),
        name="adaln",
    )(c_pad, ada_w, ada_b.reshape(DEPTH, 1, 6 * D))
    return out[:, :B].reshape(DEPTH, B, 6, D)


def _ev_in_kernel(x_ref, mod_ref, wpk_ref, wqvt_ref, p_ref, k_ref, qt_ref, vt_ref):
    h = (x_ref[...] * (1.0 + mod_ref[1:2, :]) + mod_ref[0:1, :]).astype(bf16)
    zpk = _dot(h, wpk_ref[...])
    p_ref[...] = zpk[:, :POOL_WIDTH]
    k_ref[...] = zpk[:, POOL_WIDTH:].astype(bf16)
    zt = lax.dot_general(wqvt_ref[...], h, (((1,), (1,)), ((), ())), preferred_element_type=f32)
    qt_ref[...] = (zt[:ATTN_WIDTH] * (HEAD_DIM ** -0.5)).astype(bf16)
    vt_ref[...] = zt[ATTN_WIDTH:].astype(bf16)


def _ev_in(x2d, mod_l, w_pk, w_qv_t, S):
    T = x2d.shape[0]
    tm = 512
    tpb = S // tm
    return pl.pallas_call(
        _ev_in_kernel,
        grid=(T // tm,),
        in_specs=[
            pl.BlockSpec((tm, D), lambda i: (i, 0)),
            pl.BlockSpec((None, 6, D), lambda i: (i // tpb, 0, 0)),
            pl.BlockSpec((D, POOL_WIDTH + KV_WIDTH), lambda i: (0, 0)),
            pl.BlockSpec((ATTN_WIDTH + KV_WIDTH, D), lambda i: (0, 0)),
        ],
        out_specs=[
            pl.BlockSpec((tm, POOL_WIDTH), lambda i: (i, 0)),
            pl.BlockSpec((tm, KV_WIDTH), lambda i: (i, 0)),
            pl.BlockSpec((ATTN_WIDTH, tm), lambda i: (0, i)),
            pl.BlockSpec((KV_WIDTH, tm), lambda i: (0, i)),
        ],
        out_shape=[
            jax.ShapeDtypeStruct((T, POOL_WIDTH), f32),
            jax.ShapeDtypeStruct((T, KV_WIDTH), bf16),
            jax.ShapeDtypeStruct((ATTN_WIDTH, T), bf16),
            jax.ShapeDtypeStruct((KV_WIDTH, T), bf16),
        ],
        compiler_params=pltpu.CompilerParams(vmem_limit_bytes=VMEM_LIMIT),
        name="ev_in",
    )(x2d, mod_l, w_pk, w_qv_t)


def _ev_mix_kernel(x_ref, mod_ref, p_ref, pp_ref, pn_ref, qt_ref,
                   k_ref, kp_ref, kn_ref, vt_ref, vtp_ref, vtn_ref,
                   bias_ref, sink_ref, wpool_ref, pscale_ref, wout_ref, lng_ref, lnb_ref,
                   o_ref, pext_ref, pooled_ref, kext_ref, vext_ref, ybt_ref, mix_ref, *, S, tq):
    i = pl.program_id(1)
    n_tiles = S // tq
    is_first = i == 0
    is_last = i == n_tiles - 1
    H = POOL_HALO

    p = p_ref[...]
    pext_ref[0:H, :] = jnp.where(is_first, 0.0, pp_ref[...])
    pext_ref[H:H + tq, :] = p
    pext_ref[H + tq:, :] = jnp.where(is_last, 0.0, pn_ref[...])
    near = lax.broadcasted_iota(jnp.int32, (H, 1), 0)
    for g, w in enumerate(POOL_WINDOWS):
        cs = slice(g * POOL_CH, (g + 1) * POOL_CH)
        r = w // 2
        wsum = pext_ref[H - r:H - r + tq, cs]
        for d in range(-r + 1, r + 1):
            wsum = wsum + pext_ref[H + d:H + d + tq, cs]
        pooled_ref[:, cs] = wsum / float(w + 1) - p[:, cs]
        r = w // 2
        cnt_head = (jnp.minimum(near, r) + (r + 1)).astype(f32)
        cnt_tail = (jnp.minimum(H - 1 - near, r) + (r + 1)).astype(f32)
        cnt_head = jnp.where(is_first, cnt_head, float(w + 1))
        cnt_tail = jnp.where(is_last, cnt_tail, float(w + 1))
        pooled_ref[0:H, cs] = wsum[0:H] / cnt_head - p[0:H, cs]
        pooled_ref[tq - H:tq, cs] = wsum[tq - H:tq] / cnt_tail - p[tq - H:tq, cs]
        ya = _dot(pooled_ref[:, cs].astype(bf16), wpool_ref[g])
        mix_ref[:, cs] = (ya * pscale_ref[:, cs]).astype(bf16)

    kext_ref[0:BLOCK, :] = kp_ref[...]
    kext_ref[BLOCK:BLOCK + tq, :] = k_ref[...]
    kext_ref[BLOCK + tq:, :] = kn_ref[...]
    vext_ref[:, 0:BLOCK] = vtp_ref[...]
    vext_ref[:, BLOCK:BLOCK + tq] = vt_ref[...]
    vext_ref[:, BLOCK + tq:] = vtn_ref[...]

    n_blocks = S // BLOCK
    zeros_q = jnp.zeros((HEAD_DIM, Q_GROUP * BLOCK), bf16)
    for n in range(tq // BLOCK):
        gb = i * (tq // BLOCK) + n
        variant = jnp.where(gb == 0, 1, jnp.where(gb == n_blocks - 1, 2, 0))
        cols = slice(n * BLOCK, (n + 1) * BLOCK)
        kw = kext_ref[n * BLOCK:n * BLOCK + 3 * BLOCK, :]
        for kvh in range(N_KV_HEADS):
            qst = jnp.concatenate(
                [qt_ref[(kvh * Q_GROUP + gq) * HEAD_DIM:(kvh * Q_GROUP + gq + 1) * HEAD_DIM, cols]
                 for gq in range(Q_GROUP)], axis=1)
            qst = jnp.concatenate([qst, zeros_q] if kvh == 0 else [zeros_q, qst], axis=0)
            s = _dot(kw, qst) + bias_ref[variant, kvh]
            sink = sink_ref[kvh]
            m = jnp.maximum(jnp.max(s, axis=0, keepdims=True), sink)
            e = jnp.exp(s - m)
            denom = jnp.sum(e, axis=0, keepdims=True) + jnp.exp(sink - m)
            vwt = vext_ref[kvh * HEAD_DIM:(kvh + 1) * HEAD_DIM, n * BLOCK:n * BLOCK + 3 * BLOCK]
            out = _dot(vwt, e.astype(bf16)) / denom
            for gq in range(Q_GROUP):
                hq = kvh * Q_GROUP + gq
                ybt_ref[hq * HEAD_DIM:(hq + 1) * HEAD_DIM, cols] = out[:, gq * BLOCK:(gq + 1) * BLOCK]
    mix_ref[:, POOL_WIDTH:] = ybt_ref[...].T.astype(bf16)

    y = _dot(mix_ref[...], wout_ref[...])
    xr = ALPHA * x_ref[...] + (1.0 + mod_ref[2:3, :]) * y
    o_ref[...] = _layer_norm(xr, lng_ref[...], lnb_ref[...])


def _ev_mix(x2d, mod_l, p, qt, k, vt, w_pool, pool_scale, sink, w_out, ln_g, ln_b, B, S):
    T = x2d.shape[0]
    tq = 512
    nt = S // tq
    kb = tq // BLOCK
    pb = tq // POOL_HALO
    n_kblocks = T // BLOCK
    n_pblocks = T // POOL_HALO

    def main(b, i): return (b * nt + i, 0)
    def kprev(b, i): return (jnp.maximum((b * nt + i) * kb - 1, 0), 0)
    def knext(b, i): return (jnp.minimum((b * nt + i + 1) * kb, n_kblocks - 1), 0)
    def pprev(b, i): return (jnp.maximum((b * nt + i) * pb - 1, 0), 0)
    def pnext(b, i): return (jnp.minimum((b * nt + i + 1) * pb, n_pblocks - 1), 0)
    def const2(b, i): return (0, 0)

    def tmain(b, i): return (0, b * nt + i)
    def tprev(b, i): return (0, jnp.maximum((b * nt + i) * kb - 1, 0))
    def tnext(b, i): return (0, jnp.minimum((b * nt + i + 1) * kb, n_kblocks - 1))

    assert S // BLOCK >= 2
    kj = jnp.arange(3 * BLOCK)[:, None]
    qi = jnp.arange(BLOCK)[None, :]
    dist = jnp.abs(kj - BLOCK - qi)
    slopes = 2.0 ** (-8.0 * jnp.arange(1, N_Q_HEADS + 1, dtype=f32) / N_Q_HEADS)
    alibi = -slopes[:, None, None] * dist.astype(f32)[None]
    in_window = dist <= BLOCK
    key_ok = jnp.stack([kj >= 0, kj >= BLOCK, kj < 2 * BLOCK])
    bias = jnp.where((in_window[None] & key_ok)[:, None], alibi[None], -1e30)
    bias = bias.reshape(3, N_KV_HEADS, Q_GROUP, 3 * BLOCK, BLOCK).transpose(0, 1, 3, 2, 4)
    bias = bias.reshape(3, N_KV_HEADS, 3 * BLOCK, Q_GROUP * BLOCK)
    sink_row = jnp.repeat(sink.astype(f32).reshape(N_KV_HEADS, Q_GROUP), BLOCK, axis=1)[:, None, :]

    assert tq >= 2 * POOL_HALO and max(POOL_WINDOWS) // 2 <= POOL_HALO and nt >= 2

    kernel = functools.partial(_ev_mix_kernel, S=S, tq=tq)
    return pl.pallas_call(
        kernel,
        grid=(B, nt),
        in_specs=[
            pl.BlockSpec((tq, D), main),
            pl.BlockSpec((None, 6, D), lambda b, i: (b, 0, 0)),
            pl.BlockSpec((tq, POOL_WIDTH), main),
            pl.BlockSpec((POOL_HALO, POOL_WIDTH), pprev),
            pl.BlockSpec((POOL_HALO, POOL_WIDTH), pnext),
            pl.BlockSpec((ATTN_WIDTH, tq), tmain),
            pl.BlockSpec((tq, KV_WIDTH), main),
            pl.BlockSpec((BLOCK, KV_WIDTH), kprev),
            pl.BlockSpec((BLOCK, KV_WIDTH), knext),
            pl.BlockSpec((KV_WIDTH, tq), tmain),
            pl.BlockSpec((KV_WIDTH, BLOCK), tprev),
            pl.BlockSpec((KV_WIDTH, BLOCK), tnext),
            pl.BlockSpec((3, N_KV_HEADS, 3 * BLOCK, Q_GROUP * BLOCK), lambda b, i: (0, 0, 0, 0)),
            pl.BlockSpec((N_KV_HEADS, 1, Q_GROUP * BLOCK), lambda b, i: (0, 0, 0)),
            pl.BlockSpec((len(POOL_WINDOWS), POOL_CH, POOL_CH), lambda b, i: (0, 0, 0)),
            pl.BlockSpec((1, POOL_WIDTH), const2),
            pl.BlockSpec((D, D), const2),
            pl.BlockSpec((1, D), const2),
            pl.BlockSpec((1, D), const2),
        ],
        out_specs=pl.BlockSpec((tq, D), main),
        out_shape=jax.ShapeDtypeStruct((T, D), f32),
        scratch_shapes=[
            pltpu.VMEM((tq + 2 * POOL_HALO, POOL_WIDTH), f32),
            pltpu.VMEM((tq, POOL_WIDTH), f32),
            pltpu.VMEM((tq + 2 * BLOCK, KV_WIDTH), bf16),
            pltpu.VMEM((KV_WIDTH, tq + 2 * BLOCK), bf16),
            pltpu.VMEM((ATTN_WIDTH, tq), f32),
            pltpu.VMEM((tq, D), bf16),
        ],
        compiler_params=pltpu.CompilerParams(vmem_limit_bytes=VMEM_LIMIT),
        name="ev_mix",
    )(x2d, mod_l, p, p, p, qt, k, k, k, vt, vt, vt, bias, sink_row, w_pool, pool_scale, w_out, ln_g, ln_b)


def _load_rounded(w_hbm_ref, w_ref, stage_ref, sem):
    rows = stage_ref.shape[1]
    n = w_hbm_ref.shape[0] // rows

    def copy(c):
        return pltpu.make_async_copy(w_hbm_ref.at[pl.ds(c * rows, rows), :], stage_ref.at[c % 2], sem.at[c % 2])

    copy(0).start()
    for c in range(n):
        if c + 1 < n:
            copy(c + 1).start()
        copy(c).wait()
        w_ref[pl.ds(c * rows, rows), :] = stage_ref[c % 2].astype(bf16)


def _ffn_kernel(x_ref, mod_ref, wg_hbm_ref, wu_hbm_ref, wd_hbm_ref, lng_ref, lnb_ref, o_ref,
                wg_ref, wu_ref, wd_ref, stage_in_ref, stage_out_ref, sem):
    @pl.when(pl.program_id(0) == 0)
    def _():
        _load_rounded(wg_hbm_ref, wg_ref, stage_in_ref, sem)
        _load_rounded(wu_hbm_ref, wu_ref, stage_in_ref, sem)
        _load_rounded(wd_hbm_ref, wd_ref, stage_out_ref, sem)

    x = x_ref[...]
    h = (x * (1.0 + mod_ref[4:5, :]) + mod_ref[3:4, :]).astype(bf16)
    a = _silu(_dot(h, wg_ref[...])) * _dot(h, wu_ref[...])
    y = _dot(a.astype(bf16), wd_ref[...])
    xr = ALPHA * x + (1.0 + mod_ref[5:6, :]) * y
    o_ref[...] = _layer_norm(xr, lng_ref[...], lnb_ref[...])


def _ffn(x2d, mod_l, w_gate, w_up, w_down, ln_g, ln_b, S):
    T = x2d.shape[0]
    tm = 512
    dff = w_gate.shape[1]
    tpb = S // tm
    n_stage = 8
    return pl.pallas_call(
        _ffn_kernel,
        grid=(T // tm,),
        in_specs=[
            pl.BlockSpec((tm, D), lambda i: (i, 0)),
            pl.BlockSpec((None, 6, D), lambda i: (i // tpb, 0, 0)),
            pl.BlockSpec(memory_space=pl.ANY),
            pl.BlockSpec(memory_space=pl.ANY),
            pl.BlockSpec(memory_space=pl.ANY),
            pl.BlockSpec((1, D), lambda i: (0, 0)),
            pl.BlockSpec((1, D), lambda i: (0, 0)),
        ],
        out_specs=pl.BlockSpec((tm, D), lambda i: (i, 0)),
        out_shape=jax.ShapeDtypeStruct((T, D), f32),
        scratch_shapes=[
            pltpu.VMEM((D, dff), bf16),
            pltpu.VMEM((D, dff), bf16),
            pltpu.VMEM((dff, D), bf16),
            pltpu.VMEM((2, D // n_stage, dff), f32),
            pltpu.VMEM((2, dff // n_stage, D), f32),
            pltpu.SemaphoreType.DMA((2,)),
        ],
        compiler_params=pltpu.CompilerParams(vmem_limit_bytes=VMEM_LIMIT_BIG),
        name="ffn",
    )(x2d, mod_l, w_gate, w_up, w_down, ln_g, ln_b)


def _sg_kernel(x_ref, mod_ref, win_ref, sgg_ref, sgb_ref, ws_ref, bst_ref, wout_ref, lng_ref, lnb_ref,
               o_ref, gate_ref, *, tm):
    x = x_ref[...]
    h = x * (1.0 + mod_ref[1:2, :]) + mod_ref[0:1, :]
    z = _gelu_tanh(_dot(h.astype(bf16), win_ref[...]))
    u = z[:, :D]
    v = _layer_norm(z[:, D:], sgg_ref[...], sgb_ref[...]).astype(bf16)
    for n in range(tm // CHUNK):
        rows = slice(n * CHUNK, (n + 1) * CHUNK)
        for g in range(SG_GROUPS):
            cols = slice(g * SG_CH, (g + 1) * SG_CH)
            sv = _dot(ws_ref[g], v[rows, cols]) + bst_ref[:, g:g + 1]
            gate_ref[rows, cols] = (u[rows, cols] * sv).astype(bf16)
    y = _dot(gate_ref[...], wout_ref[...])
    xr = ALPHA * x + (1.0 + mod_ref[2:3, :]) * y
    o_ref[...] = _layer_norm(xr, lng_ref[...], lnb_ref[...])


def _sg_mix(x2d, mod_l, w_in, sg_g, sg_b, w_s, b_s_t, w_out, ln_g, ln_b, S):
    T = x2d.shape[0]
    tm = 512
    tpb = S // tm
    c2 = lambda i: (0, 0)
    return pl.pallas_call(
        functools.partial(_sg_kernel, tm=tm),
        grid=(T // tm,),
        in_specs=[
            pl.BlockSpec((tm, D), lambda i: (i, 0)),
            pl.BlockSpec((None, 6, D), lambda i: (i // tpb, 0, 0)),
            pl.BlockSpec((D, 2 * D), c2),
            pl.BlockSpec((1, D), c2),
            pl.BlockSpec((1, D), c2),
            pl.BlockSpec((SG_GROUPS, CHUNK, CHUNK), lambda i: (0, 0, 0)),
            pl.BlockSpec((CHUNK, SG_GROUPS), c2),
            pl.BlockSpec((D, D), c2),
            pl.BlockSpec((1, D), c2),
            pl.BlockSpec((1, D), c2),
        ],
        out_specs=pl.BlockSpec((tm, D), lambda i: (i, 0)),
        out_shape=jax.ShapeDtypeStruct((T, D), f32),
        scratch_shapes=[pltpu.VMEM((tm, D), bf16)],
        compiler_params=pltpu.CompilerParams(vmem_limit_bytes=VMEM_LIMIT),
        name="sg_mix",
    )(x2d, mod_l, w_in, sg_g, sg_b, w_s, b_s_t, w_out, ln_g, ln_b)


def _route_kernel(x_ref, mod_ref, wrt_ref, hs_ref, route_ref, cnt_ref):
    W = ROUTE_W
    h = x_ref[...] * (1.0 + mod_ref[4:5, :]) + mod_ref[3:4, :]
    h_hi, h_lo = _split_bf16(h)
    w_hi, w_lo = _split_bf16(wrt_ref[...])
    nt = (((1,), (1,)), ((), ()))
    logits = (lax.dot_general(w_hi, h_hi, nt, preferred_element_type=f32)
              + (lax.dot_general(w_hi, h_lo, nt, preferred_element_type=f32)
                 + lax.dot_general(w_lo, h_hi, nt, preferred_element_type=f32)))

    eidx = lax.broadcasted_iota(jnp.int32, (N_EXPERTS, W), 0)
    m1 = jnp.max(logits, axis=0, keepdims=True)
    i1 = jnp.min(jnp.where(logits == m1, eidx, N_EXPERTS), axis=0, keepdims=True)
    sel1 = eidx == i1
    rest = jnp.where(sel1, -jnp.inf, logits)
    m2 = jnp.max(rest, axis=0, keepdims=True)
    i2 = jnp.min(jnp.where(rest == m2, eidx, N_EXPERTS), axis=0, keepdims=True)
    sel2 = eidx == i2
    e2 = jnp.exp(m2 - m1)
    g1 = 1.0 / (1.0 + e2)
    g2 = e2 / (1.0 + e2)

    a1 = sel1.astype(f32)
    a2 = sel2.astype(f32)
    assign = a1 + a2
    counts = jnp.sum(assign, axis=1, keepdims=True)
    grans = jnp.ceil(counts * (1.0 / GRAN))
    sub = lax.broadcasted_iota(jnp.int32, (N_EXPERTS, 1), 0)
    seg = jnp.zeros((N_EXPERTS, 1), f32)
    for e in range(N_EXPERTS - 1):
        seg = seg + jnp.where(sub > e, grans[e:e + 1, :] * GRAN, 0.0)
    tr = lax.broadcasted_iota(jnp.int32, (W, W), 0)
    tc = lax.broadcasted_iota(jnp.int32, (W, W), 1)
    upper = (tr < tc).astype(bf16)
    rank = _dot(assign.astype(bf16), upper)
    slot = seg + rank
    pos1 = jnp.sum(a1 * slot, axis=0, keepdims=True)
    pos2 = jnp.sum(a2 * slot, axis=0, keepdims=True)

    srow = lax.broadcasted_iota(jnp.int32, (CHUNK_SLOTS, W), 0)
    perm = ((srow == pos1.astype(jnp.int32)) | (srow == pos2.astype(jnp.int32)))
    hs_ref[...] = _dot(perm.astype(f32).astype(bf16), h_hi).astype(bf16)

    ridx = lax.broadcasted_iota(jnp.int32, (8, W), 0)
    route = jnp.where(ridx == 0, pos1, jnp.where(ridx == 1, pos2, jnp.where(ridx == 2, g1, jnp.where(ridx == 3, g2, 0.0))))
    route_ref[...] = route
    cnt_ref[...] = jnp.broadcast_to(counts, (N_EXPERTS, 128)).astype(jnp.int32)


def _route(x2d, mod_l, w_router_t, S):
    T = x2d.shape[0]
    W = ROUTE_W
    nc = T // W
    tpb = S // W
    return pl.pallas_call(
        _route_kernel,
        grid=(nc,),
        in_specs=[
            pl.BlockSpec((W, D), lambda c: (c, 0)),
            pl.BlockSpec((None, 6, D), lambda c: (c // tpb, 0, 0)),
            pl.BlockSpec((N_EXPERTS, D), lambda c: (0, 0)),
        ],
        out_specs=[
            pl.BlockSpec((None, CHUNK_SLOTS, D), lambda c: (c, 0, 0)),
            pl.BlockSpec((None, 8, W), lambda c: (c, 0, 0)),
            pl.BlockSpec((None, N_EXPERTS, 128), lambda c: (c, 0, 0)),
        ],
        out_shape=[
            jax.ShapeDtypeStruct((nc, CHUNK_SLOTS, D), bf16),
            jax.ShapeDtypeStruct((nc, 8, W), f32),
            jax.ShapeDtypeStruct((nc, N_EXPERTS, 128), jnp.int32),
        ],
        compiler_params=pltpu.CompilerParams(vmem_limit_bytes=VMEM_LIMIT),
        name="route",
    )(x2d, mod_l, w_router_t)


def _granule_copy(src_ref, buf_ref, sem, idx_ref, base, g):
    row = pl.multiple_of(idx_ref[base + g] * GRAN, GRAN)
    return pltpu.make_async_copy(src_ref.at[pl.ds(row, GRAN), :], buf_ref.at[pl.ds(g * GRAN, GRAN), :], sem)


def _gather_start(src_ref, buf_ref, sem, idx_ref, base, n):
    for g in range(n):
        _granule_copy(src_ref, buf_ref, sem, idx_ref, base, g).start()


def _gather_wait(src_ref, buf_ref, sem, idx_ref, base, n):
    for g in range(n):
        _granule_copy(src_ref, buf_ref, sem, idx_ref, base, g).wait()


def _expert_kernel(te_ref, tv_ref, src_ref, hs_ref, wg_ref, wu_ref, wd_ref, o_ref,
                   xbuf_ref, sem, acc_ref, wgb_ref, wub_ref, wdb_ref):
    i = pl.program_id(0)
    j = pl.program_id(1)
    n_tiles = pl.num_programs(0)
    last = pl.num_programs(1) - 1
    valid = tv_ref[i] > 0
    slot = i % 2
    nxt = jnp.minimum(i + 1, n_tiles - 1)

    @pl.when(j == 0)
    def _():
        @pl.when(i == 0)
        def _():
            _gather_start(hs_ref, xbuf_ref.at[0], sem.at[0], src_ref, 0, TILE_GRANS)

        @pl.when(valid)
        def _():
            _gather_wait(hs_ref, xbuf_ref.at[slot], sem.at[slot], src_ref, i * TILE_GRANS, TILE_GRANS)

        @pl.when((i + 1 < n_tiles) & (tv_ref[nxt] > 0))
        def _():
            _gather_start(hs_ref, xbuf_ref.at[1 - slot], sem.at[1 - slot], src_ref, nxt * TILE_GRANS, TILE_GRANS)

    n_sub = tv_ref[i]
    full = n_sub == MOE_TM // MOE_SUB

    def round_weights():
        wgb_ref[...] = wg_ref[...].astype(bf16)
        wub_ref[...] = wu_ref[...].astype(bf16)
        wdb_ref[...] = wd_ref[...].astype(bf16)

    def swiglu(x):
        a = _silu(_dot(x, wgb_ref[...])) * _dot(x, wub_ref[...])
        return _dot(a.astype(bf16), wdb_ref[...])

    @pl.when(valid & (j == 0))
    def _():
        acc_ref[...] = jnp.zeros_like(acc_ref)

    @pl.when(full)
    def _():
        round_weights()
        acc_ref[...] += swiglu(xbuf_ref[slot])

    @pl.when(valid & jnp.logical_not(full))
    def _():
        round_weights()

        def body(sb, carry):
            rows = pl.ds(pl.multiple_of(sb * MOE_SUB, MOE_SUB), MOE_SUB)
            acc_ref[rows, :] += swiglu(xbuf_ref[slot, rows, :])
            return carry

        lax.fori_loop(0, n_sub, body, 0)

    @pl.when(j == last)
    def _():
        @pl.when(valid)
        def _():
            o_ref[...] = acc_ref[...].astype(o_ref.dtype)

        @pl.when(jnp.logical_not(valid))
        def _():
            o_ref[...] = jnp.zeros_like(o_ref)


def _experts(tile_expert, tile_valid, src_of_dst, hs2d, w_gate, w_up, w_down):
    n_tiles = tile_expert.shape[0]
    dff = w_gate.shape[2]
    nff = dff // MOE_TF

    def jj(j, tv, i):
        return jnp.where(tv[i] > 0, j, nff - 1)

    grid_spec = pltpu.PrefetchScalarGridSpec(
        num_scalar_prefetch=3,
        grid=(n_tiles, nff),
        in_specs=[
            pl.BlockSpec(memory_space=pl.ANY),
            pl.BlockSpec((None, D, MOE_TF), lambda i, j, te, tv, sd: (te[i], 0, jj(j, tv, i))),
            pl.BlockSpec((None, D, MOE_TF), lambda i, j, te, tv, sd: (te[i], 0, jj(j, tv, i))),
            pl.BlockSpec((None, MOE_TF, D), lambda i, j, te, tv, sd: (te[i], jj(j, tv, i), 0)),
        ],
        out_specs=pl.BlockSpec((MOE_TM, D), lambda i, j, te, tv, sd: (i, 0)),
        scratch_shapes=[
            pltpu.VMEM((2, MOE_TM, D), bf16),
            pltpu.SemaphoreType.DMA((2,)),
            pltpu.VMEM((MOE_TM, D), f32),
            pltpu.VMEM((D, MOE_TF), bf16),
            pltpu.VMEM((D, MOE_TF), bf16),
            pltpu.VMEM((MOE_TF, D), bf16),
        ],
    )
    return pl.pallas_call(
        _expert_kernel,
        grid_spec=grid_spec,
        out_shape=jax.ShapeDtypeStruct((n_tiles * MOE_TM, D), bf16),
        compiler_params=pltpu.CompilerParams(vmem_limit_bytes=VMEM_LIMIT_BIG),
        name="experts",
    )(tile_expert, tile_valid, src_of_dst, hs2d, w_gate, w_up, w_down)


def _combine_kernel(ds_ref, x_ref, mod_ref, o_hbm_ref, rt_ref, lng_ref, lnb_ref, out_ref, obuf_ref, sem):
    W = ROUTE_W
    c = pl.program_id(0)
    nc = pl.num_programs(0)
    slot = c % 2
    nxt = jnp.minimum(c + 1, nc - 1)

    @pl.when(c == 0)
    def _():
        _gather_start(o_hbm_ref, obuf_ref.at[0], sem.at[0], ds_ref, 0, CHUNK_GRANS)

    _gather_wait(o_hbm_ref, obuf_ref.at[slot], sem.at[slot], ds_ref, c * CHUNK_GRANS, CHUNK_GRANS)

    @pl.when(c + 1 < nc)
    def _():
        _gather_start(o_hbm_ref, obuf_ref.at[1 - slot], sem.at[1 - slot], ds_ref, nxt * CHUNK_GRANS, CHUNK_GRANS)

    rt = rt_ref[...]
    scol = lax.broadcasted_iota(jnp.int32, (W, CHUNK_SLOTS), 1)
    p1 = (scol == rt[:, 0:1].astype(jnp.int32)).astype(f32).astype(bf16)
    p2 = (scol == rt[:, 1:2].astype(jnp.int32)).astype(f32).astype(bf16)
    osv = obuf_ref[slot]
    y = rt[:, 2:3] * _dot(p1, osv) + rt[:, 3:4] * _dot(p2, osv)
    xr = ALPHA * x_ref[...] + (1.0 + mod_ref[5:6, :]) * y
    out_ref[...] = _layer_norm(xr, lng_ref[...], lnb_ref[...])


def _combine(dst_of_src, x2d, mod_l, o2d, route_t, ln_g, ln_b, S):
    T = x2d.shape[0]
    W = ROUTE_W
    tpb = S // W
    grid_spec = pltpu.PrefetchScalarGridSpec(
        num_scalar_prefetch=1,
        grid=(T // W,),
        in_specs=[
            pl.BlockSpec((W, D), lambda c, ds: (c, 0)),
            pl.BlockSpec((None, 6, D), lambda c, ds: (c // tpb, 0, 0)),
            pl.BlockSpec(memory_space=pl.ANY),
            pl.BlockSpec((None, W, 8), lambda c, ds: (c, 0, 0)),
            pl.BlockSpec((1, D), lambda c, ds: (0, 0)),
            pl.BlockSpec((1, D), lambda c, ds: (0, 0)),
        ],
        out_specs=pl.BlockSpec((W, D), lambda c, ds: (c, 0)),
        scratch_shapes=[
            pltpu.VMEM((2, CHUNK_SLOTS, D), bf16),
            pltpu.SemaphoreType.DMA((2,)),
        ],
    )
    return pl.pallas_call(
        _combine_kernel,
        grid_spec=grid_spec,
        out_shape=jax.ShapeDtypeStruct((T, D), f32),
        compiler_params=pltpu.CompilerParams(vmem_limit_bytes=VMEM_LIMIT),
        name="combine",
    )(dst_of_src, x2d, mod_l, o2d, route_t, ln_g, ln_b)


def _routing_tables(counts, n_tiles):
    nc = counts.shape[0]
    gr = (counts + GRAN - 1) // GRAN
    seg_start = jnp.cumsum(gr, axis=1) - gr
    chunk_total = jnp.sum(gr, axis=1)
    prefix = jnp.cumsum(gr, axis=0) - gr
    g_e = jnp.sum(gr, axis=0)
    tiles_e = (g_e + TILE_GRANS - 1) // TILE_GRANS
    tile_end = jnp.cumsum(tiles_e)
    tile_start = tile_end - tiles_e
    total_tiles = tile_end[-1]

    i32 = jnp.int32
    er = jnp.arange(N_EXPERTS, dtype=i32)
    t = jnp.arange(n_tiles, dtype=i32)
    te = jnp.sum((t[:, None] >= tile_end[None, :]).astype(i32), axis=1)
    tile_valid = (t < total_tiles).astype(i32)
    last_e = jnp.sum((total_tiles - 1 >= tile_end).astype(i32))
    tile_expert = jnp.where(tile_valid > 0, jnp.minimum(te, N_EXPERTS - 1), last_e).astype(i32)
    oh_t = (tile_expert[:, None] == er).astype(i32)
    grans_left = jnp.sum(oh_t * (g_e - (t[:, None] - tile_start[None, :]) * TILE_GRANS), axis=1)
    tile_subs = tile_valid * jnp.clip((grans_left + SUB_GRANS - 1) // SUB_GRANS, 0, TILE_GRANS // SUB_GRANS)

    k = jnp.arange(CHUNK_GRANS, dtype=i32)
    seg_end = seg_start + gr
    e_of = jnp.sum((k[None, :, None] >= seg_end[:, None, :]).astype(i32), axis=2)
    oh_e = (jnp.minimum(e_of, N_EXPERTS - 1)[:, :, None] == er).astype(i32)
    base = tile_start[None, :] * TILE_GRANS + prefix - seg_start
    dst = jnp.sum(oh_e * base[:, None, :], axis=2) + k[None, :]
    valid_src = k[None, :] < chunk_total[:, None]
    dst_of_src = jnp.where(valid_src, dst, 0).astype(i32).reshape(-1)

    d = jnp.arange(n_tiles * TILE_GRANS, dtype=i32)
    oh_d = (jnp.repeat(tile_expert, TILE_GRANS)[:, None] == er).astype(i32)
    q = d - jnp.sum(oh_d * tile_start[None, :], axis=1) * TILE_GRANS
    incl_d = jnp.sum(oh_d[:, :, None] * (prefix + gr).T[None], axis=1)
    c_d = jnp.sum((q[:, None] >= incl_d).astype(i32), axis=1)
    oh_c = (jnp.minimum(c_d, nc - 1)[:, None] == jnp.arange(nc, dtype=i32)).astype(i32)
    cbase = jnp.arange(nc, dtype=i32)[:, None] * CHUNK_GRANS + seg_start - prefix
    sel = jnp.sum(oh_c[:, :, None] * oh_d[:, None, :] * cbase[None], axis=(1, 2))
    valid_dst = (jnp.repeat(tile_valid, TILE_GRANS) > 0) & (q >= 0) & (q < jnp.sum(oh_d * g_e[None, :], axis=1))
    src_of_dst = jnp.where(valid_dst, sel + q, 0).astype(i32)
    return tile_expert, tile_subs.astype(i32), src_of_dst, dst_of_src


def kernel(x, c, ada_w, ada_b, ln_g, ln_b, ev_w_in, ev_pool_w, ev_pool_scale, ev_sink, ev_w_out, od_w_in, od_sg_ln_g, od_sg_ln_b, od_w_s, od_b_s, od_w_out, ffn_w_gate, ffn_w_up, ffn_w_down, moe_w_router, moe_w_gate, moe_w_up, moe_w_down):
    B, S, _ = x.shape
    T = B * S
    x2d = x.reshape(T, D)
    mod = _adaln(c, ada_w, ada_b)

    w_in = ev_w_in[0].astype(bf16)
    q0, k0, v0 = POOL_WIDTH, POOL_WIDTH + ATTN_WIDTH, POOL_WIDTH + ATTN_WIDTH + KV_WIDTH
    w_pk = jnp.concatenate([w_in[:, :q0], w_in[:, k0:v0]], axis=1)
    w_qv_t = jnp.concatenate([w_in[:, q0:k0], w_in[:, v0:]], axis=1).T
    p, k, qt, vt = _ev_in(x2d, mod[0], w_pk, w_qv_t, S)
    x2d = _ev_mix(x2d, mod[0], p, qt, k, vt, ev_pool_w[0].astype(bf16), ev_pool_scale[0][None, :],
                  ev_sink[0], ev_w_out[0].astype(bf16), ln_g[0, 0][None, :], ln_b[0, 0][None, :], B, S)
    x2d = _ffn(x2d, mod[0], ffn_w_gate[0], ffn_w_up[0], ffn_w_down[0],
               ln_g[0, 1][None, :], ln_b[0, 1][None, :], S)

    x2d = _sg_mix(x2d, mod[1], od_w_in[0].astype(bf16), od_sg_ln_g[0][None, :], od_sg_ln_b[0][None, :],
                  od_w_s[0].astype(bf16), od_b_s[0].T, od_w_out[0].astype(bf16),
                  ln_g[1, 0][None, :], ln_b[1, 0][None, :], S)

    hs, route, cnt = _route(x2d, mod[1], moe_w_router[0].T, S)
    nc = T // ROUTE_W
    n_tiles = (nc * CHUNK_GRANS) // TILE_GRANS + N_EXPERTS
    tile_expert, tile_subs, src_of_dst, dst_of_src = _routing_tables(cnt[:, :, 0], n_tiles)
    o = _experts(tile_expert, tile_subs, src_of_dst, hs.reshape(nc * CHUNK_SLOTS, D),
                 moe_w_gate[0], moe_w_up[0], moe_w_down[0])
    x2d = _combine(dst_of_src, x2d, mod[1], o, jnp.swapaxes(route, 1, 2),
                   ln_g[1, 1][None, :], ln_b[1, 1][None, :], S)
    return x2d.reshape(B, S, D)
```

```python
import functools
import math

import jax
import jax.numpy as jnp
from jax import lax
from jax.experimental import pallas as pl
from jax.experimental.pallas import tpu as pltpu

D = 1024
DEPTH = 2
ALPHA = (2.0 * DEPTH) ** 0.25
LN_EPS = 1e-5

POOL_WINDOWS = (2, 4, 8, 16)
POOL_CH = 128
POOL_WIDTH = 512
HEAD_DIM = 64
N_Q_HEADS = 8
N_KV_HEADS = 2
Q_GROUP = 4
ATTN_WIDTH = 512
KV_WIDTH = 128
BLOCK = 128
EVEN_IN = 1280
POOL_HALO = 8

CHUNK = 128
SG_GROUPS = 8
SG_CH = 128

N_EXPERTS = 8

ROUTE_W = 512
GRAN = 16
CHUNK_SLOTS = 2 * ROUTE_W + N_EXPERTS * GRAN
CHUNK_GRANS = CHUNK_SLOTS // GRAN
MOE_SUB = 256
MOE_FAST = 2048
MOE_TM = MOE_FAST + MOE_SUB
TILE_GRANS = MOE_TM // GRAN
SUB_GRANS = MOE_SUB // GRAN
MOE_TF = 512

VMEM_LIMIT = 48 * 1024 * 1024
VMEM_LIMIT_BIG = 56 * 1024 * 1024

bf16 = jnp.bfloat16
f32 = jnp.float32


def _dot(a, b):
    return jnp.dot(a, b, preferred_element_type=f32)


def _split_bf16(a):
    hi = a.astype(bf16)
    lo = (a - hi.astype(f32)).astype(bf16)
    return hi, lo


def _layer_norm(x, g, b):
    mu = jnp.mean(x, axis=-1, keepdims=True)
    xc = x - mu
    var = jnp.mean(xc * xc, axis=-1, keepdims=True)
    return xc * lax.rsqrt(var + LN_EPS) * g + b


def _silu(x):
    return x * jax.nn.sigmoid(x)


def _gelu_tanh(x):
    c = math.sqrt(2.0 / math.pi)
    return x * (0.5 * (1.0 + jnp.tanh(c * (x + 0.044715 * (x * x * x)))))


def _adaln_kernel(c_ref, w_ref, b_ref, o_ref):
    cond = _silu(c_ref[...])
    c_hi, c_lo = _split_bf16(cond)
    w_hi, w_lo = _split_bf16(w_ref[...])
    acc = _dot(c_hi, w_hi) + (_dot(c_lo, w_hi) + _dot(c_hi, w_lo))
    o_ref[...] = acc + b_ref[...]


def _adaln(c, ada_w, ada_b):
    B = c.shape[0]
    tn = 1024
    c_pad = jnp.zeros((8, D), f32).at[:B].set(c)
    out = pl.pallas_call(
        _adaln_kernel,
        grid=(DEPTH, 6 * D // tn),
        in_specs=[
            pl.BlockSpec((8, D), lambda l, j: (0, 0)),
            pl.BlockSpec((None, D, tn), lambda l, j: (l, 0, j)),
            pl.BlockSpec((None, 1, tn), lambda l, j: (l, 0, j)),
        ],
        out_specs=pl.BlockSpec((None, 8, tn), lambda l, j: (l, 0, j)),
        out_shape=jax.ShapeDtypeStruct((DEPTH, 8, 6 * D), f32),
        compiler_params=pltpu.CompilerParams(vmem_limit_bytes=VMEM_LIMIT),
        name="adaln",
    )(c_pad, ada_w, ada_b.reshape(DEPTH, 1, 6 * D))
    return out[:, :B].reshape(DEPTH, B, 6, D)


def _ev_in_kernel(x_ref, mod_ref, wpk_ref, wqvt_ref, p_ref, k_ref, qt_ref, vt_ref):
    h = (x_ref[...] * (1.0 + mod_ref[1:2, :]) + mod_ref[0:1, :]).astype(bf16)
    zpk = _dot(h, wpk_ref[...])
    p_ref[...] = zpk[:, :POOL_WIDTH]
    k_ref[...] = zpk[:, POOL_WIDTH:].astype(bf16)
    zt = lax.dot_general(wqvt_ref[...], h, (((1,), (1,)), ((), ())), preferred_element_type=f32)
    qt_ref[...] = (zt[:ATTN_WIDTH] * (HEAD_DIM ** -0.5)).astype(bf16)
    vt_ref[...] = zt[ATTN_WIDTH:].astype(bf16)


def _ev_in(x2d, mod_l, w_pk, w_qv_t, S):
    T = x2d.shape[0]
    tm = 512
    tpb = S // tm
    return pl.pallas_call(
        _ev_in_kernel,
        grid=(T // tm,),
        in_specs=[
            pl.BlockSpec((tm, D), lambda i: (i, 0)),
            pl.BlockSpec((None, 6, D), lambda i: (i // tpb, 0, 0)),
            pl.BlockSpec((D, POOL_WIDTH + KV_WIDTH), lambda i: (0, 0)),
            pl.BlockSpec((ATTN_WIDTH + KV_WIDTH, D), lambda i: (0, 0)),
        ],
        out_specs=[
            pl.BlockSpec((tm, POOL_WIDTH), lambda i: (i, 0)),
            pl.BlockSpec((tm, KV_WIDTH), lambda i: (i, 0)),
            pl.BlockSpec((ATTN_WIDTH, tm), lambda i: (0, i)),
            pl.BlockSpec((KV_WIDTH, tm), lambda i: (0, i)),
        ],
        out_shape=[
            jax.ShapeDtypeStruct((T, POOL_WIDTH), f32),
            jax.ShapeDtypeStruct((T, KV_WIDTH), bf16),
            jax.ShapeDtypeStruct((ATTN_WIDTH, T), bf16),
            jax.ShapeDtypeStruct((KV_WIDTH, T), bf16),
        ],
        compiler_params=pltpu.CompilerParams(vmem_limit_bytes=VMEM_LIMIT),
        name="ev_in",
    )(x2d, mod_l, w_pk, w_qv_t)


def _ev_mix_kernel(x_ref, mod_ref, p_ref, pp_ref, pn_ref, qt_ref,
                   k_ref, kp_ref, kn_ref, vt_ref, vtp_ref, vtn_ref,
                   bias_ref, sink_ref, wpool_ref, pscale_ref, wout_ref, lng_ref, lnb_ref,
                   o_ref, pext_ref, pooled_ref, kext_ref, vext_ref, ybt_ref, mix_ref, *, S, tq):
    i = pl.program_id(1)
    n_tiles = S // tq
    is_first = i == 0
    is_last = i == n_tiles - 1
    H = POOL_HALO

    p = p_ref[...]
    pext_ref[0:H, :] = jnp.where(is_first, 0.0, pp_ref[...])
    pext_ref[H:H + tq, :] = p
    pext_ref[H + tq:, :] = jnp.where(is_last, 0.0, pn_ref[...])
    near = lax.broadcasted_iota(jnp.int32, (H, 1), 0)
    for g, w in enumerate(POOL_WINDOWS):
        cs = slice(g * POOL_CH, (g + 1) * POOL_CH)
        r = w // 2
        wsum = pext_ref[H - r:H - r + tq, cs]
        for d in range(-r + 1, r + 1):
            wsum = wsum + pext_ref[H + d:H + d + tq, cs]
        pooled_ref[:, cs] = wsum / float(w + 1) - p[:, cs]
        r = w // 2
        cnt_head = (jnp.minimum(near, r) + (r + 1)).astype(f32)
        cnt_tail = (jnp.minimum(H - 1 - near, r) + (r + 1)).astype(f32)
        cnt_head = jnp.where(is_first, cnt_head, float(w + 1))
        cnt_tail = jnp.where(is_last, cnt_tail, float(w + 1))
        pooled_ref[0:H, cs] = wsum[0:H] / cnt_head - p[0:H, cs]
        pooled_ref[tq - H:tq, cs] = wsum[tq - H:tq] / cnt_tail - p[tq - H:tq, cs]
        ya = _dot(pooled_ref[:, cs].astype(bf16), wpool_ref[g])
        mix_ref[:, cs] = (ya * pscale_ref[:, cs]).astype(bf16)

    kext_ref[0:BLOCK, :] = kp_ref[...]
    kext_ref[BLOCK:BLOCK + tq, :] = k_ref[...]
    kext_ref[BLOCK + tq:, :] = kn_ref[...]
    vext_ref[:, 0:BLOCK] = vtp_ref[...]
    vext_ref[:, BLOCK:BLOCK + tq] = vt_ref[...]
    vext_ref[:, BLOCK + tq:] = vtn_ref[...]

    n_blocks = S // BLOCK
    zeros_q = jnp.zeros((HEAD_DIM, Q_GROUP * BLOCK), bf16)
    for n in range(tq // BLOCK):
        gb = i * (tq // BLOCK) + n
        variant = jnp.where(gb == 0, 1, jnp.where(gb == n_blocks - 1, 2, 0))
        cols = slice(n * BLOCK, (n + 1) * BLOCK)
        kw = kext_ref[n * BLOCK:n * BLOCK + 3 * BLOCK, :]
        for kvh in range(N_KV_HEADS):
            qst = jnp.concatenate(
                [qt_ref[(kvh * Q_GROUP + gq) * HEAD_DIM:(kvh * Q_GROUP + gq + 1) * HEAD_DIM, cols]
                 for gq in range(Q_GROUP)], axis=1)
            qst = jnp.concatenate([qst, zeros_q] if kvh == 0 else [zeros_q, qst], axis=0)
            s = _dot(kw, qst) + bias_ref[variant, kvh]
            sink = sink_ref[kvh]
            m = jnp.maximum(jnp.max(s, axis=0, keepdims=True), sink)
            e = jnp.exp(s - m)
            denom = jnp.sum(e, axis=0, keepdims=True) + jnp.exp(sink - m)
            vwt = vext_ref[kvh * HEAD_DIM:(kvh + 1) * HEAD_DIM, n * BLOCK:n * BLOCK + 3 * BLOCK]
            out = _dot(vwt, e.astype(bf16)) / denom
            for gq in range(Q_GROUP):
                hq = kvh * Q_GROUP + gq
                ybt_ref[hq * HEAD_DIM:(hq + 1) * HEAD_DIM, cols] = out[:, gq * BLOCK:(gq + 1) * BLOCK]
    mix_ref[:, POOL_WIDTH:] = ybt_ref[...].T.astype(bf16)

    for r in range(2):
        rr = slice(r * (tq // 2), (r + 1) * (tq // 2))
        y = _dot(mix_ref[rr, :], wout_ref[...])
        xr = ALPHA * x_ref[rr, :] + (1.0 + mod_ref[2:3, :]) * y
        o_ref[rr, :] = _layer_norm(xr, lng_ref[...], lnb_ref[...])


def _ev_mix(x2d, mod_l, p, qt, k, vt, w_pool, pool_scale, sink, w_out, ln_g, ln_b, B, S):
    T = x2d.shape[0]
    tq = 512
    nt = S // tq
    kb = tq // BLOCK
    pb = tq // POOL_HALO
    n_kblocks = T // BLOCK
    n_pblocks = T // POOL_HALO

    def main(b, i): return (b * nt + i, 0)
    def kprev(b, i): return (jnp.maximum((b * nt + i) * kb - 1, 0), 0)
    def knext(b, i): return (jnp.minimum((b * nt + i + 1) * kb, n_kblocks - 1), 0)
    def pprev(b, i): return (jnp.maximum((b * nt + i) * pb - 1, 0), 0)
    def pnext(b, i): return (jnp.minimum((b * nt + i + 1) * pb, n_pblocks - 1), 0)
    def const2(b, i): return (0, 0)

    def tmain(b, i): return (0, b * nt + i)
    def tprev(b, i): return (0, jnp.maximum((b * nt + i) * kb - 1, 0))
    def tnext(b, i): return (0, jnp.minimum((b * nt + i + 1) * kb, n_kblocks - 1))

    assert S // BLOCK >= 2
    kj = jnp.arange(3 * BLOCK)[:, None]
    qi = jnp.arange(BLOCK)[None, :]
    dist = jnp.abs(kj - BLOCK - qi)
    slopes = 2.0 ** (-8.0 * jnp.arange(1, N_Q_HEADS + 1, dtype=f32) / N_Q_HEADS)
    alibi = -slopes[:, None, None] * dist.astype(f32)[None]
    in_window = dist <= BLOCK
    key_ok = jnp.stack([kj >= 0, kj >= BLOCK, kj < 2 * BLOCK])
    bias = jnp.where((in_window[None] & key_ok)[:, None], alibi[None], -1e30)
    bias = bias.reshape(3, N_KV_HEADS, Q_GROUP, 3 * BLOCK, BLOCK).transpose(0, 1, 3, 2, 4)
    bias = bias.reshape(3, N_KV_HEADS, 3 * BLOCK, Q_GROUP * BLOCK)
    sink_row = jnp.repeat(sink.astype(f32).reshape(N_KV_HEADS, Q_GROUP), BLOCK, axis=1)[:, None, :]

    assert tq >= 2 * POOL_HALO and max(POOL_WINDOWS) // 2 <= POOL_HALO and nt >= 2

    kernel = functools.partial(_ev_mix_kernel, S=S, tq=tq)
    return pl.pallas_call(
        kernel,
        grid=(B, nt),
        in_specs=[
            pl.BlockSpec((tq, D), main),
            pl.BlockSpec((None, 6, D), lambda b, i: (b, 0, 0)),
            pl.BlockSpec((tq, POOL_WIDTH), main),
            pl.BlockSpec((POOL_HALO, POOL_WIDTH), pprev),
            pl.BlockSpec((POOL_HALO, POOL_WIDTH), pnext),
            pl.BlockSpec((ATTN_WIDTH, tq), tmain),
            pl.BlockSpec((tq, KV_WIDTH), main),
            pl.BlockSpec((BLOCK, KV_WIDTH), kprev),
            pl.BlockSpec((BLOCK, KV_WIDTH), knext),
            pl.BlockSpec((KV_WIDTH, tq), tmain),
            pl.BlockSpec((KV_WIDTH, BLOCK), tprev),
            pl.BlockSpec((KV_WIDTH, BLOCK), tnext),
            pl.BlockSpec((3, N_KV_HEADS, 3 * BLOCK, Q_GROUP * BLOCK), lambda b, i: (0, 0, 0, 0)),
            pl.BlockSpec((N_KV_HEADS, 1, Q_GROUP * BLOCK), lambda b, i: (0, 0, 0)),
            pl.BlockSpec((len(POOL_WINDOWS), POOL_CH, POOL_CH), lambda b, i: (0, 0, 0)),
            pl.BlockSpec((1, POOL_WIDTH), const2),
            pl.BlockSpec((D, D), const2),
            pl.BlockSpec((1, D), const2),
            pl.BlockSpec((1, D), const2),
        ],
        out_specs=pl.BlockSpec((tq, D), main),
        out_shape=jax.ShapeDtypeStruct((T, D), f32),
        scratch_shapes=[
            pltpu.VMEM((tq + 2 * POOL_HALO, POOL_WIDTH), f32),
            pltpu.VMEM((tq, POOL_WIDTH), f32),
            pltpu.VMEM((tq + 2 * BLOCK, KV_WIDTH), bf16),
            pltpu.VMEM((KV_WIDTH, tq + 2 * BLOCK), bf16),
            pltpu.VMEM((ATTN_WIDTH, tq), f32),
            pltpu.VMEM((tq, D), bf16),
        ],
        compiler_params=pltpu.CompilerParams(vmem_limit_bytes=VMEM_LIMIT),
        name="ev_mix",
    )(x2d, mod_l, p, p, p, qt, k, k, k, vt, vt, vt, bias, sink_row, w_pool, pool_scale, w_out, ln_g, ln_b)


def _load_rounded(w_hbm_ref, w_ref, stage_ref, sem):
    rows = stage_ref.shape[1]
    n = w_hbm_ref.shape[0] // rows

    def copy(c):
        return pltpu.make_async_copy(w_hbm_ref.at[pl.ds(c * rows, rows), :], stage_ref.at[c % 2], sem.at[c % 2])

    copy(0).start()
    for c in range(n):
        if c + 1 < n:
            copy(c + 1).start()
        copy(c).wait()
        w_ref[pl.ds(c * rows, rows), :] = stage_ref[c % 2].astype(bf16)


def _ffn_kernel(x_ref, mod_ref, wg_hbm_ref, wu_hbm_ref, wd_hbm_ref, lng_ref, lnb_ref, o_ref,
                wg_ref, wu_ref, wd_ref, stage_in_ref, stage_out_ref, sem):
    @pl.when(pl.program_id(0) == 0)
    def _():
        _load_rounded(wg_hbm_ref, wg_ref, stage_in_ref, sem)
        _load_rounded(wu_hbm_ref, wu_ref, stage_in_ref, sem)
        _load_rounded(wd_hbm_ref, wd_ref, stage_out_ref, sem)

    x = x_ref[...]
    h = (x * (1.0 + mod_ref[4:5, :]) + mod_ref[3:4, :]).astype(bf16)
    a = _silu(_dot(h, wg_ref[...])) * _dot(h, wu_ref[...])
    y = _dot(a.astype(bf16), wd_ref[...])
    xr = ALPHA * x + (1.0 + mod_ref[5:6, :]) * y
    o_ref[...] = _layer_norm(xr, lng_ref[...], lnb_ref[...])


def _ffn(x2d, mod_l, w_gate, w_up, w_down, ln_g, ln_b, S):
    T = x2d.shape[0]
    tm = 512
    dff = w_gate.shape[1]
    tpb = S // tm
    n_stage = 8
    return pl.pallas_call(
        _ffn_kernel,
        grid=(T // tm,),
        in_specs=[
            pl.BlockSpec((tm, D), lambda i: (i, 0)),
            pl.BlockSpec((None, 6, D), lambda i: (i // tpb, 0, 0)),
            pl.BlockSpec(memory_space=pl.ANY),
            pl.BlockSpec(memory_space=pl.ANY),
            pl.BlockSpec(memory_space=pl.ANY),
            pl.BlockSpec((1, D), lambda i: (0, 0)),
            pl.BlockSpec((1, D), lambda i: (0, 0)),
        ],
        out_specs=pl.BlockSpec((tm, D), lambda i: (i, 0)),
        out_shape=jax.ShapeDtypeStruct((T, D), f32),
        scratch_shapes=[
            pltpu.VMEM((D, dff), bf16),
            pltpu.VMEM((D, dff), bf16),
            pltpu.VMEM((dff, D), bf16),
            pltpu.VMEM((2, D // n_stage, dff), f32),
            pltpu.VMEM((2, dff // n_stage, D), f32),
            pltpu.SemaphoreType.DMA((2,)),
        ],
        compiler_params=pltpu.CompilerParams(vmem_limit_bytes=VMEM_LIMIT_BIG),
        name="ffn",
    )(x2d, mod_l, w_gate, w_up, w_down, ln_g, ln_b)


def _sg_kernel(x_ref, mod_ref, win_ref, sgg_ref, sgb_ref, ws_ref, bst_ref, wout_ref, lng_ref, lnb_ref,
               o_ref, gate_ref, *, tm):
    x = x_ref[...]
    h = (x * (1.0 + mod_ref[1:2, :]) + mod_ref[0:1, :]).astype(bf16)
    v = _layer_norm(_gelu_tanh(_dot(h, win_ref[:, D:])), sgg_ref[...], sgb_ref[...]).astype(bf16)
    u = _gelu_tanh(_dot(h, win_ref[:, :D]))
    for n in range(tm // CHUNK):
        rows = slice(n * CHUNK, (n + 1) * CHUNK)
        for g in range(SG_GROUPS):
            cols = slice(g * SG_CH, (g + 1) * SG_CH)
            sv = _dot(ws_ref[g], v[rows, cols]) + bst_ref[:, g:g + 1]
            gate_ref[rows, cols] = (u[rows, cols] * sv).astype(bf16)
    y = _dot(gate_ref[...], wout_ref[...])
    xr = ALPHA * x + (1.0 + mod_ref[2:3, :]) * y
    o_ref[...] = _layer_norm(xr, lng_ref[...], lnb_ref[...])


def _sg_mix(x2d, mod_l, w_in, sg_g, sg_b, w_s, b_s_t, w_out, ln_g, ln_b, S):
    T = x2d.shape[0]
    tm = 512
    tpb = S // tm
    c2 = lambda i: (0, 0)
    return pl.pallas_call(
        functools.partial(_sg_kernel, tm=tm),
        grid=(T // tm,),
        in_specs=[
            pl.BlockSpec((tm, D), lambda i: (i, 0)),
            pl.BlockSpec((None, 6, D), lambda i: (i // tpb, 0, 0)),
            pl.BlockSpec((D, 2 * D), c2),
            pl.BlockSpec((1, D), c2),
            pl.BlockSpec((1, D), c2),
            pl.BlockSpec((SG_GROUPS, CHUNK, CHUNK), lambda i: (0, 0, 0)),
            pl.BlockSpec((CHUNK, SG_GROUPS), c2),
            pl.BlockSpec((D, D), c2),
            pl.BlockSpec((1, D), c2),
            pl.BlockSpec((1, D), c2),
        ],
        out_specs=pl.BlockSpec((tm, D), lambda i: (i, 0)),
        out_shape=jax.ShapeDtypeStruct((T, D), f32),
        scratch_shapes=[pltpu.VMEM((tm, D), bf16)],
        compiler_params=pltpu.CompilerParams(vmem_limit_bytes=VMEM_LIMIT),
        name="sg_mix",
    )(x2d, mod_l, w_in, sg_g, sg_b, w_s, b_s_t, w_out, ln_g, ln_b)


def _route_kernel(x_ref, mod_ref, wrt_ref, hs_ref, route_ref, cnt_ref):
    W = ROUTE_W
    h = x_ref[...] * (1.0 + mod_ref[4:5, :]) + mod_ref[3:4, :]
    h_hi, h_lo = _split_bf16(h)
    w_hi, w_lo = _split_bf16(wrt_ref[...])
    nt = (((1,), (1,)), ((), ()))
    logits = (lax.dot_general(w_hi, h_hi, nt, preferred_element_type=f32)
              + (lax.dot_general(w_hi, h_lo, nt, preferred_element_type=f32)
                 + lax.dot_general(w_lo, h_hi, nt, preferred_element_type=f32)))

    eidx = lax.broadcasted_iota(jnp.int32, (N_EXPERTS, W), 0)
    m1 = jnp.max(logits, axis=0, keepdims=True)
    i1 = jnp.min(jnp.where(logits == m1, eidx, N_EXPERTS), axis=0, keepdims=True)
    sel1 = eidx == i1
    rest = jnp.where(sel1, -jnp.inf, logits)
    m2 = jnp.max(rest, axis=0, keepdims=True)
    i2 = jnp.min(jnp.where(rest == m2, eidx, N_EXPERTS), axis=0, keepdims=True)
    sel2 = eidx == i2
    e2 = jnp.exp(m2 - m1)
    g1 = 1.0 / (1.0 + e2)
    g2 = e2 / (1.0 + e2)

    a1 = sel1.astype(f32)
    a2 = sel2.astype(f32)
    assign = a1 + a2
    counts = jnp.sum(assign, axis=1, keepdims=True)
    grans = jnp.ceil(counts * (1.0 / GRAN))
    sub = lax.broadcasted_iota(jnp.int32, (N_EXPERTS, 1), 0)
    seg = jnp.zeros((N_EXPERTS, 1), f32)
    for e in range(N_EXPERTS - 1):
        seg = seg + jnp.where(sub > e, grans[e:e + 1, :] * GRAN, 0.0)
    tr = lax.broadcasted_iota(jnp.int32, (W, W), 0)
    tc = lax.broadcasted_iota(jnp.int32, (W, W), 1)
    upper = (tr < tc).astype(bf16)
    rank = _dot(assign.astype(bf16), upper)
    slot = seg + rank
    pos1 = jnp.sum(a1 * slot, axis=0, keepdims=True)
    pos2 = jnp.sum(a2 * slot, axis=0, keepdims=True)

    srow = lax.broadcasted_iota(jnp.int32, (CHUNK_SLOTS, W), 0)
    perm = ((srow == pos1.astype(jnp.int32)) | (srow == pos2.astype(jnp.int32)))
    hs_ref[...] = _dot(perm.astype(f32).astype(bf16), h_hi).astype(bf16)

    ridx = lax.broadcasted_iota(jnp.int32, (8, W), 0)
    route = jnp.where(ridx == 0, pos1, jnp.where(ridx == 1, pos2, jnp.where(ridx == 2, g1, jnp.where(ridx == 3, g2, 0.0))))
    route_ref[...] = route
    cnt_ref[...] = jnp.broadcast_to(counts, (N_EXPERTS, 128)).astype(jnp.int32)


def _route(x2d, mod_l, w_router_t, S):
    T = x2d.shape[0]
    W = ROUTE_W
    nc = T // W
    tpb = S // W
    return pl.pallas_call(
        _route_kernel,
        grid=(nc,),
        in_specs=[
            pl.BlockSpec((W, D), lambda c: (c, 0)),
            pl.BlockSpec((None, 6, D), lambda c: (c // tpb, 0, 0)),
            pl.BlockSpec((N_EXPERTS, D), lambda c: (0, 0)),
        ],
        out_specs=[
            pl.BlockSpec((None, CHUNK_SLOTS, D), lambda c: (c, 0, 0)),
            pl.BlockSpec((None, 8, W), lambda c: (c, 0, 0)),
            pl.BlockSpec((None, N_EXPERTS, 128), lambda c: (c, 0, 0)),
        ],
        out_shape=[
            jax.ShapeDtypeStruct((nc, CHUNK_SLOTS, D), bf16),
            jax.ShapeDtypeStruct((nc, 8, W), f32),
            jax.ShapeDtypeStruct((nc, N_EXPERTS, 128), jnp.int32),
        ],
        compiler_params=pltpu.CompilerParams(vmem_limit_bytes=VMEM_LIMIT),
        name="route",
    )(x2d, mod_l, w_router_t)


def _granule_copy(src_ref, buf_ref, sem, idx_ref, base, g):
    row = pl.multiple_of(idx_ref[base + g] * GRAN, GRAN)
    return pltpu.make_async_copy(src_ref.at[pl.ds(row, GRAN), :], buf_ref.at[pl.ds(g * GRAN, GRAN), :], sem)


def _gather_start(src_ref, buf_ref, sem, idx_ref, base, n):
    for g in range(n):
        _granule_copy(src_ref, buf_ref, sem, idx_ref, base, g).start()


def _gather_wait(src_ref, buf_ref, sem, idx_ref, base, n):
    for g in range(n):
        _granule_copy(src_ref, buf_ref, sem, idx_ref, base, g).wait()


def _expert_kernel(te_ref, tv_ref, src_ref, hs_ref, wg_ref, wu_ref, wd_ref, o_ref,
                   xbuf_ref, sem, acc_ref, wgb_ref, wub_ref, wdb_ref):
    i = pl.program_id(0)
    j = pl.program_id(1)
    n_tiles = pl.num_programs(0)
    last = pl.num_programs(1) - 1
    valid = tv_ref[i] > 0
    slot = i % 2
    nxt = jnp.minimum(i + 1, n_tiles - 1)

    @pl.when(j == 0)
    def _():
        @pl.when(i == 0)
        def _():
            _gather_start(hs_ref, xbuf_ref.at[0], sem.at[0], src_ref, 0, TILE_GRANS)

        @pl.when(valid)
        def _():
            _gather_wait(hs_ref, xbuf_ref.at[slot], sem.at[slot], src_ref, i * TILE_GRANS, TILE_GRANS)

        @pl.when((i + 1 < n_tiles) & (tv_ref[nxt] > 0))
        def _():
            _gather_start(hs_ref, xbuf_ref.at[1 - slot], sem.at[1 - slot], src_ref, nxt * TILE_GRANS, TILE_GRANS)

    n_sub = tv_ref[i]
    fast_subs = MOE_FAST // MOE_SUB
    fast = n_sub >= fast_subs

    def round_weights():
        wgb_ref[...] = wg_ref[...].astype(bf16)
        wub_ref[...] = wu_ref[...].astype(bf16)
        wdb_ref[...] = wd_ref[...].astype(bf16)

    def swiglu(x):
        a = _silu(_dot(x, wgb_ref[...])) * _dot(x, wub_ref[...])
        return _dot(a.astype(bf16), wdb_ref[...])

    @pl.when(valid & (j == 0))
    def _():
        acc_ref[...] = jnp.zeros_like(acc_ref)

    @pl.when(fast)
    def _():
        round_weights()
        acc_ref[0:MOE_FAST, :] += swiglu(xbuf_ref[slot, 0:MOE_FAST, :])

    @pl.when(valid & jnp.logical_not(fast))
    def _():
        round_weights()

    @pl.when(valid)
    def _():
        def body(sb, carry):
            rows = pl.ds(pl.multiple_of(sb * MOE_SUB, MOE_SUB), MOE_SUB)
            acc_ref[rows, :] += swiglu(xbuf_ref[slot, rows, :])
            return carry

        lax.fori_loop(jnp.where(fast, fast_subs, 0), n_sub, body, 0)

    @pl.when(j == last)
    def _():
        @pl.when(valid)
        def _():
            o_ref[...] = acc_ref[...].astype(o_ref.dtype)

        @pl.when(jnp.logical_not(valid))
        def _():
            o_ref[...] = jnp.zeros_like(o_ref)


def _experts(tile_expert, tile_valid, src_of_dst, hs2d, w_gate, w_up, w_down):
    n_tiles = tile_expert.shape[0]
    dff = w_gate.shape[2]
    nff = dff // MOE_TF

    def jj(j, tv, i):
        return jnp.where(tv[i] > 0, j, nff - 1)

    grid_spec = pltpu.PrefetchScalarGridSpec(
        num_scalar_prefetch=3,
        grid=(n_tiles, nff),
        in_specs=[
            pl.BlockSpec(memory_space=pl.ANY),
            pl.BlockSpec((None, D, MOE_TF), lambda i, j, te, tv, sd: (te[i], 0, jj(j, tv, i))),
            pl.BlockSpec((None, D, MOE_TF), lambda i, j, te, tv, sd: (te[i], 0, jj(j, tv, i))),
            pl.BlockSpec((None, MOE_TF, D), lambda i, j, te, tv, sd: (te[i], jj(j, tv, i), 0)),
        ],
        out_specs=pl.BlockSpec((MOE_TM, D), lambda i, j, te, tv, sd: (i, 0)),
        scratch_shapes=[
            pltpu.VMEM((2, MOE_TM, D), bf16),
            pltpu.SemaphoreType.DMA((2,)),
            pltpu.VMEM((MOE_TM, D), f32),
            pltpu.VMEM((D, MOE_TF), bf16),
            pltpu.VMEM((D, MOE_TF), bf16),
            pltpu.VMEM((MOE_TF, D), bf16),
        ],
    )
    return pl.pallas_call(
        _expert_kernel,
        grid_spec=grid_spec,
        out_shape=jax.ShapeDtypeStruct((n_tiles * MOE_TM, D), bf16),
        compiler_params=pltpu.CompilerParams(vmem_limit_bytes=VMEM_LIMIT_BIG),
        name="experts",
    )(tile_expert, tile_valid, src_of_dst, hs2d, w_gate, w_up, w_down)


def _combine_kernel(ds_ref, x_ref, mod_ref, o_hbm_ref, rt_ref, lng_ref, lnb_ref, out_ref, obuf_ref, sem):
    W = ROUTE_W
    c = pl.program_id(0)
    nc = pl.num_programs(0)
    slot = c % 2
    nxt = jnp.minimum(c + 1, nc - 1)

    @pl.when(c == 0)
    def _():
        _gather_start(o_hbm_ref, obuf_ref.at[0], sem.at[0], ds_ref, 0, CHUNK_GRANS)

    _gather_wait(o_hbm_ref, obuf_ref.at[slot], sem.at[slot], ds_ref, c * CHUNK_GRANS, CHUNK_GRANS)

    @pl.when(c + 1 < nc)
    def _():
        _gather_start(o_hbm_ref, obuf_ref.at[1 - slot], sem.at[1 - slot], ds_ref, nxt * CHUNK_GRANS, CHUNK_GRANS)

    rt = rt_ref[...]
    scol = lax.broadcasted_iota(jnp.int32, (W, CHUNK_SLOTS), 1)
    p1 = (scol == rt[:, 0:1].astype(jnp.int32)).astype(f32).astype(bf16)
    p2 = (scol == rt[:, 1:2].astype(jnp.int32)).astype(f32).astype(bf16)
    osv = obuf_ref[slot]
    y = rt[:, 2:3] * _dot(p1, osv) + rt[:, 3:4] * _dot(p2, osv)
    xr = ALPHA * x_ref[...] + (1.0 + mod_ref[5:6, :]) * y
    out_ref[...] = _layer_norm(xr, lng_ref[...], lnb_ref[...])


def _combine(dst_of_src, x2d, mod_l, o2d, route_t, ln_g, ln_b, S):
    T = x2d.shape[0]
    W = ROUTE_W
    tpb = S // W
    grid_spec = pltpu.PrefetchScalarGridSpec(
        num_scalar_prefetch=1,
        grid=(T // W,),
        in_specs=[
            pl.BlockSpec((W, D), lambda c, ds: (c, 0)),
            pl.BlockSpec((None, 6, D), lambda c, ds: (c // tpb, 0, 0)),
            pl.BlockSpec(memory_space=pl.ANY),
            pl.BlockSpec((None, W, 8), lambda c, ds: (c, 0, 0)),
            pl.BlockSpec((1, D), lambda c, ds: (0, 0)),
            pl.BlockSpec((1, D), lambda c, ds: (0, 0)),
        ],
        out_specs=pl.BlockSpec((W, D), lambda c, ds: (c, 0)),
        scratch_shapes=[
            pltpu.VMEM((2, CHUNK_SLOTS, D), bf16),
            pltpu.SemaphoreType.DMA((2,)),
        ],
    )
    return pl.pallas_call(
        _combine_kernel,
        grid_spec=grid_spec,
        out_shape=jax.ShapeDtypeStruct((T, D), f32),
        compiler_params=pltpu.CompilerParams(vmem_limit_bytes=VMEM_LIMIT),
        name="combine",
    )(dst_of_src, x2d, mod_l, o2d, route_t, ln_g, ln_b)


def _routing_tables(counts, n_tiles):
    nc = counts.shape[0]
    gr = (counts + GRAN - 1) // GRAN
    seg_start = jnp.cumsum(gr, axis=1) - gr
    chunk_total = jnp.sum(gr, axis=1)
    prefix = jnp.cumsum(gr, axis=0) - gr
    g_e = jnp.sum(gr, axis=0)
    tiles_e = (g_e + TILE_GRANS - 1) // TILE_GRANS
    tile_end = jnp.cumsum(tiles_e)
    tile_start = tile_end - tiles_e
    total_tiles = tile_end[-1]

    i32 = jnp.int32
    er = jnp.arange(N_EXPERTS, dtype=i32)
    t = jnp.arange(n_tiles, dtype=i32)
    te = jnp.sum((t[:, None] >= tile_end[None, :]).astype(i32), axis=1)
    tile_valid = (t < total_tiles).astype(i32)
    last_e = jnp.sum((total_tiles - 1 >= tile_end).astype(i32))
    tile_expert = jnp.where(tile_valid > 0, jnp.minimum(te, N_EXPERTS - 1), last_e).astype(i32)
    oh_t = (tile_expert[:, None] == er).astype(i32)
    grans_left = jnp.sum(oh_t * (g_e - (t[:, None] - tile_start[None, :]) * TILE_GRANS), axis=1)
    tile_subs = tile_valid * jnp.clip((grans_left + SUB_GRANS - 1) // SUB_GRANS, 0, TILE_GRANS // SUB_GRANS)

    k = jnp.arange(CHUNK_GRANS, dtype=i32)
    seg_end = seg_start + gr
    e_of = jnp.sum((k[None, :, None] >= seg_end[:, None, :]).astype(i32), axis=2)
    oh_e = (jnp.minimum(e_of, N_EXPERTS - 1)[:, :, None] == er).astype(i32)
    base = tile_start[None, :] * TILE_GRANS + prefix - seg_start
    dst = jnp.sum(oh_e * base[:, None, :], axis=2) + k[None, :]
    valid_src = k[None, :] < chunk_total[:, None]
    dst_of_src = jnp.where(valid_src, dst, 0).astype(i32).reshape(-1)

    d = jnp.arange(n_tiles * TILE_GRANS, dtype=i32)
    oh_d = (jnp.repeat(tile_expert, TILE_GRANS)[:, None] == er).astype(i32)
    q = d - jnp.sum(oh_d * tile_start[None, :], axis=1) * TILE_GRANS
    incl_d = jnp.sum(oh_d[:, :, None] * (prefix + gr).T[None], axis=1)
    c_d = jnp.sum((q[:, None] >= incl_d).astype(i32), axis=1)
    oh_c = (jnp.minimum(c_d, nc - 1)[:, None] == jnp.arange(nc, dtype=i32)).astype(i32)
    cbase = jnp.arange(nc, dtype=i32)[:, None] * CHUNK_GRANS + seg_start - prefix
    sel = jnp.sum(oh_c[:, :, None] * oh_d[:, None, :] * cbase[None], axis=(1, 2))
    valid_dst = (jnp.repeat(tile_valid, TILE_GRANS) > 0) & (q >= 0) & (q < jnp.sum(oh_d * g_e[None, :], axis=1))
    src_of_dst = jnp.where(valid_dst, sel + q, 0).astype(i32)
    return tile_expert, tile_subs.astype(i32), src_of_dst, dst_of_src


def kernel(x, c, ada_w, ada_b, ln_g, ln_b, ev_w_in, ev_pool_w, ev_pool_scale, ev_sink, ev_w_out, od_w_in, od_sg_ln_g, od_sg_ln_b, od_w_s, od_b_s, od_w_out, ffn_w_gate, ffn_w_up, ffn_w_down, moe_w_router, moe_w_gate, moe_w_up, moe_w_down):
    B, S, _ = x.shape
    T = B * S
    x2d = x.reshape(T, D)
    mod = _adaln(c, ada_w, ada_b)

    w_in = ev_w_in[0].astype(bf16)
    q0, k0, v0 = POOL_WIDTH, POOL_WIDTH + ATTN_WIDTH, POOL_WIDTH + ATTN_WIDTH + KV_WIDTH
    w_pk = jnp.concatenate([w_in[:, :q0], w_in[:, k0:v0]], axis=1)
    w_qv_t = jnp.concatenate([w_in[:, q0:k0], w_in[:, v0:]], axis=1).T
    p, k, qt, vt = _ev_in(x2d, mod[0], w_pk, w_qv_t, S)
    x2d = _ev_mix(x2d, mod[0], p, qt, k, vt, ev_pool_w[0].astype(bf16), ev_pool_scale[0][None, :],
                  ev_sink[0], ev_w_out[0].astype(bf16), ln_g[0, 0][None, :], ln_b[0, 0][None, :], B, S)
    x2d = _ffn(x2d, mod[0], ffn_w_gate[0], ffn_w_up[0], ffn_w_down[0],
               ln_g[0, 1][None, :], ln_b[0, 1][None, :], S)

    x2d = _sg_mix(x2d, mod[1], od_w_in[0].astype(bf16), od_sg_ln_g[0][None, :], od_sg_ln_b[0][None, :],
                  od_w_s[0].astype(bf16), od_b_s[0].T, od_w_out[0].astype(bf16),
                  ln_g[1, 0][None, :], ln_b[1, 0][None, :], S)

    hs, route, cnt = _route(x2d, mod[1], moe_w_router[0].T, S)
    nc = T // ROUTE_W
    n_tiles = (nc * CHUNK_GRANS) // TILE_GRANS + N_EXPERTS
    tile_expert, tile_subs, src_of_dst, dst_of_src = _routing_tables(cnt[:, :, 0], n_tiles)
    o = _experts(tile_expert, tile_subs, src_of_dst, hs.reshape(nc * CHUNK_SLOTS, D),
                 moe_w_gate[0], moe_w_up[0], moe_w_down[0])
    x2d = _combine(dst_of_src, x2d, mod[1], o, jnp.swapaxes(route, 1, 2),
                   ln_g[1, 1][None, :], ln_b[1, 1][None, :], S)
    return x2d.reshape(B, S, D)
```

```python
import functools
import math

import jax
import jax.numpy as jnp
from jax import lax
from jax.experimental import pallas as pl
from jax.experimental.pallas import tpu as pltpu

D = 1024
DEPTH = 2
ALPHA = (2.0 * DEPTH) ** 0.25
LN_EPS = 1e-5

POOL_WINDOWS = (2, 4, 8, 16)
POOL_CH = 128
POOL_WIDTH = 512
HEAD_DIM = 64
N_Q_HEADS = 8
N_KV_HEADS = 2
Q_GROUP = 4
ATTN_WIDTH = 512
KV_WIDTH = 128
BLOCK = 128
EVEN_IN = 1280
POOL_HALO = 8

CHUNK = 128
SG_GROUPS = 8
SG_CH = 128

N_EXPERTS = 8

ROUTE_W = 512
GRAN = 16
CHUNK_SLOTS = 2 * ROUTE_W + N_EXPERTS * GRAN
CHUNK_GRANS = CHUNK_SLOTS // GRAN
MOE_SUB = 256
MOE_FAST = 2048
MOE_TM = MOE_FAST + MOE_SUB
TILE_GRANS = MOE_TM // GRAN
SUB_GRANS = MOE_SUB // GRAN
MOE_TF = 512

VMEM_LIMIT = 48 * 1024 * 1024
VMEM_LIMIT_BIG = 56 * 1024 * 1024

bf16 = jnp.bfloat16
f32 = jnp.float32


def _dot(a, b):
    return jnp.dot(a, b, preferred_element_type=f32)


def _split_bf16(a):
    hi = a.astype(bf16)
    lo = (a - hi.astype(f32)).astype(bf16)
    return hi, lo


def _layer_norm(x, g, b):
    mu = jnp.mean(x, axis=-1, keepdims=True)
    xc = x - mu
    var = jnp.mean(xc * xc, axis=-1, keepdims=True)
    return xc * lax.rsqrt(var + LN_EPS) * g + b


def _silu(x):
    return x * jax.nn.sigmoid(x)


def _gelu_tanh(x):
    c = math.sqrt(2.0 / math.pi)
    return x * (0.5 * (1.0 + jnp.tanh(c * (x + 0.044715 * (x * x * x)))))


def _adaln_kernel(c_ref, w_ref, b_ref, o_ref):
    cond = _silu(c_ref[...])
    c_hi, c_lo = _split_bf16(cond)
    w_hi, w_lo = _split_bf16(w_ref[...])
    acc = _dot(c_hi, w_hi) + (_dot(c_lo, w_hi) + _dot(c_hi, w_lo))
    o_ref[...] = acc + b_ref[...]


def _adaln(c, ada_w, ada_b):
    B = c.shape[0]
    tn = 1024
    c_pad = jnp.zeros((8, D), f32).at[:B].set(c)
    out = pl.pallas_call(
        _adaln_kernel,
        grid=(DEPTH, 6 * D // tn),
        in_specs=[
            pl.BlockSpec((8, D), lambda l, j: (0, 0)),
            pl.BlockSpec((None, D, tn), lambda l, j: (l, 0, j)),
            pl.BlockSpec((None, 1, tn), lambda l, j: (l, 0, j)),
        ],
        out_specs=pl.BlockSpec((None, 8, tn), lambda l, j: (l, 0, j)),
        out_shape=jax.ShapeDtypeStruct((DEPTH, 8, 6 * D), f32),
        compiler_params=pltpu.CompilerParams(vmem_limit_bytes=VMEM_LIMIT),
        name="adaln",
    )(c_pad, ada_w, ada_b.reshape(DEPTH, 1, 6 * D))
    return out[:, :B].reshape(DEPTH, B, 6, D)


def _ev_in_kernel(x_ref, mod_ref, wpk_ref, wqvt_ref, p_ref, k_ref, qt_ref, vt_ref):
    h = (x_ref[...] * (1.0 + mod_ref[1:2, :]) + mod_ref[0:1, :]).astype(bf16)
    zpk = _dot(h, wpk_ref[...])
    p_ref[...] = zpk[:, :POOL_WIDTH]
    k_ref[...] = zpk[:, POOL_WIDTH:].astype(bf16)
    zt = lax.dot_general(wqvt_ref[...], h, (((1,), (1,)), ((), ())), preferred_element_type=f32)
    qt_ref[...] = (zt[:ATTN_WIDTH] * (HEAD_DIM ** -0.5)).astype(bf16)
    vt_ref[...] = zt[ATTN_WIDTH:].astype(bf16)


def _ev_in(x2d, mod_l, w_pk, w_qv_t, S):
    T = x2d.shape[0]
    tm = 512
    tpb = S // tm
    return pl.pallas_call(
        _ev_in_kernel,
        grid=(T // tm,),
        in_specs=[
            pl.BlockSpec((tm, D), lambda i: (i, 0)),
            pl.BlockSpec((None, 6, D), lambda i: (i // tpb, 0, 0)),
            pl.BlockSpec((D, POOL_WIDTH + KV_WIDTH), lambda i: (0, 0)),
            pl.BlockSpec((ATTN_WIDTH + KV_WIDTH, D), lambda i: (0, 0)),
        ],
        out_specs=[
            pl.BlockSpec((tm, POOL_WIDTH), lambda i: (i, 0)),
            pl.BlockSpec((tm, KV_WIDTH), lambda i: (i, 0)),
            pl.BlockSpec((ATTN_WIDTH, tm), lambda i: (0, i)),
            pl.BlockSpec((KV_WIDTH, tm), lambda i: (0, i)),
        ],
        out_shape=[
            jax.ShapeDtypeStruct((T, POOL_WIDTH), f32),
            jax.ShapeDtypeStruct((T, KV_WIDTH), bf16),
            jax.ShapeDtypeStruct((ATTN_WIDTH, T), bf16),
            jax.ShapeDtypeStruct((KV_WIDTH, T), bf16),
        ],
        compiler_params=pltpu.CompilerParams(vmem_limit_bytes=VMEM_LIMIT),
        name="ev_in",
    )(x2d, mod_l, w_pk, w_qv_t)


def _ev_mix_kernel(x_ref, mod_ref, p_ref, pp_ref, pn_ref, qt_ref,
                   k_ref, kp_ref, kn_ref, vt_ref, vtp_ref, vtn_ref,
                   bias_ref, sink_ref, wpool_ref, pscale_ref, wout_ref, lng_ref, lnb_ref,
                   o_ref, pext_ref, pooled_ref, kext_ref, vext_ref, ybt_ref, mix_ref, *, S, tq):
    i = pl.program_id(1)
    n_tiles = S // tq
    is_first = i == 0
    is_last = i == n_tiles - 1
    H = POOL_HALO

    p = p_ref[...]
    pext_ref[0:H, :] = jnp.where(is_first, 0.0, pp_ref[...])
    pext_ref[H:H + tq, :] = p
    pext_ref[H + tq:, :] = jnp.where(is_last, 0.0, pn_ref[...])
    near = lax.broadcasted_iota(jnp.int32, (H, 1), 0)
    for g, w in enumerate(POOL_WINDOWS):
        cs = slice(g * POOL_CH, (g + 1) * POOL_CH)
        r = w // 2
        wsum = pext_ref[H - r:H - r + tq, cs]
        for d in range(-r + 1, r + 1):
            wsum = wsum + pext_ref[H + d:H + d + tq, cs]
        pooled_ref[:, cs] = wsum / float(w + 1) - p[:, cs]
        r = w // 2
        cnt_head = (jnp.minimum(near, r) + (r + 1)).astype(f32)
        cnt_tail = (jnp.minimum(H - 1 - near, r) + (r + 1)).astype(f32)
        cnt_head = jnp.where(is_first, cnt_head, float(w + 1))
        cnt_tail = jnp.where(is_last, cnt_tail, float(w + 1))
        pooled_ref[0:H, cs] = wsum[0:H] / cnt_head - p[0:H, cs]
        pooled_ref[tq - H:tq, cs] = wsum[tq - H:tq] / cnt_tail - p[tq - H:tq, cs]
        ya = _dot(pooled_ref[:, cs].astype(bf16), wpool_ref[g])
        mix_ref[:, cs] = (ya * pscale_ref[:, cs]).astype(bf16)

    kext_ref[0:BLOCK, :] = kp_ref[...]
    kext_ref[BLOCK:BLOCK + tq, :] = k_ref[...]
    kext_ref[BLOCK + tq:, :] = kn_ref[...]
    vext_ref[:, 0:BLOCK] = vtp_ref[...]
    vext_ref[:, BLOCK:BLOCK + tq] = vt_ref[...]
    vext_ref[:, BLOCK + tq:] = vtn_ref[...]

    n_blocks = S // BLOCK
    zeros_q = jnp.zeros((HEAD_DIM, Q_GROUP * BLOCK), bf16)
    for n in range(tq // BLOCK):
        gb = i * (tq // BLOCK) + n
        variant = jnp.where(gb == 0, 1, jnp.where(gb == n_blocks - 1, 2, 0))
        cols = slice(n * BLOCK, (n + 1) * BLOCK)
        kw = kext_ref[n * BLOCK:n * BLOCK + 3 * BLOCK, :]
        for kvh in range(N_KV_HEADS):
            qst = jnp.concatenate(
                [qt_ref[(kvh * Q_GROUP + gq) * HEAD_DIM:(kvh * Q_GROUP + gq + 1) * HEAD_DIM, cols]
                 for gq in range(Q_GROUP)], axis=1)
            qst = jnp.concatenate([qst, zeros_q] if kvh == 0 else [zeros_q, qst], axis=0)
            s = _dot(kw, qst) + bias_ref[variant, kvh]
            sink = sink_ref[kvh]
            m = jnp.maximum(jnp.max(s, axis=0, keepdims=True), sink)
            e = jnp.exp(s - m)
            denom = jnp.sum(e, axis=0, keepdims=True) + jnp.exp(sink - m)
            vwt = vext_ref[kvh * HEAD_DIM:(kvh + 1) * HEAD_DIM, n * BLOCK:n * BLOCK + 3 * BLOCK]
            out = _dot(vwt, e.astype(bf16)) / denom
            for gq in range(Q_GROUP):
                hq = kvh * Q_GROUP + gq
                ybt_ref[hq * HEAD_DIM:(hq + 1) * HEAD_DIM, cols] = out[:, gq * BLOCK:(gq + 1) * BLOCK]
    mix_ref[:, POOL_WIDTH:] = ybt_ref[...].T.astype(bf16)

    for r in range(2):
        rr = slice(r * (tq // 2), (r + 1) * (tq // 2))
        y = _dot(mix_ref[rr, :], wout_ref[...])
        xr = ALPHA * x_ref[rr, :] + (1.0 + mod_ref[2:3, :]) * y
        o_ref[rr, :] = _layer_norm(xr, lng_ref[...], lnb_ref[...])


def _ev_mix(x2d, mod_l, p, qt, k, vt, w_pool, pool_scale, sink, w_out, ln_g, ln_b, B, S):
    T = x2d.shape[0]
    tq = 512
    nt = S // tq
    kb = tq // BLOCK
    pb = tq // POOL_HALO
    n_kblocks = T // BLOCK
    n_pblocks = T // POOL_HALO

    def main(b, i): return (b * nt + i, 0)
    def kprev(b, i): return (jnp.maximum((b * nt + i) * kb - 1, 0), 0)
    def knext(b, i): return (jnp.minimum((b * nt + i + 1) * kb, n_kblocks - 1), 0)
    def pprev(b, i): return (jnp.maximum((b * nt + i) * pb - 1, 0), 0)
    def pnext(b, i): return (jnp.minimum((b * nt + i + 1) * pb, n_pblocks - 1), 0)
    def const2(b, i): return (0, 0)

    def tmain(b, i): return (0, b * nt + i)
    def tprev(b, i): return (0, jnp.maximum((b * nt + i) * kb - 1, 0))
    def tnext(b, i): return (0, jnp.minimum((b * nt + i + 1) * kb, n_kblocks - 1))

    assert S // BLOCK >= 2
    kj = jnp.arange(3 * BLOCK)[:, None]
    qi = jnp.arange(BLOCK)[None, :]
    dist = jnp.abs(kj - BLOCK - qi)
    slopes = 2.0 ** (-8.0 * jnp.arange(1, N_Q_HEADS + 1, dtype=f32) / N_Q_HEADS)
    alibi = -slopes[:, None, None] * dist.astype(f32)[None]
    in_window = dist <= BLOCK
    key_ok = jnp.stack([kj >= 0, kj >= BLOCK, kj < 2 * BLOCK])
    bias = jnp.where((in_window[None] & key_ok)[:, None], alibi[None], -1e30)
    bias = bias.reshape(3, N_KV_HEADS, Q_GROUP, 3 * BLOCK, BLOCK).transpose(0, 1, 3, 2, 4)
    bias = bias.reshape(3, N_KV_HEADS, 3 * BLOCK, Q_GROUP * BLOCK)
    sink_row = jnp.repeat(sink.astype(f32).reshape(N_KV_HEADS, Q_GROUP), BLOCK, axis=1)[:, None, :]

    assert tq >= 2 * POOL_HALO and max(POOL_WINDOWS) // 2 <= POOL_HALO and nt >= 2

    kernel = functools.partial(_ev_mix_kernel, S=S, tq=tq)
    return pl.pallas_call(
        kernel,
        grid=(B, nt),
        in_specs=[
            pl.BlockSpec((tq, D), main),
            pl.BlockSpec((None, 6, D), lambda b, i: (b, 0, 0)),
            pl.BlockSpec((tq, POOL_WIDTH), main),
            pl.BlockSpec((POOL_HALO, POOL_WIDTH), pprev),
            pl.BlockSpec((POOL_HALO, POOL_WIDTH), pnext),
            pl.BlockSpec((ATTN_WIDTH, tq), tmain),
            pl.BlockSpec((tq, KV_WIDTH), main),
            pl.BlockSpec((BLOCK, KV_WIDTH), kprev),
            pl.BlockSpec((BLOCK, KV_WIDTH), knext),
            pl.BlockSpec((KV_WIDTH, tq), tmain),
            pl.BlockSpec((KV_WIDTH, BLOCK), tprev),
            pl.BlockSpec((KV_WIDTH, BLOCK), tnext),
            pl.BlockSpec((3, N_KV_HEADS, 3 * BLOCK, Q_GROUP * BLOCK), lambda b, i: (0, 0, 0, 0)),
            pl.BlockSpec((N_KV_HEADS, 1, Q_GROUP * BLOCK), lambda b, i: (0, 0, 0)),
            pl.BlockSpec((len(POOL_WINDOWS), POOL_CH, POOL_CH), lambda b, i: (0, 0, 0)),
            pl.BlockSpec((1, POOL_WIDTH), const2),
            pl.BlockSpec((D, D), const2),
            pl.BlockSpec((1, D), const2),
            pl.BlockSpec((1, D), const2),
        ],
        out_specs=pl.BlockSpec((tq, D), main),
        out_shape=jax.ShapeDtypeStruct((T, D), f32),
        scratch_shapes=[
            pltpu.VMEM((tq + 2 * POOL_HALO, POOL_WIDTH), f32),
            pltpu.VMEM((tq, POOL_WIDTH), f32),
            pltpu.VMEM((tq + 2 * BLOCK, KV_WIDTH), bf16),
            pltpu.VMEM((KV_WIDTH, tq + 2 * BLOCK), bf16),
            pltpu.VMEM((ATTN_WIDTH, tq), f32),
            pltpu.VMEM((tq, D), bf16),
        ],
        compiler_params=pltpu.CompilerParams(vmem_limit_bytes=VMEM_LIMIT),
        name="ev_mix",
    )(x2d, mod_l, p, p, p, qt, k, k, k, vt, vt, vt, bias, sink_row, w_pool, pool_scale, w_out, ln_g, ln_b)


def _load_rounded(w_hbm_ref, w_ref, stage_ref, sem):
    rows = stage_ref.shape[1]
    n = w_hbm_ref.shape[0] // rows

    def copy(c):
        return pltpu.make_async_copy(w_hbm_ref.at[pl.ds(c * rows, rows), :], stage_ref.at[c % 2], sem.at[c % 2])

    copy(0).start()
    for c in range(n):
        if c + 1 < n:
            copy(c + 1).start()
        copy(c).wait()
        w_ref[pl.ds(c * rows, rows), :] = stage_ref[c % 2].astype(bf16)


def _ffn_kernel(x_ref, mod_ref, wg_hbm_ref, wu_hbm_ref, wd_hbm_ref, lng_ref, lnb_ref, o_ref,
                wg_ref, wu_ref, wd_ref, stage_in_ref, stage_out_ref, sem):
    @pl.when(pl.program_id(0) == 0)
    def _():
        _load_rounded(wg_hbm_ref, wg_ref, stage_in_ref, sem)
        _load_rounded(wu_hbm_ref, wu_ref, stage_in_ref, sem)
        _load_rounded(wd_hbm_ref, wd_ref, stage_out_ref, sem)

    x = x_ref[...]
    h = (x * (1.0 + mod_ref[4:5, :]) + mod_ref[3:4, :]).astype(bf16)
    a = _silu(_dot(h, wg_ref[...])) * _dot(h, wu_ref[...])
    y = _dot(a.astype(bf16), wd_ref[...])
    xr = ALPHA * x + (1.0 + mod_ref[5:6, :]) * y
    o_ref[...] = _layer_norm(xr, lng_ref[...], lnb_ref[...])


def _ffn(x2d, mod_l, w_gate, w_up, w_down, ln_g, ln_b, S):
    T = x2d.shape[0]
    tm = 512
    dff = w_gate.shape[1]
    tpb = S // tm
    n_stage = 8
    return pl.pallas_call(
        _ffn_kernel,
        grid=(T // tm,),
        in_specs=[
            pl.BlockSpec((tm, D), lambda i: (i, 0)),
            pl.BlockSpec((None, 6, D), lambda i: (i // tpb, 0, 0)),
            pl.BlockSpec(memory_space=pl.ANY),
            pl.BlockSpec(memory_space=pl.ANY),
            pl.BlockSpec(memory_space=pl.ANY),
            pl.BlockSpec((1, D), lambda i: (0, 0)),
            pl.BlockSpec((1, D), lambda i: (0, 0)),
        ],
        out_specs=pl.BlockSpec((tm, D), lambda i: (i, 0)),
        out_shape=jax.ShapeDtypeStruct((T, D), f32),
        scratch_shapes=[
            pltpu.VMEM((D, dff), bf16),
            pltpu.VMEM((D, dff), bf16),
            pltpu.VMEM((dff, D), bf16),
            pltpu.VMEM((2, D // n_stage, dff), f32),
            pltpu.VMEM((2, dff // n_stage, D), f32),
            pltpu.SemaphoreType.DMA((2,)),
        ],
        compiler_params=pltpu.CompilerParams(vmem_limit_bytes=VMEM_LIMIT_BIG),
        name="ffn",
    )(x2d, mod_l, w_gate, w_up, w_down, ln_g, ln_b)


def _sg_kernel(x_ref, mod_ref, win_ref, sgg_ref, sgb_ref, ws_ref, bst_ref, wout_ref, lng_ref, lnb_ref,
               o_ref, gate_ref, *, tm):
    x = x_ref[...]
    h = (x * (1.0 + mod_ref[1:2, :]) + mod_ref[0:1, :]).astype(bf16)
    v = _layer_norm(_gelu_tanh(_dot(h, win_ref[:, D:])), sgg_ref[...], sgb_ref[...]).astype(bf16)
    u = _gelu_tanh(_dot(h, win_ref[:, :D]))
    for n in range(tm // CHUNK):
        rows = slice(n * CHUNK, (n + 1) * CHUNK)
        for g in range(SG_GROUPS):
            cols = slice(g * SG_CH, (g + 1) * SG_CH)
            sv = _dot(ws_ref[g], v[rows, cols]) + bst_ref[:, g:g + 1]
            gate_ref[rows, cols] = (u[rows, cols] * sv).astype(bf16)
    y = _dot(gate_ref[...], wout_ref[...])
    xr = ALPHA * x + (1.0 + mod_ref[2:3, :]) * y
    o_ref[...] = _layer_norm(xr, lng_ref[...], lnb_ref[...])


def _sg_mix(x2d, mod_l, w_in, sg_g, sg_b, w_s, b_s_t, w_out, ln_g, ln_b, S):
    T = x2d.shape[0]
    tm = 512
    tpb = S // tm
    c2 = lambda i: (0, 0)
    return pl.pallas_call(
        functools.partial(_sg_kernel, tm=tm),
        grid=(T // tm,),
        in_specs=[
            pl.BlockSpec((tm, D), lambda i: (i, 0)),
            pl.BlockSpec((None, 6, D), lambda i: (i // tpb, 0, 0)),
            pl.BlockSpec((D, 2 * D), c2),
            pl.BlockSpec((1, D), c2),
            pl.BlockSpec((1, D), c2),
            pl.BlockSpec((SG_GROUPS, CHUNK, CHUNK), lambda i: (0, 0, 0)),
            pl.BlockSpec((CHUNK, SG_GROUPS), c2),
            pl.BlockSpec((D, D), c2),
            pl.BlockSpec((1, D), c2),
            pl.BlockSpec((1, D), c2),
        ],
        out_specs=pl.BlockSpec((tm, D), lambda i: (i, 0)),
        out_shape=jax.ShapeDtypeStruct((T, D), f32),
        scratch_shapes=[pltpu.VMEM((tm, D), bf16)],
        compiler_params=pltpu.CompilerParams(vmem_limit_bytes=VMEM_LIMIT),
        name="sg_mix",
    )(x2d, mod_l, w_in, sg_g, sg_b, w_s, b_s_t, w_out, ln_g, ln_b)


def _route_kernel(x_ref, mod_ref, wrt_ref, hs_ref, route_ref, cnt_ref):
    W = ROUTE_W
    h = x_ref[...] * (1.0 + mod_ref[4:5, :]) + mod_ref[3:4, :]
    h_hi, h_lo = _split_bf16(h)
    w_hi, w_lo = _split_bf16(wrt_ref[...])
    nt = (((1,), (1,)), ((), ()))
    logits = (lax.dot_general(w_hi, h_hi, nt, preferred_element_type=f32)
              + (lax.dot_general(w_hi, h_lo, nt, preferred_element_type=f32)
                 + lax.dot_general(w_lo, h_hi, nt, preferred_element_type=f32)))

    eidx = lax.broadcasted_iota(jnp.int32, (N_EXPERTS, W), 0)
    m1 = jnp.max(logits, axis=0, keepdims=True)
    i1 = jnp.min(jnp.where(logits == m1, eidx, N_EXPERTS), axis=0, keepdims=True)
    sel1 = eidx == i1
    rest = jnp.where(sel1, -jnp.inf, logits)
    m2 = jnp.max(rest, axis=0, keepdims=True)
    i2 = jnp.min(jnp.where(rest == m2, eidx, N_EXPERTS), axis=0, keepdims=True)
    sel2 = eidx == i2
    e2 = jnp.exp(m2 - m1)
    g1 = 1.0 / (1.0 + e2)
    g2 = e2 / (1.0 + e2)

    a1 = sel1.astype(f32)
    a2 = sel2.astype(f32)
    assign = a1 + a2
    counts = jnp.sum(assign, axis=1, keepdims=True)
    grans = jnp.ceil(counts * (1.0 / GRAN))
    sub = lax.broadcasted_iota(jnp.int32, (N_EXPERTS, 1), 0)
    seg = jnp.zeros((N_EXPERTS, 1), f32)
    for e in range(N_EXPERTS - 1):
        seg = seg + jnp.where(sub > e, grans[e:e + 1, :] * GRAN, 0.0)
    tr = lax.broadcasted_iota(jnp.int32, (W, W), 0)
    tc = lax.broadcasted_iota(jnp.int32, (W, W), 1)
    upper = (tr < tc).astype(bf16)
    rank = _dot(assign.astype(bf16), upper)
    slot = seg + rank
    pos1 = jnp.sum(a1 * slot, axis=0, keepdims=True)
    pos2 = jnp.sum(a2 * slot, axis=0, keepdims=True)

    srow = lax.broadcasted_iota(jnp.int32, (CHUNK_SLOTS, W), 0)
    perm = ((srow == pos1.astype(jnp.int32)) | (srow == pos2.astype(jnp.int32)))
    hs_ref[...] = _dot(perm.astype(f32).astype(bf16), h_hi).astype(bf16)

    ridx = lax.broadcasted_iota(jnp.int32, (8, W), 0)
    route = jnp.where(ridx == 0, pos1, jnp.where(ridx == 1, pos2, jnp.where(ridx == 2, g1, jnp.where(ridx == 3, g2, 0.0))))
    route_ref[...] = route
    cnt_ref[...] = jnp.broadcast_to(counts, (N_EXPERTS, 128)).astype(jnp.int32)


def _route(x2d, mod_l, w_router_t, S):
    T = x2d.shape[0]
    W = ROUTE_W
    nc = T // W
    tpb = S // W
    return pl.pallas_call(
        _route_kernel,
        grid=(nc,),
        in_specs=[
            pl.BlockSpec((W, D), lambda c: (c, 0)),
            pl.BlockSpec((None, 6, D), lambda c: (c // tpb, 0, 0)),
            pl.BlockSpec((N_EXPERTS, D), lambda c: (0, 0)),
        ],
        out_specs=[
            pl.BlockSpec((None, CHUNK_SLOTS, D), lambda c: (c, 0, 0)),
            pl.BlockSpec((None, 8, W), lambda c: (c, 0, 0)),
            pl.BlockSpec((None, N_EXPERTS, 128), lambda c: (c, 0, 0)),
        ],
        out_shape=[
            jax.ShapeDtypeStruct((nc, CHUNK_SLOTS, D), bf16),
            jax.ShapeDtypeStruct((nc, 8, W), f32),
            jax.ShapeDtypeStruct((nc, N_EXPERTS, 128), jnp.int32),
        ],
        compiler_params=pltpu.CompilerParams(vmem_limit_bytes=VMEM_LIMIT),
        name="route",
    )(x2d, mod_l, w_router_t)


def _granule_copy(src_ref, buf_ref, sem, idx_ref, base, g):
    row = pl.multiple_of(idx_ref[base + g] * GRAN, GRAN)
    return pltpu.make_async_copy(src_ref.at[pl.ds(row, GRAN), :], buf_ref.at[pl.ds(g * GRAN, GRAN), :], sem)


def _gather_start(src_ref, buf_ref, sem, idx_ref, base, n):
    for g in range(n):
        _granule_copy(src_ref, buf_ref, sem, idx_ref, base, g).start()


def _gather_wait(src_ref, buf_ref, sem, idx_ref, base, n):
    for g in range(n):
        _granule_copy(src_ref, buf_ref, sem, idx_ref, base, g).wait()


def _expert_kernel(te_ref, tv_ref, src_ref, hs_ref, wg_ref, wu_ref, wd_ref, o_ref,
                   xbuf_ref, sem, acc_ref, wgb_ref, wub_ref, wdb_ref):
    i = pl.program_id(0)
    j = pl.program_id(1)
    n_tiles = pl.num_programs(0)
    last = pl.num_programs(1) - 1
    valid = tv_ref[i] > 0
    slot = i % 2
    nxt = jnp.minimum(i + 1, n_tiles - 1)

    @pl.when(j == 0)
    def _():
        @pl.when(i == 0)
        def _():
            _gather_start(hs_ref, xbuf_ref.at[0], sem.at[0], src_ref, 0, TILE_GRANS)

        @pl.when(valid)
        def _():
            _gather_wait(hs_ref, xbuf_ref.at[slot], sem.at[slot], src_ref, i * TILE_GRANS, TILE_GRANS)

        @pl.when((i + 1 < n_tiles) & (tv_ref[nxt] > 0))
        def _():
            _gather_start(hs_ref, xbuf_ref.at[1 - slot], sem.at[1 - slot], src_ref, nxt * TILE_GRANS, TILE_GRANS)

    n_sub = tv_ref[i]
    fast_subs = MOE_FAST // MOE_SUB
    fast = n_sub >= fast_subs

    def round_weights():
        wgb_ref[...] = wg_ref[...].astype(bf16)
        wub_ref[...] = wu_ref[...].astype(bf16)
        wdb_ref[...] = wd_ref[...].astype(bf16)

    def swiglu(x):
        a = _silu(_dot(x, wgb_ref[...])) * _dot(x, wub_ref[...])
        return _dot(a.astype(bf16), wdb_ref[...])

    @pl.when(valid & (j == 0))
    def _():
        acc_ref[...] = jnp.zeros_like(acc_ref)

    @pl.when(fast)
    def _():
        round_weights()
        acc_ref[0:MOE_FAST, :] += swiglu(xbuf_ref[slot, 0:MOE_FAST, :])

    @pl.when(valid & jnp.logical_not(fast))
    def _():
        round_weights()

    @pl.when(valid)
    def _():
        def body(sb, carry):
            rows = pl.ds(pl.multiple_of(sb * MOE_SUB, MOE_SUB), MOE_SUB)
            acc_ref[rows, :] += swiglu(xbuf_ref[slot, rows, :])
            return carry

        lax.fori_loop(jnp.where(fast, fast_subs, 0), n_sub, body, 0)

    @pl.when(j == last)
    def _():
        @pl.when(valid)
        def _():
            o_ref[...] = acc_ref[...].astype(o_ref.dtype)

        @pl.when(jnp.logical_not(valid))
        def _():
            o_ref[...] = jnp.zeros_like(o_ref)


def _experts(tile_expert, tile_valid, src_of_dst, hs2d, w_gate, w_up, w_down):
    n_tiles = tile_expert.shape[0]
    dff = w_gate.shape[2]
    nff = dff // MOE_TF

    def jj(j, tv, i):
        return jnp.where(tv[i] > 0, j, nff - 1)

    grid_spec = pltpu.PrefetchScalarGridSpec(
        num_scalar_prefetch=3,
        grid=(n_tiles, nff),
        in_specs=[
            pl.BlockSpec(memory_space=pl.ANY),
            pl.BlockSpec((None, D, MOE_TF), lambda i, j, te, tv, sd: (te[i], 0, jj(j, tv, i))),
            pl.BlockSpec((None, D, MOE_TF), lambda i, j, te, tv, sd: (te[i], 0, jj(j, tv, i))),
            pl.BlockSpec((None, MOE_TF, D), lambda i, j, te, tv, sd: (te[i], jj(j, tv, i), 0)),
        ],
        out_specs=pl.BlockSpec((MOE_TM, D), lambda i, j, te, tv, sd: (i, 0)),
        scratch_shapes=[
            pltpu.VMEM((2, MOE_TM, D), bf16),
            pltpu.SemaphoreType.DMA((2,)),
            pltpu.VMEM((MOE_TM, D), f32),
            pltpu.VMEM((D, MOE_TF), bf16),
            pltpu.VMEM((D, MOE_TF), bf16),
            pltpu.VMEM((MOE_TF, D), bf16),
        ],
    )
    return pl.pallas_call(
        _expert_kernel,
        grid_spec=grid_spec,
        out_shape=jax.ShapeDtypeStruct((n_tiles * MOE_TM, D), bf16),
        compiler_params=pltpu.CompilerParams(vmem_limit_bytes=VMEM_LIMIT_BIG),
        name="experts",
    )(tile_expert, tile_valid, src_of_dst, hs2d, w_gate, w_up, w_down)


def _combine_kernel(ds_ref, x_ref, mod_ref, o_hbm_ref, rt_ref, lng_ref, lnb_ref, out_ref, obuf_ref, sem):
    W = ROUTE_W
    c = pl.program_id(0)
    nc = pl.num_programs(0)
    slot = c % 2
    nxt = jnp.minimum(c + 1, nc - 1)

    @pl.when(c == 0)
    def _():
        _gather_start(o_hbm_ref, obuf_ref.at[0], sem.at[0], ds_ref, 0, CHUNK_GRANS)

    _gather_wait(o_hbm_ref, obuf_ref.at[slot], sem.at[slot], ds_ref, c * CHUNK_GRANS, CHUNK_GRANS)

    _gather_start(o_hbm_ref, obuf_ref.at[1 - slot], sem.at[1 - slot], ds_ref, nxt * CHUNK_GRANS, CHUNK_GRANS)

    osv = obuf_ref[slot]
    scol = lax.broadcasted_iota(jnp.int32, (W // 2, CHUNK_SLOTS), 1)
    for r in range(2):
        rr = slice(r * (W // 2), (r + 1) * (W // 2))
        rt = rt_ref[rr, :]
        p1 = (scol == rt[:, 0:1].astype(jnp.int32)).astype(f32).astype(bf16)
        p2 = (scol == rt[:, 1:2].astype(jnp.int32)).astype(f32).astype(bf16)
        y = rt[:, 2:3] * _dot(p1, osv) + rt[:, 3:4] * _dot(p2, osv)
        xr = ALPHA * x_ref[rr, :] + (1.0 + mod_ref[5:6, :]) * y
        out_ref[rr, :] = _layer_norm(xr, lng_ref[...], lnb_ref[...])

    @pl.when(c == nc - 1)
    def _():
        _gather_wait(o_hbm_ref, obuf_ref.at[1 - slot], sem.at[1 - slot], ds_ref, nxt * CHUNK_GRANS, CHUNK_GRANS)


def _combine(dst_of_src, x2d, mod_l, o2d, route_t, ln_g, ln_b, S):
    T = x2d.shape[0]
    W = ROUTE_W
    tpb = S // W
    grid_spec = pltpu.PrefetchScalarGridSpec(
        num_scalar_prefetch=1,
        grid=(T // W,),
        in_specs=[
            pl.BlockSpec((W, D), lambda c, ds: (c, 0)),
            pl.BlockSpec((None, 6, D), lambda c, ds: (c // tpb, 0, 0)),
            pl.BlockSpec(memory_space=pl.ANY),
            pl.BlockSpec((None, W, 8), lambda c, ds: (c, 0, 0)),
            pl.BlockSpec((1, D), lambda c, ds: (0, 0)),
            pl.BlockSpec((1, D), lambda c, ds: (0, 0)),
        ],
        out_specs=pl.BlockSpec((W, D), lambda c, ds: (c, 0)),
        scratch_shapes=[
            pltpu.VMEM((2, CHUNK_SLOTS, D), bf16),
            pltpu.SemaphoreType.DMA((2,)),
        ],
    )
    return pl.pallas_call(
        _combine_kernel,
        grid_spec=grid_spec,
        out_shape=jax.ShapeDtypeStruct((T, D), f32),
        compiler_params=pltpu.CompilerParams(vmem_limit_bytes=VMEM_LIMIT),
        name="combine",
    )(dst_of_src, x2d, mod_l, o2d, route_t, ln_g, ln_b)


def _routing_tables(counts, n_tiles):
    nc = counts.shape[0]
    gr = (counts + GRAN - 1) // GRAN
    seg_start = jnp.cumsum(gr, axis=1) - gr
    chunk_total = jnp.sum(gr, axis=1)
    prefix = jnp.cumsum(gr, axis=0) - gr
    g_e = jnp.sum(gr, axis=0)
    tiles_e = (g_e + TILE_GRANS - 1) // TILE_GRANS
    tile_end = jnp.cumsum(tiles_e)
    tile_start = tile_end - tiles_e
    total_tiles = tile_end[-1]

    i32 = jnp.int32
    er = jnp.arange(N_EXPERTS, dtype=i32)
    t = jnp.arange(n_tiles, dtype=i32)
    te = jnp.sum((t[:, None] >= tile_end[None, :]).astype(i32), axis=1)
    tile_valid = (t < total_tiles).astype(i32)
    last_e = jnp.sum((total_tiles - 1 >= tile_end).astype(i32))
    tile_expert = jnp.where(tile_valid > 0, jnp.minimum(te, N_EXPERTS - 1), last_e).astype(i32)
    oh_t = (tile_expert[:, None] == er).astype(i32)
    grans_left = jnp.sum(oh_t * (g_e - (t[:, None] - tile_start[None, :]) * TILE_GRANS), axis=1)
    tile_subs = tile_valid * jnp.clip((grans_left + SUB_GRANS - 1) // SUB_GRANS, 0, TILE_GRANS // SUB_GRANS)

    k = jnp.arange(CHUNK_GRANS, dtype=i32)
    seg_end = seg_start + gr
    e_of = jnp.sum((k[None, :, None] >= seg_end[:, None, :]).astype(i32), axis=2)
    oh_e = (jnp.minimum(e_of, N_EXPERTS - 1)[:, :, None] == er).astype(i32)
    base = tile_start[None, :] * TILE_GRANS + prefix - seg_start
    dst = jnp.sum(oh_e * base[:, None, :], axis=2) + k[None, :]
    valid_src = k[None, :] < chunk_total[:, None]
    dst_of_src = jnp.where(valid_src, dst, 0).astype(i32).reshape(-1)

    d = jnp.arange(n_tiles * TILE_GRANS, dtype=i32)
    oh_d = (jnp.repeat(tile_expert, TILE_GRANS)[:, None] == er).astype(i32)
    q = d - jnp.sum(oh_d * tile_start[None, :], axis=1) * TILE_GRANS
    incl_d = jnp.sum(oh_d[:, :, None] * (prefix + gr).T[None], axis=1)
    c_d = jnp.sum((q[:, None] >= incl_d).astype(i32), axis=1)
    oh_c = (jnp.minimum(c_d, nc - 1)[:, None] == jnp.arange(nc, dtype=i32)).astype(i32)
    cbase = jnp.arange(nc, dtype=i32)[:, None] * CHUNK_GRANS + seg_start - prefix
    sel = jnp.sum(oh_c[:, :, None] * oh_d[:, None, :] * cbase[None], axis=(1, 2))
    valid_dst = (jnp.repeat(tile_valid, TILE_GRANS) > 0) & (q >= 0) & (q < jnp.sum(oh_d * g_e[None, :], axis=1))
    src_of_dst = jnp.where(valid_dst, sel + q, 0).astype(i32)
    return tile_expert, tile_subs.astype(i32), src_of_dst, dst_of_src


def kernel(x, c, ada_w, ada_b, ln_g, ln_b, ev_w_in, ev_pool_w, ev_pool_scale, ev_sink, ev_w_out, od_w_in, od_sg_ln_g, od_sg_ln_b, od_w_s, od_b_s, od_w_out, ffn_w_gate, ffn_w_up, ffn_w_down, moe_w_router, moe_w_gate, moe_w_up, moe_w_down):
    B, S, _ = x.shape
    T = B * S
    x2d = x.reshape(T, D)
    mod = _adaln(c, ada_w, ada_b)

    w_in = ev_w_in[0].astype(bf16)
    q0, k0, v0 = POOL_WIDTH, POOL_WIDTH + ATTN_WIDTH, POOL_WIDTH + ATTN_WIDTH + KV_WIDTH
    w_pk = jnp.concatenate([w_in[:, :q0], w_in[:, k0:v0]], axis=1)
    w_qv_t = jnp.concatenate([w_in[:, q0:k0], w_in[:, v0:]], axis=1).T
    p, k, qt, vt = _ev_in(x2d, mod[0], w_pk, w_qv_t, S)
    x2d = _ev_mix(x2d, mod[0], p, qt, k, vt, ev_pool_w[0].astype(bf16), ev_pool_scale[0][None, :],
                  ev_sink[0], ev_w_out[0].astype(bf16), ln_g[0, 0][None, :], ln_b[0, 0][None, :], B, S)
    x2d = _ffn(x2d, mod[0], ffn_w_gate[0], ffn_w_up[0], ffn_w_down[0],
               ln_g[0, 1][None, :], ln_b[0, 1][None, :], S)

    x2d = _sg_mix(x2d, mod[1], od_w_in[0].astype(bf16), od_sg_ln_g[0][None, :], od_sg_ln_b[0][None, :],
                  od_w_s[0].astype(bf16), od_b_s[0].T, od_w_out[0].astype(bf16),
                  ln_g[1, 0][None, :], ln_b[1, 0][None, :], S)

    hs, route, cnt = _route(x2d, mod[1], moe_w_router[0].T, S)
    nc = T // ROUTE_W
    n_tiles = (nc * CHUNK_GRANS) // TILE_GRANS + N_EXPERTS
    tile_expert, tile_subs, src_of_dst, dst_of_src = _routing_tables(cnt[:, :, 0], n_tiles)
    o = _experts(tile_expert, tile_subs, src_of_dst, hs.reshape(nc * CHUNK_SLOTS, D),
                 moe_w_gate[0], moe_w_up[0], moe_w_down[0])
    x2d = _combine(dst_of_src, x2d, mod[1], o, jnp.swapaxes(route, 1, 2),
                   ln_g[1, 1][None, :], ln_b[1, 1][None, :], S)
    return x2d.reshape(B, S, D)
```

```python
import functools
import math

import jax
import jax.numpy as jnp
from jax import lax
from jax.experimental import pallas as pl
from jax.experimental.pallas import tpu as pltpu

D = 1024
DEPTH = 2
ALPHA = (2.0 * DEPTH) ** 0.25
LN_EPS = 1e-5

POOL_WINDOWS = (2, 4, 8, 16)
POOL_CH = 128
POOL_WIDTH = 512
HEAD_DIM = 64
N_Q_HEADS = 8
N_KV_HEADS = 2
Q_GROUP = 4
ATTN_WIDTH = 512
KV_WIDTH = 128
BLOCK = 128
EVEN_IN = 1280
POOL_HALO = 8

CHUNK = 128
SG_GROUPS = 8
SG_CH = 128

N_EXPERTS = 8

ROUTE_W = 512
ROUTE_PER_STEP = 2
GRAN = 16
CHUNK_SLOTS = 2 * ROUTE_W + N_EXPERTS * GRAN
CHUNK_GRANS = CHUNK_SLOTS // GRAN
MOE_SUB = 256
MOE_FAST = 2048
MOE_TM = MOE_FAST + MOE_SUB
TILE_GRANS = MOE_TM // GRAN
SUB_GRANS = MOE_SUB // GRAN
MOE_TF = 512

VMEM_LIMIT = 48 * 1024 * 1024
VMEM_LIMIT_BIG = 56 * 1024 * 1024

bf16 = jnp.bfloat16
f32 = jnp.float32


def _dot(a, b):
    return jnp.dot(a, b, preferred_element_type=f32)


def _split_bf16(a):
    hi = a.astype(bf16)
    lo = (a - hi.astype(f32)).astype(bf16)
    return hi, lo


def _layer_norm(x, g, b):
    mu = jnp.mean(x, axis=-1, keepdims=True)
    xc = x - mu
    var = jnp.mean(xc * xc, axis=-1, keepdims=True)
    return xc * lax.rsqrt(var + LN_EPS) * g + b


def _silu(x):
    return x * jax.nn.sigmoid(x)


def _gelu_tanh(x):
    c = math.sqrt(2.0 / math.pi)
    return x * (0.5 * (1.0 + jnp.tanh(c * (x + 0.044715 * (x * x * x)))))


def _adaln_kernel(c_ref, w_ref, b_ref, o_ref):
    cond = _silu(c_ref[...])
    c_hi, c_lo = _split_bf16(cond)
    w_hi, w_lo = _split_bf16(w_ref[...])
    acc = _dot(c_hi, w_hi) + (_dot(c_lo, w_hi) + _dot(c_hi, w_lo))
    o_ref[...] = acc + b_ref[...]


def _adaln(c, ada_w, ada_b):
    B = c.shape[0]
    tn = 1024
    c_pad = jnp.zeros((8, D), f32).at[:B].set(c)
    out = pl.pallas_call(
        _adaln_kernel,
        grid=(DEPTH, 6 * D // tn),
        in_specs=[
            pl.BlockSpec((8, D), lambda l, j: (0, 0)),
            pl.BlockSpec((None, D, tn), lambda l, j: (l, 0, j)),
            pl.BlockSpec((None, 1, tn), lambda l, j: (l, 0, j)),
        ],
        out_specs=pl.BlockSpec((None, 8, tn), lambda l, j: (l, 0, j)),
        out_shape=jax.ShapeDtypeStruct((DEPTH, 8, 6 * D), f32),
        compiler_params=pltpu.CompilerParams(vmem_limit_bytes=VMEM_LIMIT),
        name="adaln",
    )(c_pad, ada_w, ada_b.reshape(DEPTH, 1, 6 * D))
    return out[:, :B].reshape(DEPTH, B, 6, D)


def _ev_in_kernel(x_ref, mod_ref, wpk_ref, wqvt_ref, p_ref, k_ref, qt_ref, vt_ref):
    h = (x_ref[...] * (1.0 + mod_ref[1:2, :]) + mod_ref[0:1, :]).astype(bf16)
    zpk = _dot(h, wpk_ref[...])
    p_ref[...] = zpk[:, :POOL_WIDTH]
    k_ref[...] = zpk[:, POOL_WIDTH:].astype(bf16)
    zt = lax.dot_general(wqvt_ref[...], h, (((1,), (1,)), ((), ())), preferred_element_type=f32)
    qt_ref[...] = (zt[:ATTN_WIDTH] * (HEAD_DIM ** -0.5)).astype(bf16)
    vt_ref[...] = zt[ATTN_WIDTH:].astype(bf16)


def _ev_in(x2d, mod_l, w_pk, w_qv_t, S):
    T = x2d.shape[0]
    tm = 512
    tpb = S // tm
    return pl.pallas_call(
        _ev_in_kernel,
        grid=(T // tm,),
        in_specs=[
            pl.BlockSpec((tm, D), lambda i: (i, 0)),
            pl.BlockSpec((None, 6, D), lambda i: (i // tpb, 0, 0)),
            pl.BlockSpec((D, POOL_WIDTH + KV_WIDTH), lambda i: (0, 0)),
            pl.BlockSpec((ATTN_WIDTH + KV_WIDTH, D), lambda i: (0, 0)),
        ],
        out_specs=[
            pl.BlockSpec((tm, POOL_WIDTH), lambda i: (i, 0)),
            pl.BlockSpec((tm, KV_WIDTH), lambda i: (i, 0)),
            pl.BlockSpec((ATTN_WIDTH, tm), lambda i: (0, i)),
            pl.BlockSpec((KV_WIDTH, tm), lambda i: (0, i)),
        ],
        out_shape=[
            jax.ShapeDtypeStruct((T, POOL_WIDTH), f32),
            jax.ShapeDtypeStruct((T, KV_WIDTH), bf16),
            jax.ShapeDtypeStruct((ATTN_WIDTH, T), bf16),
            jax.ShapeDtypeStruct((KV_WIDTH, T), bf16),
        ],
        compiler_params=pltpu.CompilerParams(vmem_limit_bytes=VMEM_LIMIT),
        name="ev_in",
    )(x2d, mod_l, w_pk, w_qv_t)


def _ev_mix_kernel(x_ref, mod_ref, p_ref, pp_ref, pn_ref, qt_ref,
                   k_ref, kp_ref, kn_ref, vt_ref, vtp_ref, vtn_ref,
                   bias_ref, sink_ref, wpool_ref, pscale_ref, wout_ref, lng_ref, lnb_ref,
                   o_ref, pext_ref, pooled_ref, kext_ref, vext_ref, ybt_ref, mix_ref, *, S, tq):
    i = pl.program_id(1)
    n_tiles = S // tq
    is_first = i == 0
    is_last = i == n_tiles - 1
    H = POOL_HALO

    p = p_ref[...]
    pext_ref[0:H, :] = jnp.where(is_first, 0.0, pp_ref[...])
    pext_ref[H:H + tq, :] = p
    pext_ref[H + tq:, :] = jnp.where(is_last, 0.0, pn_ref[...])
    near = lax.broadcasted_iota(jnp.int32, (H, 1), 0)

    def pool_group(g):
        w = POOL_WINDOWS[g]
        cs = slice(g * POOL_CH, (g + 1) * POOL_CH)
        r = w // 2
        wsum = pext_ref[H - r:H - r + tq, cs]
        for d in range(-r + 1, r + 1):
            wsum = wsum + pext_ref[H + d:H + d + tq, cs]
        pooled_ref[:, cs] = wsum / float(w + 1) - p[:, cs]
        cnt_head = (jnp.minimum(near, r) + (r + 1)).astype(f32)
        cnt_tail = (jnp.minimum(H - 1 - near, r) + (r + 1)).astype(f32)
        cnt_head = jnp.where(is_first, cnt_head, float(w + 1))
        cnt_tail = jnp.where(is_last, cnt_tail, float(w + 1))
        pooled_ref[0:H, cs] = wsum[0:H] / cnt_head - p[0:H, cs]
        pooled_ref[tq - H:tq, cs] = wsum[tq - H:tq] / cnt_tail - p[tq - H:tq, cs]
        ya = _dot(pooled_ref[:, cs].astype(bf16), wpool_ref[g])
        mix_ref[:, cs] = (ya * pscale_ref[:, cs]).astype(bf16)

    for g in range(len(POOL_WINDOWS)):
        pool_group(g)

    kext_ref[0:BLOCK, :] = kp_ref[...]
    kext_ref[BLOCK:BLOCK + tq, :] = k_ref[...]
    kext_ref[BLOCK + tq:, :] = kn_ref[...]
    vext_ref[:, 0:BLOCK] = vtp_ref[...]
    vext_ref[:, BLOCK:BLOCK + tq] = vt_ref[...]
    vext_ref[:, BLOCK + tq:] = vtn_ref[...]

    n_blocks = S // BLOCK
    zeros_q = jnp.zeros((HEAD_DIM, Q_GROUP * BLOCK), bf16)
    items = [(n, kvh) for n in range(tq // BLOCK) for kvh in range(N_KV_HEADS)]

    def scores(n, kvh):
        gb = i * (tq // BLOCK) + n
        variant = jnp.where(gb == 0, 1, jnp.where(gb == n_blocks - 1, 2, 0))
        cols = slice(n * BLOCK, (n + 1) * BLOCK)
        kw = kext_ref[n * BLOCK:n * BLOCK + 3 * BLOCK, :]
        qst = jnp.concatenate(
            [qt_ref[(kvh * Q_GROUP + gq) * HEAD_DIM:(kvh * Q_GROUP + gq + 1) * HEAD_DIM, cols]
             for gq in range(Q_GROUP)], axis=1)
        qst = jnp.concatenate([qst, zeros_q] if kvh == 0 else [zeros_q, qst], axis=0)
        return _dot(kw, qst) + bias_ref[variant, kvh]

    def softmax(n, kvh, s):
        sink = sink_ref[kvh]
        m = jnp.maximum(jnp.max(s, axis=0, keepdims=True), sink)
        e = jnp.exp(s - m)
        denom = jnp.sum(e, axis=0, keepdims=True) + jnp.exp(sink - m)
        return e.astype(bf16), denom

    def weighted_values(n, kvh, e, denom):
        cols = slice(n * BLOCK, (n + 1) * BLOCK)
        vwt = vext_ref[kvh * HEAD_DIM:(kvh + 1) * HEAD_DIM, n * BLOCK:n * BLOCK + 3 * BLOCK]
        out = _dot(vwt, e) / denom
        for gq in range(Q_GROUP):
            hq = kvh * Q_GROUP + gq
            ybt_ref[hq * HEAD_DIM:(hq + 1) * HEAD_DIM, cols] = out[:, gq * BLOCK:(gq + 1) * BLOCK]

    s_next = scores(*items[0])
    prev = None
    for idx, item in enumerate(items):
        s_cur = s_next
        if idx + 1 < len(items):
            s_next = scores(*items[idx + 1])
        cur = softmax(*item, s_cur)
        if prev is not None:
            weighted_values(*items[idx - 1], *prev)
        prev = cur
    weighted_values(*items[-1], *prev)
    mix_ref[:, POOL_WIDTH:] = ybt_ref[...].T.astype(bf16)

    halves = [slice(r * (tq // 2), (r + 1) * (tq // 2)) for r in range(2)]
    ys = [_dot(mix_ref[rr, :], wout_ref[...]) for rr in halves]
    for rr, y in zip(halves, ys):
        xr = ALPHA * x_ref[rr, :] + (1.0 + mod_ref[2:3, :]) * y
        o_ref[rr, :] = _layer_norm(xr, lng_ref[...], lnb_ref[...])


def _ev_mix(x2d, mod_l, p, qt, k, vt, w_pool, pool_scale, sink, w_out, ln_g, ln_b, B, S):
    T = x2d.shape[0]
    tq = 512
    nt = S // tq
    kb = tq // BLOCK
    pb = tq // POOL_HALO
    n_kblocks = T // BLOCK
    n_pblocks = T // POOL_HALO

    def main(b, i): return (b * nt + i, 0)
    def kprev(b, i): return (jnp.maximum((b * nt + i) * kb - 1, 0), 0)
    def knext(b, i): return (jnp.minimum((b * nt + i + 1) * kb, n_kblocks - 1), 0)
    def pprev(b, i): return (jnp.maximum((b * nt + i) * pb - 1, 0), 0)
    def pnext(b, i): return (jnp.minimum((b * nt + i + 1) * pb, n_pblocks - 1), 0)
    def const2(b, i): return (0, 0)

    def tmain(b, i): return (0, b * nt + i)
    def tprev(b, i): return (0, jnp.maximum((b * nt + i) * kb - 1, 0))
    def tnext(b, i): return (0, jnp.minimum((b * nt + i + 1) * kb, n_kblocks - 1))

    assert S // BLOCK >= 2
    kj = jnp.arange(3 * BLOCK)[:, None]
    qi = jnp.arange(BLOCK)[None, :]
    dist = jnp.abs(kj - BLOCK - qi)
    slopes = 2.0 ** (-8.0 * jnp.arange(1, N_Q_HEADS + 1, dtype=f32) / N_Q_HEADS)
    alibi = -slopes[:, None, None] * dist.astype(f32)[None]
    in_window = dist <= BLOCK
    key_ok = jnp.stack([kj >= 0, kj >= BLOCK, kj < 2 * BLOCK])
    bias = jnp.where((in_window[None] & key_ok)[:, None], alibi[None], -1e30)
    bias = bias.reshape(3, N_KV_HEADS, Q_GROUP, 3 * BLOCK, BLOCK).transpose(0, 1, 3, 2, 4)
    bias = bias.reshape(3, N_KV_HEADS, 3 * BLOCK, Q_GROUP * BLOCK)
    sink_row = jnp.repeat(sink.astype(f32).reshape(N_KV_HEADS, Q_GROUP), BLOCK, axis=1)[:, None, :]

    assert tq >= 2 * POOL_HALO and max(POOL_WINDOWS) // 2 <= POOL_HALO and nt >= 2

    kernel = functools.partial(_ev_mix_kernel, S=S, tq=tq)
    return pl.pallas_call(
        kernel,
        grid=(B, nt),
        in_specs=[
            pl.BlockSpec((tq, D), main),
            pl.BlockSpec((None, 6, D), lambda b, i: (b, 0, 0)),
            pl.BlockSpec((tq, POOL_WIDTH), main),
            pl.BlockSpec((POOL_HALO, POOL_WIDTH), pprev),
            pl.BlockSpec((POOL_HALO, POOL_WIDTH), pnext),
            pl.BlockSpec((ATTN_WIDTH, tq), tmain),
            pl.BlockSpec((tq, KV_WIDTH), main),
            pl.BlockSpec((BLOCK, KV_WIDTH), kprev),
            pl.BlockSpec((BLOCK, KV_WIDTH), knext),
            pl.BlockSpec((KV_WIDTH, tq), tmain),
            pl.BlockSpec((KV_WIDTH, BLOCK), tprev),
            pl.BlockSpec((KV_WIDTH, BLOCK), tnext),
            pl.BlockSpec((3, N_KV_HEADS, 3 * BLOCK, Q_GROUP * BLOCK), lambda b, i: (0, 0, 0, 0)),
            pl.BlockSpec((N_KV_HEADS, 1, Q_GROUP * BLOCK), lambda b, i: (0, 0, 0)),
            pl.BlockSpec((len(POOL_WINDOWS), POOL_CH, POOL_CH), lambda b, i: (0, 0, 0)),
            pl.BlockSpec((1, POOL_WIDTH), const2),
            pl.BlockSpec((D, D), const2),
            pl.BlockSpec((1, D), const2),
            pl.BlockSpec((1, D), const2),
        ],
        out_specs=pl.BlockSpec((tq, D), main),
        out_shape=jax.ShapeDtypeStruct((T, D), f32),
        scratch_shapes=[
            pltpu.VMEM((tq + 2 * POOL_HALO, POOL_WIDTH), f32),
            pltpu.VMEM((tq, POOL_WIDTH), f32),
            pltpu.VMEM((tq + 2 * BLOCK, KV_WIDTH), bf16),
            pltpu.VMEM((KV_WIDTH, tq + 2 * BLOCK), bf16),
            pltpu.VMEM((ATTN_WIDTH, tq), f32),
            pltpu.VMEM((tq, D), bf16),
        ],
        compiler_params=pltpu.CompilerParams(vmem_limit_bytes=VMEM_LIMIT),
        name="ev_mix",
    )(x2d, mod_l, p, p, p, qt, k, k, k, vt, vt, vt, bias, sink_row, w_pool, pool_scale, w_out, ln_g, ln_b)


def _load_rounded(w_hbm_ref, w_ref, stage_ref, sem):
    rows = stage_ref.shape[1]
    n = w_hbm_ref.shape[0] // rows

    def copy(c):
        return pltpu.make_async_copy(w_hbm_ref.at[pl.ds(c * rows, rows), :], stage_ref.at[c % 2], sem.at[c % 2])

    copy(0).start()
    for c in range(n):
        if c + 1 < n:
            copy(c + 1).start()
        copy(c).wait()
        w_ref[pl.ds(c * rows, rows), :] = stage_ref[c % 2].astype(bf16)


def _ffn_kernel(x_ref, mod_ref, wg_hbm_ref, wu_hbm_ref, wd_hbm_ref, lng_ref, lnb_ref, o_ref,
                wg_ref, wu_ref, wd_ref, stage_in_ref, stage_out_ref, sem):
    @pl.when(pl.program_id(0) == 0)
    def _():
        _load_rounded(wg_hbm_ref, wg_ref, stage_in_ref, sem)
        _load_rounded(wu_hbm_ref, wu_ref, stage_in_ref, sem)
        _load_rounded(wd_hbm_ref, wd_ref, stage_out_ref, sem)

    x = x_ref[...]
    h = (x * (1.0 + mod_ref[4:5, :]) + mod_ref[3:4, :]).astype(bf16)
    a = _silu(_dot(h, wg_ref[...])) * _dot(h, wu_ref[...])
    y = _dot(a.astype(bf16), wd_ref[...])
    xr = ALPHA * x + (1.0 + mod_ref[5:6, :]) * y
    o_ref[...] = _layer_norm(xr, lng_ref[...], lnb_ref[...])


def _ffn(x2d, mod_l, w_gate, w_up, w_down, ln_g, ln_b, S):
    T = x2d.shape[0]
    tm = 512
    dff = w_gate.shape[1]
    tpb = S // tm
    n_stage = 8
    return pl.pallas_call(
        _ffn_kernel,
        grid=(T // tm,),
        in_specs=[
            pl.BlockSpec((tm, D), lambda i: (i, 0)),
            pl.BlockSpec((None, 6, D), lambda i: (i // tpb, 0, 0)),
            pl.BlockSpec(memory_space=pl.ANY),
            pl.BlockSpec(memory_space=pl.ANY),
            pl.BlockSpec(memory_space=pl.ANY),
            pl.BlockSpec((1, D), lambda i: (0, 0)),
            pl.BlockSpec((1, D), lambda i: (0, 0)),
        ],
        out_specs=pl.BlockSpec((tm, D), lambda i: (i, 0)),
        out_shape=jax.ShapeDtypeStruct((T, D), f32),
        scratch_shapes=[
            pltpu.VMEM((D, dff), bf16),
            pltpu.VMEM((D, dff), bf16),
            pltpu.VMEM((dff, D), bf16),
            pltpu.VMEM((2, D // n_stage, dff), f32),
            pltpu.VMEM((2, dff // n_stage, D), f32),
            pltpu.SemaphoreType.DMA((2,)),
        ],
        compiler_params=pltpu.CompilerParams(vmem_limit_bytes=VMEM_LIMIT_BIG),
        name="ffn",
    )(x2d, mod_l, w_gate, w_up, w_down, ln_g, ln_b)


def _sg_kernel(x_ref, mod_ref, win_ref, sgg_ref, sgb_ref, ws_ref, bst_ref, wout_ref, lng_ref, lnb_ref,
               o_ref, gate_ref, *, tm):
    x = x_ref[...]
    h = (x * (1.0 + mod_ref[1:2, :]) + mod_ref[0:1, :]).astype(bf16)
    v = _layer_norm(_gelu_tanh(_dot(h, win_ref[:, D:])), sgg_ref[...], sgb_ref[...]).astype(bf16)
    u = _gelu_tanh(_dot(h, win_ref[:, :D]))
    for n in range(tm // CHUNK):
        rows = slice(n * CHUNK, (n + 1) * CHUNK)
        for g in range(SG_GROUPS):
            cols = slice(g * SG_CH, (g + 1) * SG_CH)
            sv = _dot(ws_ref[g], v[rows, cols]) + bst_ref[:, g:g + 1]
            gate_ref[rows, cols] = (u[rows, cols] * sv).astype(bf16)
    y = _dot(gate_ref[...], wout_ref[...])
    xr = ALPHA * x + (1.0 + mod_ref[2:3, :]) * y
    o_ref[...] = _layer_norm(xr, lng_ref[...], lnb_ref[...])


def _sg_mix(x2d, mod_l, w_in, sg_g, sg_b, w_s, b_s_t, w_out, ln_g, ln_b, S):
    T = x2d.shape[0]
    tm = 512
    tpb = S // tm
    c2 = lambda i: (0, 0)
    return pl.pallas_call(
        functools.partial(_sg_kernel, tm=tm),
        grid=(T // tm,),
        in_specs=[
            pl.BlockSpec((tm, D), lambda i: (i, 0)),
            pl.BlockSpec((None, 6, D), lambda i: (i // tpb, 0, 0)),
            pl.BlockSpec((D, 2 * D), c2),
            pl.BlockSpec((1, D), c2),
            pl.BlockSpec((1, D), c2),
            pl.BlockSpec((SG_GROUPS, CHUNK, CHUNK), lambda i: (0, 0, 0)),
            pl.BlockSpec((CHUNK, SG_GROUPS), c2),
            pl.BlockSpec((D, D), c2),
            pl.BlockSpec((1, D), c2),
            pl.BlockSpec((1, D), c2),
        ],
        out_specs=pl.BlockSpec((tm, D), lambda i: (i, 0)),
        out_shape=jax.ShapeDtypeStruct((T, D), f32),
        scratch_shapes=[pltpu.VMEM((tm, D), bf16)],
        compiler_params=pltpu.CompilerParams(vmem_limit_bytes=VMEM_LIMIT),
        name="sg_mix",
    )(x2d, mod_l, w_in, sg_g, sg_b, w_s, b_s_t, w_out, ln_g, ln_b)


def _route_kernel(x_ref, mod_ref, wrt_ref, hs_ref, route_ref, cnt_ref):
    W = ROUTE_W
    w_hi, w_lo = _split_bf16(wrt_ref[...])
    nt = (((1,), (1,)), ((), ()))
    eidx = lax.broadcasted_iota(jnp.int32, (N_EXPERTS, W), 0)
    sub = lax.broadcasted_iota(jnp.int32, (N_EXPERTS, 1), 0)
    tr = lax.broadcasted_iota(jnp.int32, (W, W), 0)
    tc = lax.broadcasted_iota(jnp.int32, (W, W), 1)
    upper = (tr < tc).astype(bf16)
    srow = lax.broadcasted_iota(jnp.int32, (CHUNK_SLOTS, W), 0)
    ridx = lax.broadcasted_iota(jnp.int32, (8, W), 0)

    def assign_slots(k):
        h = x_ref[k * W:(k + 1) * W, :] * (1.0 + mod_ref[4:5, :]) + mod_ref[3:4, :]
        h_hi, h_lo = _split_bf16(h)
        logits = (lax.dot_general(w_hi, h_hi, nt, preferred_element_type=f32)
                  + (lax.dot_general(w_hi, h_lo, nt, preferred_element_type=f32)
                     + lax.dot_general(w_lo, h_hi, nt, preferred_element_type=f32)))
        m1 = jnp.max(logits, axis=0, keepdims=True)
        i1 = jnp.min(jnp.where(logits == m1, eidx, N_EXPERTS), axis=0, keepdims=True)
        sel1 = eidx == i1
        rest = jnp.where(sel1, -jnp.inf, logits)
        m2 = jnp.max(rest, axis=0, keepdims=True)
        i2 = jnp.min(jnp.where(rest == m2, eidx, N_EXPERTS), axis=0, keepdims=True)
        sel2 = eidx == i2
        e2 = jnp.exp(m2 - m1)
        g1 = 1.0 / (1.0 + e2)
        g2 = e2 / (1.0 + e2)

        a1 = sel1.astype(f32)
        a2 = sel2.astype(f32)
        assign = a1 + a2
        counts = jnp.sum(assign, axis=1, keepdims=True)
        grans = jnp.ceil(counts * (1.0 / GRAN))
        seg = jnp.zeros((N_EXPERTS, 1), f32)
        for e in range(N_EXPERTS - 1):
            seg = seg + jnp.where(sub > e, grans[e:e + 1, :] * GRAN, 0.0)
        rank = _dot(assign.astype(bf16), upper)
        slot = seg + rank
        pos1 = jnp.sum(a1 * slot, axis=0, keepdims=True)
        pos2 = jnp.sum(a2 * slot, axis=0, keepdims=True)
        route_ref[k] = jnp.where(ridx == 0, pos1, jnp.where(ridx == 1, pos2,
                                 jnp.where(ridx == 2, g1, jnp.where(ridx == 3, g2, 0.0))))
        cnt_ref[k] = jnp.broadcast_to(counts, (N_EXPERTS, 128)).astype(jnp.int32)
        return h_hi, pos1, pos2

    def sort_rows(k, h_hi, pos1, pos2):
        perm = ((srow == pos1.astype(jnp.int32)) | (srow == pos2.astype(jnp.int32)))
        hs_ref[k] = _dot(perm.astype(f32).astype(bf16), h_hi).astype(bf16)

    routed = [assign_slots(k) for k in range(ROUTE_PER_STEP)]
    for k in range(ROUTE_PER_STEP):
        sort_rows(k, *routed[k])


def _route(x2d, mod_l, w_router_t, S):
    T = x2d.shape[0]
    W = ROUTE_W
    R = ROUTE_PER_STEP
    nc = T // W
    tpb = S // (R * W)
    return pl.pallas_call(
        _route_kernel,
        grid=(nc // R,),
        in_specs=[
            pl.BlockSpec((R * W, D), lambda c: (c, 0)),
            pl.BlockSpec((None, 6, D), lambda c: (c // tpb, 0, 0)),
            pl.BlockSpec((N_EXPERTS, D), lambda c: (0, 0)),
        ],
        out_specs=[
            pl.BlockSpec((R, CHUNK_SLOTS, D), lambda c: (c, 0, 0)),
            pl.BlockSpec((R, 8, W), lambda c: (c, 0, 0)),
            pl.BlockSpec((R, N_EXPERTS, 128), lambda c: (c, 0, 0)),
        ],
        out_shape=[
            jax.ShapeDtypeStruct((nc, CHUNK_SLOTS, D), bf16),
            jax.ShapeDtypeStruct((nc, 8, W), f32),
            jax.ShapeDtypeStruct((nc, N_EXPERTS, 128), jnp.int32),
        ],
        compiler_params=pltpu.CompilerParams(vmem_limit_bytes=VMEM_LIMIT),
        name="route",
    )(x2d, mod_l, w_router_t)


def _granule_copy(src_ref, buf_ref, sem, idx_ref, base, g):
    row = pl.multiple_of(idx_ref[base + g] * GRAN, GRAN)
    return pltpu.make_async_copy(src_ref.at[pl.ds(row, GRAN), :], buf_ref.at[pl.ds(g * GRAN, GRAN), :], sem)


def _gather_start(src_ref, buf_ref, sem, idx_ref, base, n):
    for g in range(n):
        _granule_copy(src_ref, buf_ref, sem, idx_ref, base, g).start()


def _gather_wait(src_ref, buf_ref, sem, idx_ref, base, n):
    for g in range(n):
        _granule_copy(src_ref, buf_ref, sem, idx_ref, base, g).wait()


def _expert_kernel(te_ref, tv_ref, src_ref, hs_ref, wg_ref, wu_ref, wd_ref, o_ref,
                   xbuf_ref, sem, acc_ref, wgb_ref, wub_ref, wdb_ref):
    i = pl.program_id(0)
    j = pl.program_id(1)
    n_tiles = pl.num_programs(0)
    last = pl.num_programs(1) - 1
    valid = tv_ref[i] > 0
    slot = i % 2
    nxt = jnp.minimum(i + 1, n_tiles - 1)

    @pl.when(j == 0)
    def _():
        @pl.when(i == 0)
        def _():
            _gather_start(hs_ref, xbuf_ref.at[0], sem.at[0], src_ref, 0, TILE_GRANS)

        @pl.when(valid)
        def _():
            _gather_wait(hs_ref, xbuf_ref.at[slot], sem.at[slot], src_ref, i * TILE_GRANS, TILE_GRANS)

        @pl.when((i + 1 < n_tiles) & (tv_ref[nxt] > 0))
        def _():
            _gather_start(hs_ref, xbuf_ref.at[1 - slot], sem.at[1 - slot], src_ref, nxt * TILE_GRANS, TILE_GRANS)

    n_sub = tv_ref[i]
    fast_subs = MOE_FAST // MOE_SUB
    fast = n_sub >= fast_subs

    def round_weights():
        wgb_ref[...] = wg_ref[...].astype(bf16)
        wub_ref[...] = wu_ref[...].astype(bf16)
        wdb_ref[...] = wd_ref[...].astype(bf16)

    def swiglu(x):
        a = _silu(_dot(x, wgb_ref[...])) * _dot(x, wub_ref[...])
        return _dot(a.astype(bf16), wdb_ref[...])

    @pl.when(valid & (j == 0))
    def _():
        acc_ref[...] = jnp.zeros_like(acc_ref)

    @pl.when(fast)
    def _():
        round_weights()
        acc_ref[0:MOE_FAST, :] += swiglu(xbuf_ref[slot, 0:MOE_FAST, :])

    @pl.when(valid & jnp.logical_not(fast))
    def _():
        round_weights()

    @pl.when(valid)
    def _():
        def body(sb, carry):
            rows = pl.ds(pl.multiple_of(sb * MOE_SUB, MOE_SUB), MOE_SUB)
            acc_ref[rows, :] += swiglu(xbuf_ref[slot, rows, :])
            return carry

        lax.fori_loop(jnp.where(fast, fast_subs, 0), n_sub, body, 0)

    @pl.when(j == last)
    def _():
        @pl.when(valid)
        def _():
            o_ref[...] = acc_ref[...].astype(o_ref.dtype)

        @pl.when(jnp.logical_not(valid))
        def _():
            o_ref[...] = jnp.zeros_like(o_ref)


def _experts(tile_expert, tile_valid, src_of_dst, hs2d, w_gate, w_up, w_down):
    n_tiles = tile_expert.shape[0]
    dff = w_gate.shape[2]
    nff = dff // MOE_TF

    def jj(j, tv, i):
        return jnp.where(tv[i] > 0, j, nff - 1)

    grid_spec = pltpu.PrefetchScalarGridSpec(
        num_scalar_prefetch=3,
        grid=(n_tiles, nff),
        in_specs=[
            pl.BlockSpec(memory_space=pl.ANY),
            pl.BlockSpec((None, D, MOE_TF), lambda i, j, te, tv, sd: (te[i], 0, jj(j, tv, i))),
            pl.BlockSpec((None, D, MOE_TF), lambda i, j, te, tv, sd: (te[i], 0, jj(j, tv, i))),
            pl.BlockSpec((None, MOE_TF, D), lambda i, j, te, tv, sd: (te[i], jj(j, tv, i), 0)),
        ],
        out_specs=pl.BlockSpec((MOE_TM, D), lambda i, j, te, tv, sd: (i, 0)),
        scratch_shapes=[
            pltpu.VMEM((2, MOE_TM, D), bf16),
            pltpu.SemaphoreType.DMA((2,)),
            pltpu.VMEM((MOE_TM, D), f32),
            pltpu.VMEM((D, MOE_TF), bf16),
            pltpu.VMEM((D, MOE_TF), bf16),
            pltpu.VMEM((MOE_TF, D), bf16),
        ],
    )
    return pl.pallas_call(
        _expert_kernel,
        grid_spec=grid_spec,
        out_shape=jax.ShapeDtypeStruct((n_tiles * MOE_TM, D), bf16),
        compiler_params=pltpu.CompilerParams(vmem_limit_bytes=VMEM_LIMIT_BIG),
        name="experts",
    )(tile_expert, tile_valid, src_of_dst, hs2d, w_gate, w_up, w_down)


def _combine_kernel(ds_ref, x_ref, mod_ref, o_hbm_ref, rt_ref, lng_ref, lnb_ref, out_ref, obuf_ref, sem):
    W = ROUTE_W
    c = pl.program_id(0)
    nc = pl.num_programs(0)
    slot = c % 2
    nxt = jnp.minimum(c + 1, nc - 1)

    @pl.when(c == 0)
    def _():
        _gather_start(o_hbm_ref, obuf_ref.at[0], sem.at[0], ds_ref, 0, CHUNK_GRANS)

    _gather_wait(o_hbm_ref, obuf_ref.at[slot], sem.at[slot], ds_ref, c * CHUNK_GRANS, CHUNK_GRANS)

    _gather_start(o_hbm_ref, obuf_ref.at[1 - slot], sem.at[1 - slot], ds_ref, nxt * CHUNK_GRANS, CHUNK_GRANS)

    osv = obuf_ref[slot]
    scol = lax.broadcasted_iota(jnp.int32, (W // 2, CHUNK_SLOTS), 1)
    for r in range(2):
        rr = slice(r * (W // 2), (r + 1) * (W // 2))
        rt = rt_ref[rr, :]
        p1 = (scol == rt[:, 0:1].astype(jnp.int32)).astype(f32).astype(bf16)
        p2 = (scol == rt[:, 1:2].astype(jnp.int32)).astype(f32).astype(bf16)
        y = rt[:, 2:3] * _dot(p1, osv) + rt[:, 3:4] * _dot(p2, osv)
        xr = ALPHA * x_ref[rr, :] + (1.0 + mod_ref[5:6, :]) * y
        out_ref[rr, :] = _layer_norm(xr, lng_ref[...], lnb_ref[...])

    @pl.when(c == nc - 1)
    def _():
        _gather_wait(o_hbm_ref, obuf_ref.at[1 - slot], sem.at[1 - slot], ds_ref, nxt * CHUNK_GRANS, CHUNK_GRANS)


def _combine(dst_of_src, x2d, mod_l, o2d, route_t, ln_g, ln_b, S):
    T = x2d.shape[0]
    W = ROUTE_W
    tpb = S // W
    grid_spec = pltpu.PrefetchScalarGridSpec(
        num_scalar_prefetch=1,
        grid=(T // W,),
        in_specs=[
            pl.BlockSpec((W, D), lambda c, ds: (c, 0)),
            pl.BlockSpec((None, 6, D), lambda c, ds: (c // tpb, 0, 0)),
            pl.BlockSpec(memory_space=pl.ANY),
            pl.BlockSpec((None, W, 8), lambda c, ds: (c, 0, 0)),
            pl.BlockSpec((1, D), lambda c, ds: (0, 0)),
            pl.BlockSpec((1, D), lambda c, ds: (0, 0)),
        ],
        out_specs=pl.BlockSpec((W, D), lambda c, ds: (c, 0)),
        scratch_shapes=[
            pltpu.VMEM((2, CHUNK_SLOTS, D), bf16),
            pltpu.SemaphoreType.DMA((2,)),
        ],
    )
    return pl.pallas_call(
        _combine_kernel,
        grid_spec=grid_spec,
        out_shape=jax.ShapeDtypeStruct((T, D), f32),
        compiler_params=pltpu.CompilerParams(vmem_limit_bytes=VMEM_LIMIT),
        name="combine",
    )(dst_of_src, x2d, mod_l, o2d, route_t, ln_g, ln_b)


def _routing_tables(counts, n_tiles):
    nc = counts.shape[0]
    gr = (counts + GRAN - 1) // GRAN
    seg_start = jnp.cumsum(gr, axis=1) - gr
    chunk_total = jnp.sum(gr, axis=1)
    prefix = jnp.cumsum(gr, axis=0) - gr
    g_e = jnp.sum(gr, axis=0)
    tiles_e = (g_e + TILE_GRANS - 1) // TILE_GRANS
    tile_end = jnp.cumsum(tiles_e)
    tile_start = tile_end - tiles_e
    total_tiles = tile_end[-1]

    i32 = jnp.int32
    er = jnp.arange(N_EXPERTS, dtype=i32)
    t = jnp.arange(n_tiles, dtype=i32)
    te = jnp.sum((t[:, None] >= tile_end[None, :]).astype(i32), axis=1)
    tile_valid = (t < total_tiles).astype(i32)
    last_e = jnp.sum((total_tiles - 1 >= tile_end).astype(i32))
    tile_expert = jnp.where(tile_valid > 0, jnp.minimum(te, N_EXPERTS - 1), last_e).astype(i32)
    oh_t = (tile_expert[:, None] == er).astype(i32)
    grans_left = jnp.sum(oh_t * (g_e - (t[:, None] - tile_start[None, :]) * TILE_GRANS), axis=1)
    tile_subs = tile_valid * jnp.clip((grans_left + SUB_GRANS - 1) // SUB_GRANS, 0, TILE_GRANS // SUB_GRANS)

    k = jnp.arange(CHUNK_GRANS, dtype=i32)
    seg_end = seg_start + gr
    e_of = jnp.sum((k[None, :, None] >= seg_end[:, None, :]).astype(i32), axis=2)
    oh_e = (jnp.minimum(e_of, N_EXPERTS - 1)[:, :, None] == er).astype(i32)
    base = tile_start[None, :] * TILE_GRANS + prefix - seg_start
    dst = jnp.sum(oh_e * base[:, None, :], axis=2) + k[None, :]
    valid_src = k[None, :] < chunk_total[:, None]
    dst_of_src = jnp.where(valid_src, dst, 0).astype(i32).reshape(-1)

    d = jnp.arange(n_tiles * TILE_GRANS, dtype=i32)
    oh_d = (jnp.repeat(tile_expert, TILE_GRANS)[:, None] == er).astype(i32)
    q = d - jnp.sum(oh_d * tile_start[None, :], axis=1) * TILE_GRANS
    incl_d = jnp.sum(oh_d[:, :, None] * (prefix + gr).T[None], axis=1)
    c_d = jnp.sum((q[:, None] >= incl_d).astype(i32), axis=1)
    oh_c = (jnp.minimum(c_d, nc - 1)[:, None] == jnp.arange(nc, dtype=i32)).astype(i32)
    cbase = jnp.arange(nc, dtype=i32)[:, None] * CHUNK_GRANS + seg_start - prefix
    sel = jnp.sum(oh_c[:, :, None] * oh_d[:, None, :] * cbase[None], axis=(1, 2))
    valid_dst = (jnp.repeat(tile_valid, TILE_GRANS) > 0) & (q >= 0) & (q < jnp.sum(oh_d * g_e[None, :], axis=1))
    src_of_dst = jnp.where(valid_dst, sel + q, 0).astype(i32)
    return tile_expert, tile_subs.astype(i32), src_of_dst, dst_of_src


def kernel(x, c, ada_w, ada_b, ln_g, ln_b, ev_w_in, ev_pool_w, ev_pool_scale, ev_sink, ev_w_out, od_w_in, od_sg_ln_g, od_sg_ln_b, od_w_s, od_b_s, od_w_out, ffn_w_gate, ffn_w_up, ffn_w_down, moe_w_router, moe_w_gate, moe_w_up, moe_w_down):
    B, S, _ = x.shape
    T = B * S
    x2d = x.reshape(T, D)
    mod = _adaln(c, ada_w, ada_b)

    w_in = ev_w_in[0].astype(bf16)
    q0, k0, v0 = POOL_WIDTH, POOL_WIDTH + ATTN_WIDTH, POOL_WIDTH + ATTN_WIDTH + KV_WIDTH
    w_pk = jnp.concatenate([w_in[:, :q0], w_in[:, k0:v0]], axis=1)
    w_qv_t = jnp.concatenate([w_in[:, q0:k0], w_in[:, v0:]], axis=1).T
    p, k, qt, vt = _ev_in(x2d, mod[0], w_pk, w_qv_t, S)
    x2d = _ev_mix(x2d, mod[0], p, qt, k, vt, ev_pool_w[0].astype(bf16), ev_pool_scale[0][None, :],
                  ev_sink[0], ev_w_out[0].astype(bf16), ln_g[0, 0][None, :], ln_b[0, 0][None, :], B, S)
    x2d = _ffn(x2d, mod[0], ffn_w_gate[0], ffn_w_up[0], ffn_w_down[0],
               ln_g[0, 1][None, :], ln_b[0, 1][None, :], S)

    x2d = _sg_mix(x2d, mod[1], od_w_in[0].astype(bf16), od_sg_ln_g[0][None, :], od_sg_ln_b[0][None, :],
                  od_w_s[0].astype(bf16), od_b_s[0].T, od_w_out[0].astype(bf16),
                  ln_g[1, 0][None, :], ln_b[1, 0][None, :], S)

    hs, route, cnt = _route(x2d, mod[1], moe_w_router[0].T, S)
    nc = T // ROUTE_W
    n_tiles = (nc * CHUNK_GRANS) // TILE_GRANS + N_EXPERTS
    tile_expert, tile_subs, src_of_dst, dst_of_src = _routing_tables(cnt[:, :, 0], n_tiles)
    o = _experts(tile_expert, tile_subs, src_of_dst, hs.reshape(nc * CHUNK_SLOTS, D),
                 moe_w_gate[0], moe_w_up[0], moe_w_down[0])
    x2d = _combine(dst_of_src, x2d, mod[1], o, jnp.swapaxes(route, 1, 2),
                   ln_g[1, 1][None, :], ln_b[1, 1][None, :], S)
    return x2d.reshape(B, S, D)
```

```python
import functools
import math

import jax
import jax.numpy as jnp
from jax import lax
from jax.experimental import pallas as pl
from jax.experimental.pallas import tpu as pltpu

D = 1024
DEPTH = 2
ALPHA = (2.0 * DEPTH) ** 0.25
LN_EPS = 1e-5

POOL_WINDOWS = (2, 4, 8, 16)
POOL_CH = 128
POOL_WIDTH = 512
HEAD_DIM = 64
N_Q_HEADS = 8
N_KV_HEADS = 2
Q_GROUP = 4
ATTN_WIDTH = 512
KV_WIDTH = 128
BLOCK = 128
EVEN_IN = 1280
POOL_HALO = 8

CHUNK = 128
SG_GROUPS = 8
SG_CH = 128

N_EXPERTS = 8

ROUTE_W = 512
ROUTE_PER_STEP = 2
GRAN = 16
CHUNK_SLOTS = 2 * ROUTE_W + N_EXPERTS * GRAN
CHUNK_GRANS = CHUNK_SLOTS // GRAN
MOE_SUB = 256
MOE_FAST = 2048
MOE_TM = MOE_FAST + MOE_SUB
TILE_GRANS = MOE_TM // GRAN
SUB_GRANS = MOE_SUB // GRAN
MOE_TF = 512

VMEM_LIMIT = 48 * 1024 * 1024
VMEM_LIMIT_BIG = 56 * 1024 * 1024

bf16 = jnp.bfloat16
f32 = jnp.float32


def _dot(a, b):
    return jnp.dot(a, b, preferred_element_type=f32)


def _split_bf16(a):
    hi = a.astype(bf16)
    lo = (a - hi.astype(f32)).astype(bf16)
    return hi, lo


def _layer_norm(x, g, b):
    mu = jnp.mean(x, axis=-1, keepdims=True)
    xc = x - mu
    var = jnp.mean(xc * xc, axis=-1, keepdims=True)
    return xc * lax.rsqrt(var + LN_EPS) * g + b


def _silu(x):
    return x * jax.nn.sigmoid(x)


def _gelu_tanh(x):
    c = math.sqrt(2.0 / math.pi)
    return x * (0.5 * (1.0 + jnp.tanh(c * (x + 0.044715 * (x * x * x)))))


def _adaln_kernel(c_ref, w_ref, b_ref, o_ref):
    cond = _silu(c_ref[...])
    c_hi, c_lo = _split_bf16(cond)
    w_hi, w_lo = _split_bf16(w_ref[...])
    acc = _dot(c_hi, w_hi) + (_dot(c_lo, w_hi) + _dot(c_hi, w_lo))
    o_ref[...] = acc + b_ref[...]


def _adaln(c, ada_w, ada_b):
    B = c.shape[0]
    tn = 1024
    c_pad = jnp.zeros((8, D), f32).at[:B].set(c)
    out = pl.pallas_call(
        _adaln_kernel,
        grid=(DEPTH, 6 * D // tn),
        in_specs=[
            pl.BlockSpec((8, D), lambda l, j: (0, 0)),
            pl.BlockSpec((None, D, tn), lambda l, j: (l, 0, j)),
            pl.BlockSpec((None, 1, tn), lambda l, j: (l, 0, j)),
        ],
        out_specs=pl.BlockSpec((None, 8, tn), lambda l, j: (l, 0, j)),
        out_shape=jax.ShapeDtypeStruct((DEPTH, 8, 6 * D), f32),
        compiler_params=pltpu.CompilerParams(vmem_limit_bytes=VMEM_LIMIT),
        name="adaln",
    )(c_pad, ada_w, ada_b.reshape(DEPTH, 1, 6 * D))
    return out[:, :B].reshape(DEPTH, B, 6, D)


def _ev_in_kernel(x_ref, mod_ref, wpk_ref, wqvt_ref, p_ref, k_ref, qt_ref, vt_ref):
    h = (x_ref[...] * (1.0 + mod_ref[1:2, :]) + mod_ref[0:1, :]).astype(bf16)
    zpk = _dot(h, wpk_ref[...])
    p_ref[...] = zpk[:, :POOL_WIDTH]
    k_ref[...] = zpk[:, POOL_WIDTH:].astype(bf16)
    zt = lax.dot_general(wqvt_ref[...], h, (((1,), (1,)), ((), ())), preferred_element_type=f32)
    qt_ref[...] = (zt[:ATTN_WIDTH] * (HEAD_DIM ** -0.5)).astype(bf16)
    vt_ref[...] = zt[ATTN_WIDTH:].astype(bf16)


def _ev_in(x2d, mod_l, w_pk, w_qv_t, S):
    T = x2d.shape[0]
    tm = 512
    tpb = S // tm
    return pl.pallas_call(
        _ev_in_kernel,
        grid=(T // tm,),
        in_specs=[
            pl.BlockSpec((tm, D), lambda i: (i, 0)),
            pl.BlockSpec((None, 6, D), lambda i: (i // tpb, 0, 0)),
            pl.BlockSpec((D, POOL_WIDTH + KV_WIDTH), lambda i: (0, 0)),
            pl.BlockSpec((ATTN_WIDTH + KV_WIDTH, D), lambda i: (0, 0)),
        ],
        out_specs=[
            pl.BlockSpec((tm, POOL_WIDTH), lambda i: (i, 0)),
            pl.BlockSpec((tm, KV_WIDTH), lambda i: (i, 0)),
            pl.BlockSpec((ATTN_WIDTH, tm), lambda i: (0, i)),
            pl.BlockSpec((KV_WIDTH, tm), lambda i: (0, i)),
        ],
        out_shape=[
            jax.ShapeDtypeStruct((T, POOL_WIDTH), f32),
            jax.ShapeDtypeStruct((T, KV_WIDTH), bf16),
            jax.ShapeDtypeStruct((ATTN_WIDTH, T), bf16),
            jax.ShapeDtypeStruct((KV_WIDTH, T), bf16),
        ],
        compiler_params=pltpu.CompilerParams(vmem_limit_bytes=VMEM_LIMIT),
        name="ev_in",
    )(x2d, mod_l, w_pk, w_qv_t)


def _ev_mix_kernel(x_ref, mod_ref, p_ref, pp_ref, pn_ref, qt_ref,
                   k_ref, kp_ref, kn_ref, vt_ref, vtp_ref, vtn_ref,
                   bias_ref, sink_ref, wpool_ref, pscale_ref, wout_ref, lng_ref, lnb_ref,
                   o_ref, pext_ref, pooled_ref, kext_ref, vext_ref, ybt_ref, mix_ref, *, S, tq):
    i = pl.program_id(1)
    n_tiles = S // tq
    is_first = i == 0
    is_last = i == n_tiles - 1
    H = POOL_HALO

    p = p_ref[...]
    pext_ref[0:H, :] = jnp.where(is_first, 0.0, pp_ref[...])
    pext_ref[H:H + tq, :] = p
    pext_ref[H + tq:, :] = jnp.where(is_last, 0.0, pn_ref[...])
    near = lax.broadcasted_iota(jnp.int32, (H, 1), 0)

    def pool_group(g):
        w = POOL_WINDOWS[g]
        cs = slice(g * POOL_CH, (g + 1) * POOL_CH)
        r = w // 2
        wsum = pext_ref[H - r:H - r + tq, cs]
        for d in range(-r + 1, r + 1):
            wsum = wsum + pext_ref[H + d:H + d + tq, cs]
        pooled_ref[:, cs] = wsum / float(w + 1) - p[:, cs]
        cnt_head = (jnp.minimum(near, r) + (r + 1)).astype(f32)
        cnt_tail = (jnp.minimum(H - 1 - near, r) + (r + 1)).astype(f32)
        cnt_head = jnp.where(is_first, cnt_head, float(w + 1))
        cnt_tail = jnp.where(is_last, cnt_tail, float(w + 1))
        pooled_ref[0:H, cs] = wsum[0:H] / cnt_head - p[0:H, cs]
        pooled_ref[tq - H:tq, cs] = wsum[tq - H:tq] / cnt_tail - p[tq - H:tq, cs]
        ya = _dot(pooled_ref[:, cs].astype(bf16), wpool_ref[g])
        mix_ref[:, cs] = (ya * pscale_ref[:, cs]).astype(bf16)

    for g in range(len(POOL_WINDOWS)):
        pool_group(g)

    kext_ref[0:BLOCK, :] = kp_ref[...]
    kext_ref[BLOCK:BLOCK + tq, :] = k_ref[...]
    kext_ref[BLOCK + tq:, :] = kn_ref[...]
    vext_ref[:, 0:BLOCK] = vtp_ref[...]
    vext_ref[:, BLOCK:BLOCK + tq] = vt_ref[...]
    vext_ref[:, BLOCK + tq:] = vtn_ref[...]

    n_blocks = S // BLOCK
    zeros_q = jnp.zeros((HEAD_DIM, Q_GROUP * BLOCK), bf16)
    items = [(n, kvh) for n in range(tq // BLOCK) for kvh in range(N_KV_HEADS)]

    def scores(n, kvh):
        gb = i * (tq // BLOCK) + n
        variant = jnp.where(gb == 0, 1, jnp.where(gb == n_blocks - 1, 2, 0))
        cols = slice(n * BLOCK, (n + 1) * BLOCK)
        kw = kext_ref[n * BLOCK:n * BLOCK + 3 * BLOCK, :]
        qst = jnp.concatenate(
            [qt_ref[(kvh * Q_GROUP + gq) * HEAD_DIM:(kvh * Q_GROUP + gq + 1) * HEAD_DIM, cols]
             for gq in range(Q_GROUP)], axis=1)
        qst = jnp.concatenate([qst, zeros_q] if kvh == 0 else [zeros_q, qst], axis=0)
        return _dot(kw, qst) + bias_ref[variant, kvh]

    def softmax(n, kvh, s):
        sink = sink_ref[kvh]
        m = jnp.maximum(jnp.max(s, axis=0, keepdims=True), sink)
        e = jnp.exp(s - m)
        denom = jnp.sum(e, axis=0, keepdims=True) + jnp.exp(sink - m)
        return e.astype(bf16), denom

    def weighted_values(n, kvh, e, denom):
        cols = slice(n * BLOCK, (n + 1) * BLOCK)
        vwt = vext_ref[kvh * HEAD_DIM:(kvh + 1) * HEAD_DIM, n * BLOCK:n * BLOCK + 3 * BLOCK]
        out = _dot(vwt, e) / denom
        for gq in range(Q_GROUP):
            hq = kvh * Q_GROUP + gq
            ybt_ref[hq * HEAD_DIM:(hq + 1) * HEAD_DIM, cols] = out[:, gq * BLOCK:(gq + 1) * BLOCK]

    s_next = scores(*items[0])
    prev = None
    for idx, item in enumerate(items):
        s_cur = s_next
        if idx + 1 < len(items):
            s_next = scores(*items[idx + 1])
        cur = softmax(*item, s_cur)
        if prev is not None:
            weighted_values(*items[idx - 1], *prev)
        prev = cur
    weighted_values(*items[-1], *prev)
    mix_ref[:, POOL_WIDTH:] = ybt_ref[...].T.astype(bf16)

    halves = [slice(r * (tq // 2), (r + 1) * (tq // 2)) for r in range(2)]
    ys = [_dot(mix_ref[rr, :], wout_ref[...]) for rr in halves]
    for rr, y in zip(halves, ys):
        xr = ALPHA * x_ref[rr, :] + (1.0 + mod_ref[2:3, :]) * y
        o_ref[rr, :] = _layer_norm(xr, lng_ref[...], lnb_ref[...])


def _ev_mix(x2d, mod_l, p, qt, k, vt, w_pool, pool_scale, sink, w_out, ln_g, ln_b, B, S):
    T = x2d.shape[0]
    tq = 512
    nt = S // tq
    kb = tq // BLOCK
    pb = tq // POOL_HALO
    n_kblocks = T // BLOCK
    n_pblocks = T // POOL_HALO

    def main(b, i): return (b * nt + i, 0)
    def kprev(b, i): return (jnp.maximum((b * nt + i) * kb - 1, 0), 0)
    def knext(b, i): return (jnp.minimum((b * nt + i + 1) * kb, n_kblocks - 1), 0)
    def pprev(b, i): return (jnp.maximum((b * nt + i) * pb - 1, 0), 0)
    def pnext(b, i): return (jnp.minimum((b * nt + i + 1) * pb, n_pblocks - 1), 0)
    def const2(b, i): return (0, 0)

    def tmain(b, i): return (0, b * nt + i)
    def tprev(b, i): return (0, jnp.maximum((b * nt + i) * kb - 1, 0))
    def tnext(b, i): return (0, jnp.minimum((b * nt + i + 1) * kb, n_kblocks - 1))

    assert S // BLOCK >= 2
    kj = jnp.arange(3 * BLOCK)[:, None]
    qi = jnp.arange(BLOCK)[None, :]
    dist = jnp.abs(kj - BLOCK - qi)
    slopes = 2.0 ** (-8.0 * jnp.arange(1, N_Q_HEADS + 1, dtype=f32) / N_Q_HEADS)
    alibi = -slopes[:, None, None] * dist.astype(f32)[None]
    in_window = dist <= BLOCK
    key_ok = jnp.stack([kj >= 0, kj >= BLOCK, kj < 2 * BLOCK])
    bias = jnp.where((in_window[None] & key_ok)[:, None], alibi[None], -1e30)
    bias = bias.reshape(3, N_KV_HEADS, Q_GROUP, 3 * BLOCK, BLOCK).transpose(0, 1, 3, 2, 4)
    bias = bias.reshape(3, N_KV_HEADS, 3 * BLOCK, Q_GROUP * BLOCK)
    sink_row = jnp.repeat(sink.astype(f32).reshape(N_KV_HEADS, Q_GROUP), BLOCK, axis=1)[:, None, :]

    assert tq >= 2 * POOL_HALO and max(POOL_WINDOWS) // 2 <= POOL_HALO and nt >= 2

    kernel = functools.partial(_ev_mix_kernel, S=S, tq=tq)
    return pl.pallas_call(
        kernel,
        grid=(B, nt),
        in_specs=[
            pl.BlockSpec((tq, D), main),
            pl.BlockSpec((None, 6, D), lambda b, i: (b, 0, 0)),
            pl.BlockSpec((tq, POOL_WIDTH), main),
            pl.BlockSpec((POOL_HALO, POOL_WIDTH), pprev),
            pl.BlockSpec((POOL_HALO, POOL_WIDTH), pnext),
            pl.BlockSpec((ATTN_WIDTH, tq), tmain),
            pl.BlockSpec((tq, KV_WIDTH), main),
            pl.BlockSpec((BLOCK, KV_WIDTH), kprev),
            pl.BlockSpec((BLOCK, KV_WIDTH), knext),
            pl.BlockSpec((KV_WIDTH, tq), tmain),
            pl.BlockSpec((KV_WIDTH, BLOCK), tprev),
            pl.BlockSpec((KV_WIDTH, BLOCK), tnext),
            pl.BlockSpec((3, N_KV_HEADS, 3 * BLOCK, Q_GROUP * BLOCK), lambda b, i: (0, 0, 0, 0)),
            pl.BlockSpec((N_KV_HEADS, 1, Q_GROUP * BLOCK), lambda b, i: (0, 0, 0)),
            pl.BlockSpec((len(POOL_WINDOWS), POOL_CH, POOL_CH), lambda b, i: (0, 0, 0)),
            pl.BlockSpec((1, POOL_WIDTH), const2),
            pl.BlockSpec((D, D), const2),
            pl.BlockSpec((1, D), const2),
            pl.BlockSpec((1, D), const2),
        ],
        out_specs=pl.BlockSpec((tq, D), main),
        out_shape=jax.ShapeDtypeStruct((T, D), f32),
        scratch_shapes=[
            pltpu.VMEM((tq + 2 * POOL_HALO, POOL_WIDTH), f32),
            pltpu.VMEM((tq, POOL_WIDTH), f32),
            pltpu.VMEM((tq + 2 * BLOCK, KV_WIDTH), bf16),
            pltpu.VMEM((KV_WIDTH, tq + 2 * BLOCK), bf16),
            pltpu.VMEM((ATTN_WIDTH, tq), f32),
            pltpu.VMEM((tq, D), bf16),
        ],
        compiler_params=pltpu.CompilerParams(vmem_limit_bytes=VMEM_LIMIT),
        name="ev_mix",
    )(x2d, mod_l, p, p, p, qt, k, k, k, vt, vt, vt, bias, sink_row, w_pool, pool_scale, w_out, ln_g, ln_b)


def _load_rounded(w_hbm_ref, w_ref, stage_ref, sem):
    rows = stage_ref.shape[1]
    n = w_hbm_ref.shape[0] // rows

    def copy(c):
        return pltpu.make_async_copy(w_hbm_ref.at[pl.ds(c * rows, rows), :], stage_ref.at[c % 2], sem.at[c % 2])

    copy(0).start()
    for c in range(n):
        if c + 1 < n:
            copy(c + 1).start()
        copy(c).wait()
        w_ref[pl.ds(c * rows, rows), :] = stage_ref[c % 2].astype(bf16)


def _ffn_kernel(x_ref, mod_ref, wg_hbm_ref, wu_hbm_ref, wd_hbm_ref, lng_ref, lnb_ref, o_ref,
                wg_ref, wu_ref, wd_ref, stage_in_ref, stage_out_ref, sem):
    @pl.when(pl.program_id(0) == 0)
    def _():
        _load_rounded(wg_hbm_ref, wg_ref, stage_in_ref, sem)
        _load_rounded(wu_hbm_ref, wu_ref, stage_in_ref, sem)
        _load_rounded(wd_hbm_ref, wd_ref, stage_out_ref, sem)

    n_pieces = 2
    piece = x_ref.shape[0] // n_pieces
    parts = [pl.ds(q * piece, piece) for q in range(n_pieces)]

    def gate_up(rows):
        x = x_ref[rows, :]
        h = (x * (1.0 + mod_ref[4:5, :]) + mod_ref[3:4, :]).astype(bf16)
        return x, _dot(h, wg_ref[...]), _dot(h, wu_ref[...])

    nxt = gate_up(parts[0])
    for q, rows in enumerate(parts):
        x, g, u = nxt
        if q + 1 < n_pieces:
            nxt = gate_up(parts[q + 1])
        y = _dot((_silu(g) * u).astype(bf16), wd_ref[...])
        xr = ALPHA * x + (1.0 + mod_ref[5:6, :]) * y
        o_ref[rows, :] = _layer_norm(xr, lng_ref[...], lnb_ref[...])


def _ffn(x2d, mod_l, w_gate, w_up, w_down, ln_g, ln_b, S):
    T = x2d.shape[0]
    tm = 512
    dff = w_gate.shape[1]
    tpb = S // tm
    n_stage = 8
    return pl.pallas_call(
        _ffn_kernel,
        grid=(T // tm,),
        in_specs=[
            pl.BlockSpec((tm, D), lambda i: (i, 0)),
            pl.BlockSpec((None, 6, D), lambda i: (i // tpb, 0, 0)),
            pl.BlockSpec(memory_space=pl.ANY),
            pl.BlockSpec(memory_space=pl.ANY),
            pl.BlockSpec(memory_space=pl.ANY),
            pl.BlockSpec((1, D), lambda i: (0, 0)),
            pl.BlockSpec((1, D), lambda i: (0, 0)),
        ],
        out_specs=pl.BlockSpec((tm, D), lambda i: (i, 0)),
        out_shape=jax.ShapeDtypeStruct((T, D), f32),
        scratch_shapes=[
            pltpu.VMEM((D, dff), bf16),
            pltpu.VMEM((D, dff), bf16),
            pltpu.VMEM((dff, D), bf16),
            pltpu.VMEM((2, D // n_stage, dff), f32),
            pltpu.VMEM((2, dff // n_stage, D), f32),
            pltpu.SemaphoreType.DMA((2,)),
        ],
        compiler_params=pltpu.CompilerParams(vmem_limit_bytes=VMEM_LIMIT_BIG),
        name="ffn",
    )(x2d, mod_l, w_gate, w_up, w_down, ln_g, ln_b)


def _sg_kernel(x_ref, mod_ref, win_ref, sgg_ref, sgb_ref, ws_ref, bst_ref, wout_ref, lng_ref, lnb_ref,
               o_ref, gate_ref, *, tm):
    n_pieces = 2
    piece = tm // n_pieces

    def project(q):
        x = x_ref[q * piece:(q + 1) * piece, :]
        h = (x * (1.0 + mod_ref[1:2, :]) + mod_ref[0:1, :]).astype(bf16)
        return x, _dot(h, win_ref[:, D:]), _dot(h, win_ref[:, :D])

    nxt = project(0)
    for q in range(n_pieces):
        x, zv, zu = nxt
        if q + 1 < n_pieces:
            nxt = project(q + 1)
        v = _layer_norm(_gelu_tanh(zv), sgg_ref[...], sgb_ref[...]).astype(bf16)
        u = _gelu_tanh(zu)
        for n in range(piece // CHUNK):
            rows = slice(n * CHUNK, (n + 1) * CHUNK)
            grows = slice(q * piece + n * CHUNK, q * piece + (n + 1) * CHUNK)
            for g in range(SG_GROUPS):
                cols = slice(g * SG_CH, (g + 1) * SG_CH)
                sv = _dot(ws_ref[g], v[rows, cols]) + bst_ref[:, g:g + 1]
                gate_ref[grows, cols] = (u[rows, cols] * sv).astype(bf16)
        y = _dot(gate_ref[q * piece:(q + 1) * piece, :], wout_ref[...])
        xr = ALPHA * x + (1.0 + mod_ref[2:3, :]) * y
        o_ref[q * piece:(q + 1) * piece, :] = _layer_norm(xr, lng_ref[...], lnb_ref[...])


def _sg_mix(x2d, mod_l, w_in, sg_g, sg_b, w_s, b_s_t, w_out, ln_g, ln_b, S):
    T = x2d.shape[0]
    tm = 512
    tpb = S // tm
    c2 = lambda i: (0, 0)
    return pl.pallas_call(
        functools.partial(_sg_kernel, tm=tm),
        grid=(T // tm,),
        in_specs=[
            pl.BlockSpec((tm, D), lambda i: (i, 0)),
            pl.BlockSpec((None, 6, D), lambda i: (i // tpb, 0, 0)),
            pl.BlockSpec((D, 2 * D), c2),
            pl.BlockSpec((1, D), c2),
            pl.BlockSpec((1, D), c2),
            pl.BlockSpec((SG_GROUPS, CHUNK, CHUNK), lambda i: (0, 0, 0)),
            pl.BlockSpec((CHUNK, SG_GROUPS), c2),
            pl.BlockSpec((D, D), c2),
            pl.BlockSpec((1, D), c2),
            pl.BlockSpec((1, D), c2),
        ],
        out_specs=pl.BlockSpec((tm, D), lambda i: (i, 0)),
        out_shape=jax.ShapeDtypeStruct((T, D), f32),
        scratch_shapes=[pltpu.VMEM((tm, D), bf16)],
        compiler_params=pltpu.CompilerParams(vmem_limit_bytes=VMEM_LIMIT),
        name="sg_mix",
    )(x2d, mod_l, w_in, sg_g, sg_b, w_s, b_s_t, w_out, ln_g, ln_b)


def _route_kernel(x_ref, mod_ref, wrt_ref, hs_ref, route_ref, cnt_ref):
    W = ROUTE_W
    w_hi, w_lo = _split_bf16(wrt_ref[...])
    nt = (((1,), (1,)), ((), ()))
    eidx = lax.broadcasted_iota(jnp.int32, (N_EXPERTS, W), 0)
    sub = lax.broadcasted_iota(jnp.int32, (N_EXPERTS, 1), 0)
    tr = lax.broadcasted_iota(jnp.int32, (W, W), 0)
    tc = lax.broadcasted_iota(jnp.int32, (W, W), 1)
    upper = (tr < tc).astype(bf16)
    srow = lax.broadcasted_iota(jnp.int32, (CHUNK_SLOTS, W), 0)
    ridx = lax.broadcasted_iota(jnp.int32, (8, W), 0)

    def assign_slots(k):
        h = x_ref[k * W:(k + 1) * W, :] * (1.0 + mod_ref[4:5, :]) + mod_ref[3:4, :]
        h_hi, h_lo = _split_bf16(h)
        logits = (lax.dot_general(w_hi, h_hi, nt, preferred_element_type=f32)
                  + (lax.dot_general(w_hi, h_lo, nt, preferred_element_type=f32)
                     + lax.dot_general(w_lo, h_hi, nt, preferred_element_type=f32)))
        m1 = jnp.max(logits, axis=0, keepdims=True)
        i1 = jnp.min(jnp.where(logits == m1, eidx, N_EXPERTS), axis=0, keepdims=True)
        sel1 = eidx == i1
        rest = jnp.where(sel1, -jnp.inf, logits)
        m2 = jnp.max(rest, axis=0, keepdims=True)
        i2 = jnp.min(jnp.where(rest == m2, eidx, N_EXPERTS), axis=0, keepdims=True)
        sel2 = eidx == i2
        e2 = jnp.exp(m2 - m1)
        g1 = 1.0 / (1.0 + e2)
        g2 = e2 / (1.0 + e2)

        a1 = sel1.astype(f32)
        a2 = sel2.astype(f32)
        assign = a1 + a2
        counts = jnp.sum(assign, axis=1, keepdims=True)
        grans = jnp.ceil(counts * (1.0 / GRAN))
        seg = jnp.zeros((N_EXPERTS, 1), f32)
        for e in range(N_EXPERTS - 1):
            seg = seg + jnp.where(sub > e, grans[e:e + 1, :] * GRAN, 0.0)
        rank = _dot(assign.astype(bf16), upper)
        slot = seg + rank
        pos1 = jnp.sum(a1 * slot, axis=0, keepdims=True)
        pos2 = jnp.sum(a2 * slot, axis=0, keepdims=True)
        route_ref[k] = jnp.where(ridx == 0, pos1, jnp.where(ridx == 1, pos2,
                                 jnp.where(ridx == 2, g1, jnp.where(ridx == 3, g2, 0.0))))
        cnt_ref[k] = jnp.broadcast_to(counts, (N_EXPERTS, 128)).astype(jnp.int32)
        return h_hi, pos1, pos2

    def sort_rows(k, h_hi, pos1, pos2):
        perm = ((srow == pos1.astype(jnp.int32)) | (srow == pos2.astype(jnp.int32)))
        hs_ref[k] = _dot(perm.astype(f32).astype(bf16), h_hi).astype(bf16)

    routed = [assign_slots(k) for k in range(ROUTE_PER_STEP)]
    for k in range(ROUTE_PER_STEP):
        sort_rows(k, *routed[k])


def _route(x2d, mod_l, w_router_t, S):
    T = x2d.shape[0]
    W = ROUTE_W
    R = ROUTE_PER_STEP
    nc = T // W
    tpb = S // (R * W)
    return pl.pallas_call(
        _route_kernel,
        grid=(nc // R,),
        in_specs=[
            pl.BlockSpec((R * W, D), lambda c: (c, 0)),
            pl.BlockSpec((None, 6, D), lambda c: (c // tpb, 0, 0)),
            pl.BlockSpec((N_EXPERTS, D), lambda c: (0, 0)),
        ],
        out_specs=[
            pl.BlockSpec((R, CHUNK_SLOTS, D), lambda c: (c, 0, 0)),
            pl.BlockSpec((R, 8, W), lambda c: (c, 0, 0)),
            pl.BlockSpec((R, N_EXPERTS, 128), lambda c: (c, 0, 0)),
        ],
        out_shape=[
            jax.ShapeDtypeStruct((nc, CHUNK_SLOTS, D), bf16),
            jax.ShapeDtypeStruct((nc, 8, W), f32),
            jax.ShapeDtypeStruct((nc, N_EXPERTS, 128), jnp.int32),
        ],
        compiler_params=pltpu.CompilerParams(vmem_limit_bytes=VMEM_LIMIT),
        name="route",
    )(x2d, mod_l, w_router_t)


def _granule_copy(src_ref, buf_ref, sem, idx_ref, base, g):
    row = pl.multiple_of(idx_ref[base + g] * GRAN, GRAN)
    return pltpu.make_async_copy(src_ref.at[pl.ds(row, GRAN), :], buf_ref.at[pl.ds(g * GRAN, GRAN), :], sem)


def _gather_start(src_ref, buf_ref, sem, idx_ref, base, n):
    for g in range(n):
        _granule_copy(src_ref, buf_ref, sem, idx_ref, base, g).start()


def _gather_wait(src_ref, buf_ref, sem, idx_ref, base, n):
    for g in range(n):
        _granule_copy(src_ref, buf_ref, sem, idx_ref, base, g).wait()


def _expert_kernel(te_ref, tv_ref, src_ref, hs_ref, wg_ref, wu_ref, wd_ref, o_ref,
                   xbuf_ref, sem, acc_ref, wgb_ref, wub_ref, wdb_ref):
    i = pl.program_id(0)
    j = pl.program_id(1)
    n_tiles = pl.num_programs(0)
    last = pl.num_programs(1) - 1
    valid = tv_ref[i] > 0
    slot = i % 2
    nxt = jnp.minimum(i + 1, n_tiles - 1)

    @pl.when(j == 0)
    def _():
        @pl.when(i == 0)
        def _():
            _gather_start(hs_ref, xbuf_ref.at[0], sem.at[0], src_ref, 0, TILE_GRANS)

        @pl.when(valid)
        def _():
            _gather_wait(hs_ref, xbuf_ref.at[slot], sem.at[slot], src_ref, i * TILE_GRANS, TILE_GRANS)

        @pl.when((i + 1 < n_tiles) & (tv_ref[nxt] > 0))
        def _():
            _gather_start(hs_ref, xbuf_ref.at[1 - slot], sem.at[1 - slot], src_ref, nxt * TILE_GRANS, TILE_GRANS)

    n_sub = tv_ref[i]
    fast_subs = MOE_FAST // MOE_SUB
    fast = n_sub >= fast_subs

    def round_weights():
        wgb_ref[...] = wg_ref[...].astype(bf16)
        wub_ref[...] = wu_ref[...].astype(bf16)
        wdb_ref[...] = wd_ref[...].astype(bf16)

    def swiglu(x):
        a = _silu(_dot(x, wgb_ref[...])) * _dot(x, wub_ref[...])
        return _dot(a.astype(bf16), wdb_ref[...])

    @pl.when(valid & (j == 0))
    def _():
        acc_ref[...] = jnp.zeros_like(acc_ref)

    @pl.when(fast)
    def _():
        round_weights()
        quarter = MOE_FAST // 4
        parts = [pl.ds(q * quarter, quarter) for q in range(4)]

        def gate_up(rows):
            x = xbuf_ref[slot, rows, :]
            return _dot(x, wgb_ref[...]), _dot(x, wub_ref[...])

        nxt_gu = gate_up(parts[0])
        for q, rows in enumerate(parts):
            g, u = nxt_gu
            if q + 1 < len(parts):
                nxt_gu = gate_up(parts[q + 1])
            acc_ref[rows, :] += _dot((_silu(g) * u).astype(bf16), wdb_ref[...])

    @pl.when(valid & jnp.logical_not(fast))
    def _():
        round_weights()

    @pl.when(valid)
    def _():
        def body(sb, carry):
            rows = pl.ds(pl.multiple_of(sb * MOE_SUB, MOE_SUB), MOE_SUB)
            acc_ref[rows, :] += swiglu(xbuf_ref[slot, rows, :])
            return carry

        lax.fori_loop(jnp.where(fast, fast_subs, 0), n_sub, body, 0)

    @pl.when(j == last)
    def _():
        @pl.when(valid)
        def _():
            o_ref[...] = acc_ref[...].astype(o_ref.dtype)

        @pl.when(jnp.logical_not(valid))
        def _():
            o_ref[...] = jnp.zeros_like(o_ref)


def _experts(tile_expert, tile_valid, src_of_dst, hs2d, w_gate, w_up, w_down):
    n_tiles = tile_expert.shape[0]
    dff = w_gate.shape[2]
    nff = dff // MOE_TF

    def jj(j, tv, i):
        return jnp.where(tv[i] > 0, j, nff - 1)

    grid_spec = pltpu.PrefetchScalarGridSpec(
        num_scalar_prefetch=3,
        grid=(n_tiles, nff),
        in_specs=[
            pl.BlockSpec(memory_space=pl.ANY),
            pl.BlockSpec((None, D, MOE_TF), lambda i, j, te, tv, sd: (te[i], 0, jj(j, tv, i))),
            pl.BlockSpec((None, D, MOE_TF), lambda i, j, te, tv, sd: (te[i], 0, jj(j, tv, i))),
            pl.BlockSpec((None, MOE_TF, D), lambda i, j, te, tv, sd: (te[i], jj(j, tv, i), 0)),
        ],
        out_specs=pl.BlockSpec((MOE_TM, D), lambda i, j, te, tv, sd: (i, 0)),
        scratch_shapes=[
            pltpu.VMEM((2, MOE_TM, D), bf16),
            pltpu.SemaphoreType.DMA((2,)),
            pltpu.VMEM((MOE_TM, D), f32),
            pltpu.VMEM((D, MOE_TF), bf16),
            pltpu.VMEM((D, MOE_TF), bf16),
            pltpu.VMEM((MOE_TF, D), bf16),
        ],
    )
    return pl.pallas_call(
        _expert_kernel,
        grid_spec=grid_spec,
        out_shape=jax.ShapeDtypeStruct((n_tiles * MOE_TM, D), bf16),
        compiler_params=pltpu.CompilerParams(vmem_limit_bytes=VMEM_LIMIT_BIG),
        name="experts",
    )(tile_expert, tile_valid, src_of_dst, hs2d, w_gate, w_up, w_down)


def _combine_kernel(ds_ref, x_ref, mod_ref, o_hbm_ref, rt_ref, lng_ref, lnb_ref, out_ref, obuf_ref, sem):
    W = ROUTE_W
    c = pl.program_id(0)
    nc = pl.num_programs(0)
    slot = c % 2
    nxt = jnp.minimum(c + 1, nc - 1)

    @pl.when(c == 0)
    def _():
        _gather_start(o_hbm_ref, obuf_ref.at[0], sem.at[0], ds_ref, 0, CHUNK_GRANS)

    _gather_wait(o_hbm_ref, obuf_ref.at[slot], sem.at[slot], ds_ref, c * CHUNK_GRANS, CHUNK_GRANS)

    _gather_start(o_hbm_ref, obuf_ref.at[1 - slot], sem.at[1 - slot], ds_ref, nxt * CHUNK_GRANS, CHUNK_GRANS)

    osv = obuf_ref[slot]
    scol = lax.broadcasted_iota(jnp.int32, (W // 2, CHUNK_SLOTS), 1)
    for r in range(2):
        rr = slice(r * (W // 2), (r + 1) * (W // 2))
        rt = rt_ref[rr, :]
        p1 = (scol == rt[:, 0:1].astype(jnp.int32)).astype(f32).astype(bf16)
        p2 = (scol == rt[:, 1:2].astype(jnp.int32)).astype(f32).astype(bf16)
        y = rt[:, 2:3] * _dot(p1, osv) + rt[:, 3:4] * _dot(p2, osv)
        xr = ALPHA * x_ref[rr, :] + (1.0 + mod_ref[5:6, :]) * y
        out_ref[rr, :] = _layer_norm(xr, lng_ref[...], lnb_ref[...])

    @pl.when(c == nc - 1)
    def _():
        _gather_wait(o_hbm_ref, obuf_ref.at[1 - slot], sem.at[1 - slot], ds_ref, nxt * CHUNK_GRANS, CHUNK_GRANS)


def _combine(dst_of_src, x2d, mod_l, o2d, route_t, ln_g, ln_b, S):
    T = x2d.shape[0]
    W = ROUTE_W
    tpb = S // W
    grid_spec = pltpu.PrefetchScalarGridSpec(
        num_scalar_prefetch=1,
        grid=(T // W,),
        in_specs=[
            pl.BlockSpec((W, D), lambda c, ds: (c, 0)),
            pl.BlockSpec((None, 6, D), lambda c, ds: (c // tpb, 0, 0)),
            pl.BlockSpec(memory_space=pl.ANY),
            pl.BlockSpec((None, W, 8), lambda c, ds: (c, 0, 0)),
            pl.BlockSpec((1, D), lambda c, ds: (0, 0)),
            pl.BlockSpec((1, D), lambda c, ds: (0, 0)),
        ],
        out_specs=pl.BlockSpec((W, D), lambda c, ds: (c, 0)),
        scratch_shapes=[
            pltpu.VMEM((2, CHUNK_SLOTS, D), bf16),
            pltpu.SemaphoreType.DMA((2,)),
        ],
    )
    return pl.pallas_call(
        _combine_kernel,
        grid_spec=grid_spec,
        out_shape=jax.ShapeDtypeStruct((T, D), f32),
        compiler_params=pltpu.CompilerParams(vmem_limit_bytes=VMEM_LIMIT),
        name="combine",
    )(dst_of_src, x2d, mod_l, o2d, route_t, ln_g, ln_b)


def _routing_tables(counts, n_tiles):
    nc = counts.shape[0]
    gr = (counts + GRAN - 1) // GRAN
    seg_start = jnp.cumsum(gr, axis=1) - gr
    chunk_total = jnp.sum(gr, axis=1)
    prefix = jnp.cumsum(gr, axis=0) - gr
    g_e = jnp.sum(gr, axis=0)
    tiles_e = (g_e + TILE_GRANS - 1) // TILE_GRANS
    tile_end = jnp.cumsum(tiles_e)
    tile_start = tile_end - tiles_e
    total_tiles = tile_end[-1]

    i32 = jnp.int32
    er = jnp.arange(N_EXPERTS, dtype=i32)
    t = jnp.arange(n_tiles, dtype=i32)
    te = jnp.sum((t[:, None] >= tile_end[None, :]).astype(i32), axis=1)
    tile_valid = (t < total_tiles).astype(i32)
    last_e = jnp.sum((total_tiles - 1 >= tile_end).astype(i32))
    tile_expert = jnp.where(tile_valid > 0, jnp.minimum(te, N_EXPERTS - 1), last_e).astype(i32)
    oh_t = (tile_expert[:, None] == er).astype(i32)
    grans_left = jnp.sum(oh_t * (g_e - (t[:, None] - tile_start[None, :]) * TILE_GRANS), axis=1)
    tile_subs = tile_valid * jnp.clip((grans_left + SUB_GRANS - 1) // SUB_GRANS, 0, TILE_GRANS // SUB_GRANS)

    k = jnp.arange(CHUNK_GRANS, dtype=i32)
    seg_end = seg_start + gr
    e_of = jnp.sum((k[None, :, None] >= seg_end[:, None, :]).astype(i32), axis=2)
    oh_e = (jnp.minimum(e_of, N_EXPERTS - 1)[:, :, None] == er).astype(i32)
    base = tile_start[None, :] * TILE_GRANS + prefix - seg_start
    dst = jnp.sum(oh_e * base[:, None, :], axis=2) + k[None, :]
    valid_src = k[None, :] < chunk_total[:, None]
    dst_of_src = jnp.where(valid_src, dst, 0).astype(i32).reshape(-1)

    d = jnp.arange(n_tiles * TILE_GRANS, dtype=i32)
    oh_d = (jnp.repeat(tile_expert, TILE_GRANS)[:, None] == er).astype(i32)
    q = d - jnp.sum(oh_d * tile_start[None, :], axis=1) * TILE_GRANS
    incl_d = jnp.sum(oh_d[:, :, None] * (prefix + gr).T[None], axis=1)
    c_d = jnp.sum((q[:, None] >= incl_d).astype(i32), axis=1)
    oh_c = (jnp.minimum(c_d, nc - 1)[:, None] == jnp.arange(nc, dtype=i32)).astype(i32)
    cbase = jnp.arange(nc, dtype=i32)[:, None] * CHUNK_GRANS + seg_start - prefix
    sel = jnp.sum(oh_c[:, :, None] * oh_d[:, None, :] * cbase[None], axis=(1, 2))
    valid_dst = (jnp.repeat(tile_valid, TILE_GRANS) > 0) & (q >= 0) & (q < jnp.sum(oh_d * g_e[None, :], axis=1))
    src_of_dst = jnp.where(valid_dst, sel + q, 0).astype(i32)
    return tile_expert, tile_subs.astype(i32), src_of_dst, dst_of_src


def kernel(x, c, ada_w, ada_b, ln_g, ln_b, ev_w_in, ev_pool_w, ev_pool_scale, ev_sink, ev_w_out, od_w_in, od_sg_ln_g, od_sg_ln_b, od_w_s, od_b_s, od_w_out, ffn_w_gate, ffn_w_up, ffn_w_down, moe_w_router, moe_w_gate, moe_w_up, moe_w_down):
    B, S, _ = x.shape
    T = B * S
    x2d = x.reshape(T, D)
    mod = _adaln(c, ada_w, ada_b)

    w_in = ev_w_in[0].astype(bf16)
    q0, k0, v0 = POOL_WIDTH, POOL_WIDTH + ATTN_WIDTH, POOL_WIDTH + ATTN_WIDTH + KV_WIDTH
    w_pk = jnp.concatenate([w_in[:, :q0], w_in[:, k0:v0]], axis=1)
    w_qv_t = jnp.concatenate([w_in[:, q0:k0], w_in[:, v0:]], axis=1).T
    p, k, qt, vt = _ev_in(x2d, mod[0], w_pk, w_qv_t, S)
    x2d = _ev_mix(x2d, mod[0], p, qt, k, vt, ev_pool_w[0].astype(bf16), ev_pool_scale[0][None, :],
                  ev_sink[0], ev_w_out[0].astype(bf16), ln_g[0, 0][None, :], ln_b[0, 0][None, :], B, S)
    x2d = _ffn(x2d, mod[0], ffn_w_gate[0], ffn_w_up[0], ffn_w_down[0],
               ln_g[0, 1][None, :], ln_b[0, 1][None, :], S)

    x2d = _sg_mix(x2d, mod[1], od_w_in[0].astype(bf16), od_sg_ln_g[0][None, :], od_sg_ln_b[0][None, :],
                  od_w_s[0].astype(bf16), od_b_s[0].T, od_w_out[0].astype(bf16),
                  ln_g[1, 0][None, :], ln_b[1, 0][None, :], S)

    hs, route, cnt = _route(x2d, mod[1], moe_w_router[0].T, S)
    nc = T // ROUTE_W
    n_tiles = (nc * CHUNK_GRANS) // TILE_GRANS + N_EXPERTS
    tile_expert, tile_subs, src_of_dst, dst_of_src = _routing_tables(cnt[:, :, 0], n_tiles)
    o = _experts(tile_expert, tile_subs, src_of_dst, hs.reshape(nc * CHUNK_SLOTS, D),
                 moe_w_gate[0], moe_w_up[0], moe_w_down[0])
    x2d = _combine(dst_of_src, x2d, mod[1], o, jnp.swapaxes(route, 1, 2),
                   ln_g[1, 1][None, :], ln_b[1, 1][None, :], S)
    return x2d.reshape(B, S, D)
```

```python
import functools
import math

import jax
import jax.numpy as jnp
from jax import lax
from jax.experimental import pallas as pl
from jax.experimental.pallas import tpu as pltpu

D = 1024
DEPTH = 2
ALPHA = (2.0 * DEPTH) ** 0.25
LN_EPS = 1e-5

POOL_WINDOWS = (2, 4, 8, 16)
POOL_CH = 128
POOL_WIDTH = 512
HEAD_DIM = 64
N_Q_HEADS = 8
N_KV_HEADS = 2
Q_GROUP = 4
ATTN_WIDTH = 512
KV_WIDTH = 128
BLOCK = 128
EVEN_IN = 1280
POOL_HALO = 8

CHUNK = 128
SG_GROUPS = 8
SG_CH = 128

N_EXPERTS = 8

ROUTE_W = 512
ROUTE_PER_STEP = 2
GRAN = 16
CHUNK_SLOTS = 2 * ROUTE_W + N_EXPERTS * GRAN
CHUNK_GRANS = CHUNK_SLOTS // GRAN
MOE_SUB = 256
MOE_FAST = 2048
MOE_TM = MOE_FAST + MOE_SUB
TILE_GRANS = MOE_TM // GRAN
SUB_GRANS = MOE_SUB // GRAN
MOE_TF = 512

VMEM_LIMIT = 48 * 1024 * 1024
VMEM_LIMIT_BIG = 56 * 1024 * 1024

bf16 = jnp.bfloat16
f32 = jnp.float32


def _dot(a, b):
    return jnp.dot(a, b, preferred_element_type=f32)


def _split_bf16(a):
    hi = a.astype(bf16)
    lo = (a - hi.astype(f32)).astype(bf16)
    return hi, lo


def _layer_norm(x, g, b):
    mu = jnp.mean(x, axis=-1, keepdims=True)
    xc = x - mu
    var = jnp.mean(xc * xc, axis=-1, keepdims=True)
    return xc * lax.rsqrt(var + LN_EPS) * g + b


def _silu(x):
    return x * jax.nn.sigmoid(x)


def _gelu_tanh(x):
    c = math.sqrt(2.0 / math.pi)
    return x * (0.5 * (1.0 + jnp.tanh(c * (x + 0.044715 * (x * x * x)))))


def _adaln_kernel(c_ref, w_ref, b_ref, o_ref):
    cond = _silu(c_ref[...])
    c_hi, c_lo = _split_bf16(cond)
    w_hi, w_lo = _split_bf16(w_ref[...])
    acc = _dot(c_hi, w_hi) + (_dot(c_lo, w_hi) + _dot(c_hi, w_lo))
    o_ref[...] = acc + b_ref[...]


def _adaln(c, ada_w, ada_b):
    B = c.shape[0]
    tn = 1024
    c_pad = jnp.zeros((8, D), f32).at[:B].set(c)
    out = pl.pallas_call(
        _adaln_kernel,
        grid=(DEPTH, 6 * D // tn),
        in_specs=[
            pl.BlockSpec((8, D), lambda l, j: (0, 0)),
            pl.BlockSpec((None, D, tn), lambda l, j: (l, 0, j)),
            pl.BlockSpec((None, 1, tn), lambda l, j: (l, 0, j)),
        ],
        out_specs=pl.BlockSpec((None, 8, tn), lambda l, j: (l, 0, j)),
        out_shape=jax.ShapeDtypeStruct((DEPTH, 8, 6 * D), f32),
        compiler_params=pltpu.CompilerParams(vmem_limit_bytes=VMEM_LIMIT),
        name="adaln",
    )(c_pad, ada_w, ada_b.reshape(DEPTH, 1, 6 * D))
    return out[:, :B].reshape(DEPTH, B, 6, D)


def _ev_in_kernel(x_ref, mod_ref, wpk_ref, wqvt_ref, p_ref, k_ref, qt_ref, vt_ref):
    h = (x_ref[...] * (1.0 + mod_ref[1:2, :]) + mod_ref[0:1, :]).astype(bf16)
    zpk = _dot(h, wpk_ref[...])
    p_ref[...] = zpk[:, :POOL_WIDTH]
    k_ref[...] = zpk[:, POOL_WIDTH:].astype(bf16)
    zt = lax.dot_general(wqvt_ref[...], h, (((1,), (1,)), ((), ())), preferred_element_type=f32)
    qt_ref[...] = (zt[:ATTN_WIDTH] * (HEAD_DIM ** -0.5)).astype(bf16)
    vt_ref[...] = zt[ATTN_WIDTH:].astype(bf16)


def _ev_in(x2d, mod_l, w_pk, w_qv_t, S):
    T = x2d.shape[0]
    tm = 512
    tpb = S // tm
    return pl.pallas_call(
        _ev_in_kernel,
        grid=(T // tm,),
        in_specs=[
            pl.BlockSpec((tm, D), lambda i: (i, 0)),
            pl.BlockSpec((None, 6, D), lambda i: (i // tpb, 0, 0)),
            pl.BlockSpec((D, POOL_WIDTH + KV_WIDTH), lambda i: (0, 0)),
            pl.BlockSpec((ATTN_WIDTH + KV_WIDTH, D), lambda i: (0, 0)),
        ],
        out_specs=[
            pl.BlockSpec((tm, POOL_WIDTH), lambda i: (i, 0)),
            pl.BlockSpec((tm, KV_WIDTH), lambda i: (i, 0)),
            pl.BlockSpec((ATTN_WIDTH, tm), lambda i: (0, i)),
            pl.BlockSpec((KV_WIDTH, tm), lambda i: (0, i)),
        ],
        out_shape=[
            jax.ShapeDtypeStruct((T, POOL_WIDTH), f32),
            jax.ShapeDtypeStruct((T, KV_WIDTH), bf16),
            jax.ShapeDtypeStruct((ATTN_WIDTH, T), bf16),
            jax.ShapeDtypeStruct((KV_WIDTH, T), bf16),
        ],
        compiler_params=pltpu.CompilerParams(vmem_limit_bytes=VMEM_LIMIT),
        name="ev_in",
    )(x2d, mod_l, w_pk, w_qv_t)


def _ev_mix_kernel(x_ref, mod_ref, p_ref, pp_ref, pn_ref, qt_ref,
                   k_ref, kp_ref, kn_ref, vt_ref, vtp_ref, vtn_ref,
                   bias_ref, sink_ref, wpool_ref, pscale_ref, wout_ref, lng_ref, lnb_ref,
                   o_ref, pext_ref, lvl_ref, pooled_ref, kext_ref, vext_ref, ybt_ref, mix_ref, *, S, tq):
    i = pl.program_id(1)
    n_tiles = S // tq
    is_first = i == 0
    is_last = i == n_tiles - 1
    H = POOL_HALO

    p = p_ref[...]
    pext_ref[0:H, :] = jnp.where(is_first, 0.0, pp_ref[...])
    pext_ref[H:H + tq, :] = p
    pext_ref[H + tq:2 * H + tq, :] = jnp.where(is_last, 0.0, pn_ref[...])
    pext_ref[2 * H + tq:, :] = jnp.zeros((pext_ref.shape[0] - 2 * H - tq, POOL_WIDTH), f32)
    near = lax.broadcasted_iota(jnp.int32, (H, 1), 0)

    def src_rows(src, k, off, n, cs):
        return src[off:off + n, cs] if src is pext_ref else src[k - 1, off:off + n, cs]

    def pool_group(g):
        w = POOL_WINDOWS[g]
        cs = slice(g * POOL_CH, (g + 1) * POOL_CH)
        r = w // 2
        src, length, k = pext_ref, pext_ref.shape[0] - H, 0
        while 2 ** k < r:
            step = 2 ** k
            lvl_ref[k, 0:length, cs] = src_rows(src, k, 0, length, cs) + src_rows(src, k, step, length, cs)
            src, length, k = lvl_ref, length - H, k + 1
        half_run = 2 ** k
        wsum = (src_rows(src, k, H - r, tq, cs) + src_rows(src, k, H - r + half_run, tq, cs)
                + pext_ref[H + r:H + r + tq, cs])
        pooled_ref[:, cs] = wsum / float(w + 1) - p[:, cs]
        cnt_head = (jnp.minimum(near, r) + (r + 1)).astype(f32)
        cnt_tail = (jnp.minimum(H - 1 - near, r) + (r + 1)).astype(f32)
        cnt_head = jnp.where(is_first, cnt_head, float(w + 1))
        cnt_tail = jnp.where(is_last, cnt_tail, float(w + 1))
        pooled_ref[0:H, cs] = wsum[0:H] / cnt_head - p[0:H, cs]
        pooled_ref[tq - H:tq, cs] = wsum[tq - H:tq] / cnt_tail - p[tq - H:tq, cs]
        ya = _dot(pooled_ref[:, cs].astype(bf16), wpool_ref[g])
        mix_ref[:, cs] = (ya * pscale_ref[:, cs]).astype(bf16)

    for g in range(len(POOL_WINDOWS)):
        pool_group(g)

    kext_ref[0:BLOCK, :] = kp_ref[...]
    kext_ref[BLOCK:BLOCK + tq, :] = k_ref[...]
    kext_ref[BLOCK + tq:, :] = kn_ref[...]
    vext_ref[:, 0:BLOCK] = vtp_ref[...]
    vext_ref[:, BLOCK:BLOCK + tq] = vt_ref[...]
    vext_ref[:, BLOCK + tq:] = vtn_ref[...]

    n_blocks = S // BLOCK
    zeros_q = jnp.zeros((HEAD_DIM, Q_GROUP * BLOCK), bf16)
    items = [(n, kvh) for n in range(tq // BLOCK) for kvh in range(N_KV_HEADS)]

    def scores(n, kvh):
        gb = i * (tq // BLOCK) + n
        variant = jnp.where(gb == 0, 1, jnp.where(gb == n_blocks - 1, 2, 0))
        cols = slice(n * BLOCK, (n + 1) * BLOCK)
        kw = kext_ref[n * BLOCK:n * BLOCK + 3 * BLOCK, :]
        qst = jnp.concatenate(
            [qt_ref[(kvh * Q_GROUP + gq) * HEAD_DIM:(kvh * Q_GROUP + gq + 1) * HEAD_DIM, cols]
             for gq in range(Q_GROUP)], axis=1)
        qst = jnp.concatenate([qst, zeros_q] if kvh == 0 else [zeros_q, qst], axis=0)
        return _dot(kw, qst) + bias_ref[variant, kvh]

    def softmax(n, kvh, s):
        sink = sink_ref[kvh]
        m = jnp.maximum(jnp.max(s, axis=0, keepdims=True), sink)
        e = jnp.exp(s - m)
        denom = jnp.sum(e, axis=0, keepdims=True) + jnp.exp(sink - m)
        return e.astype(bf16), denom

    def weighted_values(n, kvh, e, denom):
        cols = slice(n * BLOCK, (n + 1) * BLOCK)
        vwt = vext_ref[kvh * HEAD_DIM:(kvh + 1) * HEAD_DIM, n * BLOCK:n * BLOCK + 3 * BLOCK]
        out = _dot(vwt, e) / denom
        for gq in range(Q_GROUP):
            hq = kvh * Q_GROUP + gq
            ybt_ref[hq * HEAD_DIM:(hq + 1) * HEAD_DIM, cols] = out[:, gq * BLOCK:(gq + 1) * BLOCK]

    s_next = scores(*items[0])
    prev = None
    for idx, item in enumerate(items):
        s_cur = s_next
        if idx + 1 < len(items):
            s_next = scores(*items[idx + 1])
        cur = softmax(*item, s_cur)
        if prev is not None:
            weighted_values(*items[idx - 1], *prev)
        prev = cur
    weighted_values(*items[-1], *prev)
    mix_ref[:, POOL_WIDTH:] = ybt_ref[...].T.astype(bf16)

    halves = [slice(r * (tq // 2), (r + 1) * (tq // 2)) for r in range(2)]
    ys = [_dot(mix_ref[rr, :], wout_ref[...]) for rr in halves]
    for rr, y in zip(halves, ys):
        xr = ALPHA * x_ref[rr, :] + (1.0 + mod_ref[2:3, :]) * y
        o_ref[rr, :] = _layer_norm(xr, lng_ref[...], lnb_ref[...])


def _ev_mix(x2d, mod_l, p, qt, k, vt, w_pool, pool_scale, sink, w_out, ln_g, ln_b, B, S):
    T = x2d.shape[0]
    tq = 512
    nt = S // tq
    kb = tq // BLOCK
    pb = tq // POOL_HALO
    n_kblocks = T // BLOCK
    n_pblocks = T // POOL_HALO

    def main(b, i): return (b * nt + i, 0)
    def kprev(b, i): return (jnp.maximum((b * nt + i) * kb - 1, 0), 0)
    def knext(b, i): return (jnp.minimum((b * nt + i + 1) * kb, n_kblocks - 1), 0)
    def pprev(b, i): return (jnp.maximum((b * nt + i) * pb - 1, 0), 0)
    def pnext(b, i): return (jnp.minimum((b * nt + i + 1) * pb, n_pblocks - 1), 0)
    def const2(b, i): return (0, 0)

    def tmain(b, i): return (0, b * nt + i)
    def tprev(b, i): return (0, jnp.maximum((b * nt + i) * kb - 1, 0))
    def tnext(b, i): return (0, jnp.minimum((b * nt + i + 1) * kb, n_kblocks - 1))

    assert S // BLOCK >= 2
    kj = jnp.arange(3 * BLOCK)[:, None]
    qi = jnp.arange(BLOCK)[None, :]
    dist = jnp.abs(kj - BLOCK - qi)
    slopes = 2.0 ** (-8.0 * jnp.arange(1, N_Q_HEADS + 1, dtype=f32) / N_Q_HEADS)
    alibi = -slopes[:, None, None] * dist.astype(f32)[None]
    in_window = dist <= BLOCK
    key_ok = jnp.stack([kj >= 0, kj >= BLOCK, kj < 2 * BLOCK])
    bias = jnp.where((in_window[None] & key_ok)[:, None], alibi[None], -1e30)
    bias = bias.reshape(3, N_KV_HEADS, Q_GROUP, 3 * BLOCK, BLOCK).transpose(0, 1, 3, 2, 4)
    bias = bias.reshape(3, N_KV_HEADS, 3 * BLOCK, Q_GROUP * BLOCK)
    sink_row = jnp.repeat(sink.astype(f32).reshape(N_KV_HEADS, Q_GROUP), BLOCK, axis=1)[:, None, :]

    assert tq >= 2 * POOL_HALO and max(POOL_WINDOWS) // 2 <= POOL_HALO and nt >= 2

    kernel = functools.partial(_ev_mix_kernel, S=S, tq=tq)
    return pl.pallas_call(
        kernel,
        grid=(B, nt),
        in_specs=[
            pl.BlockSpec((tq, D), main),
            pl.BlockSpec((None, 6, D), lambda b, i: (b, 0, 0)),
            pl.BlockSpec((tq, POOL_WIDTH), main),
            pl.BlockSpec((POOL_HALO, POOL_WIDTH), pprev),
            pl.BlockSpec((POOL_HALO, POOL_WIDTH), pnext),
            pl.BlockSpec((ATTN_WIDTH, tq), tmain),
            pl.BlockSpec((tq, KV_WIDTH), main),
            pl.BlockSpec((BLOCK, KV_WIDTH), kprev),
            pl.BlockSpec((BLOCK, KV_WIDTH), knext),
            pl.BlockSpec((KV_WIDTH, tq), tmain),
            pl.BlockSpec((KV_WIDTH, BLOCK), tprev),
            pl.BlockSpec((KV_WIDTH, BLOCK), tnext),
            pl.BlockSpec((3, N_KV_HEADS, 3 * BLOCK, Q_GROUP * BLOCK), lambda b, i: (0, 0, 0, 0)),
            pl.BlockSpec((N_KV_HEADS, 1, Q_GROUP * BLOCK), lambda b, i: (0, 0, 0)),
            pl.BlockSpec((len(POOL_WINDOWS), POOL_CH, POOL_CH), lambda b, i: (0, 0, 0)),
            pl.BlockSpec((1, POOL_WIDTH), const2),
            pl.BlockSpec((D, D), const2),
            pl.BlockSpec((1, D), const2),
            pl.BlockSpec((1, D), const2),
        ],
        out_specs=pl.BlockSpec((tq, D), main),
        out_shape=jax.ShapeDtypeStruct((T, D), f32),
        scratch_shapes=[
            pltpu.VMEM((tq + 4 * POOL_HALO, POOL_WIDTH), f32),
            pltpu.VMEM((3, tq + 3 * POOL_HALO, POOL_WIDTH), f32),
            pltpu.VMEM((tq, POOL_WIDTH), f32),
            pltpu.VMEM((tq + 2 * BLOCK, KV_WIDTH), bf16),
            pltpu.VMEM((KV_WIDTH, tq + 2 * BLOCK), bf16),
            pltpu.VMEM((ATTN_WIDTH, tq), f32),
            pltpu.VMEM((tq, D), bf16),
        ],
        compiler_params=pltpu.CompilerParams(vmem_limit_bytes=VMEM_LIMIT),
        name="ev_mix",
    )(x2d, mod_l, p, p, p, qt, k, k, k, vt, vt, vt, bias, sink_row, w_pool, pool_scale, w_out, ln_g, ln_b)


def _load_rounded(w_hbm_ref, w_ref, stage_ref, sem):
    rows = stage_ref.shape[1]
    n = w_hbm_ref.shape[0] // rows

    def copy(c):
        return pltpu.make_async_copy(w_hbm_ref.at[pl.ds(c * rows, rows), :], stage_ref.at[c % 2], sem.at[c % 2])

    copy(0).start()
    for c in range(n):
        if c + 1 < n:
            copy(c + 1).start()
        copy(c).wait()
        w_ref[pl.ds(c * rows, rows), :] = stage_ref[c % 2].astype(bf16)


def _ffn_kernel(x_ref, mod_ref, wg_hbm_ref, wu_hbm_ref, wd_hbm_ref, lng_ref, lnb_ref, o_ref,
                wg_ref, wu_ref, wd_ref, stage_in_ref, stage_out_ref, sem):
    @pl.when(pl.program_id(0) == 0)
    def _():
        _load_rounded(wg_hbm_ref, wg_ref, stage_in_ref, sem)
        _load_rounded(wu_hbm_ref, wu_ref, stage_in_ref, sem)
        _load_rounded(wd_hbm_ref, wd_ref, stage_out_ref, sem)

    n_pieces = 2
    piece = x_ref.shape[0] // n_pieces
    parts = [pl.ds(q * piece, piece) for q in range(n_pieces)]

    def gate_up(rows):
        x = x_ref[rows, :]
        h = (x * (1.0 + mod_ref[4:5, :]) + mod_ref[3:4, :]).astype(bf16)
        return x, _dot(h, wg_ref[...]), _dot(h, wu_ref[...])

    nxt = gate_up(parts[0])
    for q, rows in enumerate(parts):
        x, g, u = nxt
        if q + 1 < n_pieces:
            nxt = gate_up(parts[q + 1])
        y = _dot((_silu(g) * u).astype(bf16), wd_ref[...])
        xr = ALPHA * x + (1.0 + mod_ref[5:6, :]) * y
        o_ref[rows, :] = _layer_norm(xr, lng_ref[...], lnb_ref[...])


def _ffn(x2d, mod_l, w_gate, w_up, w_down, ln_g, ln_b, S):
    T = x2d.shape[0]
    tm = 512
    dff = w_gate.shape[1]
    tpb = S // tm
    n_stage = 8
    return pl.pallas_call(
        _ffn_kernel,
        grid=(T // tm,),
        in_specs=[
            pl.BlockSpec((tm, D), lambda i: (i, 0)),
            pl.BlockSpec((None, 6, D), lambda i: (i // tpb, 0, 0)),
            pl.BlockSpec(memory_space=pl.ANY),
            pl.BlockSpec(memory_space=pl.ANY),
            pl.BlockSpec(memory_space=pl.ANY),
            pl.BlockSpec((1, D), lambda i: (0, 0)),
            pl.BlockSpec((1, D), lambda i: (0, 0)),
        ],
        out_specs=pl.BlockSpec((tm, D), lambda i: (i, 0)),
        out_shape=jax.ShapeDtypeStruct((T, D), f32),
        scratch_shapes=[
            pltpu.VMEM((D, dff), bf16),
            pltpu.VMEM((D, dff), bf16),
            pltpu.VMEM((dff, D), bf16),
            pltpu.VMEM((2, D // n_stage, dff), f32),
            pltpu.VMEM((2, dff // n_stage, D), f32),
            pltpu.SemaphoreType.DMA((2,)),
        ],
        compiler_params=pltpu.CompilerParams(vmem_limit_bytes=VMEM_LIMIT_BIG),
        name="ffn",
    )(x2d, mod_l, w_gate, w_up, w_down, ln_g, ln_b)


def _sg_kernel(x_ref, mod_ref, win_ref, sgg_ref, sgb_ref, ws_ref, bst_ref, wout_ref, lng_ref, lnb_ref,
               o_ref, gate_ref, *, tm):
    n_pieces = 2
    piece = tm // n_pieces

    def project(q):
        x = x_ref[q * piece:(q + 1) * piece, :]
        h = (x * (1.0 + mod_ref[1:2, :]) + mod_ref[0:1, :]).astype(bf16)
        return x, _dot(h, win_ref[:, D:]), _dot(h, win_ref[:, :D])

    nxt = project(0)
    for q in range(n_pieces):
        x, zv, zu = nxt
        if q + 1 < n_pieces:
            nxt = project(q + 1)
        v = _layer_norm(_gelu_tanh(zv), sgg_ref[...], sgb_ref[...]).astype(bf16)
        u = _gelu_tanh(zu)
        for n in range(piece // CHUNK):
            rows = slice(n * CHUNK, (n + 1) * CHUNK)
            grows = slice(q * piece + n * CHUNK, q * piece + (n + 1) * CHUNK)
            for g in range(SG_GROUPS):
                cols = slice(g * SG_CH, (g + 1) * SG_CH)
                sv = _dot(ws_ref[g], v[rows, cols]) + bst_ref[:, g:g + 1]
                gate_ref[grows, cols] = (u[rows, cols] * sv).astype(bf16)
        y = _dot(gate_ref[q * piece:(q + 1) * piece, :], wout_ref[...])
        xr = ALPHA * x + (1.0 + mod_ref[2:3, :]) * y
        o_ref[q * piece:(q + 1) * piece, :] = _layer_norm(xr, lng_ref[...], lnb_ref[...])


def _sg_mix(x2d, mod_l, w_in, sg_g, sg_b, w_s, b_s_t, w_out, ln_g, ln_b, S):
    T = x2d.shape[0]
    tm = 512
    tpb = S // tm
    c2 = lambda i: (0, 0)
    return pl.pallas_call(
        functools.partial(_sg_kernel, tm=tm),
        grid=(T // tm,),
        in_specs=[
            pl.BlockSpec((tm, D), lambda i: (i, 0)),
            pl.BlockSpec((None, 6, D), lambda i: (i // tpb, 0, 0)),
            pl.BlockSpec((D, 2 * D), c2),
            pl.BlockSpec((1, D), c2),
            pl.BlockSpec((1, D), c2),
            pl.BlockSpec((SG_GROUPS, CHUNK, CHUNK), lambda i: (0, 0, 0)),
            pl.BlockSpec((CHUNK, SG_GROUPS), c2),
            pl.BlockSpec((D, D), c2),
            pl.BlockSpec((1, D), c2),
            pl.BlockSpec((1, D), c2),
        ],
        out_specs=pl.BlockSpec((tm, D), lambda i: (i, 0)),
        out_shape=jax.ShapeDtypeStruct((T, D), f32),
        scratch_shapes=[pltpu.VMEM((tm, D), bf16)],
        compiler_params=pltpu.CompilerParams(vmem_limit_bytes=VMEM_LIMIT),
        name="sg_mix",
    )(x2d, mod_l, w_in, sg_g, sg_b, w_s, b_s_t, w_out, ln_g, ln_b)


def _route_kernel(x_ref, mod_ref, wrt_ref, hs_ref, route_ref, cnt_ref):
    W = ROUTE_W
    w_hi, w_lo = _split_bf16(wrt_ref[...])
    nt = (((1,), (1,)), ((), ()))
    eidx = lax.broadcasted_iota(jnp.int32, (N_EXPERTS, W), 0)
    sub = lax.broadcasted_iota(jnp.int32, (N_EXPERTS, 1), 0)
    tr = lax.broadcasted_iota(jnp.int32, (W, W), 0)
    tc = lax.broadcasted_iota(jnp.int32, (W, W), 1)
    upper = (tr < tc).astype(bf16)
    srow = lax.broadcasted_iota(jnp.int32, (CHUNK_SLOTS, W), 0)
    ridx = lax.broadcasted_iota(jnp.int32, (8, W), 0)

    def assign_slots(k):
        h = x_ref[k * W:(k + 1) * W, :] * (1.0 + mod_ref[4:5, :]) + mod_ref[3:4, :]
        h_hi, h_lo = _split_bf16(h)
        logits = (lax.dot_general(w_hi, h_hi, nt, preferred_element_type=f32)
                  + (lax.dot_general(w_hi, h_lo, nt, preferred_element_type=f32)
                     + lax.dot_general(w_lo, h_hi, nt, preferred_element_type=f32)))
        m1 = jnp.max(logits, axis=0, keepdims=True)
        i1 = jnp.min(jnp.where(logits == m1, eidx, N_EXPERTS), axis=0, keepdims=True)
        sel1 = eidx == i1
        rest = jnp.where(sel1, -jnp.inf, logits)
        m2 = jnp.max(rest, axis=0, keepdims=True)
        i2 = jnp.min(jnp.where(rest == m2, eidx, N_EXPERTS), axis=0, keepdims=True)
        sel2 = eidx == i2
        e2 = jnp.exp(m2 - m1)
        g1 = 1.0 / (1.0 + e2)
        g2 = e2 / (1.0 + e2)

        a1 = sel1.astype(f32)
        a2 = sel2.astype(f32)
        assign = a1 + a2
        counts = jnp.sum(assign, axis=1, keepdims=True)
        grans = jnp.ceil(counts * (1.0 / GRAN))
        seg = jnp.zeros((N_EXPERTS, 1), f32)
        for e in range(N_EXPERTS - 1):
            seg = seg + jnp.where(sub > e, grans[e:e + 1, :] * GRAN, 0.0)
        rank = _dot(assign.astype(bf16), upper)
        slot = seg + rank
        pos1 = jnp.sum(a1 * slot, axis=0, keepdims=True)
        pos2 = jnp.sum(a2 * slot, axis=0, keepdims=True)
        route_ref[k] = jnp.where(ridx == 0, pos1, jnp.where(ridx == 1, pos2,
                                 jnp.where(ridx == 2, g1, jnp.where(ridx == 3, g2, 0.0))))
        cnt_ref[k] = jnp.broadcast_to(counts, (N_EXPERTS, 128)).astype(jnp.int32)
        return h_hi, pos1, pos2

    def sort_rows(k, h_hi, pos1, pos2):
        perm = ((srow == pos1.astype(jnp.int32)) | (srow == pos2.astype(jnp.int32)))
        hs_ref[k] = _dot(perm.astype(f32).astype(bf16), h_hi).astype(bf16)

    routed = [assign_slots(k) for k in range(ROUTE_PER_STEP)]
    for k in range(ROUTE_PER_STEP):
        sort_rows(k, *routed[k])


def _route(x2d, mod_l, w_router_t, S):
    T = x2d.shape[0]
    W = ROUTE_W
    R = ROUTE_PER_STEP
    nc = T // W
    tpb = S // (R * W)
    return pl.pallas_call(
        _route_kernel,
        grid=(nc // R,),
        in_specs=[
            pl.BlockSpec((R * W, D), lambda c: (c, 0)),
            pl.BlockSpec((None, 6, D), lambda c: (c // tpb, 0, 0)),
            pl.BlockSpec((N_EXPERTS, D), lambda c: (0, 0)),
        ],
        out_specs=[
            pl.BlockSpec((R, CHUNK_SLOTS, D), lambda c: (c, 0, 0)),
            pl.BlockSpec((R, 8, W), lambda c: (c, 0, 0)),
            pl.BlockSpec((R, N_EXPERTS, 128), lambda c: (c, 0, 0)),
        ],
        out_shape=[
            jax.ShapeDtypeStruct((nc, CHUNK_SLOTS, D), bf16),
            jax.ShapeDtypeStruct((nc, 8, W), f32),
            jax.ShapeDtypeStruct((nc, N_EXPERTS, 128), jnp.int32),
        ],
        compiler_params=pltpu.CompilerParams(vmem_limit_bytes=VMEM_LIMIT),
        name="route",
    )(x2d, mod_l, w_router_t)


def _granule_copy(src_ref, buf_ref, sem, idx_ref, base, g):
    row = pl.multiple_of(idx_ref[base + g] * GRAN, GRAN)
    return pltpu.make_async_copy(src_ref.at[pl.ds(row, GRAN), :], buf_ref.at[pl.ds(g * GRAN, GRAN), :], sem)


def _gather_start(src_ref, buf_ref, sem, idx_ref, base, n):
    for g in range(n):
        _granule_copy(src_ref, buf_ref, sem, idx_ref, base, g).start()


def _gather_wait(src_ref, buf_ref, sem, idx_ref, base, n):
    for g in range(n):
        _granule_copy(src_ref, buf_ref, sem, idx_ref, base, g).wait()


def _expert_kernel(te_ref, tv_ref, src_ref, hs_ref, wg_ref, wu_ref, wd_ref, o_ref,
                   xbuf_ref, sem, acc_ref, wgb_ref, wub_ref, wdb_ref):
    i = pl.program_id(0)
    j = pl.program_id(1)
    n_tiles = pl.num_programs(0)
    last = pl.num_programs(1) - 1
    valid = tv_ref[i] > 0
    slot = i % 2
    nxt = jnp.minimum(i + 1, n_tiles - 1)

    @pl.when(j == 0)
    def _():
        @pl.when(i == 0)
        def _():
            _gather_start(hs_ref, xbuf_ref.at[0], sem.at[0], src_ref, 0, TILE_GRANS)

        @pl.when(valid)
        def _():
            _gather_wait(hs_ref, xbuf_ref.at[slot], sem.at[slot], src_ref, i * TILE_GRANS, TILE_GRANS)

        @pl.when((i + 1 < n_tiles) & (tv_ref[nxt] > 0))
        def _():
            _gather_start(hs_ref, xbuf_ref.at[1 - slot], sem.at[1 - slot], src_ref, nxt * TILE_GRANS, TILE_GRANS)

    n_sub = tv_ref[i]
    fast_subs = MOE_FAST // MOE_SUB
    fast = n_sub >= fast_subs

    def round_weights():
        wgb_ref[...] = wg_ref[...].astype(bf16)
        wub_ref[...] = wu_ref[...].astype(bf16)
        wdb_ref[...] = wd_ref[...].astype(bf16)

    def swiglu(x):
        a = _silu(_dot(x, wgb_ref[...])) * _dot(x, wub_ref[...])
        return _dot(a.astype(bf16), wdb_ref[...])

    @pl.when(valid & (j == 0))
    def _():
        acc_ref[...] = jnp.zeros_like(acc_ref)

    @pl.when(fast)
    def _():
        round_weights()
        quarter = MOE_FAST // 8
        parts = [pl.ds(q * quarter, quarter) for q in range(8)]

        def gate_up(rows):
            x = xbuf_ref[slot, rows, :]
            return _dot(x, wgb_ref[...]), _dot(x, wub_ref[...])

        nxt_gu = gate_up(parts[0])
        for q, rows in enumerate(parts):
            g, u = nxt_gu
            if q + 1 < len(parts):
                nxt_gu = gate_up(parts[q + 1])
            acc_ref[rows, :] += _dot((_silu(g) * u).astype(bf16), wdb_ref[...])

    @pl.when(valid & jnp.logical_not(fast))
    def _():
        round_weights()

    @pl.when(valid)
    def _():
        def body(sb, carry):
            rows = pl.ds(pl.multiple_of(sb * MOE_SUB, MOE_SUB), MOE_SUB)
            acc_ref[rows, :] += swiglu(xbuf_ref[slot, rows, :])
            return carry

        lax.fori_loop(jnp.where(fast, fast_subs, 0), n_sub, body, 0)

    @pl.when(j == last)
    def _():
        @pl.when(valid)
        def _():
            o_ref[...] = acc_ref[...].astype(o_ref.dtype)

        @pl.when(jnp.logical_not(valid))
        def _():
            o_ref[...] = jnp.zeros_like(o_ref)


def _experts(tile_expert, tile_valid, src_of_dst, hs2d, w_gate, w_up, w_down):
    n_tiles = tile_expert.shape[0]
    dff = w_gate.shape[2]
    nff = dff // MOE_TF

    def jj(j, tv, i):
        return jnp.where(tv[i] > 0, j, nff - 1)

    grid_spec = pltpu.PrefetchScalarGridSpec(
        num_scalar_prefetch=3,
        grid=(n_tiles, nff),
        in_specs=[
            pl.BlockSpec(memory_space=pl.ANY),
            pl.BlockSpec((None, D, MOE_TF), lambda i, j, te, tv, sd: (te[i], 0, jj(j, tv, i))),
            pl.BlockSpec((None, D, MOE_TF), lambda i, j, te, tv, sd: (te[i], 0, jj(j, tv, i))),
            pl.BlockSpec((None, MOE_TF, D), lambda i, j, te, tv, sd: (te[i], jj(j, tv, i), 0)),
        ],
        out_specs=pl.BlockSpec((MOE_TM, D), lambda i, j, te, tv, sd: (i, 0)),
        scratch_shapes=[
            pltpu.VMEM((2, MOE_TM, D), bf16),
            pltpu.SemaphoreType.DMA((2,)),
            pltpu.VMEM((MOE_TM, D), f32),
            pltpu.VMEM((D, MOE_TF), bf16),
            pltpu.VMEM((D, MOE_TF), bf16),
            pltpu.VMEM((MOE_TF, D), bf16),
        ],
    )
    return pl.pallas_call(
        _expert_kernel,
        grid_spec=grid_spec,
        out_shape=jax.ShapeDtypeStruct((n_tiles * MOE_TM, D), bf16),
        compiler_params=pltpu.CompilerParams(vmem_limit_bytes=VMEM_LIMIT_BIG),
        name="experts",
    )(tile_expert, tile_valid, src_of_dst, hs2d, w_gate, w_up, w_down)


def _combine_kernel(ds_ref, x_ref, mod_ref, o_hbm_ref, rt_ref, lng_ref, lnb_ref, out_ref, obuf_ref, sem):
    W = ROUTE_W
    c = pl.program_id(0)
    nc = pl.num_programs(0)
    slot = c % 2
    nxt = jnp.minimum(c + 1, nc - 1)

    @pl.when(c == 0)
    def _():
        _gather_start(o_hbm_ref, obuf_ref.at[0], sem.at[0], ds_ref, 0, CHUNK_GRANS)

    _gather_wait(o_hbm_ref, obuf_ref.at[slot], sem.at[slot], ds_ref, c * CHUNK_GRANS, CHUNK_GRANS)

    _gather_start(o_hbm_ref, obuf_ref.at[1 - slot], sem.at[1 - slot], ds_ref, nxt * CHUNK_GRANS, CHUNK_GRANS)

    osv = obuf_ref[slot]
    scol = lax.broadcasted_iota(jnp.int32, (W // 2, CHUNK_SLOTS), 1)
    for r in range(2):
        rr = slice(r * (W // 2), (r + 1) * (W // 2))
        rt = rt_ref[rr, :]
        p1 = (scol == rt[:, 0:1].astype(jnp.int32)).astype(f32).astype(bf16)
        p2 = (scol == rt[:, 1:2].astype(jnp.int32)).astype(f32).astype(bf16)
        y = rt[:, 2:3] * _dot(p1, osv) + rt[:, 3:4] * _dot(p2, osv)
        xr = ALPHA * x_ref[rr, :] + (1.0 + mod_ref[5:6, :]) * y
        out_ref[rr, :] = _layer_norm(xr, lng_ref[...], lnb_ref[...])

    @pl.when(c == nc - 1)
    def _():
        _gather_wait(o_hbm_ref, obuf_ref.at[1 - slot], sem.at[1 - slot], ds_ref, nxt * CHUNK_GRANS, CHUNK_GRANS)


def _combine(dst_of_src, x2d, mod_l, o2d, route_t, ln_g, ln_b, S):
    T = x2d.shape[0]
    W = ROUTE_W
    tpb = S // W
    grid_spec = pltpu.PrefetchScalarGridSpec(
        num_scalar_prefetch=1,
        grid=(T // W,),
        in_specs=[
            pl.BlockSpec((W, D), lambda c, ds: (c, 0)),
            pl.BlockSpec((None, 6, D), lambda c, ds: (c // tpb, 0, 0)),
            pl.BlockSpec(memory_space=pl.ANY),
            pl.BlockSpec((None, W, 8), lambda c, ds: (c, 0, 0)),
            pl.BlockSpec((1, D), lambda c, ds: (0, 0)),
            pl.BlockSpec((1, D), lambda c, ds: (0, 0)),
        ],
        out_specs=pl.BlockSpec((W, D), lambda c, ds: (c, 0)),
        scratch_shapes=[
            pltpu.VMEM((2, CHUNK_SLOTS, D), bf16),
            pltpu.SemaphoreType.DMA((2,)),
        ],
    )
    return pl.pallas_call(
        _combine_kernel,
        grid_spec=grid_spec,
        out_shape=jax.ShapeDtypeStruct((T, D), f32),
        compiler_params=pltpu.CompilerParams(vmem_limit_bytes=VMEM_LIMIT),
        name="combine",
    )(dst_of_src, x2d, mod_l, o2d, route_t, ln_g, ln_b)


def _routing_tables(counts, n_tiles):
    nc = counts.shape[0]
    gr = (counts + GRAN - 1) // GRAN
    seg_start = jnp.cumsum(gr, axis=1) - gr
    chunk_total = jnp.sum(gr, axis=1)
    prefix = jnp.cumsum(gr, axis=0) - gr
    g_e = jnp.sum(gr, axis=0)
    tiles_e = (g_e + TILE_GRANS - 1) // TILE_GRANS
    tile_end = jnp.cumsum(tiles_e)
    tile_start = tile_end - tiles_e
    total_tiles = tile_end[-1]

    i32 = jnp.int32
    er = jnp.arange(N_EXPERTS, dtype=i32)
    t = jnp.arange(n_tiles, dtype=i32)
    te = jnp.sum((t[:, None] >= tile_end[None, :]).astype(i32), axis=1)
    tile_valid = (t < total_tiles).astype(i32)
    last_e = jnp.sum((total_tiles - 1 >= tile_end).astype(i32))
    tile_expert = jnp.where(tile_valid > 0, jnp.minimum(te, N_EXPERTS - 1), last_e).astype(i32)
    oh_t = (tile_expert[:, None] == er).astype(i32)
    grans_left = jnp.sum(oh_t * (g_e - (t[:, None] - tile_start[None, :]) * TILE_GRANS), axis=1)
    tile_subs = tile_valid * jnp.clip((grans_left + SUB_GRANS - 1) // SUB_GRANS, 0, TILE_GRANS // SUB_GRANS)

    k = jnp.arange(CHUNK_GRANS, dtype=i32)
    seg_end = seg_start + gr
    e_of = jnp.sum((k[None, :, None] >= seg_end[:, None, :]).astype(i32), axis=2)
    oh_e = (jnp.minimum(e_of, N_EXPERTS - 1)[:, :, None] == er).astype(i32)
    base = tile_start[None, :] * TILE_GRANS + prefix - seg_start
    dst = jnp.sum(oh_e * base[:, None, :], axis=2) + k[None, :]
    valid_src = k[None, :] < chunk_total[:, None]
    dst_of_src = jnp.where(valid_src, dst, 0).astype(i32).reshape(-1)

    d = jnp.arange(n_tiles * TILE_GRANS, dtype=i32)
    oh_d = (jnp.repeat(tile_expert, TILE_GRANS)[:, None] == er).astype(i32)
    q = d - jnp.sum(oh_d * tile_start[None, :], axis=1) * TILE_GRANS
    incl_d = jnp.sum(oh_d[:, :, None] * (prefix + gr).T[None], axis=1)
    c_d = jnp.sum((q[:, None] >= incl_d).astype(i32), axis=1)
    oh_c = (jnp.minimum(c_d, nc - 1)[:, None] == jnp.arange(nc, dtype=i32)).astype(i32)
    cbase = jnp.arange(nc, dtype=i32)[:, None] * CHUNK_GRANS + seg_start - prefix
    sel = jnp.sum(oh_c[:, :, None] * oh_d[:, None, :] * cbase[None], axis=(1, 2))
    valid_dst = (jnp.repeat(tile_valid, TILE_GRANS) > 0) & (q >= 0) & (q < jnp.sum(oh_d * g_e[None, :], axis=1))
    src_of_dst = jnp.where(valid_dst, sel + q, 0).astype(i32)
    return tile_expert, tile_subs.astype(i32), src_of_dst, dst_of_src


def kernel(x, c, ada_w, ada_b, ln_g, ln_b, ev_w_in, ev_pool_w, ev_pool_scale, ev_sink, ev_w_out, od_w_in, od_sg_ln_g, od_sg_ln_b, od_w_s, od_b_s, od_w_out, ffn_w_gate, ffn_w_up, ffn_w_down, moe_w_router, moe_w_gate, moe_w_up, moe_w_down):
    B, S, _ = x.shape
    T = B * S
    x2d = x.reshape(T, D)
    mod = _adaln(c, ada_w, ada_b)

    w_in = ev_w_in[0].astype(bf16)
    q0, k0, v0 = POOL_WIDTH, POOL_WIDTH + ATTN_WIDTH, POOL_WIDTH + ATTN_WIDTH + KV_WIDTH
    w_pk = jnp.concatenate([w_in[:, :q0], w_in[:, k0:v0]], axis=1)
    w_qv_t = jnp.concatenate([w_in[:, q0:k0], w_in[:, v0:]], axis=1).T
    p, k, qt, vt = _ev_in(x2d, mod[0], w_pk, w_qv_t, S)
    x2d = _ev_mix(x2d, mod[0], p, qt, k, vt, ev_pool_w[0].astype(bf16), ev_pool_scale[0][None, :],
                  ev_sink[0], ev_w_out[0].astype(bf16), ln_g[0, 0][None, :], ln_b[0, 0][None, :], B, S)
    x2d = _ffn(x2d, mod[0], ffn_w_gate[0], ffn_w_up[0], ffn_w_down[0],
               ln_g[0, 1][None, :], ln_b[0, 1][None, :], S)

    x2d = _sg_mix(x2d, mod[1], od_w_in[0].astype(bf16), od_sg_ln_g[0][None, :], od_sg_ln_b[0][None, :],
                  od_w_s[0].astype(bf16), od_b_s[0].T, od_w_out[0].astype(bf16),
                  ln_g[1, 0][None, :], ln_b[1, 0][None, :], S)

    hs, route, cnt = _route(x2d, mod[1], moe_w_router[0].T, S)
    nc = T // ROUTE_W
    n_tiles = (nc * CHUNK_GRANS) // TILE_GRANS + N_EXPERTS
    tile_expert, tile_subs, src_of_dst, dst_of_src = _routing_tables(cnt[:, :, 0], n_tiles)
    o = _experts(tile_expert, tile_subs, src_of_dst, hs.reshape(nc * CHUNK_SLOTS, D),
                 moe_w_gate[0], moe_w_up[0], moe_w_down[0])
    x2d = _combine(dst_of_src, x2d, mod[1], o, jnp.swapaxes(route, 1, 2),
                   ln_g[1, 1][None, :], ln_b[1, 1][None, :], S)
    return x2d.reshape(B, S, D)
```

```python
import functools
import math

import jax
import jax.numpy as jnp
import numpy as np
from jax import lax
from jax.experimental import pallas as pl
from jax.experimental.pallas import tpu as pltpu

D = 1024
DEPTH = 2
ALPHA = (2.0 * DEPTH) ** 0.25
LN_EPS = 1e-5

POOL_WINDOWS = (2, 4, 8, 16)
POOL_CH = 128
POOL_WIDTH = 512
HEAD_DIM = 64
N_Q_HEADS = 8
N_KV_HEADS = 2
Q_GROUP = 4
ATTN_WIDTH = 512
KV_WIDTH = 128
BLOCK = 128
POOL_HALO = 8

CHUNK = 128
SG_GROUPS = 8
SG_CH = 128

N_EXPERTS = 8

ROUTE_W = 512
ROUTE_PER_STEP = 2
GRAN = 16
CHUNK_SLOTS = 2 * ROUTE_W + N_EXPERTS * GRAN
CHUNK_GRANS = CHUNK_SLOTS // GRAN
MOE_SUB = 256
MOE_FAST = 2048
MOE_TM = MOE_FAST + MOE_SUB
TILE_GRANS = MOE_TM // GRAN
SUB_GRANS = MOE_SUB // GRAN
MOE_TF = 512

VMEM_LIMIT = 48 * 1024 * 1024
VMEM_LIMIT_BIG = 56 * 1024 * 1024

bf16 = jnp.bfloat16
f32 = jnp.float32


def _dot(a, b):
    return jnp.dot(a, b, preferred_element_type=f32)


def _split_bf16(a):
    hi = a.astype(bf16)
    lo = (a - hi.astype(f32)).astype(bf16)
    return hi, lo


def _layer_norm(x, g, b):
    mu = jnp.mean(x, axis=-1, keepdims=True)
    xc = x - mu
    var = jnp.mean(xc * xc, axis=-1, keepdims=True)
    return xc * lax.rsqrt(var + LN_EPS) * g + b


def _silu(x):
    return x * jax.nn.sigmoid(x)


def _gelu_tanh(x):
    c = math.sqrt(2.0 / math.pi)
    return x * (0.5 * (1.0 + jnp.tanh(c * (x + 0.044715 * (x * x * x)))))


def _adaln_kernel(c_ref, w_ref, b_ref, o_ref):
    cond = _silu(c_ref[...])
    c_hi, c_lo = _split_bf16(cond)
    w_hi, w_lo = _split_bf16(w_ref[...])
    acc = _dot(c_hi, w_hi) + (_dot(c_lo, w_hi) + _dot(c_hi, w_lo))
    o_ref[...] = acc + b_ref[...]


def _adaln(c, ada_w, ada_b):
    B = c.shape[0]
    tn = 2048
    c_pad = jnp.zeros((8, D), f32).at[:B].set(c)
    out = pl.pallas_call(
        _adaln_kernel,
        grid=(DEPTH, 6 * D // tn),
        in_specs=[
            pl.BlockSpec((8, D), lambda l, j: (0, 0)),
            pl.BlockSpec((None, D, tn), lambda l, j: (l, 0, j)),
            pl.BlockSpec((None, 1, tn), lambda l, j: (l, 0, j)),
        ],
        out_specs=pl.BlockSpec((None, 8, tn), lambda l, j: (l, 0, j)),
        out_shape=jax.ShapeDtypeStruct((DEPTH, 8, 6 * D), f32),
        compiler_params=pltpu.CompilerParams(vmem_limit_bytes=VMEM_LIMIT),
        name="adaln",
    )(c_pad, ada_w, ada_b.reshape(DEPTH, 1, 6 * D))
    return out[:, :B].reshape(DEPTH, B, 6, D)


def _ev_in_kernel(x_ref, mod_ref, wpk_ref, wqvt_ref, p_ref, k_ref, qt_ref, vt_ref):
    h = (x_ref[...] * (1.0 + mod_ref[1:2, :]) + mod_ref[0:1, :]).astype(bf16)
    zpk = _dot(h, wpk_ref[...])
    p_ref[...] = zpk[:, :POOL_WIDTH]
    k_ref[...] = zpk[:, POOL_WIDTH:].astype(bf16)
    zt = lax.dot_general(wqvt_ref[...], h, (((1,), (1,)), ((), ())), preferred_element_type=f32)
    qt_ref[...] = (zt[:ATTN_WIDTH] * (HEAD_DIM ** -0.5)).astype(bf16)
    vt_ref[...] = zt[ATTN_WIDTH:].astype(bf16)


def _ev_in(x2d, mod_l, w_pk, w_qv_t, S):
    T = x2d.shape[0]
    tm = 512
    tpb = S // tm
    return pl.pallas_call(
        _ev_in_kernel,
        grid=(T // tm,),
        in_specs=[
            pl.BlockSpec((tm, D), lambda i: (i, 0)),
            pl.BlockSpec((None, 6, D), lambda i: (i // tpb, 0, 0)),
            pl.BlockSpec((D, POOL_WIDTH + KV_WIDTH), lambda i: (0, 0)),
            pl.BlockSpec((ATTN_WIDTH + KV_WIDTH, D), lambda i: (0, 0)),
        ],
        out_specs=[
            pl.BlockSpec((tm, POOL_WIDTH), lambda i: (i, 0)),
            pl.BlockSpec((tm, KV_WIDTH), lambda i: (i, 0)),
            pl.BlockSpec((ATTN_WIDTH, tm), lambda i: (0, i)),
            pl.BlockSpec((KV_WIDTH, tm), lambda i: (0, i)),
        ],
        out_shape=[
            jax.ShapeDtypeStruct((T, POOL_WIDTH), f32),
            jax.ShapeDtypeStruct((T, KV_WIDTH), bf16),
            jax.ShapeDtypeStruct((ATTN_WIDTH, T), bf16),
            jax.ShapeDtypeStruct((KV_WIDTH, T), bf16),
        ],
        compiler_params=pltpu.CompilerParams(vmem_limit_bytes=VMEM_LIMIT),
        name="ev_in",
    )(x2d, mod_l, w_pk, w_qv_t)


def _ev_mix_kernel(x_ref, mod_ref, p_ref, pp_ref, pn_ref, qt_ref,
                   k_ref, kp_ref, kn_ref, vt_ref, vtp_ref, vtn_ref,
                   bias_ref, sink_ref, wpool_ref, pscale_ref, wout_hbm_ref, lng_ref, lnb_ref,
                   o_ref, pext_ref, lvl_ref, pooled_ref, kext_ref, vext_ref, ybt_ref, mix_ref,
                   wout_ref, stage_ref, sem, *, S, tq):
    i = pl.program_id(1)

    @pl.when((pl.program_id(0) == 0) & (i == 0))
    def _():
        _load_rounded(wout_hbm_ref, wout_ref, stage_ref, sem)

    n_tiles = S // tq
    is_first = i == 0
    is_last = i == n_tiles - 1
    H = POOL_HALO

    p = p_ref[...]
    pext_ref[0:H, :] = jnp.where(is_first, 0.0, pp_ref[...])
    pext_ref[H:H + tq, :] = p
    pext_ref[H + tq:2 * H + tq, :] = jnp.where(is_last, 0.0, pn_ref[...])
    pext_ref[2 * H + tq:, :] = jnp.zeros((pext_ref.shape[0] - 2 * H - tq, POOL_WIDTH), f32)
    near = lax.broadcasted_iota(jnp.int32, (H, 1), 0)

    def src_rows(src, k, off, n, cs):
        return src[off:off + n, cs] if src is pext_ref else src[k - 1, off:off + n, cs]

    def pool_group(g):
        w = POOL_WINDOWS[g]
        cs = slice(g * POOL_CH, (g + 1) * POOL_CH)
        r = w // 2
        src, length, k = pext_ref, pext_ref.shape[0] - H, 0
        while 2 ** k < r:
            step = 2 ** k
            lvl_ref[k, 0:length, cs] = src_rows(src, k, 0, length, cs) + src_rows(src, k, step, length, cs)
            src, length, k = lvl_ref, length - H, k + 1
        half_run = 2 ** k
        wsum = (src_rows(src, k, H - r, tq, cs) + src_rows(src, k, H - r + half_run, tq, cs)
                + pext_ref[H + r:H + r + tq, cs])
        pooled_ref[:, cs] = wsum / float(w + 1) - p[:, cs]
        cnt_head = (jnp.minimum(near, r) + (r + 1)).astype(f32)
        cnt_tail = (jnp.minimum(H - 1 - near, r) + (r + 1)).astype(f32)
        cnt_head = jnp.where(is_first, cnt_head, float(w + 1))
        cnt_tail = jnp.where(is_last, cnt_tail, float(w + 1))
        pooled_ref[0:H, cs] = wsum[0:H] / cnt_head - p[0:H, cs]
        pooled_ref[tq - H:tq, cs] = wsum[tq - H:tq] / cnt_tail - p[tq - H:tq, cs]
        ya = _dot(pooled_ref[:, cs].astype(bf16), wpool_ref[g])
        mix_ref[:, cs] = (ya * pscale_ref[:, cs]).astype(bf16)

    for g in range(len(POOL_WINDOWS)):
        pool_group(g)

    kext_ref[0:BLOCK, :] = kp_ref[...]
    kext_ref[BLOCK:BLOCK + tq, :] = k_ref[...]
    kext_ref[BLOCK + tq:, :] = kn_ref[...]
    vext_ref[:, 0:BLOCK] = vtp_ref[...]
    vext_ref[:, BLOCK:BLOCK + tq] = vt_ref[...]
    vext_ref[:, BLOCK + tq:] = vtn_ref[...]

    n_blocks = S // BLOCK
    zeros_q = jnp.zeros((HEAD_DIM, Q_GROUP * BLOCK), bf16)
    items = [(n, kvh) for n in range(tq // BLOCK) for kvh in range(N_KV_HEADS)]

    def scores(n, kvh):
        gb = i * (tq // BLOCK) + n
        variant = jnp.where(gb == 0, 1, jnp.where(gb == n_blocks - 1, 2, 0))
        cols = slice(n * BLOCK, (n + 1) * BLOCK)
        kw = kext_ref[n * BLOCK:n * BLOCK + 3 * BLOCK, :]
        qst = jnp.concatenate(
            [qt_ref[(kvh * Q_GROUP + gq) * HEAD_DIM:(kvh * Q_GROUP + gq + 1) * HEAD_DIM, cols]
             for gq in range(Q_GROUP)], axis=1)
        qst = jnp.concatenate([qst, zeros_q] if kvh == 0 else [zeros_q, qst], axis=0)
        return _dot(kw, qst) + bias_ref[variant, kvh]

    def softmax(n, kvh, s):
        sink = sink_ref[kvh]
        m = jnp.maximum(jnp.max(s, axis=0, keepdims=True), sink)
        e = jnp.exp(s - m)
        denom = jnp.sum(e, axis=0, keepdims=True) + jnp.exp(sink - m)
        return e.astype(bf16), denom

    def weighted_values(n, kvh, e, denom):
        cols = slice(n * BLOCK, (n + 1) * BLOCK)
        vwt = vext_ref[kvh * HEAD_DIM:(kvh + 1) * HEAD_DIM, n * BLOCK:n * BLOCK + 3 * BLOCK]
        out = _dot(vwt, e) / denom
        for gq in range(Q_GROUP):
            hq = kvh * Q_GROUP + gq
            ybt_ref[hq * HEAD_DIM:(hq + 1) * HEAD_DIM, cols] = out[:, gq * BLOCK:(gq + 1) * BLOCK]

    s_next = scores(*items[0])
    prev = None
    for idx, item in enumerate(items):
        s_cur = s_next
        if idx + 1 < len(items):
            s_next = scores(*items[idx + 1])
        cur = softmax(*item, s_cur)
        if prev is not None:
            weighted_values(*items[idx - 1], *prev)
        prev = cur
    weighted_values(*items[-1], *prev)
    mix_ref[:, POOL_WIDTH:] = ybt_ref[...].T.astype(bf16)

    halves = [slice(r * (tq // 2), (r + 1) * (tq // 2)) for r in range(2)]
    ys = [_dot(mix_ref[rr, :], wout_ref[...]) for rr in halves]
    for rr, y in zip(halves, ys):
        xr = ALPHA * x_ref[rr, :] + (1.0 + mod_ref[2:3, :]) * y
        o_ref[rr, :] = _layer_norm(xr, lng_ref[...], lnb_ref[...])


def _ev_mix(x2d, mod_l, p, qt, k, vt, w_pool, pool_scale, sink, w_out, ln_g, ln_b, B, S):
    T = x2d.shape[0]
    tq = 512
    nt = S // tq
    kb = tq // BLOCK
    pb = tq // POOL_HALO
    n_kblocks = T // BLOCK
    n_pblocks = T // POOL_HALO

    def main(b, i): return (b * nt + i, 0)
    def kprev(b, i): return (jnp.maximum((b * nt + i) * kb - 1, 0), 0)
    def knext(b, i): return (jnp.minimum((b * nt + i + 1) * kb, n_kblocks - 1), 0)
    def pprev(b, i): return (jnp.maximum((b * nt + i) * pb - 1, 0), 0)
    def pnext(b, i): return (jnp.minimum((b * nt + i + 1) * pb, n_pblocks - 1), 0)
    def const2(b, i): return (0, 0)

    def tmain(b, i): return (0, b * nt + i)
    def tprev(b, i): return (0, jnp.maximum((b * nt + i) * kb - 1, 0))
    def tnext(b, i): return (0, jnp.minimum((b * nt + i + 1) * kb, n_kblocks - 1))

    assert S // BLOCK >= 2
    kj = np.arange(3 * BLOCK)[:, None]
    qi = np.arange(BLOCK)[None, :]
    dist = np.abs(kj - BLOCK - qi)
    slopes = np.float32(2.0) ** (np.float32(-8.0) * np.arange(1, N_Q_HEADS + 1, dtype=np.float32) / N_Q_HEADS)
    alibi = -slopes[:, None, None] * dist.astype(np.float32)[None]
    in_window = dist <= BLOCK
    key_ok = np.stack([kj >= 0, kj >= BLOCK, kj < 2 * BLOCK])
    bias = np.where((in_window[None] & key_ok)[:, None], alibi[None], np.float32(-1e30))
    bias = bias.reshape(3, N_KV_HEADS, Q_GROUP, 3 * BLOCK, BLOCK).transpose(0, 1, 3, 2, 4)
    bias = jnp.asarray(bias.reshape(3, N_KV_HEADS, 3 * BLOCK, Q_GROUP * BLOCK), f32)
    sink_row = jnp.repeat(sink.astype(f32).reshape(N_KV_HEADS, Q_GROUP), BLOCK, axis=1)[:, None, :]

    assert tq >= 2 * POOL_HALO and max(POOL_WINDOWS) // 2 <= POOL_HALO and nt >= 2

    kernel = functools.partial(_ev_mix_kernel, S=S, tq=tq)
    return pl.pallas_call(
        kernel,
        grid=(B, nt),
        in_specs=[
            pl.BlockSpec((tq, D), main),
            pl.BlockSpec((None, 6, D), lambda b, i: (b, 0, 0)),
            pl.BlockSpec((tq, POOL_WIDTH), main),
            pl.BlockSpec((POOL_HALO, POOL_WIDTH), pprev),
            pl.BlockSpec((POOL_HALO, POOL_WIDTH), pnext),
            pl.BlockSpec((ATTN_WIDTH, tq), tmain),
            pl.BlockSpec((tq, KV_WIDTH), main),
            pl.BlockSpec((BLOCK, KV_WIDTH), kprev),
            pl.BlockSpec((BLOCK, KV_WIDTH), knext),
            pl.BlockSpec((KV_WIDTH, tq), tmain),
            pl.BlockSpec((KV_WIDTH, BLOCK), tprev),
            pl.BlockSpec((KV_WIDTH, BLOCK), tnext),
            pl.BlockSpec((3, N_KV_HEADS, 3 * BLOCK, Q_GROUP * BLOCK), lambda b, i: (0, 0, 0, 0)),
            pl.BlockSpec((N_KV_HEADS, 1, Q_GROUP * BLOCK), lambda b, i: (0, 0, 0)),
            pl.BlockSpec((len(POOL_WINDOWS), POOL_CH, POOL_CH), lambda b, i: (0, 0, 0)),
            pl.BlockSpec((1, POOL_WIDTH), const2),
            pl.BlockSpec(memory_space=pl.ANY),
            pl.BlockSpec((1, D), const2),
            pl.BlockSpec((1, D), const2),
        ],
        out_specs=pl.BlockSpec((tq, D), main),
        out_shape=jax.ShapeDtypeStruct((T, D), f32),
        scratch_shapes=[
            pltpu.VMEM((tq + 4 * POOL_HALO, POOL_WIDTH), f32),
            pltpu.VMEM((3, tq + 3 * POOL_HALO, POOL_WIDTH), f32),
            pltpu.VMEM((tq, POOL_WIDTH), f32),
            pltpu.VMEM((tq + 2 * BLOCK, KV_WIDTH), bf16),
            pltpu.VMEM((KV_WIDTH, tq + 2 * BLOCK), bf16),
            pltpu.VMEM((ATTN_WIDTH, tq), f32),
            pltpu.VMEM((tq, D), bf16),
            pltpu.VMEM((D, D), bf16),
            pltpu.VMEM((2, D // 8, D), f32),
            pltpu.SemaphoreType.DMA((2,)),
        ],
        compiler_params=pltpu.CompilerParams(vmem_limit_bytes=VMEM_LIMIT),
        name="ev_mix",
    )(x2d, mod_l, p, p, p, qt, k, k, k, vt, vt, vt, bias, sink_row, w_pool, pool_scale, w_out, ln_g, ln_b)


def _load_rounded(w_hbm_ref, w_ref, stage_ref, sem):
    rows = stage_ref.shape[1]
    n = w_hbm_ref.shape[0] // rows

    def copy(c):
        return pltpu.make_async_copy(w_hbm_ref.at[pl.ds(c * rows, rows), :], stage_ref.at[c % 2], sem.at[c % 2])

    copy(0).start()
    for c in range(n):
        if c + 1 < n:
            copy(c + 1).start()
        copy(c).wait()
        w_ref[pl.ds(c * rows, rows), :] = stage_ref[c % 2].astype(bf16)


def _ffn_kernel(x_ref, mod_ref, wg_hbm_ref, wu_hbm_ref, wd_hbm_ref, lng_ref, lnb_ref, o_ref,
                wg_ref, wu_ref, wd_ref, stage_in_ref, stage_out_ref, sem):
    @pl.when(pl.program_id(0) == 0)
    def _():
        _load_rounded(wg_hbm_ref, wg_ref, stage_in_ref, sem)
        _load_rounded(wu_hbm_ref, wu_ref, stage_in_ref, sem)
        _load_rounded(wd_hbm_ref, wd_ref, stage_out_ref, sem)

    n_pieces = 2
    piece = x_ref.shape[0] // n_pieces
    parts = [pl.ds(q * piece, piece) for q in range(n_pieces)]

    def gate_up(rows):
        x = x_ref[rows, :]
        h = (x * (1.0 + mod_ref[4:5, :]) + mod_ref[3:4, :]).astype(bf16)
        return x, _dot(h, wg_ref[...]), _dot(h, wu_ref[...])

    nxt = gate_up(parts[0])
    for q, rows in enumerate(parts):
        x, g, u = nxt
        if q + 1 < n_pieces:
            nxt = gate_up(parts[q + 1])
        y = _dot((_silu(g) * u).astype(bf16), wd_ref[...])
        xr = ALPHA * x + (1.0 + mod_ref[5:6, :]) * y
        o_ref[rows, :] = _layer_norm(xr, lng_ref[...], lnb_ref[...])


def _ffn(x2d, mod_l, w_gate, w_up, w_down, ln_g, ln_b, S):
    T = x2d.shape[0]
    tm = 512
    dff = w_gate.shape[1]
    tpb = S // tm
    n_stage = 8
    return pl.pallas_call(
        _ffn_kernel,
        grid=(T // tm,),
        in_specs=[
            pl.BlockSpec((tm, D), lambda i: (i, 0)),
            pl.BlockSpec((None, 6, D), lambda i: (i // tpb, 0, 0)),
            pl.BlockSpec(memory_space=pl.ANY),
            pl.BlockSpec(memory_space=pl.ANY),
            pl.BlockSpec(memory_space=pl.ANY),
            pl.BlockSpec((1, D), lambda i: (0, 0)),
            pl.BlockSpec((1, D), lambda i: (0, 0)),
        ],
        out_specs=pl.BlockSpec((tm, D), lambda i: (i, 0)),
        out_shape=jax.ShapeDtypeStruct((T, D), f32),
        scratch_shapes=[
            pltpu.VMEM((D, dff), bf16),
            pltpu.VMEM((D, dff), bf16),
            pltpu.VMEM((dff, D), bf16),
            pltpu.VMEM((2, D // n_stage, dff), f32),
            pltpu.VMEM((2, dff // n_stage, D), f32),
            pltpu.SemaphoreType.DMA((2,)),
        ],
        compiler_params=pltpu.CompilerParams(vmem_limit_bytes=VMEM_LIMIT_BIG),
        name="ffn",
    )(x2d, mod_l, w_gate, w_up, w_down, ln_g, ln_b)


def _sg_kernel(x_ref, mod_ref, win_hbm_ref, sgg_ref, sgb_ref, ws_ref, bst_ref, wout_hbm_ref, lng_ref, lnb_ref,
               o_ref, gate_ref, win_ref, wout_ref, stage_in_ref, stage_out_ref, sem, *, tm):
    @pl.when(pl.program_id(0) == 0)
    def _():
        _load_rounded(win_hbm_ref, win_ref, stage_in_ref, sem)
        _load_rounded(wout_hbm_ref, wout_ref, stage_out_ref, sem)

    n_pieces = 2
    piece = tm // n_pieces

    def project(q):
        x = x_ref[q * piece:(q + 1) * piece, :]
        h = (x * (1.0 + mod_ref[1:2, :]) + mod_ref[0:1, :]).astype(bf16)
        return x, _dot(h, win_ref[:, D:]), _dot(h, win_ref[:, :D])

    nxt = project(0)
    for q in range(n_pieces):
        x, zv, zu = nxt
        if q + 1 < n_pieces:
            nxt = project(q + 1)
        v = _layer_norm(_gelu_tanh(zv), sgg_ref[...], sgb_ref[...]).astype(bf16)
        u = _gelu_tanh(zu)
        for n in range(piece // CHUNK):
            rows = slice(n * CHUNK, (n + 1) * CHUNK)
            grows = slice(q * piece + n * CHUNK, q * piece + (n + 1) * CHUNK)
            for g in range(SG_GROUPS):
                cols = slice(g * SG_CH, (g + 1) * SG_CH)
                sv = _dot(ws_ref[g], v[rows, cols]) + bst_ref[:, g:g + 1]
                gate_ref[grows, cols] = (u[rows, cols] * sv).astype(bf16)
        y = _dot(gate_ref[q * piece:(q + 1) * piece, :], wout_ref[...])
        xr = ALPHA * x + (1.0 + mod_ref[2:3, :]) * y
        o_ref[q * piece:(q + 1) * piece, :] = _layer_norm(xr, lng_ref[...], lnb_ref[...])


def _sg_mix(x2d, mod_l, w_in, sg_g, sg_b, w_s, b_s_t, w_out, ln_g, ln_b, S):
    T = x2d.shape[0]
    tm = 512
    tpb = S // tm
    c2 = lambda i: (0, 0)
    n_stage = 8
    return pl.pallas_call(
        functools.partial(_sg_kernel, tm=tm),
        grid=(T // tm,),
        in_specs=[
            pl.BlockSpec((tm, D), lambda i: (i, 0)),
            pl.BlockSpec((None, 6, D), lambda i: (i // tpb, 0, 0)),
            pl.BlockSpec(memory_space=pl.ANY),
            pl.BlockSpec((1, D), c2),
            pl.BlockSpec((1, D), c2),
            pl.BlockSpec((SG_GROUPS, CHUNK, CHUNK), lambda i: (0, 0, 0)),
            pl.BlockSpec((CHUNK, SG_GROUPS), c2),
            pl.BlockSpec(memory_space=pl.ANY),
            pl.BlockSpec((1, D), c2),
            pl.BlockSpec((1, D), c2),
        ],
        out_specs=pl.BlockSpec((tm, D), lambda i: (i, 0)),
        out_shape=jax.ShapeDtypeStruct((T, D), f32),
        scratch_shapes=[
            pltpu.VMEM((tm, D), bf16),
            pltpu.VMEM((D, 2 * D), bf16),
            pltpu.VMEM((D, D), bf16),
            pltpu.VMEM((2, D // n_stage, 2 * D), f32),
            pltpu.VMEM((2, D // n_stage, D), f32),
            pltpu.SemaphoreType.DMA((2,)),
        ],
        compiler_params=pltpu.CompilerParams(vmem_limit_bytes=VMEM_LIMIT),
        name="sg_mix",
    )(x2d, mod_l, w_in, sg_g, sg_b, w_s, b_s_t, w_out, ln_g, ln_b)


def _route_kernel(x_ref, mod_ref, wrt_ref, hs_ref, route_ref, cnt_ref):
    W = ROUTE_W
    w_hi, w_lo = _split_bf16(wrt_ref[...])
    nt = (((1,), (1,)), ((), ()))
    eidx = lax.broadcasted_iota(jnp.int32, (N_EXPERTS, W), 0)
    sub = lax.broadcasted_iota(jnp.int32, (N_EXPERTS, 1), 0)
    tr = lax.broadcasted_iota(jnp.int32, (W, W), 0)
    tc = lax.broadcasted_iota(jnp.int32, (W, W), 1)
    upper = (tr < tc).astype(bf16)
    srow = lax.broadcasted_iota(jnp.int32, (CHUNK_SLOTS, W), 0)
    ridx = lax.broadcasted_iota(jnp.int32, (8, W), 0)

    def assign_slots(k):
        h = x_ref[k * W:(k + 1) * W, :] * (1.0 + mod_ref[4:5, :]) + mod_ref[3:4, :]
        h_hi, h_lo = _split_bf16(h)
        logits = (lax.dot_general(w_hi, h_hi, nt, preferred_element_type=f32)
                  + (lax.dot_general(w_hi, h_lo, nt, preferred_element_type=f32)
                     + lax.dot_general(w_lo, h_hi, nt, preferred_element_type=f32)))
        m1 = jnp.max(logits, axis=0, keepdims=True)
        i1 = jnp.min(jnp.where(logits == m1, eidx, N_EXPERTS), axis=0, keepdims=True)
        sel1 = eidx == i1
        rest = jnp.where(sel1, -jnp.inf, logits)
        m2 = jnp.max(rest, axis=0, keepdims=True)
        i2 = jnp.min(jnp.where(rest == m2, eidx, N_EXPERTS), axis=0, keepdims=True)
        sel2 = eidx == i2
        e2 = jnp.exp(m2 - m1)
        g1 = 1.0 / (1.0 + e2)
        g2 = e2 / (1.0 + e2)

        a1 = sel1.astype(f32)
        a2 = sel2.astype(f32)
        assign = a1 + a2
        counts = jnp.sum(assign, axis=1, keepdims=True)
        grans = jnp.ceil(counts * (1.0 / GRAN))
        seg = jnp.zeros((N_EXPERTS, 1), f32)
        for e in range(N_EXPERTS - 1):
            seg = seg + jnp.where(sub > e, grans[e:e + 1, :] * GRAN, 0.0)
        rank = _dot(assign.astype(bf16), upper)
        slot = seg + rank
        pos1 = jnp.sum(a1 * slot, axis=0, keepdims=True)
        pos2 = jnp.sum(a2 * slot, axis=0, keepdims=True)
        route_ref[k] = jnp.where(ridx == 0, pos1, jnp.where(ridx == 1, pos2,
                                 jnp.where(ridx == 2, g1, jnp.where(ridx == 3, g2, 0.0))))
        cnt_ref[k] = jnp.broadcast_to(counts, (N_EXPERTS, 128)).astype(jnp.int32)
        return h_hi, pos1, pos2

    def sort_rows(k, h_hi, pos1, pos2):
        perm = ((srow == pos1.astype(jnp.int32)) | (srow == pos2.astype(jnp.int32)))
        hs_ref[k] = _dot(perm.astype(f32).astype(bf16), h_hi).astype(bf16)

    routed = [assign_slots(k) for k in range(ROUTE_PER_STEP)]
    for k in range(ROUTE_PER_STEP):
        sort_rows(k, *routed[k])


def _route(x2d, mod_l, w_router_t, S):
    T = x2d.shape[0]
    W = ROUTE_W
    R = ROUTE_PER_STEP
    nc = T // W
    tpb = S // (R * W)
    return pl.pallas_call(
        _route_kernel,
        grid=(nc // R,),
        in_specs=[
            pl.BlockSpec((R * W, D), lambda c: (c, 0)),
            pl.BlockSpec((None, 6, D), lambda c: (c // tpb, 0, 0)),
            pl.BlockSpec((N_EXPERTS, D), lambda c: (0, 0)),
        ],
        out_specs=[
            pl.BlockSpec((R, CHUNK_SLOTS, D), lambda c: (c, 0, 0)),
            pl.BlockSpec((R, 8, W), lambda c: (c, 0, 0)),
            pl.BlockSpec((R, N_EXPERTS, 128), lambda c: (c, 0, 0)),
        ],
        out_shape=[
            jax.ShapeDtypeStruct((nc, CHUNK_SLOTS, D), bf16),
            jax.ShapeDtypeStruct((nc, 8, W), f32),
            jax.ShapeDtypeStruct((nc, N_EXPERTS, 128), jnp.int32),
        ],
        compiler_params=pltpu.CompilerParams(vmem_limit_bytes=VMEM_LIMIT),
        name="route",
    )(x2d, mod_l, w_router_t)


def _granule_copy(src_ref, buf_ref, sem, idx_ref, base, g):
    row = pl.multiple_of(idx_ref[base + g] * GRAN, GRAN)
    return pltpu.make_async_copy(src_ref.at[pl.ds(row, GRAN), :], buf_ref.at[pl.ds(g * GRAN, GRAN), :], sem)


def _gather_start(src_ref, buf_ref, sem, idx_ref, base, n):
    for g in range(n):
        _granule_copy(src_ref, buf_ref, sem, idx_ref, base, g).start()


def _gather_wait(src_ref, buf_ref, sem, idx_ref, base, n):
    for g in range(n):
        _granule_copy(src_ref, buf_ref, sem, idx_ref, base, g).wait()


def _expert_kernel(te_ref, tv_ref, src_ref, hs_ref, wg_ref, wu_ref, wd_ref, o_ref,
                   xbuf_ref, sem, acc_ref, wgb_ref, wub_ref, wdb_ref):
    i = pl.program_id(0)
    j = pl.program_id(1)
    n_tiles = pl.num_programs(0)
    last = pl.num_programs(1) - 1
    valid = tv_ref[i] > 0
    slot = i % 2
    nxt = jnp.minimum(i + 1, n_tiles - 1)

    @pl.when(j == 0)
    def _():
        @pl.when(i == 0)
        def _():
            _gather_start(hs_ref, xbuf_ref.at[0], sem.at[0], src_ref, 0, TILE_GRANS)

        @pl.when(valid)
        def _():
            _gather_wait(hs_ref, xbuf_ref.at[slot], sem.at[slot], src_ref, i * TILE_GRANS, TILE_GRANS)

        @pl.when((i + 1 < n_tiles) & (tv_ref[nxt] > 0))
        def _():
            _gather_start(hs_ref, xbuf_ref.at[1 - slot], sem.at[1 - slot], src_ref, nxt * TILE_GRANS, TILE_GRANS)

    n_sub = tv_ref[i]
    fast_subs = MOE_FAST // MOE_SUB
    fast = n_sub >= fast_subs

    def round_weights():
        wgb_ref[...] = wg_ref[...].astype(bf16)
        wub_ref[...] = wu_ref[...].astype(bf16)
        wdb_ref[...] = wd_ref[...].astype(bf16)

    def swiglu(x):
        a = _silu(_dot(x, wgb_ref[...])) * _dot(x, wub_ref[...])
        return _dot(a.astype(bf16), wdb_ref[...])

    @pl.when(valid & (j == 0))
    def _():
        acc_ref[...] = jnp.zeros_like(acc_ref)

    @pl.when(fast)
    def _():
        round_weights()
        quarter = MOE_FAST // 8
        parts = [pl.ds(q * quarter, quarter) for q in range(8)]

        def gate_up(rows):
            x = xbuf_ref[slot, rows, :]
            return _dot(x, wgb_ref[...]), _dot(x, wub_ref[...])

        nxt_gu = gate_up(parts[0])
        for q, rows in enumerate(parts):
            g, u = nxt_gu
            if q + 1 < len(parts):
                nxt_gu = gate_up(parts[q + 1])
            acc_ref[rows, :] += _dot((_silu(g) * u).astype(bf16), wdb_ref[...])

    @pl.when(valid & jnp.logical_not(fast))
    def _():
        round_weights()

    @pl.when(valid)
    def _():
        def body(sb, carry):
            rows = pl.ds(pl.multiple_of(sb * MOE_SUB, MOE_SUB), MOE_SUB)
            acc_ref[rows, :] += swiglu(xbuf_ref[slot, rows, :])
            return carry

        lax.fori_loop(jnp.where(fast, fast_subs, 0), n_sub, body, 0)

    @pl.when(j == last)
    def _():
        @pl.when(valid)
        def _():
            o_ref[...] = acc_ref[...].astype(o_ref.dtype)

        @pl.when(jnp.logical_not(valid))
        def _():
            o_ref[...] = jnp.zeros_like(o_ref)


def _experts(tile_expert, tile_valid, src_of_dst, hs2d, w_gate, w_up, w_down):
    n_tiles = tile_expert.shape[0]
    dff = w_gate.shape[2]
    nff = dff // MOE_TF

    def jj(j, tv, i):
        return jnp.where(tv[i] > 0, j, nff - 1)

    grid_spec = pltpu.PrefetchScalarGridSpec(
        num_scalar_prefetch=3,
        grid=(n_tiles, nff),
        in_specs=[
            pl.BlockSpec(memory_space=pl.ANY),
            pl.BlockSpec((None, D, MOE_TF), lambda i, j, te, tv, sd: (te[i], 0, jj(j, tv, i))),
            pl.BlockSpec((None, D, MOE_TF), lambda i, j, te, tv, sd: (te[i], 0, jj(j, tv, i))),
            pl.BlockSpec((None, MOE_TF, D), lambda i, j, te, tv, sd: (te[i], jj(j, tv, i), 0)),
        ],
        out_specs=pl.BlockSpec((MOE_TM, D), lambda i, j, te, tv, sd: (i, 0)),
        scratch_shapes=[
            pltpu.VMEM((2, MOE_TM, D), bf16),
            pltpu.SemaphoreType.DMA((2,)),
            pltpu.VMEM((MOE_TM, D), f32),
            pltpu.VMEM((D, MOE_TF), bf16),
            pltpu.VMEM((D, MOE_TF), bf16),
            pltpu.VMEM((MOE_TF, D), bf16),
        ],
    )
    return pl.pallas_call(
        _expert_kernel,
        grid_spec=grid_spec,
        out_shape=jax.ShapeDtypeStruct((n_tiles * MOE_TM, D), bf16),
        compiler_params=pltpu.CompilerParams(vmem_limit_bytes=VMEM_LIMIT_BIG),
        name="experts",
    )(tile_expert, tile_valid, src_of_dst, hs2d, w_gate, w_up, w_down)


def _combine_kernel(ds_ref, x_ref, mod_ref, o_hbm_ref, rt_ref, lng_ref, lnb_ref, out_ref, obuf_ref, sem):
    W = ROUTE_W
    c = pl.program_id(0)
    nc = pl.num_programs(0)
    slot = c % 2
    nxt = jnp.minimum(c + 1, nc - 1)

    @pl.when(c == 0)
    def _():
        _gather_start(o_hbm_ref, obuf_ref.at[0], sem.at[0], ds_ref, 0, CHUNK_GRANS)

    _gather_wait(o_hbm_ref, obuf_ref.at[slot], sem.at[slot], ds_ref, c * CHUNK_GRANS, CHUNK_GRANS)

    _gather_start(o_hbm_ref, obuf_ref.at[1 - slot], sem.at[1 - slot], ds_ref, nxt * CHUNK_GRANS, CHUNK_GRANS)

    osv = obuf_ref[slot]
    scol = lax.broadcasted_iota(jnp.int32, (W // 2, CHUNK_SLOTS), 1)
    for r in range(2):
        rr = slice(r * (W // 2), (r + 1) * (W // 2))
        rt = rt_ref[rr, :]
        p1 = (scol == rt[:, 0:1].astype(jnp.int32)).astype(f32).astype(bf16)
        p2 = (scol == rt[:, 1:2].astype(jnp.int32)).astype(f32).astype(bf16)
        y = rt[:, 2:3] * _dot(p1, osv) + rt[:, 3:4] * _dot(p2, osv)
        xr = ALPHA * x_ref[rr, :] + (1.0 + mod_ref[5:6, :]) * y
        out_ref[rr, :] = _layer_norm(xr, lng_ref[...], lnb_ref[...])

    @pl.when(c == nc - 1)
    def _():
        _gather_wait(o_hbm_ref, obuf_ref.at[1 - slot], sem.at[1 - slot], ds_ref, nxt * CHUNK_GRANS, CHUNK_GRANS)


def _combine(dst_of_src, x2d, mod_l, o2d, route_t, ln_g, ln_b, S):
    T = x2d.shape[0]
    W = ROUTE_W
    tpb = S // W
    grid_spec = pltpu.PrefetchScalarGridSpec(
        num_scalar_prefetch=1,
        grid=(T // W,),
        in_specs=[
            pl.BlockSpec((W, D), lambda c, ds: (c, 0)),
            pl.BlockSpec((None, 6, D), lambda c, ds: (c // tpb, 0, 0)),
            pl.BlockSpec(memory_space=pl.ANY),
            pl.BlockSpec((None, W, 8), lambda c, ds: (c, 0, 0)),
            pl.BlockSpec((1, D), lambda c, ds: (0, 0)),
            pl.BlockSpec((1, D), lambda c, ds: (0, 0)),
        ],
        out_specs=pl.BlockSpec((W, D), lambda c, ds: (c, 0)),
        scratch_shapes=[
            pltpu.VMEM((2, CHUNK_SLOTS, D), bf16),
            pltpu.SemaphoreType.DMA((2,)),
        ],
    )
    return pl.pallas_call(
        _combine_kernel,
        grid_spec=grid_spec,
        out_shape=jax.ShapeDtypeStruct((T, D), f32),
        compiler_params=pltpu.CompilerParams(vmem_limit_bytes=VMEM_LIMIT),
        name="combine",
    )(dst_of_src, x2d, mod_l, o2d, route_t, ln_g, ln_b)


def _routing_tables(counts, n_tiles):
    nc = counts.shape[0]
    gr = (counts + GRAN - 1) // GRAN
    seg_start = jnp.cumsum(gr, axis=1) - gr
    chunk_total = jnp.sum(gr, axis=1)
    prefix = jnp.cumsum(gr, axis=0) - gr
    g_e = jnp.sum(gr, axis=0)
    tiles_e = (g_e + TILE_GRANS - 1) // TILE_GRANS
    tile_end = jnp.cumsum(tiles_e)
    tile_start = tile_end - tiles_e
    total_tiles = tile_end[-1]

    i32 = jnp.int32
    er = jnp.arange(N_EXPERTS, dtype=i32)
    t = jnp.arange(n_tiles, dtype=i32)
    te = jnp.sum((t[:, None] >= tile_end[None, :]).astype(i32), axis=1)
    tile_valid = (t < total_tiles).astype(i32)
    last_e = jnp.sum((total_tiles - 1 >= tile_end).astype(i32))
    tile_expert = jnp.where(tile_valid > 0, jnp.minimum(te, N_EXPERTS - 1), last_e).astype(i32)
    oh_t = (tile_expert[:, None] == er).astype(i32)
    grans_left = jnp.sum(oh_t * (g_e - (t[:, None] - tile_start[None, :]) * TILE_GRANS), axis=1)
    tile_subs = tile_valid * jnp.clip((grans_left + SUB_GRANS - 1) // SUB_GRANS, 0, TILE_GRANS // SUB_GRANS)

    k = jnp.arange(CHUNK_GRANS, dtype=i32)
    seg_end = seg_start + gr
    e_of = jnp.sum((k[None, :, None] >= seg_end[:, None, :]).astype(i32), axis=2)
    oh_e = (jnp.minimum(e_of, N_EXPERTS - 1)[:, :, None] == er).astype(i32)
    base = tile_start[None, :] * TILE_GRANS + prefix - seg_start
    dst = jnp.sum(oh_e * base[:, None, :], axis=2) + k[None, :]
    valid_src = k[None, :] < chunk_total[:, None]
    dst_of_src = jnp.where(valid_src, dst, 0).astype(i32).reshape(-1)

    d = jnp.arange(n_tiles * TILE_GRANS, dtype=i32)
    oh_d = (jnp.repeat(tile_expert, TILE_GRANS)[:, None] == er).astype(i32)
    q = d - jnp.sum(oh_d * tile_start[None, :], axis=1) * TILE_GRANS
    incl_d = jnp.sum(oh_d[:, :, None] * (prefix + gr).T[None], axis=1)
    c_d = jnp.sum((q[:, None] >= incl_d).astype(i32), axis=1)
    oh_c = (jnp.minimum(c_d, nc - 1)[:, None] == jnp.arange(nc, dtype=i32)).astype(i32)
    cbase = jnp.arange(nc, dtype=i32)[:, None] * CHUNK_GRANS + seg_start - prefix
    sel = jnp.sum(oh_c[:, :, None] * oh_d[:, None, :] * cbase[None], axis=(1, 2))
    valid_dst = (jnp.repeat(tile_valid, TILE_GRANS) > 0) & (q >= 0) & (q < jnp.sum(oh_d * g_e[None, :], axis=1))
    src_of_dst = jnp.where(valid_dst, sel + q, 0).astype(i32)
    return tile_expert, tile_subs.astype(i32), src_of_dst, dst_of_src


def kernel(x, c, ada_w, ada_b, ln_g, ln_b, ev_w_in, ev_pool_w, ev_pool_scale, ev_sink, ev_w_out, od_w_in, od_sg_ln_g, od_sg_ln_b, od_w_s, od_b_s, od_w_out, ffn_w_gate, ffn_w_up, ffn_w_down, moe_w_router, moe_w_gate, moe_w_up, moe_w_down):
    B, S, _ = x.shape
    T = B * S
    assert x.shape[-1] == D and ada_w.shape == (DEPTH, D, 6 * D) and B <= 8
    assert ev_w_in.shape == (1, D, POOL_WIDTH + ATTN_WIDTH + 2 * KV_WIDTH) and moe_w_gate.shape[:3] == (1, N_EXPERTS, D)
    assert S % (ROUTE_PER_STEP * ROUTE_W) == 0 and moe_w_gate.shape[3] % MOE_TF == 0
    x2d = x.reshape(T, D)
    mod = _adaln(c, ada_w, ada_b)

    w_in = ev_w_in[0].astype(bf16)
    q0, k0, v0 = POOL_WIDTH, POOL_WIDTH + ATTN_WIDTH, POOL_WIDTH + ATTN_WIDTH + KV_WIDTH
    w_pk = jnp.concatenate([w_in[:, :q0], w_in[:, k0:v0]], axis=1)
    w_qv_t = jnp.concatenate([w_in[:, q0:k0], w_in[:, v0:]], axis=1).T
    p, k, qt, vt = _ev_in(x2d, mod[0], w_pk, w_qv_t, S)
    x2d = _ev_mix(x2d, mod[0], p, qt, k, vt, ev_pool_w[0].astype(bf16), ev_pool_scale[0][None, :],
                  ev_sink[0], ev_w_out[0], ln_g[0, 0][None, :], ln_b[0, 0][None, :], B, S)
    x2d = _ffn(x2d, mod[0], ffn_w_gate[0], ffn_w_up[0], ffn_w_down[0],
               ln_g[0, 1][None, :], ln_b[0, 1][None, :], S)

    x2d = _sg_mix(x2d, mod[1], od_w_in[0], od_sg_ln_g[0][None, :], od_sg_ln_b[0][None, :],
                  od_w_s[0].astype(bf16), od_b_s[0].T, od_w_out[0],
                  ln_g[1, 0][None, :], ln_b[1, 0][None, :], S)

    hs, route, cnt = _route(x2d, mod[1], moe_w_router[0].T, S)
    nc = T // ROUTE_W
    n_tiles = (nc * CHUNK_GRANS) // TILE_GRANS + N_EXPERTS
    tile_expert, tile_subs, src_of_dst, dst_of_src = _routing_tables(cnt[:, :, 0], n_tiles)
    o = _experts(tile_expert, tile_subs, src_of_dst, hs.reshape(nc * CHUNK_SLOTS, D),
                 moe_w_gate[0], moe_w_up[0], moe_w_down[0])
    x2d = _combine(dst_of_src, x2d, mod[1], o, jnp.swapaxes(route, 1, 2),
                   ln_g[1, 1][None, :], ln_b[1, 1][None, :], S)
    return x2d.reshape(B, S, D)
```

```python
import functools
import math

import jax
import jax.numpy as jnp
import numpy as np
from jax import lax
from jax.experimental import pallas as pl
from jax.experimental.pallas import tpu as pltpu

D = 1024
DEPTH = 2
ALPHA = (2.0 * DEPTH) ** 0.25
LN_EPS = 1e-5

POOL_WINDOWS = (2, 4, 8, 16)
POOL_CH = 128
POOL_WIDTH = 512
HEAD_DIM = 64
N_Q_HEADS = 8
N_KV_HEADS = 2
Q_GROUP = 4
ATTN_WIDTH = 512
KV_WIDTH = 128
BLOCK = 128
POOL_HALO = 8

CHUNK = 128
SG_GROUPS = 8
SG_CH = 128

N_EXPERTS = 8

ROUTE_W = 512
ROUTE_PER_STEP = 4
GRAN = 16
CHUNK_SLOTS = 2 * ROUTE_W + N_EXPERTS * GRAN
CHUNK_GRANS = CHUNK_SLOTS // GRAN
MOE_SUB = 256
MOE_FAST = 2048
MOE_TM = MOE_FAST + MOE_SUB
TILE_GRANS = MOE_TM // GRAN
SUB_GRANS = MOE_SUB // GRAN
MOE_TF = 512

VMEM_LIMIT = 48 * 1024 * 1024
VMEM_LIMIT_BIG = 56 * 1024 * 1024

bf16 = jnp.bfloat16
f32 = jnp.float32


def _dot(a, b):
    return jnp.dot(a, b, preferred_element_type=f32)


def _split_bf16(a):
    hi = a.astype(bf16)
    lo = (a - hi.astype(f32)).astype(bf16)
    return hi, lo


def _layer_norm(x, g, b):
    mu = jnp.mean(x, axis=-1, keepdims=True)
    xc = x - mu
    var = jnp.mean(xc * xc, axis=-1, keepdims=True)
    return xc * lax.rsqrt(var + LN_EPS) * g + b


def _silu(x):
    return x * jax.nn.sigmoid(x)


def _gelu_tanh(x):
    c = math.sqrt(2.0 / math.pi)
    return x * (0.5 * (1.0 + jnp.tanh(c * (x + 0.044715 * (x * x * x)))))


def _adaln_kernel(c_ref, w_ref, b_ref, o_ref):
    cond = _silu(c_ref[...])
    c_hi, c_lo = _split_bf16(cond)
    w_hi, w_lo = _split_bf16(w_ref[...])
    acc = _dot(c_hi, w_hi) + (_dot(c_lo, w_hi) + _dot(c_hi, w_lo))
    o_ref[...] = acc + b_ref[...]


def _adaln(c, ada_w, ada_b):
    B = c.shape[0]
    tn = 2048
    c_pad = jnp.zeros((8, D), f32).at[:B].set(c)
    out = pl.pallas_call(
        _adaln_kernel,
        grid=(DEPTH, 6 * D // tn),
        in_specs=[
            pl.BlockSpec((8, D), lambda l, j: (0, 0)),
            pl.BlockSpec((None, D, tn), lambda l, j: (l, 0, j)),
            pl.BlockSpec((None, 1, tn), lambda l, j: (l, 0, j)),
        ],
        out_specs=pl.BlockSpec((None, 8, tn), lambda l, j: (l, 0, j)),
        out_shape=jax.ShapeDtypeStruct((DEPTH, 8, 6 * D), f32),
        compiler_params=pltpu.CompilerParams(vmem_limit_bytes=VMEM_LIMIT),
        name="adaln",
    )(c_pad, ada_w, ada_b.reshape(DEPTH, 1, 6 * D))
    return out[:, :B].reshape(DEPTH, B, 6, D)


def _ev_in_kernel(x_ref, mod_ref, wpk_ref, wqvt_ref, p_ref, k_ref, qt_ref, vt_ref):
    h = (x_ref[...] * (1.0 + mod_ref[1:2, :]) + mod_ref[0:1, :]).astype(bf16)
    zpk = _dot(h, wpk_ref[...])
    p_ref[...] = zpk[:, :POOL_WIDTH]
    k_ref[...] = zpk[:, POOL_WIDTH:].astype(bf16)
    zt = lax.dot_general(wqvt_ref[...], h, (((1,), (1,)), ((), ())), preferred_element_type=f32)
    qt_ref[...] = (zt[:ATTN_WIDTH] * (HEAD_DIM ** -0.5)).astype(bf16)
    vt_ref[...] = zt[ATTN_WIDTH:].astype(bf16)


def _ev_in(x2d, mod_l, w_pk, w_qv_t, S):
    T = x2d.shape[0]
    tm = 512
    tpb = S // tm
    return pl.pallas_call(
        _ev_in_kernel,
        grid=(T // tm,),
        in_specs=[
            pl.BlockSpec((tm, D), lambda i: (i, 0)),
            pl.BlockSpec((None, 6, D), lambda i: (i // tpb, 0, 0)),
            pl.BlockSpec((D, POOL_WIDTH + KV_WIDTH), lambda i: (0, 0)),
            pl.BlockSpec((ATTN_WIDTH + KV_WIDTH, D), lambda i: (0, 0)),
        ],
        out_specs=[
            pl.BlockSpec((tm, POOL_WIDTH), lambda i: (i, 0)),
            pl.BlockSpec((tm, KV_WIDTH), lambda i: (i, 0)),
            pl.BlockSpec((ATTN_WIDTH, tm), lambda i: (0, i)),
            pl.BlockSpec((KV_WIDTH, tm), lambda i: (0, i)),
        ],
        out_shape=[
            jax.ShapeDtypeStruct((T, POOL_WIDTH), f32),
            jax.ShapeDtypeStruct((T, KV_WIDTH), bf16),
            jax.ShapeDtypeStruct((ATTN_WIDTH, T), bf16),
            jax.ShapeDtypeStruct((KV_WIDTH, T), bf16),
        ],
        compiler_params=pltpu.CompilerParams(vmem_limit_bytes=VMEM_LIMIT),
        name="ev_in",
    )(x2d, mod_l, w_pk, w_qv_t)


def _ev_mix_kernel(x_ref, mod_ref, p_ref, pp_ref, pn_ref, qt_ref,
                   k_ref, kp_ref, kn_ref, vt_ref, vtp_ref, vtn_ref,
                   bias_ref, sink_ref, wpool_ref, pscale_ref, wout_hbm_ref, lng_ref, lnb_ref,
                   o_ref, pext_ref, lvl_ref, pooled_ref, kext_ref, vext_ref, ybt_ref, mix_ref,
                   wout_ref, stage_ref, sem, *, S, tq):
    i = pl.program_id(1)

    @pl.when((pl.program_id(0) == 0) & (i == 0))
    def _():
        _load_rounded(wout_hbm_ref, wout_ref, stage_ref, sem)

    n_tiles = S // tq
    is_first = i == 0
    is_last = i == n_tiles - 1
    H = POOL_HALO

    p = p_ref[...]
    pext_ref[0:H, :] = jnp.where(is_first, 0.0, pp_ref[...])
    pext_ref[H:H + tq, :] = p
    pext_ref[H + tq:2 * H + tq, :] = jnp.where(is_last, 0.0, pn_ref[...])
    pext_ref[2 * H + tq:, :] = jnp.zeros((pext_ref.shape[0] - 2 * H - tq, POOL_WIDTH), f32)
    near = lax.broadcasted_iota(jnp.int32, (H, 1), 0)

    def src_rows(src, k, off, n, cs):
        return src[off:off + n, cs] if src is pext_ref else src[k - 1, off:off + n, cs]

    def pool_group(g):
        w = POOL_WINDOWS[g]
        cs = slice(g * POOL_CH, (g + 1) * POOL_CH)
        r = w // 2
        src, length, k = pext_ref, pext_ref.shape[0] - H, 0
        while 2 ** k < r:
            step = 2 ** k
            lvl_ref[k, 0:length, cs] = src_rows(src, k, 0, length, cs) + src_rows(src, k, step, length, cs)
            src, length, k = lvl_ref, length - H, k + 1
        half_run = 2 ** k
        wsum = (src_rows(src, k, H - r, tq, cs) + src_rows(src, k, H - r + half_run, tq, cs)
                + pext_ref[H + r:H + r + tq, cs])
        pooled_ref[:, cs] = wsum / float(w + 1) - p[:, cs]
        cnt_head = (jnp.minimum(near, r) + (r + 1)).astype(f32)
        cnt_tail = (jnp.minimum(H - 1 - near, r) + (r + 1)).astype(f32)
        cnt_head = jnp.where(is_first, cnt_head, float(w + 1))
        cnt_tail = jnp.where(is_last, cnt_tail, float(w + 1))
        pooled_ref[0:H, cs] = wsum[0:H] / cnt_head - p[0:H, cs]
        pooled_ref[tq - H:tq, cs] = wsum[tq - H:tq] / cnt_tail - p[tq - H:tq, cs]
        ya = _dot(pooled_ref[:, cs].astype(bf16), wpool_ref[g])
        mix_ref[:, cs] = (ya * pscale_ref[:, cs]).astype(bf16)

    for g in range(len(POOL_WINDOWS)):
        pool_group(g)

    kext_ref[0:BLOCK, :] = kp_ref[...]
    kext_ref[BLOCK:BLOCK + tq, :] = k_ref[...]
    kext_ref[BLOCK + tq:, :] = kn_ref[...]
    vext_ref[:, 0:BLOCK] = vtp_ref[...]
    vext_ref[:, BLOCK:BLOCK + tq] = vt_ref[...]
    vext_ref[:, BLOCK + tq:] = vtn_ref[...]

    n_blocks = S // BLOCK
    zeros_q = jnp.zeros((HEAD_DIM, Q_GROUP * BLOCK), bf16)
    items = [(n, kvh) for n in range(tq // BLOCK) for kvh in range(N_KV_HEADS)]

    def scores(n, kvh):
        gb = i * (tq // BLOCK) + n
        variant = jnp.where(gb == 0, 1, jnp.where(gb == n_blocks - 1, 2, 0))
        cols = slice(n * BLOCK, (n + 1) * BLOCK)
        kw = kext_ref[n * BLOCK:n * BLOCK + 3 * BLOCK, :]
        qst = jnp.concatenate(
            [qt_ref[(kvh * Q_GROUP + gq) * HEAD_DIM:(kvh * Q_GROUP + gq + 1) * HEAD_DIM, cols]
             for gq in range(Q_GROUP)], axis=1)
        qst = jnp.concatenate([qst, zeros_q] if kvh == 0 else [zeros_q, qst], axis=0)
        return _dot(kw, qst) + bias_ref[variant, kvh]

    def softmax(n, kvh, s):
        sink = sink_ref[kvh]
        m = jnp.maximum(jnp.max(s, axis=0, keepdims=True), sink)
        e = jnp.exp(s - m)
        denom = jnp.sum(e, axis=0, keepdims=True) + jnp.exp(sink - m)
        return e.astype(bf16), denom

    def weighted_values(n, kvh, e, denom):
        cols = slice(n * BLOCK, (n + 1) * BLOCK)
        vwt = vext_ref[kvh * HEAD_DIM:(kvh + 1) * HEAD_DIM, n * BLOCK:n * BLOCK + 3 * BLOCK]
        out = _dot(vwt, e) / denom
        for gq in range(Q_GROUP):
            hq = kvh * Q_GROUP + gq
            ybt_ref[hq * HEAD_DIM:(hq + 1) * HEAD_DIM, cols] = out[:, gq * BLOCK:(gq + 1) * BLOCK]

    s_next = scores(*items[0])
    prev = None
    for idx, item in enumerate(items):
        s_cur = s_next
        if idx + 1 < len(items):
            s_next = scores(*items[idx + 1])
        cur = softmax(*item, s_cur)
        if prev is not None:
            weighted_values(*items[idx - 1], *prev)
        prev = cur
    weighted_values(*items[-1], *prev)
    mix_ref[:, POOL_WIDTH:] = ybt_ref[...].T.astype(bf16)

    halves = [slice(r * (tq // 2), (r + 1) * (tq // 2)) for r in range(2)]
    ys = [_dot(mix_ref[rr, :], wout_ref[...]) for rr in halves]
    for rr, y in zip(halves, ys):
        xr = ALPHA * x_ref[rr, :] + (1.0 + mod_ref[2:3, :]) * y
        o_ref[rr, :] = _layer_norm(xr, lng_ref[...], lnb_ref[...])


def _ev_mix(x2d, mod_l, p, qt, k, vt, w_pool, pool_scale, sink, w_out, ln_g, ln_b, B, S):
    T = x2d.shape[0]
    tq = 512
    nt = S // tq
    kb = tq // BLOCK
    pb = tq // POOL_HALO
    n_kblocks = T // BLOCK
    n_pblocks = T // POOL_HALO

    def main(b, i): return (b * nt + i, 0)
    def kprev(b, i): return (jnp.maximum((b * nt + i) * kb - 1, 0), 0)
    def knext(b, i): return (jnp.minimum((b * nt + i + 1) * kb, n_kblocks - 1), 0)
    def pprev(b, i): return (jnp.maximum((b * nt + i) * pb - 1, 0), 0)
    def pnext(b, i): return (jnp.minimum((b * nt + i + 1) * pb, n_pblocks - 1), 0)
    def const2(b, i): return (0, 0)

    def tmain(b, i): return (0, b * nt + i)
    def tprev(b, i): return (0, jnp.maximum((b * nt + i) * kb - 1, 0))
    def tnext(b, i): return (0, jnp.minimum((b * nt + i + 1) * kb, n_kblocks - 1))

    assert S // BLOCK >= 2
    kj = np.arange(3 * BLOCK)[:, None]
    qi = np.arange(BLOCK)[None, :]
    dist = np.abs(kj - BLOCK - qi)
    slopes = np.float32(2.0) ** (np.float32(-8.0) * np.arange(1, N_Q_HEADS + 1, dtype=np.float32) / N_Q_HEADS)
    alibi = -slopes[:, None, None] * dist.astype(np.float32)[None]
    in_window = dist <= BLOCK
    key_ok = np.stack([kj >= 0, kj >= BLOCK, kj < 2 * BLOCK])
    bias = np.where((in_window[None] & key_ok)[:, None], alibi[None], np.float32(-1e30))
    bias = bias.reshape(3, N_KV_HEADS, Q_GROUP, 3 * BLOCK, BLOCK).transpose(0, 1, 3, 2, 4)
    bias = jnp.asarray(bias.reshape(3, N_KV_HEADS, 3 * BLOCK, Q_GROUP * BLOCK), f32)
    sink_row = jnp.repeat(sink.astype(f32).reshape(N_KV_HEADS, Q_GROUP), BLOCK, axis=1)[:, None, :]

    assert tq >= 2 * POOL_HALO and max(POOL_WINDOWS) // 2 <= POOL_HALO and nt >= 2

    kernel = functools.partial(_ev_mix_kernel, S=S, tq=tq)
    return pl.pallas_call(
        kernel,
        grid=(B, nt),
        in_specs=[
            pl.BlockSpec((tq, D), main),
            pl.BlockSpec((None, 6, D), lambda b, i: (b, 0, 0)),
            pl.BlockSpec((tq, POOL_WIDTH), main),
            pl.BlockSpec((POOL_HALO, POOL_WIDTH), pprev),
            pl.BlockSpec((POOL_HALO, POOL_WIDTH), pnext),
            pl.BlockSpec((ATTN_WIDTH, tq), tmain),
            pl.BlockSpec((tq, KV_WIDTH), main),
            pl.BlockSpec((BLOCK, KV_WIDTH), kprev),
            pl.BlockSpec((BLOCK, KV_WIDTH), knext),
            pl.BlockSpec((KV_WIDTH, tq), tmain),
            pl.BlockSpec((KV_WIDTH, BLOCK), tprev),
            pl.BlockSpec((KV_WIDTH, BLOCK), tnext),
            pl.BlockSpec((3, N_KV_HEADS, 3 * BLOCK, Q_GROUP * BLOCK), lambda b, i: (0, 0, 0, 0)),
            pl.BlockSpec((N_KV_HEADS, 1, Q_GROUP * BLOCK), lambda b, i: (0, 0, 0)),
            pl.BlockSpec((len(POOL_WINDOWS), POOL_CH, POOL_CH), lambda b, i: (0, 0, 0)),
            pl.BlockSpec((1, POOL_WIDTH), const2),
            pl.BlockSpec(memory_space=pl.ANY),
            pl.BlockSpec((1, D), const2),
            pl.BlockSpec((1, D), const2),
        ],
        out_specs=pl.BlockSpec((tq, D), main),
        out_shape=jax.ShapeDtypeStruct((T, D), f32),
        scratch_shapes=[
            pltpu.VMEM((tq + 4 * POOL_HALO, POOL_WIDTH), f32),
            pltpu.VMEM((3, tq + 3 * POOL_HALO, POOL_WIDTH), f32),
            pltpu.VMEM((tq, POOL_WIDTH), f32),
            pltpu.VMEM((tq + 2 * BLOCK, KV_WIDTH), bf16),
            pltpu.VMEM((KV_WIDTH, tq + 2 * BLOCK), bf16),
            pltpu.VMEM((ATTN_WIDTH, tq), f32),
            pltpu.VMEM((tq, D), bf16),
            pltpu.VMEM((D, D), bf16),
            pltpu.VMEM((2, D // 8, D), f32),
            pltpu.SemaphoreType.DMA((2,)),
        ],
        compiler_params=pltpu.CompilerParams(vmem_limit_bytes=VMEM_LIMIT),
        name="ev_mix",
    )(x2d, mod_l, p, p, p, qt, k, k, k, vt, vt, vt, bias, sink_row, w_pool, pool_scale, w_out, ln_g, ln_b)


def _load_rounded(w_hbm_ref, w_ref, stage_ref, sem):
    rows = stage_ref.shape[1]
    n = w_hbm_ref.shape[0] // rows

    def copy(c):
        return pltpu.make_async_copy(w_hbm_ref.at[pl.ds(c * rows, rows), :], stage_ref.at[c % 2], sem.at[c % 2])

    copy(0).start()
    for c in range(n):
        if c + 1 < n:
            copy(c + 1).start()
        copy(c).wait()
        w_ref[pl.ds(c * rows, rows), :] = stage_ref[c % 2].astype(bf16)


def _ffn_kernel(x_ref, mod_ref, wg_hbm_ref, wu_hbm_ref, wd_hbm_ref, lng_ref, lnb_ref, o_ref,
                wg_ref, wu_ref, wd_ref, stage_in_ref, stage_out_ref, sem):
    @pl.when(pl.program_id(0) == 0)
    def _():
        _load_rounded(wg_hbm_ref, wg_ref, stage_in_ref, sem)
        _load_rounded(wu_hbm_ref, wu_ref, stage_in_ref, sem)
        _load_rounded(wd_hbm_ref, wd_ref, stage_out_ref, sem)

    n_pieces = 2
    piece = x_ref.shape[0] // n_pieces
    parts = [pl.ds(q * piece, piece) for q in range(n_pieces)]

    def gate_up(rows):
        x = x_ref[rows, :]
        h = (x * (1.0 + mod_ref[4:5, :]) + mod_ref[3:4, :]).astype(bf16)
        return x, _dot(h, wg_ref[...]), _dot(h, wu_ref[...])

    nxt = gate_up(parts[0])
    for q, rows in enumerate(parts):
        x, g, u = nxt
        if q + 1 < n_pieces:
            nxt = gate_up(parts[q + 1])
        y = _dot((_silu(g) * u).astype(bf16), wd_ref[...])
        xr = ALPHA * x + (1.0 + mod_ref[5:6, :]) * y
        o_ref[rows, :] = _layer_norm(xr, lng_ref[...], lnb_ref[...])


def _ffn(x2d, mod_l, w_gate, w_up, w_down, ln_g, ln_b, S):
    T = x2d.shape[0]
    tm = 512
    dff = w_gate.shape[1]
    tpb = S // tm
    n_stage = 8
    return pl.pallas_call(
        _ffn_kernel,
        grid=(T // tm,),
        in_specs=[
            pl.BlockSpec((tm, D), lambda i: (i, 0)),
            pl.BlockSpec((None, 6, D), lambda i: (i // tpb, 0, 0)),
            pl.BlockSpec(memory_space=pl.ANY),
            pl.BlockSpec(memory_space=pl.ANY),
            pl.BlockSpec(memory_space=pl.ANY),
            pl.BlockSpec((1, D), lambda i: (0, 0)),
            pl.BlockSpec((1, D), lambda i: (0, 0)),
        ],
        out_specs=pl.BlockSpec((tm, D), lambda i: (i, 0)),
        out_shape=jax.ShapeDtypeStruct((T, D), f32),
        scratch_shapes=[
            pltpu.VMEM((D, dff), bf16),
            pltpu.VMEM((D, dff), bf16),
            pltpu.VMEM((dff, D), bf16),
            pltpu.VMEM((2, D // n_stage, dff), f32),
            pltpu.VMEM((2, dff // n_stage, D), f32),
            pltpu.SemaphoreType.DMA((2,)),
        ],
        compiler_params=pltpu.CompilerParams(vmem_limit_bytes=VMEM_LIMIT_BIG),
        name="ffn",
    )(x2d, mod_l, w_gate, w_up, w_down, ln_g, ln_b)


def _sg_kernel(x_ref, mod_ref, win_hbm_ref, sgg_ref, sgb_ref, ws_ref, bst_ref, wout_hbm_ref, lng_ref, lnb_ref,
               o_ref, gate_ref, win_ref, wout_ref, stage_in_ref, stage_out_ref, sem, *, tm):
    @pl.when(pl.program_id(0) == 0)
    def _():
        _load_rounded(win_hbm_ref, win_ref, stage_in_ref, sem)
        _load_rounded(wout_hbm_ref, wout_ref, stage_out_ref, sem)

    n_pieces = 2
    piece = tm // n_pieces

    def project(q):
        x = x_ref[q * piece:(q + 1) * piece, :]
        h = (x * (1.0 + mod_ref[1:2, :]) + mod_ref[0:1, :]).astype(bf16)
        return x, _dot(h, win_ref[:, D:]), _dot(h, win_ref[:, :D])

    nxt = project(0)
    for q in range(n_pieces):
        x, zv, zu = nxt
        if q + 1 < n_pieces:
            nxt = project(q + 1)
        v = _layer_norm(_gelu_tanh(zv), sgg_ref[...], sgb_ref[...]).astype(bf16)
        u = _gelu_tanh(zu)
        for n in range(piece // CHUNK):
            rows = slice(n * CHUNK, (n + 1) * CHUNK)
            grows = slice(q * piece + n * CHUNK, q * piece + (n + 1) * CHUNK)
            for g in range(SG_GROUPS):
                cols = slice(g * SG_CH, (g + 1) * SG_CH)
                sv = _dot(ws_ref[g], v[rows, cols]) + bst_ref[:, g:g + 1]
                gate_ref[grows, cols] = (u[rows, cols] * sv).astype(bf16)
        y = _dot(gate_ref[q * piece:(q + 1) * piece, :], wout_ref[...])
        xr = ALPHA * x + (1.0 + mod_ref[2:3, :]) * y
        o_ref[q * piece:(q + 1) * piece, :] = _layer_norm(xr, lng_ref[...], lnb_ref[...])


def _sg_mix(x2d, mod_l, w_in, sg_g, sg_b, w_s, b_s_t, w_out, ln_g, ln_b, S):
    T = x2d.shape[0]
    tm = 512
    tpb = S // tm
    c2 = lambda i: (0, 0)
    n_stage = 8
    return pl.pallas_call(
        functools.partial(_sg_kernel, tm=tm),
        grid=(T // tm,),
        in_specs=[
            pl.BlockSpec((tm, D), lambda i: (i, 0)),
            pl.BlockSpec((None, 6, D), lambda i: (i // tpb, 0, 0)),
            pl.BlockSpec(memory_space=pl.ANY),
            pl.BlockSpec((1, D), c2),
            pl.BlockSpec((1, D), c2),
            pl.BlockSpec((SG_GROUPS, CHUNK, CHUNK), lambda i: (0, 0, 0)),
            pl.BlockSpec((CHUNK, SG_GROUPS), c2),
            pl.BlockSpec(memory_space=pl.ANY),
            pl.BlockSpec((1, D), c2),
            pl.BlockSpec((1, D), c2),
        ],
        out_specs=pl.BlockSpec((tm, D), lambda i: (i, 0)),
        out_shape=jax.ShapeDtypeStruct((T, D), f32),
        scratch_shapes=[
            pltpu.VMEM((tm, D), bf16),
            pltpu.VMEM((D, 2 * D), bf16),
            pltpu.VMEM((D, D), bf16),
            pltpu.VMEM((2, D // n_stage, 2 * D), f32),
            pltpu.VMEM((2, D // n_stage, D), f32),
            pltpu.SemaphoreType.DMA((2,)),
        ],
        compiler_params=pltpu.CompilerParams(vmem_limit_bytes=VMEM_LIMIT),
        name="sg_mix",
    )(x2d, mod_l, w_in, sg_g, sg_b, w_s, b_s_t, w_out, ln_g, ln_b)


def _route_kernel(x_ref, mod_ref, wrt_ref, hs_ref, route_ref, cnt_ref):
    W = ROUTE_W
    w_hi, w_lo = _split_bf16(wrt_ref[...])
    nt = (((1,), (1,)), ((), ()))
    eidx = lax.broadcasted_iota(jnp.int32, (N_EXPERTS, W), 0)
    sub = lax.broadcasted_iota(jnp.int32, (N_EXPERTS, 1), 0)
    tr = lax.broadcasted_iota(jnp.int32, (W, W), 0)
    tc = lax.broadcasted_iota(jnp.int32, (W, W), 1)
    upper = (tr < tc).astype(bf16)
    srow = lax.broadcasted_iota(jnp.int32, (CHUNK_SLOTS, W), 0)
    ridx = lax.broadcasted_iota(jnp.int32, (8, W), 0)

    def assign_slots(k):
        h = x_ref[k * W:(k + 1) * W, :] * (1.0 + mod_ref[4:5, :]) + mod_ref[3:4, :]
        h_hi, h_lo = _split_bf16(h)
        logits = (lax.dot_general(w_hi, h_hi, nt, preferred_element_type=f32)
                  + (lax.dot_general(w_hi, h_lo, nt, preferred_element_type=f32)
                     + lax.dot_general(w_lo, h_hi, nt, preferred_element_type=f32)))
        m1 = jnp.max(logits, axis=0, keepdims=True)
        i1 = jnp.min(jnp.where(logits == m1, eidx, N_EXPERTS), axis=0, keepdims=True)
        sel1 = eidx == i1
        rest = jnp.where(sel1, -jnp.inf, logits)
        m2 = jnp.max(rest, axis=0, keepdims=True)
        i2 = jnp.min(jnp.where(rest == m2, eidx, N_EXPERTS), axis=0, keepdims=True)
        sel2 = eidx == i2
        e2 = jnp.exp(m2 - m1)
        g1 = 1.0 / (1.0 + e2)
        g2 = e2 / (1.0 + e2)

        a1 = sel1.astype(f32)
        a2 = sel2.astype(f32)
        assign = a1 + a2
        counts = jnp.sum(assign, axis=1, keepdims=True)
        grans = jnp.ceil(counts * (1.0 / GRAN))
        seg = jnp.zeros((N_EXPERTS, 1), f32)
        for e in range(N_EXPERTS - 1):
            seg = seg + jnp.where(sub > e, grans[e:e + 1, :] * GRAN, 0.0)
        rank = _dot(assign.astype(bf16), upper)
        slot = seg + rank
        pos1 = jnp.sum(a1 * slot, axis=0, keepdims=True)
        pos2 = jnp.sum(a2 * slot, axis=0, keepdims=True)
        route_ref[k] = jnp.where(ridx == 0, pos1, jnp.where(ridx == 1, pos2,
                                 jnp.where(ridx == 2, g1, jnp.where(ridx == 3, g2, 0.0))))
        cnt_ref[k] = jnp.broadcast_to(counts, (N_EXPERTS, 128)).astype(jnp.int32)
        return h_hi, pos1, pos2

    def sort_rows(k, h_hi, pos1, pos2):
        perm = ((srow == pos1.astype(jnp.int32)) | (srow == pos2.astype(jnp.int32)))
        hs_ref[k] = _dot(perm.astype(f32).astype(bf16), h_hi).astype(bf16)

    nxt = assign_slots(0)
    for k in range(ROUTE_PER_STEP):
        cur = nxt
        if k + 1 < ROUTE_PER_STEP:
            nxt = assign_slots(k + 1)
        sort_rows(k, *cur)


def _route(x2d, mod_l, w_router_t, S):
    T = x2d.shape[0]
    W = ROUTE_W
    R = ROUTE_PER_STEP
    nc = T // W
    tpb = S // (R * W)
    return pl.pallas_call(
        _route_kernel,
        grid=(nc // R,),
        in_specs=[
            pl.BlockSpec((R * W, D), lambda c: (c, 0)),
            pl.BlockSpec((None, 6, D), lambda c: (c // tpb, 0, 0)),
            pl.BlockSpec((N_EXPERTS, D), lambda c: (0, 0)),
        ],
        out_specs=[
            pl.BlockSpec((R, CHUNK_SLOTS, D), lambda c: (c, 0, 0)),
            pl.BlockSpec((R, 8, W), lambda c: (c, 0, 0)),
            pl.BlockSpec((R, N_EXPERTS, 128), lambda c: (c, 0, 0)),
        ],
        out_shape=[
            jax.ShapeDtypeStruct((nc, CHUNK_SLOTS, D), bf16),
            jax.ShapeDtypeStruct((nc, 8, W), f32),
            jax.ShapeDtypeStruct((nc, N_EXPERTS, 128), jnp.int32),
        ],
        compiler_params=pltpu.CompilerParams(vmem_limit_bytes=VMEM_LIMIT),
        name="route",
    )(x2d, mod_l, w_router_t)


def _granule_copy(src_ref, buf_ref, sem, idx_ref, base, g):
    row = pl.multiple_of(idx_ref[base + g] * GRAN, GRAN)
    return pltpu.make_async_copy(src_ref.at[pl.ds(row, GRAN), :], buf_ref.at[pl.ds(g * GRAN, GRAN), :], sem)


def _gather_start(src_ref, buf_ref, sem, idx_ref, base, n):
    for g in range(n):
        _granule_copy(src_ref, buf_ref, sem, idx_ref, base, g).start()


def _gather_wait(src_ref, buf_ref, sem, idx_ref, base, n):
    for g in range(n):
        _granule_copy(src_ref, buf_ref, sem, idx_ref, base, g).wait()


def _expert_kernel(te_ref, tv_ref, src_ref, hs_ref, wg_ref, wu_ref, wd_ref, o_ref,
                   xbuf_ref, sem, acc_ref, wgb_ref, wub_ref, wdb_ref):
    i = pl.program_id(0)
    j = pl.program_id(1)
    n_tiles = pl.num_programs(0)
    last = pl.num_programs(1) - 1
    valid = tv_ref[i] > 0
    slot = i % 2
    nxt = jnp.minimum(i + 1, n_tiles - 1)

    @pl.when(j == 0)
    def _():
        @pl.when(i == 0)
        def _():
            _gather_start(hs_ref, xbuf_ref.at[0], sem.at[0], src_ref, 0, TILE_GRANS)

        @pl.when(valid)
        def _():
            _gather_wait(hs_ref, xbuf_ref.at[slot], sem.at[slot], src_ref, i * TILE_GRANS, TILE_GRANS)

        @pl.when((i + 1 < n_tiles) & (tv_ref[nxt] > 0))
        def _():
            _gather_start(hs_ref, xbuf_ref.at[1 - slot], sem.at[1 - slot], src_ref, nxt * TILE_GRANS, TILE_GRANS)

    n_sub = tv_ref[i]
    fast_subs = MOE_FAST // MOE_SUB
    fast = n_sub >= fast_subs

    def round_weights():
        wgb_ref[...] = wg_ref[...].astype(bf16)
        wub_ref[...] = wu_ref[...].astype(bf16)
        wdb_ref[...] = wd_ref[...].astype(bf16)

    def swiglu(x):
        a = _silu(_dot(x, wgb_ref[...])) * _dot(x, wub_ref[...])
        return _dot(a.astype(bf16), wdb_ref[...])

    @pl.when(valid & (j == 0))
    def _():
        acc_ref[...] = jnp.zeros_like(acc_ref)

    def dense_block(n_pieces):
        round_weights()
        parts = [pl.ds(q * MOE_SUB, MOE_SUB) for q in range(n_pieces)]

        def gate_up(rows):
            x = xbuf_ref[slot, rows, :]
            return _dot(x, wgb_ref[...]), _dot(x, wub_ref[...])

        nxt_gu = gate_up(parts[0])
        for q, rows in enumerate(parts):
            g, u = nxt_gu
            if q + 1 < len(parts):
                nxt_gu = gate_up(parts[q + 1])
            acc_ref[rows, :] += _dot((_silu(g) * u).astype(bf16), wdb_ref[...])

    for n_dense in (MOE_TM // MOE_SUB, fast_subs):
        pl.when(n_sub == n_dense)(functools.partial(dense_block, n_dense))

    @pl.when(valid & jnp.logical_not(fast))
    def _():
        round_weights()

        def body(sb, carry):
            rows = pl.ds(pl.multiple_of(sb * MOE_SUB, MOE_SUB), MOE_SUB)
            acc_ref[rows, :] += swiglu(xbuf_ref[slot, rows, :])
            return carry

        lax.fori_loop(0, n_sub, body, 0)

    @pl.when(j == last)
    def _():
        @pl.when(valid)
        def _():
            o_ref[...] = acc_ref[...].astype(o_ref.dtype)

        @pl.when(jnp.logical_not(valid))
        def _():
            o_ref[...] = jnp.zeros_like(o_ref)


def _experts(tile_expert, tile_valid, src_of_dst, hs2d, w_gate, w_up, w_down):
    n_tiles = tile_expert.shape[0]
    dff = w_gate.shape[2]
    nff = dff // MOE_TF

    def jj(j, tv, i):
        return jnp.where(tv[i] > 0, j, nff - 1)

    grid_spec = pltpu.PrefetchScalarGridSpec(
        num_scalar_prefetch=3,
        grid=(n_tiles, nff),
        in_specs=[
            pl.BlockSpec(memory_space=pl.ANY),
            pl.BlockSpec((None, D, MOE_TF), lambda i, j, te, tv, sd: (te[i], 0, jj(j, tv, i))),
            pl.BlockSpec((None, D, MOE_TF), lambda i, j, te, tv, sd: (te[i], 0, jj(j, tv, i))),
            pl.BlockSpec((None, MOE_TF, D), lambda i, j, te, tv, sd: (te[i], jj(j, tv, i), 0)),
        ],
        out_specs=pl.BlockSpec((MOE_TM, D), lambda i, j, te, tv, sd: (i, 0)),
        scratch_shapes=[
            pltpu.VMEM((2, MOE_TM, D), bf16),
            pltpu.SemaphoreType.DMA((2,)),
            pltpu.VMEM((MOE_TM, D), f32),
            pltpu.VMEM((D, MOE_TF), bf16),
            pltpu.VMEM((D, MOE_TF), bf16),
            pltpu.VMEM((MOE_TF, D), bf16),
        ],
    )
    return pl.pallas_call(
        _expert_kernel,
        grid_spec=grid_spec,
        out_shape=jax.ShapeDtypeStruct((n_tiles * MOE_TM, D), bf16),
        compiler_params=pltpu.CompilerParams(vmem_limit_bytes=VMEM_LIMIT_BIG),
        name="experts",
    )(tile_expert, tile_valid, src_of_dst, hs2d, w_gate, w_up, w_down)


def _combine_kernel(ds_ref, x_ref, mod_ref, o_hbm_ref, rt_ref, lng_ref, lnb_ref, out_ref, obuf_ref, sem):
    W = ROUTE_W
    c = pl.program_id(0)
    nc = pl.num_programs(0)
    slot = c % 2
    nxt = jnp.minimum(c + 1, nc - 1)

    @pl.when(c == 0)
    def _():
        _gather_start(o_hbm_ref, obuf_ref.at[0], sem.at[0], ds_ref, 0, CHUNK_GRANS)

    _gather_wait(o_hbm_ref, obuf_ref.at[slot], sem.at[slot], ds_ref, c * CHUNK_GRANS, CHUNK_GRANS)

    _gather_start(o_hbm_ref, obuf_ref.at[1 - slot], sem.at[1 - slot], ds_ref, nxt * CHUNK_GRANS, CHUNK_GRANS)

    osv = obuf_ref[slot]
    scol = lax.broadcasted_iota(jnp.int32, (W // 2, CHUNK_SLOTS), 1)
    for r in range(2):
        rr = slice(r * (W // 2), (r + 1) * (W // 2))
        rt = rt_ref[rr, :]
        p1 = (scol == rt[:, 0:1].astype(jnp.int32)).astype(f32).astype(bf16)
        p2 = (scol == rt[:, 1:2].astype(jnp.int32)).astype(f32).astype(bf16)
        y = rt[:, 2:3] * _dot(p1, osv) + rt[:, 3:4] * _dot(p2, osv)
        xr = ALPHA * x_ref[rr, :] + (1.0 + mod_ref[5:6, :]) * y
        out_ref[rr, :] = _layer_norm(xr, lng_ref[...], lnb_ref[...])

    @pl.when(c == nc - 1)
    def _():
        _gather_wait(o_hbm_ref, obuf_ref.at[1 - slot], sem.at[1 - slot], ds_ref, nxt * CHUNK_GRANS, CHUNK_GRANS)


def _combine(dst_of_src, x2d, mod_l, o2d, route_t, ln_g, ln_b, S):
    T = x2d.shape[0]
    W = ROUTE_W
    tpb = S // W
    grid_spec = pltpu.PrefetchScalarGridSpec(
        num_scalar_prefetch=1,
        grid=(T // W,),
        in_specs=[
            pl.BlockSpec((W, D), lambda c, ds: (c, 0)),
            pl.BlockSpec((None, 6, D), lambda c, ds: (c // tpb, 0, 0)),
            pl.BlockSpec(memory_space=pl.ANY),
            pl.BlockSpec((None, W, 8), lambda c, ds: (c, 0, 0)),
            pl.BlockSpec((1, D), lambda c, ds: (0, 0)),
            pl.BlockSpec((1, D), lambda c, ds: (0, 0)),
        ],
        out_specs=pl.BlockSpec((W, D), lambda c, ds: (c, 0)),
        scratch_shapes=[
            pltpu.VMEM((2, CHUNK_SLOTS, D), bf16),
            pltpu.SemaphoreType.DMA((2,)),
        ],
    )
    return pl.pallas_call(
        _combine_kernel,
        grid_spec=grid_spec,
        out_shape=jax.ShapeDtypeStruct((T, D), f32),
        compiler_params=pltpu.CompilerParams(vmem_limit_bytes=VMEM_LIMIT),
        name="combine",
    )(dst_of_src, x2d, mod_l, o2d, route_t, ln_g, ln_b)


def _routing_tables(counts, n_tiles):
    nc = counts.shape[0]
    gr = (counts + GRAN - 1) // GRAN
    seg_start = jnp.cumsum(gr, axis=1) - gr
    chunk_total = jnp.sum(gr, axis=1)
    prefix = jnp.cumsum(gr, axis=0) - gr
    g_e = jnp.sum(gr, axis=0)
    tiles_e = (g_e + TILE_GRANS - 1) // TILE_GRANS
    tile_end = jnp.cumsum(tiles_e)
    tile_start = tile_end - tiles_e
    total_tiles = tile_end[-1]

    i32 = jnp.int32
    er = jnp.arange(N_EXPERTS, dtype=i32)
    t = jnp.arange(n_tiles, dtype=i32)
    te = jnp.sum((t[:, None] >= tile_end[None, :]).astype(i32), axis=1)
    tile_valid = (t < total_tiles).astype(i32)
    last_e = jnp.sum((total_tiles - 1 >= tile_end).astype(i32))
    tile_expert = jnp.where(tile_valid > 0, jnp.minimum(te, N_EXPERTS - 1), last_e).astype(i32)
    oh_t = (tile_expert[:, None] == er).astype(i32)
    grans_left = jnp.sum(oh_t * (g_e - (t[:, None] - tile_start[None, :]) * TILE_GRANS), axis=1)
    tile_subs = tile_valid * jnp.clip((grans_left + SUB_GRANS - 1) // SUB_GRANS, 0, TILE_GRANS // SUB_GRANS)

    k = jnp.arange(CHUNK_GRANS, dtype=i32)
    seg_end = seg_start + gr
    e_of = jnp.sum((k[None, :, None] >= seg_end[:, None, :]).astype(i32), axis=2)
    oh_e = (jnp.minimum(e_of, N_EXPERTS - 1)[:, :, None] == er).astype(i32)
    base = tile_start[None, :] * TILE_GRANS + prefix - seg_start
    dst = jnp.sum(oh_e * base[:, None, :], axis=2) + k[None, :]
    valid_src = k[None, :] < chunk_total[:, None]
    dst_of_src = jnp.where(valid_src, dst, 0).astype(i32).reshape(-1)

    d = jnp.arange(n_tiles * TILE_GRANS, dtype=i32)
    oh_d = (jnp.repeat(tile_expert, TILE_GRANS)[:, None] == er).astype(i32)
    q = d - jnp.sum(oh_d * tile_start[None, :], axis=1) * TILE_GRANS
    incl_d = jnp.sum(oh_d[:, :, None] * (prefix + gr).T[None], axis=1)
    c_d = jnp.sum((q[:, None] >= incl_d).astype(i32), axis=1)
    oh_c = (jnp.minimum(c_d, nc - 1)[:, None] == jnp.arange(nc, dtype=i32)).astype(i32)
    cbase = jnp.arange(nc, dtype=i32)[:, None] * CHUNK_GRANS + seg_start - prefix
    sel = jnp.sum(oh_c[:, :, None] * oh_d[:, None, :] * cbase[None], axis=(1, 2))
    valid_dst = (jnp.repeat(tile_valid, TILE_GRANS) > 0) & (q >= 0) & (q < jnp.sum(oh_d * g_e[None, :], axis=1))
    src_of_dst = jnp.where(valid_dst, sel + q, 0).astype(i32)
    return tile_expert, tile_subs.astype(i32), src_of_dst, dst_of_src


def kernel(x, c, ada_w, ada_b, ln_g, ln_b, ev_w_in, ev_pool_w, ev_pool_scale, ev_sink, ev_w_out, od_w_in, od_sg_ln_g, od_sg_ln_b, od_w_s, od_b_s, od_w_out, ffn_w_gate, ffn_w_up, ffn_w_down, moe_w_router, moe_w_gate, moe_w_up, moe_w_down):
    B, S, _ = x.shape
    T = B * S
    assert x.shape[-1] == D and ada_w.shape == (DEPTH, D, 6 * D) and B <= 8
    assert ev_w_in.shape == (1, D, POOL_WIDTH + ATTN_WIDTH + 2 * KV_WIDTH) and moe_w_gate.shape[:3] == (1, N_EXPERTS, D)
    assert S % (ROUTE_PER_STEP * ROUTE_W) == 0 and moe_w_gate.shape[3] % MOE_TF == 0
    x2d = x.reshape(T, D)
    mod = _adaln(c, ada_w, ada_b)

    w_in = ev_w_in[0].astype(bf16)
    q0, k0, v0 = POOL_WIDTH, POOL_WIDTH + ATTN_WIDTH, POOL_WIDTH + ATTN_WIDTH + KV_WIDTH
    w_pk = jnp.concatenate([w_in[:, :q0], w_in[:, k0:v0]], axis=1)
    w_qv_t = jnp.concatenate([w_in[:, q0:k0], w_in[:, v0:]], axis=1).T
    p, k, qt, vt = _ev_in(x2d, mod[0], w_pk, w_qv_t, S)
    x2d = _ev_mix(x2d, mod[0], p, qt, k, vt, ev_pool_w[0].astype(bf16), ev_pool_scale[0][None, :],
                  ev_sink[0], ev_w_out[0], ln_g[0, 0][None, :], ln_b[0, 0][None, :], B, S)
    x2d = _ffn(x2d, mod[0], ffn_w_gate[0], ffn_w_up[0], ffn_w_down[0],
               ln_g[0, 1][None, :], ln_b[0, 1][None, :], S)

    x2d = _sg_mix(x2d, mod[1], od_w_in[0], od_sg_ln_g[0][None, :], od_sg_ln_b[0][None, :],
                  od_w_s[0].astype(bf16), od_b_s[0].T, od_w_out[0],
                  ln_g[1, 0][None, :], ln_b[1, 0][None, :], S)

    hs, route, cnt = _route(x2d, mod[1], moe_w_router[0].T, S)
    nc = T // ROUTE_W
    n_tiles = (nc * CHUNK_GRANS) // TILE_GRANS + N_EXPERTS
    tile_expert, tile_subs, src_of_dst, dst_of_src = _routing_tables(cnt[:, :, 0], n_tiles)
    o = _experts(tile_expert, tile_subs, src_of_dst, hs.reshape(nc * CHUNK_SLOTS, D),
                 moe_w_gate[0], moe_w_up[0], moe_w_down[0])
    x2d = _combine(dst_of_src, x2d, mod[1], o, jnp.swapaxes(route, 1, 2),
                   ln_g[1, 1][None, :], ln_b[1, 1][None, :], S)
    return x2d.reshape(B, S, D)
```

```python
import functools
import math

import jax
import jax.numpy as jnp
import numpy as np
from jax import lax
from jax.experimental import pallas as pl
from jax.experimental.pallas import tpu as pltpu

D = 1024
DEPTH = 2
ALPHA = (2.0 * DEPTH) ** 0.25
LN_EPS = 1e-5

POOL_WINDOWS = (2, 4, 8, 16)
POOL_CH = 128
POOL_WIDTH = 512
HEAD_DIM = 64
N_Q_HEADS = 8
N_KV_HEADS = 2
Q_GROUP = 4
ATTN_WIDTH = 512
KV_WIDTH = 128
BLOCK = 128
POOL_HALO = 8

CHUNK = 128
SG_GROUPS = 8
SG_CH = 128

N_EXPERTS = 8

ROUTE_W = 512
ROUTE_PER_STEP = 2
GRAN = 16
CHUNK_SLOTS = 2 * ROUTE_W + N_EXPERTS * GRAN
CHUNK_GRANS = CHUNK_SLOTS // GRAN
MOE_SUB = 256
MOE_FAST = 2048
MOE_TM = MOE_FAST + MOE_SUB
TILE_GRANS = MOE_TM // GRAN
SUB_GRANS = MOE_SUB // GRAN
MOE_TF = 512

VMEM_LIMIT = 48 * 1024 * 1024
VMEM_LIMIT_BIG = 56 * 1024 * 1024

bf16 = jnp.bfloat16
f32 = jnp.float32


def _dot(a, b):
    return jnp.dot(a, b, preferred_element_type=f32)


def _split_bf16(a):
    hi = a.astype(bf16)
    lo = (a - hi.astype(f32)).astype(bf16)
    return hi, lo


def _layer_norm(x, g, b):
    mu = jnp.mean(x, axis=-1, keepdims=True)
    xc = x - mu
    var = jnp.mean(xc * xc, axis=-1, keepdims=True)
    return xc * lax.rsqrt(var + LN_EPS) * g + b


def _silu(x):
    return x * jax.nn.sigmoid(x)


def _gelu_tanh(x):
    c = math.sqrt(2.0 / math.pi)
    return x * (0.5 * (1.0 + jnp.tanh(c * (x + 0.044715 * (x * x * x)))))


def _adaln_kernel(c_ref, w_ref, b_ref, o_ref):
    cond = _silu(c_ref[...])
    c_hi, c_lo = _split_bf16(cond)
    w_hi, w_lo = _split_bf16(w_ref[...])
    acc = _dot(c_hi, w_hi) + (_dot(c_lo, w_hi) + _dot(c_hi, w_lo))
    o_ref[...] = acc + b_ref[...]


def _adaln(c, ada_w, ada_b):
    B = c.shape[0]
    tn = 2048
    c_pad = jnp.zeros((8, D), f32).at[:B].set(c)
    out = pl.pallas_call(
        _adaln_kernel,
        grid=(DEPTH, 6 * D // tn),
        in_specs=[
            pl.BlockSpec((8, D), lambda l, j: (0, 0)),
            pl.BlockSpec((None, D, tn), lambda l, j: (l, 0, j)),
            pl.BlockSpec((None, 1, tn), lambda l, j: (l, 0, j)),
        ],
        out_specs=pl.BlockSpec((None, 8, tn), lambda l, j: (l, 0, j)),
        out_shape=jax.ShapeDtypeStruct((DEPTH, 8, 6 * D), f32),
        compiler_params=pltpu.CompilerParams(vmem_limit_bytes=VMEM_LIMIT),
        name="adaln",
    )(c_pad, ada_w, ada_b.reshape(DEPTH, 1, 6 * D))
    return out[:, :B].reshape(DEPTH, B, 6, D)


def _ev_in_kernel(x_ref, mod_ref, wpk_ref, wqvt_ref, p_ref, k_ref, qt_ref, vt_ref):
    h = (x_ref[...] * (1.0 + mod_ref[1:2, :]) + mod_ref[0:1, :]).astype(bf16)
    zpk = _dot(h, wpk_ref[...])
    p_ref[...] = zpk[:, :POOL_WIDTH]
    k_ref[...] = zpk[:, POOL_WIDTH:].astype(bf16)
    zt = lax.dot_general(wqvt_ref[...], h, (((1,), (1,)), ((), ())), preferred_element_type=f32)
    qt_ref[...] = (zt[:ATTN_WIDTH] * (HEAD_DIM ** -0.5)).astype(bf16)
    vt_ref[...] = zt[ATTN_WIDTH:].astype(bf16)


def _ev_in(x2d, mod_l, w_pk, w_qv_t, S):
    T = x2d.shape[0]
    tm = 512
    tpb = S // tm
    return pl.pallas_call(
        _ev_in_kernel,
        grid=(T // tm,),
        in_specs=[
            pl.BlockSpec((tm, D), lambda i: (i, 0)),
            pl.BlockSpec((None, 6, D), lambda i: (i // tpb, 0, 0)),
            pl.BlockSpec((D, POOL_WIDTH + KV_WIDTH), lambda i: (0, 0)),
            pl.BlockSpec((ATTN_WIDTH + KV_WIDTH, D), lambda i: (0, 0)),
        ],
        out_specs=[
            pl.BlockSpec((tm, POOL_WIDTH), lambda i: (i, 0)),
            pl.BlockSpec((tm, KV_WIDTH), lambda i: (i, 0)),
            pl.BlockSpec((ATTN_WIDTH, tm), lambda i: (0, i)),
            pl.BlockSpec((KV_WIDTH, tm), lambda i: (0, i)),
        ],
        out_shape=[
            jax.ShapeDtypeStruct((T, POOL_WIDTH), f32),
            jax.ShapeDtypeStruct((T, KV_WIDTH), bf16),
            jax.ShapeDtypeStruct((ATTN_WIDTH, T), bf16),
            jax.ShapeDtypeStruct((KV_WIDTH, T), bf16),
        ],
        compiler_params=pltpu.CompilerParams(vmem_limit_bytes=VMEM_LIMIT),
        name="ev_in",
    )(x2d, mod_l, w_pk, w_qv_t)


def _ev_mix_kernel(x_ref, mod_ref, p_ref, pp_ref, pn_ref, qt_ref,
                   k_ref, kp_ref, kn_ref, vt_ref, vtp_ref, vtn_ref,
                   bias_ref, sink_ref, wpool_ref, pscale_ref, wout_hbm_ref, lng_ref, lnb_ref,
                   o_ref, pext_ref, lvl_ref, pooled_ref, kext_ref, vext_ref, ybt_ref, mix_ref,
                   wout_ref, stage_ref, sem, *, S, tq):
    i = pl.program_id(1)

    @pl.when((pl.program_id(0) == 0) & (i == 0))
    def _():
        _load_rounded(wout_hbm_ref, wout_ref, stage_ref, sem)

    n_tiles = S // tq
    is_first = i == 0
    is_last = i == n_tiles - 1
    H = POOL_HALO

    p = p_ref[...]
    pext_ref[0:H, :] = jnp.where(is_first, 0.0, pp_ref[...])
    pext_ref[H:H + tq, :] = p
    pext_ref[H + tq:2 * H + tq, :] = jnp.where(is_last, 0.0, pn_ref[...])
    pext_ref[2 * H + tq:, :] = jnp.zeros((pext_ref.shape[0] - 2 * H - tq, POOL_WIDTH), f32)
    near = lax.broadcasted_iota(jnp.int32, (H, 1), 0)

    def src_rows(src, k, off, n, cs):
        return src[off:off + n, cs] if src is pext_ref else src[k - 1, off:off + n, cs]

    def pool_group(g):
        w = POOL_WINDOWS[g]
        cs = slice(g * POOL_CH, (g + 1) * POOL_CH)
        r = w // 2
        src, length, k = pext_ref, pext_ref.shape[0] - H, 0
        while 2 ** k < r:
            step = 2 ** k
            lvl_ref[k, 0:length, cs] = src_rows(src, k, 0, length, cs) + src_rows(src, k, step, length, cs)
            src, length, k = lvl_ref, length - H, k + 1
        half_run = 2 ** k
        wsum = (src_rows(src, k, H - r, tq, cs) + src_rows(src, k, H - r + half_run, tq, cs)
                + pext_ref[H + r:H + r + tq, cs])
        pooled_ref[:, cs] = wsum / float(w + 1) - p[:, cs]
        cnt_head = (jnp.minimum(near, r) + (r + 1)).astype(f32)
        cnt_tail = (jnp.minimum(H - 1 - near, r) + (r + 1)).astype(f32)
        cnt_head = jnp.where(is_first, cnt_head, float(w + 1))
        cnt_tail = jnp.where(is_last, cnt_tail, float(w + 1))
        pooled_ref[0:H, cs] = wsum[0:H] / cnt_head - p[0:H, cs]
        pooled_ref[tq - H:tq, cs] = wsum[tq - H:tq] / cnt_tail - p[tq - H:tq, cs]
        ya = _dot(pooled_ref[:, cs].astype(bf16), wpool_ref[g])
        mix_ref[:, cs] = (ya * pscale_ref[:, cs]).astype(bf16)

    for g in range(len(POOL_WINDOWS)):
        pool_group(g)

    kext_ref[0:BLOCK, :] = kp_ref[...]
    kext_ref[BLOCK:BLOCK + tq, :] = k_ref[...]
    kext_ref[BLOCK + tq:, :] = kn_ref[...]
    vext_ref[:, 0:BLOCK] = vtp_ref[...]
    vext_ref[:, BLOCK:BLOCK + tq] = vt_ref[...]
    vext_ref[:, BLOCK + tq:] = vtn_ref[...]

    n_blocks = S // BLOCK
    zeros_q = jnp.zeros((HEAD_DIM, Q_GROUP * BLOCK), bf16)
    items = [(n, kvh) for n in range(tq // BLOCK) for kvh in range(N_KV_HEADS)]

    def scores(n, kvh):
        gb = i * (tq // BLOCK) + n
        variant = jnp.where(gb == 0, 1, jnp.where(gb == n_blocks - 1, 2, 0))
        cols = slice(n * BLOCK, (n + 1) * BLOCK)
        kw = kext_ref[n * BLOCK:n * BLOCK + 3 * BLOCK, :]
        qst = jnp.concatenate(
            [qt_ref[(kvh * Q_GROUP + gq) * HEAD_DIM:(kvh * Q_GROUP + gq + 1) * HEAD_DIM, cols]
             for gq in range(Q_GROUP)], axis=1)
        qst = jnp.concatenate([qst, zeros_q] if kvh == 0 else [zeros_q, qst], axis=0)
        return _dot(kw, qst) + bias_ref[variant, kvh]

    def softmax(n, kvh, s):
        sink = sink_ref[kvh]
        m = jnp.maximum(jnp.max(s, axis=0, keepdims=True), sink)
        e = jnp.exp(s - m)
        denom = jnp.sum(e, axis=0, keepdims=True) + jnp.exp(sink - m)
        return e.astype(bf16), denom

    def weighted_values(n, kvh, e, denom):
        cols = slice(n * BLOCK, (n + 1) * BLOCK)
        vwt = vext_ref[kvh * HEAD_DIM:(kvh + 1) * HEAD_DIM, n * BLOCK:n * BLOCK + 3 * BLOCK]
        out = _dot(vwt, e) / denom
        for gq in range(Q_GROUP):
            hq = kvh * Q_GROUP + gq
            ybt_ref[hq * HEAD_DIM:(hq + 1) * HEAD_DIM, cols] = out[:, gq * BLOCK:(gq + 1) * BLOCK]

    s_next = scores(*items[0])
    prev = None
    for idx, item in enumerate(items):
        s_cur = s_next
        if idx + 1 < len(items):
            s_next = scores(*items[idx + 1])
        cur = softmax(*item, s_cur)
        if prev is not None:
            weighted_values(*items[idx - 1], *prev)
        prev = cur
    weighted_values(*items[-1], *prev)
    mix_ref[:, POOL_WIDTH:] = ybt_ref[...].T.astype(bf16)

    halves = [slice(r * (tq // 2), (r + 1) * (tq // 2)) for r in range(2)]
    ys = [_dot(mix_ref[rr, :], wout_ref[...]) for rr in halves]
    for rr, y in zip(halves, ys):
        xr = ALPHA * x_ref[rr, :] + (1.0 + mod_ref[2:3, :]) * y
        o_ref[rr, :] = _layer_norm(xr, lng_ref[...], lnb_ref[...])


def _ev_mix(x2d, mod_l, p, qt, k, vt, w_pool, pool_scale, sink, w_out, ln_g, ln_b, B, S):
    T = x2d.shape[0]
    tq = 512
    nt = S // tq
    kb = tq // BLOCK
    pb = tq // POOL_HALO
    n_kblocks = T // BLOCK
    n_pblocks = T // POOL_HALO

    def main(b, i): return (b * nt + i, 0)
    def kprev(b, i): return (jnp.maximum((b * nt + i) * kb - 1, 0), 0)
    def knext(b, i): return (jnp.minimum((b * nt + i + 1) * kb, n_kblocks - 1), 0)
    def pprev(b, i): return (jnp.maximum((b * nt + i) * pb - 1, 0), 0)
    def pnext(b, i): return (jnp.minimum((b * nt + i + 1) * pb, n_pblocks - 1), 0)
    def const2(b, i): return (0, 0)

    def tmain(b, i): return (0, b * nt + i)
    def tprev(b, i): return (0, jnp.maximum((b * nt + i) * kb - 1, 0))
    def tnext(b, i): return (0, jnp.minimum((b * nt + i + 1) * kb, n_kblocks - 1))

    assert S // BLOCK >= 2
    kj = np.arange(3 * BLOCK)[:, None]
    qi = np.arange(BLOCK)[None, :]
    dist = np.abs(kj - BLOCK - qi)
    slopes = np.float32(2.0) ** (np.float32(-8.0) * np.arange(1, N_Q_HEADS + 1, dtype=np.float32) / N_Q_HEADS)
    alibi = -slopes[:, None, None] * dist.astype(np.float32)[None]
    in_window = dist <= BLOCK
    key_ok = np.stack([kj >= 0, kj >= BLOCK, kj < 2 * BLOCK])
    bias = np.where((in_window[None] & key_ok)[:, None], alibi[None], np.float32(-1e30))
    bias = bias.reshape(3, N_KV_HEADS, Q_GROUP, 3 * BLOCK, BLOCK).transpose(0, 1, 3, 2, 4)
    bias = jnp.asarray(bias.reshape(3, N_KV_HEADS, 3 * BLOCK, Q_GROUP * BLOCK), f32)
    sink_row = jnp.repeat(sink.astype(f32).reshape(N_KV_HEADS, Q_GROUP), BLOCK, axis=1)[:, None, :]

    assert tq >= 2 * POOL_HALO and max(POOL_WINDOWS) // 2 <= POOL_HALO and nt >= 2

    kernel = functools.partial(_ev_mix_kernel, S=S, tq=tq)
    return pl.pallas_call(
        kernel,
        grid=(B, nt),
        in_specs=[
            pl.BlockSpec((tq, D), main),
            pl.BlockSpec((None, 6, D), lambda b, i: (b, 0, 0)),
            pl.BlockSpec((tq, POOL_WIDTH), main),
            pl.BlockSpec((POOL_HALO, POOL_WIDTH), pprev),
            pl.BlockSpec((POOL_HALO, POOL_WIDTH), pnext),
            pl.BlockSpec((ATTN_WIDTH, tq), tmain),
            pl.BlockSpec((tq, KV_WIDTH), main),
            pl.BlockSpec((BLOCK, KV_WIDTH), kprev),
            pl.BlockSpec((BLOCK, KV_WIDTH), knext),
            pl.BlockSpec((KV_WIDTH, tq), tmain),
            pl.BlockSpec((KV_WIDTH, BLOCK), tprev),
            pl.BlockSpec((KV_WIDTH, BLOCK), tnext),
            pl.BlockSpec((3, N_KV_HEADS, 3 * BLOCK, Q_GROUP * BLOCK), lambda b, i: (0, 0, 0, 0)),
            pl.BlockSpec((N_KV_HEADS, 1, Q_GROUP * BLOCK), lambda b, i: (0, 0, 0)),
            pl.BlockSpec((len(POOL_WINDOWS), POOL_CH, POOL_CH), lambda b, i: (0, 0, 0)),
            pl.BlockSpec((1, POOL_WIDTH), const2),
            pl.BlockSpec(memory_space=pl.ANY),
            pl.BlockSpec((1, D), const2),
            pl.BlockSpec((1, D), const2),
        ],
        out_specs=pl.BlockSpec((tq, D), main),
        out_shape=jax.ShapeDtypeStruct((T, D), f32),
        scratch_shapes=[
            pltpu.VMEM((tq + 4 * POOL_HALO, POOL_WIDTH), f32),
            pltpu.VMEM((3, tq + 3 * POOL_HALO, POOL_WIDTH), f32),
            pltpu.VMEM((tq, POOL_WIDTH), f32),
            pltpu.VMEM((tq + 2 * BLOCK, KV_WIDTH), bf16),
            pltpu.VMEM((KV_WIDTH, tq + 2 * BLOCK), bf16),
            pltpu.VMEM((ATTN_WIDTH, tq), f32),
            pltpu.VMEM((tq, D), bf16),
            pltpu.VMEM((D, D), bf16),
            pltpu.VMEM((2, D // 8, D), f32),
            pltpu.SemaphoreType.DMA((2,)),
        ],
        compiler_params=pltpu.CompilerParams(vmem_limit_bytes=VMEM_LIMIT),
        name="ev_mix",
    )(x2d, mod_l, p, p, p, qt, k, k, k, vt, vt, vt, bias, sink_row, w_pool, pool_scale, w_out, ln_g, ln_b)


def _load_rounded(w_hbm_ref, w_ref, stage_ref, sem):
    rows = stage_ref.shape[1]
    n = w_hbm_ref.shape[0] // rows

    def copy(c):
        return pltpu.make_async_copy(w_hbm_ref.at[pl.ds(c * rows, rows), :], stage_ref.at[c % 2], sem.at[c % 2])

    copy(0).start()
    for c in range(n):
        if c + 1 < n:
            copy(c + 1).start()
        copy(c).wait()
        w_ref[pl.ds(c * rows, rows), :] = stage_ref[c % 2].astype(bf16)


def _ffn_kernel(x_ref, mod_ref, wg_hbm_ref, wu_hbm_ref, wd_hbm_ref, lng_ref, lnb_ref, o_ref,
                wg_ref, wu_ref, wd_ref, stage_in_ref, stage_out_ref, sem):
    @pl.when(pl.program_id(0) == 0)
    def _():
        _load_rounded(wg_hbm_ref, wg_ref, stage_in_ref, sem)
        _load_rounded(wu_hbm_ref, wu_ref, stage_in_ref, sem)
        _load_rounded(wd_hbm_ref, wd_ref, stage_out_ref, sem)

    n_pieces = 2
    piece = x_ref.shape[0] // n_pieces
    parts = [pl.ds(q * piece, piece) for q in range(n_pieces)]

    def gate_up(rows):
        x = x_ref[rows, :]
        h = (x * (1.0 + mod_ref[4:5, :]) + mod_ref[3:4, :]).astype(bf16)
        return x, _dot(h, wg_ref[...]), _dot(h, wu_ref[...])

    nxt = gate_up(parts[0])
    for q, rows in enumerate(parts):
        x, g, u = nxt
        if q + 1 < n_pieces:
            nxt = gate_up(parts[q + 1])
        y = _dot((_silu(g) * u).astype(bf16), wd_ref[...])
        xr = ALPHA * x + (1.0 + mod_ref[5:6, :]) * y
        o_ref[rows, :] = _layer_norm(xr, lng_ref[...], lnb_ref[...])


def _ffn(x2d, mod_l, w_gate, w_up, w_down, ln_g, ln_b, S):
    T = x2d.shape[0]
    tm = 512
    dff = w_gate.shape[1]
    tpb = S // tm
    n_stage = 8
    return pl.pallas_call(
        _ffn_kernel,
        grid=(T // tm,),
        in_specs=[
            pl.BlockSpec((tm, D), lambda i: (i, 0)),
            pl.BlockSpec((None, 6, D), lambda i: (i // tpb, 0, 0)),
            pl.BlockSpec(memory_space=pl.ANY),
            pl.BlockSpec(memory_space=pl.ANY),
            pl.BlockSpec(memory_space=pl.ANY),
            pl.BlockSpec((1, D), lambda i: (0, 0)),
            pl.BlockSpec((1, D), lambda i: (0, 0)),
        ],
        out_specs=pl.BlockSpec((tm, D), lambda i: (i, 0)),
        out_shape=jax.ShapeDtypeStruct((T, D), f32),
        scratch_shapes=[
            pltpu.VMEM((D, dff), bf16),
            pltpu.VMEM((D, dff), bf16),
            pltpu.VMEM((dff, D), bf16),
            pltpu.VMEM((2, D // n_stage, dff), f32),
            pltpu.VMEM((2, dff // n_stage, D), f32),
            pltpu.SemaphoreType.DMA((2,)),
        ],
        compiler_params=pltpu.CompilerParams(vmem_limit_bytes=VMEM_LIMIT_BIG),
        name="ffn",
    )(x2d, mod_l, w_gate, w_up, w_down, ln_g, ln_b)


def _sg_kernel(x_ref, mod_ref, win_hbm_ref, sgg_ref, sgb_ref, ws_ref, bst_ref, wout_hbm_ref, lng_ref, lnb_ref,
               o_ref, gate_ref, win_ref, wout_ref, stage_in_ref, stage_out_ref, sem, *, tm):
    @pl.when(pl.program_id(0) == 0)
    def _():
        _load_rounded(win_hbm_ref, win_ref, stage_in_ref, sem)
        _load_rounded(wout_hbm_ref, wout_ref, stage_out_ref, sem)

    n_pieces = 2
    piece = tm // n_pieces

    def project(q):
        x = x_ref[q * piece:(q + 1) * piece, :]
        h = (x * (1.0 + mod_ref[1:2, :]) + mod_ref[0:1, :]).astype(bf16)
        return x, _dot(h, win_ref[:, D:]), _dot(h, win_ref[:, :D])

    nxt = project(0)
    for q in range(n_pieces):
        x, zv, zu = nxt
        if q + 1 < n_pieces:
            nxt = project(q + 1)
        v = _layer_norm(_gelu_tanh(zv), sgg_ref[...], sgb_ref[...]).astype(bf16)
        u = _gelu_tanh(zu)
        for n in range(piece // CHUNK):
            rows = slice(n * CHUNK, (n + 1) * CHUNK)
            grows = slice(q * piece + n * CHUNK, q * piece + (n + 1) * CHUNK)
            for g in range(SG_GROUPS):
                cols = slice(g * SG_CH, (g + 1) * SG_CH)
                sv = _dot(ws_ref[g], v[rows, cols]) + bst_ref[:, g:g + 1]
                gate_ref[grows, cols] = (u[rows, cols] * sv).astype(bf16)
        y = _dot(gate_ref[q * piece:(q + 1) * piece, :], wout_ref[...])
        xr = ALPHA * x + (1.0 + mod_ref[2:3, :]) * y
        o_ref[q * piece:(q + 1) * piece, :] = _layer_norm(xr, lng_ref[...], lnb_ref[...])


def _sg_mix(x2d, mod_l, w_in, sg_g, sg_b, w_s, b_s_t, w_out, ln_g, ln_b, S):
    T = x2d.shape[0]
    tm = 512
    tpb = S // tm
    c2 = lambda i: (0, 0)
    n_stage = 8
    return pl.pallas_call(
        functools.partial(_sg_kernel, tm=tm),
        grid=(T // tm,),
        in_specs=[
            pl.BlockSpec((tm, D), lambda i: (i, 0)),
            pl.BlockSpec((None, 6, D), lambda i: (i // tpb, 0, 0)),
            pl.BlockSpec(memory_space=pl.ANY),
            pl.BlockSpec((1, D), c2),
            pl.BlockSpec((1, D), c2),
            pl.BlockSpec((SG_GROUPS, CHUNK, CHUNK), lambda i: (0, 0, 0)),
            pl.BlockSpec((CHUNK, SG_GROUPS), c2),
            pl.BlockSpec(memory_space=pl.ANY),
            pl.BlockSpec((1, D), c2),
            pl.BlockSpec((1, D), c2),
        ],
        out_specs=pl.BlockSpec((tm, D), lambda i: (i, 0)),
        out_shape=jax.ShapeDtypeStruct((T, D), f32),
        scratch_shapes=[
            pltpu.VMEM((tm, D), bf16),
            pltpu.VMEM((D, 2 * D), bf16),
            pltpu.VMEM((D, D), bf16),
            pltpu.VMEM((2, D // n_stage, 2 * D), f32),
            pltpu.VMEM((2, D // n_stage, D), f32),
            pltpu.SemaphoreType.DMA((2,)),
        ],
        compiler_params=pltpu.CompilerParams(vmem_limit_bytes=VMEM_LIMIT),
        name="sg_mix",
    )(x2d, mod_l, w_in, sg_g, sg_b, w_s, b_s_t, w_out, ln_g, ln_b)


def _route_kernel(x_ref, mod_ref, wrt_ref, hs_ref, route_ref, cnt_ref):
    W = ROUTE_W
    w_hi, w_lo = _split_bf16(wrt_ref[...])
    nt = (((1,), (1,)), ((), ()))
    eidx = lax.broadcasted_iota(jnp.int32, (N_EXPERTS, W), 0)
    sub = lax.broadcasted_iota(jnp.int32, (N_EXPERTS, 1), 0)
    tr = lax.broadcasted_iota(jnp.int32, (W, W), 0)
    tc = lax.broadcasted_iota(jnp.int32, (W, W), 1)
    upper = (tr < tc).astype(bf16)
    srow = lax.broadcasted_iota(jnp.int32, (CHUNK_SLOTS, W), 0)
    ridx = lax.broadcasted_iota(jnp.int32, (8, W), 0)

    def assign_slots(k):
        h = x_ref[k * W:(k + 1) * W, :] * (1.0 + mod_ref[4:5, :]) + mod_ref[3:4, :]
        h_hi, h_lo = _split_bf16(h)
        logits = (lax.dot_general(w_hi, h_hi, nt, preferred_element_type=f32)
                  + (lax.dot_general(w_hi, h_lo, nt, preferred_element_type=f32)
                     + lax.dot_general(w_lo, h_hi, nt, preferred_element_type=f32)))
        m1 = jnp.max(logits, axis=0, keepdims=True)
        i1 = jnp.min(jnp.where(logits == m1, eidx, N_EXPERTS), axis=0, keepdims=True)
        sel1 = eidx == i1
        rest = jnp.where(sel1, -jnp.inf, logits)
        m2 = jnp.max(rest, axis=0, keepdims=True)
        i2 = jnp.min(jnp.where(rest == m2, eidx, N_EXPERTS), axis=0, keepdims=True)
        sel2 = eidx == i2
        e2 = jnp.exp(m2 - m1)
        g1 = 1.0 / (1.0 + e2)
        g2 = e2 / (1.0 + e2)

        a1 = sel1.astype(f32)
        a2 = sel2.astype(f32)
        assign = a1 + a2
        counts = jnp.sum(assign, axis=1, keepdims=True)
        grans = jnp.ceil(counts * (1.0 / GRAN))
        seg = jnp.zeros((N_EXPERTS, 1), f32)
        for e in range(N_EXPERTS - 1):
            seg = seg + jnp.where(sub > e, grans[e:e + 1, :] * GRAN, 0.0)
        rank = _dot(assign.astype(bf16), upper)
        slot = seg + rank
        pos1 = jnp.sum(a1 * slot, axis=0, keepdims=True)
        pos2 = jnp.sum(a2 * slot, axis=0, keepdims=True)
        route_ref[k] = jnp.where(ridx == 0, pos1, jnp.where(ridx == 1, pos2,
                                 jnp.where(ridx == 2, g1, jnp.where(ridx == 3, g2, 0.0))))
        cnt_ref[k] = jnp.broadcast_to(counts, (N_EXPERTS, 128)).astype(jnp.int32)
        return h_hi, pos1, pos2

    def sort_rows(k, h_hi, pos1, pos2):
        perm = ((srow == pos1.astype(jnp.int32)) | (srow == pos2.astype(jnp.int32)))
        hs_ref[k] = _dot(perm.astype(f32).astype(bf16), h_hi).astype(bf16)

    nxt = assign_slots(0)
    for k in range(ROUTE_PER_STEP):
        cur = nxt
        if k + 1 < ROUTE_PER_STEP:
            nxt = assign_slots(k + 1)
        sort_rows(k, *cur)


def _route(x2d, mod_l, w_router_t, S):
    T = x2d.shape[0]
    W = ROUTE_W
    R = ROUTE_PER_STEP
    nc = T // W
    tpb = S // (R * W)
    return pl.pallas_call(
        _route_kernel,
        grid=(nc // R,),
        in_specs=[
            pl.BlockSpec((R * W, D), lambda c: (c, 0)),
            pl.BlockSpec((None, 6, D), lambda c: (c // tpb, 0, 0)),
            pl.BlockSpec((N_EXPERTS, D), lambda c: (0, 0)),
        ],
        out_specs=[
            pl.BlockSpec((R, CHUNK_SLOTS, D), lambda c: (c, 0, 0)),
            pl.BlockSpec((R, 8, W), lambda c: (c, 0, 0)),
            pl.BlockSpec((R, N_EXPERTS, 128), lambda c: (c, 0, 0)),
        ],
        out_shape=[
            jax.ShapeDtypeStruct((nc, CHUNK_SLOTS, D), bf16),
            jax.ShapeDtypeStruct((nc, 8, W), f32),
            jax.ShapeDtypeStruct((nc, N_EXPERTS, 128), jnp.int32),
        ],
        compiler_params=pltpu.CompilerParams(vmem_limit_bytes=VMEM_LIMIT),
        name="route",
    )(x2d, mod_l, w_router_t)


def _granule_copy(src_ref, buf_ref, sem, idx_ref, base, g):
    row = pl.multiple_of(idx_ref[base + g] * GRAN, GRAN)
    return pltpu.make_async_copy(src_ref.at[pl.ds(row, GRAN), :], buf_ref.at[pl.ds(g * GRAN, GRAN), :], sem)


def _gather_start(src_ref, buf_ref, sem, idx_ref, base, n):
    for g in range(n):
        _granule_copy(src_ref, buf_ref, sem, idx_ref, base, g).start()


def _gather_wait(src_ref, buf_ref, sem, idx_ref, base, n):
    for g in range(n):
        _granule_copy(src_ref, buf_ref, sem, idx_ref, base, g).wait()


def _expert_kernel(te_ref, tv_ref, src_ref, hs_ref, wg_ref, wu_ref, wd_ref, o_ref,
                   xbuf_ref, sem, acc_ref, wgb_ref, wub_ref, wdb_ref):
    i = pl.program_id(0)
    j = pl.program_id(1)
    n_tiles = pl.num_programs(0)
    last = pl.num_programs(1) - 1
    valid = tv_ref[i] > 0
    slot = i % 2
    nxt = jnp.minimum(i + 1, n_tiles - 1)

    @pl.when(j == 0)
    def _():
        @pl.when(i == 0)
        def _():
            _gather_start(hs_ref, xbuf_ref.at[0], sem.at[0], src_ref, 0, TILE_GRANS)

        @pl.when(valid)
        def _():
            _gather_wait(hs_ref, xbuf_ref.at[slot], sem.at[slot], src_ref, i * TILE_GRANS, TILE_GRANS)

        @pl.when((i + 1 < n_tiles) & (tv_ref[nxt] > 0))
        def _():
            _gather_start(hs_ref, xbuf_ref.at[1 - slot], sem.at[1 - slot], src_ref, nxt * TILE_GRANS, TILE_GRANS)

    n_sub = tv_ref[i]
    fast_subs = MOE_FAST // MOE_SUB
    fast = n_sub >= fast_subs

    def round_weights():
        wgb_ref[...] = wg_ref[...].astype(bf16)
        wub_ref[...] = wu_ref[...].astype(bf16)
        wdb_ref[...] = wd_ref[...].astype(bf16)

    def swiglu(x):
        a = _silu(_dot(x, wgb_ref[...])) * _dot(x, wub_ref[...])
        return _dot(a.astype(bf16), wdb_ref[...])

    sparse = valid & jnp.logical_not(fast)

    @pl.when((sparse | (i == 0)) & (j == 0))
    def _():
        acc_ref[...] = jnp.zeros_like(acc_ref)

    def dense_block(n_pieces):
        round_weights()
        parts = [pl.ds(q * MOE_SUB, MOE_SUB) for q in range(n_pieces)]

        def gate_up(rows):
            x = xbuf_ref[slot, rows, :]
            return _dot(x, wgb_ref[...]), _dot(x, wub_ref[...])

        nxt_gu = gate_up(parts[0])
        for q, rows in enumerate(parts):
            g, u = nxt_gu
            if q + 1 < len(parts):
                nxt_gu = gate_up(parts[q + 1])
            y = _dot((_silu(g) * u).astype(bf16), wdb_ref[...])
            total = jnp.where(j == 0, y, acc_ref[rows, :] + y)
            acc_ref[rows, :] = total
            o_ref[rows, :] = total.astype(o_ref.dtype)
        if n_pieces * MOE_SUB < MOE_TM:
            o_ref[n_pieces * MOE_SUB:, :] = jnp.zeros((MOE_TM - n_pieces * MOE_SUB, D), o_ref.dtype)

    for n_dense in (MOE_TM // MOE_SUB, fast_subs):
        pl.when(n_sub == n_dense)(functools.partial(dense_block, n_dense))

    @pl.when(sparse)
    def _():
        round_weights()

        def body(sb, carry):
            rows = pl.ds(pl.multiple_of(sb * MOE_SUB, MOE_SUB), MOE_SUB)
            acc_ref[rows, :] += swiglu(xbuf_ref[slot, rows, :])
            return carry

        lax.fori_loop(0, n_sub, body, 0)

    @pl.when(j == last)
    def _():
        @pl.when(sparse)
        def _():
            o_ref[...] = acc_ref[...].astype(o_ref.dtype)

        @pl.when(jnp.logical_not(valid))
        def _():
            o_ref[...] = jnp.zeros_like(o_ref)


def _experts(tile_expert, tile_valid, src_of_dst, hs2d, w_gate, w_up, w_down):
    n_tiles = tile_expert.shape[0]
    dff = w_gate.shape[2]
    nff = dff // MOE_TF

    def jj(j, tv, i):
        return jnp.where(tv[i] > 0, j, nff - 1)

    grid_spec = pltpu.PrefetchScalarGridSpec(
        num_scalar_prefetch=3,
        grid=(n_tiles, nff),
        in_specs=[
            pl.BlockSpec(memory_space=pl.ANY),
            pl.BlockSpec((None, D, MOE_TF), lambda i, j, te, tv, sd: (te[i], 0, jj(j, tv, i))),
            pl.BlockSpec((None, D, MOE_TF), lambda i, j, te, tv, sd: (te[i], 0, jj(j, tv, i))),
            pl.BlockSpec((None, MOE_TF, D), lambda i, j, te, tv, sd: (te[i], jj(j, tv, i), 0)),
        ],
        out_specs=pl.BlockSpec((MOE_TM, D), lambda i, j, te, tv, sd: (i, 0)),
        scratch_shapes=[
            pltpu.VMEM((2, MOE_TM, D), bf16),
            pltpu.SemaphoreType.DMA((2,)),
            pltpu.VMEM((MOE_TM, D), f32),
            pltpu.VMEM((D, MOE_TF), bf16),
            pltpu.VMEM((D, MOE_TF), bf16),
            pltpu.VMEM((MOE_TF, D), bf16),
        ],
    )
    return pl.pallas_call(
        _expert_kernel,
        grid_spec=grid_spec,
        out_shape=jax.ShapeDtypeStruct((n_tiles * MOE_TM, D), bf16),
        compiler_params=pltpu.CompilerParams(vmem_limit_bytes=VMEM_LIMIT_BIG),
        name="experts",
    )(tile_expert, tile_valid, src_of_dst, hs2d, w_gate, w_up, w_down)


def _combine_kernel(ds_ref, x_ref, mod_ref, o_hbm_ref, rt_ref, lng_ref, lnb_ref, out_ref, obuf_ref, sem):
    W = ROUTE_W
    c = pl.program_id(0)
    nc = pl.num_programs(0)
    slot = c % 2
    nxt = jnp.minimum(c + 1, nc - 1)

    @pl.when(c == 0)
    def _():
        _gather_start(o_hbm_ref, obuf_ref.at[0], sem.at[0], ds_ref, 0, CHUNK_GRANS)

    _gather_wait(o_hbm_ref, obuf_ref.at[slot], sem.at[slot], ds_ref, c * CHUNK_GRANS, CHUNK_GRANS)

    _gather_start(o_hbm_ref, obuf_ref.at[1 - slot], sem.at[1 - slot], ds_ref, nxt * CHUNK_GRANS, CHUNK_GRANS)

    osv = obuf_ref[slot]
    scol = lax.broadcasted_iota(jnp.int32, (W // 2, CHUNK_SLOTS), 1)
    for r in range(2):
        rr = slice(r * (W // 2), (r + 1) * (W // 2))
        rt = rt_ref[rr, :]
        p1 = (scol == rt[:, 0:1].astype(jnp.int32)).astype(f32).astype(bf16)
        p2 = (scol == rt[:, 1:2].astype(jnp.int32)).astype(f32).astype(bf16)
        y = rt[:, 2:3] * _dot(p1, osv) + rt[:, 3:4] * _dot(p2, osv)
        xr = ALPHA * x_ref[rr, :] + (1.0 + mod_ref[5:6, :]) * y
        out_ref[rr, :] = _layer_norm(xr, lng_ref[...], lnb_ref[...])

    @pl.when(c == nc - 1)
    def _():
        _gather_wait(o_hbm_ref, obuf_ref.at[1 - slot], sem.at[1 - slot], ds_ref, nxt * CHUNK_GRANS, CHUNK_GRANS)


def _combine(dst_of_src, x2d, mod_l, o2d, route_t, ln_g, ln_b, S):
    T = x2d.shape[0]
    W = ROUTE_W
    tpb = S // W
    grid_spec = pltpu.PrefetchScalarGridSpec(
        num_scalar_prefetch=1,
        grid=(T // W,),
        in_specs=[
            pl.BlockSpec((W, D), lambda c, ds: (c, 0)),
            pl.BlockSpec((None, 6, D), lambda c, ds: (c // tpb, 0, 0)),
            pl.BlockSpec(memory_space=pl.ANY),
            pl.BlockSpec((None, W, 8), lambda c, ds: (c, 0, 0)),
            pl.BlockSpec((1, D), lambda c, ds: (0, 0)),
            pl.BlockSpec((1, D), lambda c, ds: (0, 0)),
        ],
        out_specs=pl.BlockSpec((W, D), lambda c, ds: (c, 0)),
        scratch_shapes=[
            pltpu.VMEM((2, CHUNK_SLOTS, D), bf16),
            pltpu.SemaphoreType.DMA((2,)),
        ],
    )
    return pl.pallas_call(
        _combine_kernel,
        grid_spec=grid_spec,
        out_shape=jax.ShapeDtypeStruct((T, D), f32),
        compiler_params=pltpu.CompilerParams(vmem_limit_bytes=VMEM_LIMIT),
        name="combine",
    )(dst_of_src, x2d, mod_l, o2d, route_t, ln_g, ln_b)


def _routing_tables(counts, n_tiles):
    nc = counts.shape[0]
    gr = (counts + GRAN - 1) // GRAN
    seg_start = jnp.cumsum(gr, axis=1) - gr
    chunk_total = jnp.sum(gr, axis=1)
    prefix = jnp.cumsum(gr, axis=0) - gr
    g_e = jnp.sum(gr, axis=0)
    tiles_e = (g_e + TILE_GRANS - 1) // TILE_GRANS
    tile_end = jnp.cumsum(tiles_e)
    tile_start = tile_end - tiles_e
    total_tiles = tile_end[-1]

    i32 = jnp.int32
    er = jnp.arange(N_EXPERTS, dtype=i32)
    t = jnp.arange(n_tiles, dtype=i32)
    te = jnp.sum((t[:, None] >= tile_end[None, :]).astype(i32), axis=1)
    tile_valid = (t < total_tiles).astype(i32)
    last_e = jnp.sum((total_tiles - 1 >= tile_end).astype(i32))
    tile_expert = jnp.where(tile_valid > 0, jnp.minimum(te, N_EXPERTS - 1), last_e).astype(i32)
    oh_t = (tile_expert[:, None] == er).astype(i32)
    grans_left = jnp.sum(oh_t * (g_e - (t[:, None] - tile_start[None, :]) * TILE_GRANS), axis=1)
    tile_subs = tile_valid * jnp.clip((grans_left + SUB_GRANS - 1) // SUB_GRANS, 0, TILE_GRANS // SUB_GRANS)

    k = jnp.arange(CHUNK_GRANS, dtype=i32)
    seg_end = seg_start + gr
    e_of = jnp.sum((k[None, :, None] >= seg_end[:, None, :]).astype(i32), axis=2)
    oh_e = (jnp.minimum(e_of, N_EXPERTS - 1)[:, :, None] == er).astype(i32)
    base = tile_start[None, :] * TILE_GRANS + prefix - seg_start
    dst = jnp.sum(oh_e * base[:, None, :], axis=2) + k[None, :]
    valid_src = k[None, :] < chunk_total[:, None]
    dst_of_src = jnp.where(valid_src, dst, 0).astype(i32).reshape(-1)

    d = jnp.arange(n_tiles * TILE_GRANS, dtype=i32)
    oh_d = (jnp.repeat(tile_expert, TILE_GRANS)[:, None] == er).astype(i32)
    q = d - jnp.sum(oh_d * tile_start[None, :], axis=1) * TILE_GRANS
    incl_d = jnp.sum(oh_d[:, :, None] * (prefix + gr).T[None], axis=1)
    c_d = jnp.sum((q[:, None] >= incl_d).astype(i32), axis=1)
    oh_c = (jnp.minimum(c_d, nc - 1)[:, None] == jnp.arange(nc, dtype=i32)).astype(i32)
    cbase = jnp.arange(nc, dtype=i32)[:, None] * CHUNK_GRANS + seg_start - prefix
    sel = jnp.sum(oh_c[:, :, None] * oh_d[:, None, :] * cbase[None], axis=(1, 2))
    valid_dst = (jnp.repeat(tile_valid, TILE_GRANS) > 0) & (q >= 0) & (q < jnp.sum(oh_d * g_e[None, :], axis=1))
    src_of_dst = jnp.where(valid_dst, sel + q, 0).astype(i32)
    return tile_expert, tile_subs.astype(i32), src_of_dst, dst_of_src


def kernel(x, c, ada_w, ada_b, ln_g, ln_b, ev_w_in, ev_pool_w, ev_pool_scale, ev_sink, ev_w_out, od_w_in, od_sg_ln_g, od_sg_ln_b, od_w_s, od_b_s, od_w_out, ffn_w_gate, ffn_w_up, ffn_w_down, moe_w_router, moe_w_gate, moe_w_up, moe_w_down):
    B, S, _ = x.shape
    T = B * S
    assert x.shape[-1] == D and ada_w.shape == (DEPTH, D, 6 * D) and B <= 8
    assert ev_w_in.shape == (1, D, POOL_WIDTH + ATTN_WIDTH + 2 * KV_WIDTH) and moe_w_gate.shape[:3] == (1, N_EXPERTS, D)
    assert S % (ROUTE_PER_STEP * ROUTE_W) == 0 and moe_w_gate.shape[3] % MOE_TF == 0
    x2d = x.reshape(T, D)
    mod = _adaln(c, ada_w, ada_b)

    w_in = ev_w_in[0].astype(bf16)
    q0, k0, v0 = POOL_WIDTH, POOL_WIDTH + ATTN_WIDTH, POOL_WIDTH + ATTN_WIDTH + KV_WIDTH
    w_pk = jnp.concatenate([w_in[:, :q0], w_in[:, k0:v0]], axis=1)
    w_qv_t = jnp.concatenate([w_in[:, q0:k0], w_in[:, v0:]], axis=1).T
    p, k, qt, vt = _ev_in(x2d, mod[0], w_pk, w_qv_t, S)
    x2d = _ev_mix(x2d, mod[0], p, qt, k, vt, ev_pool_w[0].astype(bf16), ev_pool_scale[0][None, :],
                  ev_sink[0], ev_w_out[0], ln_g[0, 0][None, :], ln_b[0, 0][None, :], B, S)
    x2d = _ffn(x2d, mod[0], ffn_w_gate[0], ffn_w_up[0], ffn_w_down[0],
               ln_g[0, 1][None, :], ln_b[0, 1][None, :], S)

    x2d = _sg_mix(x2d, mod[1], od_w_in[0], od_sg_ln_g[0][None, :], od_sg_ln_b[0][None, :],
                  od_w_s[0].astype(bf16), od_b_s[0].T, od_w_out[0],
                  ln_g[1, 0][None, :], ln_b[1, 0][None, :], S)

    hs, route, cnt = _route(x2d, mod[1], moe_w_router[0].T, S)
    nc = T // ROUTE_W
    n_tiles = (nc * CHUNK_GRANS) // TILE_GRANS + N_EXPERTS
    tile_expert, tile_subs, src_of_dst, dst_of_src = _routing_tables(cnt[:, :, 0], n_tiles)
    o = _experts(tile_expert, tile_subs, src_of_dst, hs.reshape(nc * CHUNK_SLOTS, D),
                 moe_w_gate[0], moe_w_up[0], moe_w_down[0])
    x2d = _combine(dst_of_src, x2d, mod[1], o, jnp.swapaxes(route, 1, 2),
                   ln_g[1, 1][None, :], ln_b[1, 1][None, :], S)
    return x2d.reshape(B, S, D)
```

```python
import functools
import math

import jax
import jax.numpy as jnp
import numpy as np
from jax import lax
from jax.experimental import pallas as pl
from jax.experimental.pallas import tpu as pltpu

D = 1024
DEPTH = 2
ALPHA = (2.0 * DEPTH) ** 0.25
LN_EPS = 1e-5

POOL_WINDOWS = (2, 4, 8, 16)
POOL_CH = 128
POOL_WIDTH = 512
HEAD_DIM = 64
N_Q_HEADS = 8
N_KV_HEADS = 2
Q_GROUP = 4
ATTN_WIDTH = 512
KV_WIDTH = 128
BLOCK = 128
POOL_HALO = 8

CHUNK = 128
SG_GROUPS = 8
SG_CH = 128

N_EXPERTS = 8

ROUTE_W = 512
ROUTE_PER_STEP = 2
GRAN = 16
CHUNK_SLOTS = 2 * ROUTE_W + N_EXPERTS * GRAN
CHUNK_GRANS = CHUNK_SLOTS // GRAN
MOE_SUB = 256
MOE_FAST = 2048
MOE_TM = MOE_FAST + MOE_SUB
TILE_GRANS = MOE_TM // GRAN
SUB_GRANS = MOE_SUB // GRAN
MOE_TF = 512

VMEM_LIMIT = 48 * 1024 * 1024
VMEM_LIMIT_BIG = 56 * 1024 * 1024

bf16 = jnp.bfloat16
f32 = jnp.float32


def _dot(a, b):
    return jnp.dot(a, b, preferred_element_type=f32)


def _split_bf16(a):
    hi = a.astype(bf16)
    lo = (a - hi.astype(f32)).astype(bf16)
    return hi, lo


def _layer_norm(x, g, b):
    mu = jnp.mean(x, axis=-1, keepdims=True)
    xc = x - mu
    var = jnp.mean(xc * xc, axis=-1, keepdims=True)
    return xc * lax.rsqrt(var + LN_EPS) * g + b


def _silu(x):
    return x * jax.nn.sigmoid(x)


def _gelu_tanh(x):
    c = math.sqrt(2.0 / math.pi)
    return x * (0.5 * (1.0 + jnp.tanh(c * (x + 0.044715 * (x * x * x)))))


def _adaln_kernel(c_ref, w_ref, b_ref, o_ref):
    cond = _silu(c_ref[...])
    c_hi, c_lo = _split_bf16(cond)
    w_hi, w_lo = _split_bf16(w_ref[...])
    acc = _dot(c_hi, w_hi) + (_dot(c_lo, w_hi) + _dot(c_hi, w_lo))
    o_ref[...] = acc + b_ref[...]


def _adaln(c, ada_w, ada_b):
    B = c.shape[0]
    tn = 2048
    c_pad = jnp.zeros((8, D), f32).at[:B].set(c)
    out = pl.pallas_call(
        _adaln_kernel,
        grid=(DEPTH, 6 * D // tn),
        in_specs=[
            pl.BlockSpec((8, D), lambda l, j: (0, 0)),
            pl.BlockSpec((None, D, tn), lambda l, j: (l, 0, j)),
            pl.BlockSpec((None, 1, tn), lambda l, j: (l, 0, j)),
        ],
        out_specs=pl.BlockSpec((None, 8, tn), lambda l, j: (l, 0, j)),
        out_shape=jax.ShapeDtypeStruct((DEPTH, 8, 6 * D), f32),
        compiler_params=pltpu.CompilerParams(vmem_limit_bytes=VMEM_LIMIT),
        name="adaln",
    )(c_pad, ada_w, ada_b.reshape(DEPTH, 1, 6 * D))
    return out[:, :B].reshape(DEPTH, B, 6, D)


def _ev_in_kernel(x_ref, mod_ref, wpk_ref, wqvt_ref, p_ref, k_ref, qt_ref, vt_ref):
    h = (x_ref[...] * (1.0 + mod_ref[1:2, :]) + mod_ref[0:1, :]).astype(bf16)
    zpk = _dot(h, wpk_ref[...])
    p_ref[...] = zpk[:, :POOL_WIDTH]
    k_ref[...] = zpk[:, POOL_WIDTH:].astype(bf16)
    zt = lax.dot_general(wqvt_ref[...], h, (((1,), (1,)), ((), ())), preferred_element_type=f32)
    qt_ref[...] = (zt[:ATTN_WIDTH] * (HEAD_DIM ** -0.5)).astype(bf16)
    vt_ref[...] = zt[ATTN_WIDTH:].astype(bf16)


def _ev_in(x2d, mod_l, w_pk, w_qv_t, S):
    T = x2d.shape[0]
    tm = 1024
    tpb = S // tm
    return pl.pallas_call(
        _ev_in_kernel,
        grid=(T // tm,),
        in_specs=[
            pl.BlockSpec((tm, D), lambda i: (i, 0)),
            pl.BlockSpec((None, 6, D), lambda i: (i // tpb, 0, 0)),
            pl.BlockSpec((D, POOL_WIDTH + KV_WIDTH), lambda i: (0, 0)),
            pl.BlockSpec((ATTN_WIDTH + KV_WIDTH, D), lambda i: (0, 0)),
        ],
        out_specs=[
            pl.BlockSpec((tm, POOL_WIDTH), lambda i: (i, 0)),
            pl.BlockSpec((tm, KV_WIDTH), lambda i: (i, 0)),
            pl.BlockSpec((ATTN_WIDTH, tm), lambda i: (0, i)),
            pl.BlockSpec((KV_WIDTH, tm), lambda i: (0, i)),
        ],
        out_shape=[
            jax.ShapeDtypeStruct((T, POOL_WIDTH), f32),
            jax.ShapeDtypeStruct((T, KV_WIDTH), bf16),
            jax.ShapeDtypeStruct((ATTN_WIDTH, T), bf16),
            jax.ShapeDtypeStruct((KV_WIDTH, T), bf16),
        ],
        compiler_params=pltpu.CompilerParams(vmem_limit_bytes=VMEM_LIMIT),
        name="ev_in",
    )(x2d, mod_l, w_pk, w_qv_t)


def _ev_mix_kernel(x_ref, mod_ref, p_ref, pp_ref, pn_ref, qt_ref,
                   k_ref, kp_ref, kn_ref, vt_ref, vtp_ref, vtn_ref,
                   bias_ref, sink_ref, wpool_ref, pscale_ref, wout_hbm_ref, lng_ref, lnb_ref,
                   o_ref, pext_ref, lvl_ref, pooled_ref, kext_ref, vext_ref, ybt_ref, mix_ref,
                   wout_ref, stage_ref, sem, *, S, tq):
    i = pl.program_id(1)

    @pl.when((pl.program_id(0) == 0) & (i == 0))
    def _():
        _load_rounded(wout_hbm_ref, wout_ref, stage_ref, sem)

    n_tiles = S // tq
    is_first = i == 0
    is_last = i == n_tiles - 1
    H = POOL_HALO

    p = p_ref[...]
    pext_ref[0:H, :] = jnp.where(is_first, 0.0, pp_ref[...])
    pext_ref[H:H + tq, :] = p
    pext_ref[H + tq:2 * H + tq, :] = jnp.where(is_last, 0.0, pn_ref[...])
    pext_ref[2 * H + tq:, :] = jnp.zeros((pext_ref.shape[0] - 2 * H - tq, POOL_WIDTH), f32)
    near = lax.broadcasted_iota(jnp.int32, (H, 1), 0)

    def src_rows(src, k, off, n, cs):
        return src[off:off + n, cs] if src is pext_ref else src[k - 1, off:off + n, cs]

    def pool_group(g):
        w = POOL_WINDOWS[g]
        cs = slice(g * POOL_CH, (g + 1) * POOL_CH)
        r = w // 2
        src, length, k = pext_ref, pext_ref.shape[0] - H, 0
        while 2 ** k < r:
            step = 2 ** k
            lvl_ref[k, 0:length, cs] = src_rows(src, k, 0, length, cs) + src_rows(src, k, step, length, cs)
            src, length, k = lvl_ref, length - H, k + 1
        half_run = 2 ** k
        wsum = (src_rows(src, k, H - r, tq, cs) + src_rows(src, k, H - r + half_run, tq, cs)
                + pext_ref[H + r:H + r + tq, cs])
        pooled_ref[:, cs] = wsum / float(w + 1) - p[:, cs]
        cnt_head = (jnp.minimum(near, r) + (r + 1)).astype(f32)
        cnt_tail = (jnp.minimum(H - 1 - near, r) + (r + 1)).astype(f32)
        cnt_head = jnp.where(is_first, cnt_head, float(w + 1))
        cnt_tail = jnp.where(is_last, cnt_tail, float(w + 1))
        pooled_ref[0:H, cs] = wsum[0:H] / cnt_head - p[0:H, cs]
        pooled_ref[tq - H:tq, cs] = wsum[tq - H:tq] / cnt_tail - p[tq - H:tq, cs]
        ya = _dot(pooled_ref[:, cs].astype(bf16), wpool_ref[g])
        mix_ref[:, cs] = (ya * pscale_ref[:, cs]).astype(bf16)

    for g in range(len(POOL_WINDOWS)):
        pool_group(g)

    kext_ref[0:BLOCK, :] = kp_ref[...]
    kext_ref[BLOCK:BLOCK + tq, :] = k_ref[...]
    kext_ref[BLOCK + tq:, :] = kn_ref[...]
    vext_ref[:, 0:BLOCK] = vtp_ref[...]
    vext_ref[:, BLOCK:BLOCK + tq] = vt_ref[...]
    vext_ref[:, BLOCK + tq:] = vtn_ref[...]

    n_blocks = S // BLOCK
    zeros_q = jnp.zeros((HEAD_DIM, Q_GROUP * BLOCK), bf16)
    items = [(n, kvh) for n in range(tq // BLOCK) for kvh in range(N_KV_HEADS)]

    def scores(n, kvh):
        gb = i * (tq // BLOCK) + n
        variant = jnp.where(gb == 0, 1, jnp.where(gb == n_blocks - 1, 2, 0))
        cols = slice(n * BLOCK, (n + 1) * BLOCK)
        kw = kext_ref[n * BLOCK:n * BLOCK + 3 * BLOCK, :]
        qst = jnp.concatenate(
            [qt_ref[(kvh * Q_GROUP + gq) * HEAD_DIM:(kvh * Q_GROUP + gq + 1) * HEAD_DIM, cols]
             for gq in range(Q_GROUP)], axis=1)
        qst = jnp.concatenate([qst, zeros_q] if kvh == 0 else [zeros_q, qst], axis=0)
        return _dot(kw, qst) + bias_ref[variant, kvh]

    def softmax(n, kvh, s):
        sink = sink_ref[kvh]
        m = jnp.maximum(jnp.max(s, axis=0, keepdims=True), sink)
        e = jnp.exp(s - m)
        denom = jnp.sum(e, axis=0, keepdims=True) + jnp.exp(sink - m)
        return e.astype(bf16), denom

    def weighted_values(n, kvh, e, denom):
        cols = slice(n * BLOCK, (n + 1) * BLOCK)
        vwt = vext_ref[kvh * HEAD_DIM:(kvh + 1) * HEAD_DIM, n * BLOCK:n * BLOCK + 3 * BLOCK]
        out = _dot(vwt, e) / denom
        for gq in range(Q_GROUP):
            hq = kvh * Q_GROUP + gq
            ybt_ref[hq * HEAD_DIM:(hq + 1) * HEAD_DIM, cols] = out[:, gq * BLOCK:(gq + 1) * BLOCK]

    s_next = scores(*items[0])
    prev = None
    for idx, item in enumerate(items):
        s_cur = s_next
        if idx + 1 < len(items):
            s_next = scores(*items[idx + 1])
        cur = softmax(*item, s_cur)
        if prev is not None:
            weighted_values(*items[idx - 1], *prev)
        prev = cur
    weighted_values(*items[-1], *prev)
    mix_ref[:, POOL_WIDTH:] = ybt_ref[...].T.astype(bf16)

    halves = [slice(r * (tq // 2), (r + 1) * (tq // 2)) for r in range(2)]
    ys = [_dot(mix_ref[rr, :], wout_ref[...]) for rr in halves]
    for rr, y in zip(halves, ys):
        xr = ALPHA * x_ref[rr, :] + (1.0 + mod_ref[2:3, :]) * y
        o_ref[rr, :] = _layer_norm(xr, lng_ref[...], lnb_ref[...])


def _ev_mix(x2d, mod_l, p, qt, k, vt, w_pool, pool_scale, sink, w_out, ln_g, ln_b, B, S):
    T = x2d.shape[0]
    tq = 1024
    nt = S // tq
    kb = tq // BLOCK
    pb = tq // POOL_HALO
    n_kblocks = T // BLOCK
    n_pblocks = T // POOL_HALO

    def main(b, i): return (b * nt + i, 0)
    def kprev(b, i): return (jnp.maximum((b * nt + i) * kb - 1, 0), 0)
    def knext(b, i): return (jnp.minimum((b * nt + i + 1) * kb, n_kblocks - 1), 0)
    def pprev(b, i): return (jnp.maximum((b * nt + i) * pb - 1, 0), 0)
    def pnext(b, i): return (jnp.minimum((b * nt + i + 1) * pb, n_pblocks - 1), 0)
    def const2(b, i): return (0, 0)

    def tmain(b, i): return (0, b * nt + i)
    def tprev(b, i): return (0, jnp.maximum((b * nt + i) * kb - 1, 0))
    def tnext(b, i): return (0, jnp.minimum((b * nt + i + 1) * kb, n_kblocks - 1))

    assert S // BLOCK >= 2
    kj = np.arange(3 * BLOCK)[:, None]
    qi = np.arange(BLOCK)[None, :]
    dist = np.abs(kj - BLOCK - qi)
    slopes = np.float32(2.0) ** (np.float32(-8.0) * np.arange(1, N_Q_HEADS + 1, dtype=np.float32) / N_Q_HEADS)
    alibi = -slopes[:, None, None] * dist.astype(np.float32)[None]
    in_window = dist <= BLOCK
    key_ok = np.stack([kj >= 0, kj >= BLOCK, kj < 2 * BLOCK])
    bias = np.where((in_window[None] & key_ok)[:, None], alibi[None], np.float32(-1e30))
    bias = bias.reshape(3, N_KV_HEADS, Q_GROUP, 3 * BLOCK, BLOCK).transpose(0, 1, 3, 2, 4)
    bias = jnp.asarray(bias.reshape(3, N_KV_HEADS, 3 * BLOCK, Q_GROUP * BLOCK), f32)
    sink_row = jnp.repeat(sink.astype(f32).reshape(N_KV_HEADS, Q_GROUP), BLOCK, axis=1)[:, None, :]

    assert tq >= 2 * POOL_HALO and max(POOL_WINDOWS) // 2 <= POOL_HALO and nt >= 2

    kernel = functools.partial(_ev_mix_kernel, S=S, tq=tq)
    return pl.pallas_call(
        kernel,
        grid=(B, nt),
        in_specs=[
            pl.BlockSpec((tq, D), main),
            pl.BlockSpec((None, 6, D), lambda b, i: (b, 0, 0)),
            pl.BlockSpec((tq, POOL_WIDTH), main),
            pl.BlockSpec((POOL_HALO, POOL_WIDTH), pprev),
            pl.BlockSpec((POOL_HALO, POOL_WIDTH), pnext),
            pl.BlockSpec((ATTN_WIDTH, tq), tmain),
            pl.BlockSpec((tq, KV_WIDTH), main),
            pl.BlockSpec((BLOCK, KV_WIDTH), kprev),
            pl.BlockSpec((BLOCK, KV_WIDTH), knext),
            pl.BlockSpec((KV_WIDTH, tq), tmain),
            pl.BlockSpec((KV_WIDTH, BLOCK), tprev),
            pl.BlockSpec((KV_WIDTH, BLOCK), tnext),
            pl.BlockSpec((3, N_KV_HEADS, 3 * BLOCK, Q_GROUP * BLOCK), lambda b, i: (0, 0, 0, 0)),
            pl.BlockSpec((N_KV_HEADS, 1, Q_GROUP * BLOCK), lambda b, i: (0, 0, 0)),
            pl.BlockSpec((len(POOL_WINDOWS), POOL_CH, POOL_CH), lambda b, i: (0, 0, 0)),
            pl.BlockSpec((1, POOL_WIDTH), const2),
            pl.BlockSpec(memory_space=pl.ANY),
            pl.BlockSpec((1, D), const2),
            pl.BlockSpec((1, D), const2),
        ],
        out_specs=pl.BlockSpec((tq, D), main),
        out_shape=jax.ShapeDtypeStruct((T, D), f32),
        scratch_shapes=[
            pltpu.VMEM((tq + 4 * POOL_HALO, POOL_WIDTH), f32),
            pltpu.VMEM((3, tq + 3 * POOL_HALO, POOL_WIDTH), f32),
            pltpu.VMEM((tq, POOL_WIDTH), f32),
            pltpu.VMEM((tq + 2 * BLOCK, KV_WIDTH), bf16),
            pltpu.VMEM((KV_WIDTH, tq + 2 * BLOCK), bf16),
            pltpu.VMEM((ATTN_WIDTH, tq), f32),
            pltpu.VMEM((tq, D), bf16),
            pltpu.VMEM((D, D), bf16),
            pltpu.VMEM((2, D // 8, D), f32),
            pltpu.SemaphoreType.DMA((2,)),
        ],
        compiler_params=pltpu.CompilerParams(vmem_limit_bytes=VMEM_LIMIT),
        name="ev_mix",
    )(x2d, mod_l, p, p, p, qt, k, k, k, vt, vt, vt, bias, sink_row, w_pool, pool_scale, w_out, ln_g, ln_b)


def _load_rounded(w_hbm_ref, w_ref, stage_ref, sem):
    rows = stage_ref.shape[1]
    n = w_hbm_ref.shape[0] // rows

    def copy(c):
        return pltpu.make_async_copy(w_hbm_ref.at[pl.ds(c * rows, rows), :], stage_ref.at[c % 2], sem.at[c % 2])

    copy(0).start()
    for c in range(n):
        if c + 1 < n:
            copy(c + 1).start()
        copy(c).wait()
        w_ref[pl.ds(c * rows, rows), :] = stage_ref[c % 2].astype(bf16)


def _ffn_kernel(x_ref, mod_ref, wg_hbm_ref, wu_hbm_ref, wd_hbm_ref, lng_ref, lnb_ref, o_ref,
                wg_ref, wu_ref, wd_ref, stage_in_ref, stage_out_ref, sem):
    @pl.when(pl.program_id(0) == 0)
    def _():
        _load_rounded(wg_hbm_ref, wg_ref, stage_in_ref, sem)
        _load_rounded(wu_hbm_ref, wu_ref, stage_in_ref, sem)
        _load_rounded(wd_hbm_ref, wd_ref, stage_out_ref, sem)

    piece = 256
    n_pieces = x_ref.shape[0] // piece
    parts = [pl.ds(q * piece, piece) for q in range(n_pieces)]

    def gate_up(rows):
        x = x_ref[rows, :]
        h = (x * (1.0 + mod_ref[4:5, :]) + mod_ref[3:4, :]).astype(bf16)
        return x, _dot(h, wg_ref[...]), _dot(h, wu_ref[...])

    nxt = gate_up(parts[0])
    for q, rows in enumerate(parts):
        x, g, u = nxt
        if q + 1 < n_pieces:
            nxt = gate_up(parts[q + 1])
        y = _dot((_silu(g) * u).astype(bf16), wd_ref[...])
        xr = ALPHA * x + (1.0 + mod_ref[5:6, :]) * y
        o_ref[rows, :] = _layer_norm(xr, lng_ref[...], lnb_ref[...])


def _ffn(x2d, mod_l, w_gate, w_up, w_down, ln_g, ln_b, S):
    T = x2d.shape[0]
    tm = 1024
    dff = w_gate.shape[1]
    tpb = S // tm
    n_stage = 8
    return pl.pallas_call(
        _ffn_kernel,
        grid=(T // tm,),
        in_specs=[
            pl.BlockSpec((tm, D), lambda i: (i, 0)),
            pl.BlockSpec((None, 6, D), lambda i: (i // tpb, 0, 0)),
            pl.BlockSpec(memory_space=pl.ANY),
            pl.BlockSpec(memory_space=pl.ANY),
            pl.BlockSpec(memory_space=pl.ANY),
            pl.BlockSpec((1, D), lambda i: (0, 0)),
            pl.BlockSpec((1, D), lambda i: (0, 0)),
        ],
        out_specs=pl.BlockSpec((tm, D), lambda i: (i, 0)),
        out_shape=jax.ShapeDtypeStruct((T, D), f32),
        scratch_shapes=[
            pltpu.VMEM((D, dff), bf16),
            pltpu.VMEM((D, dff), bf16),
            pltpu.VMEM((dff, D), bf16),
            pltpu.VMEM((2, D // n_stage, dff), f32),
            pltpu.VMEM((2, dff // n_stage, D), f32),
            pltpu.SemaphoreType.DMA((2,)),
        ],
        compiler_params=pltpu.CompilerParams(vmem_limit_bytes=VMEM_LIMIT_BIG),
        name="ffn",
    )(x2d, mod_l, w_gate, w_up, w_down, ln_g, ln_b)


def _sg_kernel(x_ref, mod_ref, win_hbm_ref, sgg_ref, sgb_ref, ws_ref, bst_ref, wout_hbm_ref, lng_ref, lnb_ref,
               o_ref, gate_ref, win_ref, wout_ref, stage_in_ref, stage_out_ref, sem, *, tm):
    @pl.when(pl.program_id(0) == 0)
    def _():
        _load_rounded(win_hbm_ref, win_ref, stage_in_ref, sem)
        _load_rounded(wout_hbm_ref, wout_ref, stage_out_ref, sem)

    piece = 256
    n_pieces = tm // piece

    def project(q):
        x = x_ref[q * piece:(q + 1) * piece, :]
        h = (x * (1.0 + mod_ref[1:2, :]) + mod_ref[0:1, :]).astype(bf16)
        return x, _dot(h, win_ref[:, D:]), _dot(h, win_ref[:, :D])

    nxt = project(0)
    for q in range(n_pieces):
        x, zv, zu = nxt
        if q + 1 < n_pieces:
            nxt = project(q + 1)
        v = _layer_norm(_gelu_tanh(zv), sgg_ref[...], sgb_ref[...]).astype(bf16)
        u = _gelu_tanh(zu)
        for n in range(piece // CHUNK):
            rows = slice(n * CHUNK, (n + 1) * CHUNK)
            grows = slice(q * piece + n * CHUNK, q * piece + (n + 1) * CHUNK)
            for g in range(SG_GROUPS):
                cols = slice(g * SG_CH, (g + 1) * SG_CH)
                sv = _dot(ws_ref[g], v[rows, cols]) + bst_ref[:, g:g + 1]
                gate_ref[grows, cols] = (u[rows, cols] * sv).astype(bf16)
        y = _dot(gate_ref[q * piece:(q + 1) * piece, :], wout_ref[...])
        xr = ALPHA * x + (1.0 + mod_ref[2:3, :]) * y
        o_ref[q * piece:(q + 1) * piece, :] = _layer_norm(xr, lng_ref[...], lnb_ref[...])


def _sg_mix(x2d, mod_l, w_in, sg_g, sg_b, w_s, b_s_t, w_out, ln_g, ln_b, S):
    T = x2d.shape[0]
    tm = 1024
    tpb = S // tm
    c2 = lambda i: (0, 0)
    n_stage = 8
    return pl.pallas_call(
        functools.partial(_sg_kernel, tm=tm),
        grid=(T // tm,),
        in_specs=[
            pl.BlockSpec((tm, D), lambda i: (i, 0)),
            pl.BlockSpec((None, 6, D), lambda i: (i // tpb, 0, 0)),
            pl.BlockSpec(memory_space=pl.ANY),
            pl.BlockSpec((1, D), c2),
            pl.BlockSpec((1, D), c2),
            pl.BlockSpec((SG_GROUPS, CHUNK, CHUNK), lambda i: (0, 0, 0)),
            pl.BlockSpec((CHUNK, SG_GROUPS), c2),
            pl.BlockSpec(memory_space=pl.ANY),
            pl.BlockSpec((1, D), c2),
            pl.BlockSpec((1, D), c2),
        ],
        out_specs=pl.BlockSpec((tm, D), lambda i: (i, 0)),
        out_shape=jax.ShapeDtypeStruct((T, D), f32),
        scratch_shapes=[
            pltpu.VMEM((tm, D), bf16),
            pltpu.VMEM((D, 2 * D), bf16),
            pltpu.VMEM((D, D), bf16),
            pltpu.VMEM((2, D // n_stage, 2 * D), f32),
            pltpu.VMEM((2, D // n_stage, D), f32),
            pltpu.SemaphoreType.DMA((2,)),
        ],
        compiler_params=pltpu.CompilerParams(vmem_limit_bytes=VMEM_LIMIT),
        name="sg_mix",
    )(x2d, mod_l, w_in, sg_g, sg_b, w_s, b_s_t, w_out, ln_g, ln_b)


def _route_kernel(x_ref, mod_ref, wrt_ref, hs_ref, route_ref, cnt_ref):
    W = ROUTE_W
    w_hi, w_lo = _split_bf16(wrt_ref[...])
    nt = (((1,), (1,)), ((), ()))
    eidx = lax.broadcasted_iota(jnp.int32, (N_EXPERTS, W), 0)
    sub = lax.broadcasted_iota(jnp.int32, (N_EXPERTS, 1), 0)
    tr = lax.broadcasted_iota(jnp.int32, (W, W), 0)
    tc = lax.broadcasted_iota(jnp.int32, (W, W), 1)
    upper = (tr < tc).astype(bf16)
    srow = lax.broadcasted_iota(jnp.int32, (CHUNK_SLOTS, W), 0)
    ridx = lax.broadcasted_iota(jnp.int32, (8, W), 0)

    def assign_slots(k):
        h = x_ref[k * W:(k + 1) * W, :] * (1.0 + mod_ref[4:5, :]) + mod_ref[3:4, :]
        h_hi, h_lo = _split_bf16(h)
        logits = (lax.dot_general(w_hi, h_hi, nt, preferred_element_type=f32)
                  + (lax.dot_general(w_hi, h_lo, nt, preferred_element_type=f32)
                     + lax.dot_general(w_lo, h_hi, nt, preferred_element_type=f32)))
        m1 = jnp.max(logits, axis=0, keepdims=True)
        i1 = jnp.min(jnp.where(logits == m1, eidx, N_EXPERTS), axis=0, keepdims=True)
        sel1 = eidx == i1
        rest = jnp.where(sel1, -jnp.inf, logits)
        m2 = jnp.max(rest, axis=0, keepdims=True)
        i2 = jnp.min(jnp.where(rest == m2, eidx, N_EXPERTS), axis=0, keepdims=True)
        sel2 = eidx == i2
        e2 = jnp.exp(m2 - m1)
        g1 = 1.0 / (1.0 + e2)
        g2 = e2 / (1.0 + e2)

        a1 = sel1.astype(f32)
        a2 = sel2.astype(f32)
        assign = a1 + a2
        counts = jnp.sum(assign, axis=1, keepdims=True)
        grans = jnp.ceil(counts * (1.0 / GRAN))
        seg = jnp.zeros((N_EXPERTS, 1), f32)
        for e in range(N_EXPERTS - 1):
            seg = seg + jnp.where(sub > e, grans[e:e + 1, :] * GRAN, 0.0)
        rank = _dot(assign.astype(bf16), upper)
        slot = seg + rank
        pos1 = jnp.sum(a1 * slot, axis=0, keepdims=True)
        pos2 = jnp.sum(a2 * slot, axis=0, keepdims=True)
        route_ref[k] = jnp.where(ridx == 0, pos1, jnp.where(ridx == 1, pos2,
                                 jnp.where(ridx == 2, g1, jnp.where(ridx == 3, g2, 0.0))))
        cnt_ref[k] = jnp.broadcast_to(counts, (N_EXPERTS, 128)).astype(jnp.int32)
        return h_hi, pos1, pos2

    def sort_rows(k, h_hi, pos1, pos2):
        perm = ((srow == pos1.astype(jnp.int32)) | (srow == pos2.astype(jnp.int32)))
        hs_ref[k] = _dot(perm.astype(f32).astype(bf16), h_hi).astype(bf16)

    nxt = assign_slots(0)
    for k in range(ROUTE_PER_STEP):
        cur = nxt
        if k + 1 < ROUTE_PER_STEP:
            nxt = assign_slots(k + 1)
        sort_rows(k, *cur)


def _route(x2d, mod_l, w_router_t, S):
    T = x2d.shape[0]
    W = ROUTE_W
    R = ROUTE_PER_STEP
    nc = T // W
    tpb = S // (R * W)
    return pl.pallas_call(
        _route_kernel,
        grid=(nc // R,),
        in_specs=[
            pl.BlockSpec((R * W, D), lambda c: (c, 0)),
            pl.BlockSpec((None, 6, D), lambda c: (c // tpb, 0, 0)),
            pl.BlockSpec((N_EXPERTS, D), lambda c: (0, 0)),
        ],
        out_specs=[
            pl.BlockSpec((R, CHUNK_SLOTS, D), lambda c: (c, 0, 0)),
            pl.BlockSpec((R, 8, W), lambda c: (c, 0, 0)),
            pl.BlockSpec((R, N_EXPERTS, 128), lambda c: (c, 0, 0)),
        ],
        out_shape=[
            jax.ShapeDtypeStruct((nc, CHUNK_SLOTS, D), bf16),
            jax.ShapeDtypeStruct((nc, 8, W), f32),
            jax.ShapeDtypeStruct((nc, N_EXPERTS, 128), jnp.int32),
        ],
        compiler_params=pltpu.CompilerParams(vmem_limit_bytes=VMEM_LIMIT),
        name="route",
    )(x2d, mod_l, w_router_t)


def _granule_copy(src_ref, buf_ref, sem, idx_ref, base, g):
    row = pl.multiple_of(idx_ref[base + g] * GRAN, GRAN)
    return pltpu.make_async_copy(src_ref.at[pl.ds(row, GRAN), :], buf_ref.at[pl.ds(g * GRAN, GRAN), :], sem)


def _gather_start(src_ref, buf_ref, sem, idx_ref, base, n):
    for g in range(n):
        _granule_copy(src_ref, buf_ref, sem, idx_ref, base, g).start()


def _gather_wait(src_ref, buf_ref, sem, idx_ref, base, n):
    for g in range(n):
        _granule_copy(src_ref, buf_ref, sem, idx_ref, base, g).wait()


def _expert_kernel(te_ref, tv_ref, src_ref, hs_ref, wg_ref, wu_ref, wd_ref, o_ref,
                   xbuf_ref, sem, acc_ref, wgb_ref, wub_ref, wdb_ref):
    i = pl.program_id(0)
    j = pl.program_id(1)
    n_tiles = pl.num_programs(0)
    last = pl.num_programs(1) - 1
    valid = tv_ref[i] > 0
    slot = i % 2
    nxt = jnp.minimum(i + 1, n_tiles - 1)

    @pl.when(j == 0)
    def _():
        @pl.when(i == 0)
        def _():
            _gather_start(hs_ref, xbuf_ref.at[0], sem.at[0], src_ref, 0, TILE_GRANS)

        @pl.when(valid)
        def _():
            _gather_wait(hs_ref, xbuf_ref.at[slot], sem.at[slot], src_ref, i * TILE_GRANS, TILE_GRANS)

        @pl.when((i + 1 < n_tiles) & (tv_ref[nxt] > 0))
        def _():
            _gather_start(hs_ref, xbuf_ref.at[1 - slot], sem.at[1 - slot], src_ref, nxt * TILE_GRANS, TILE_GRANS)

    n_sub = tv_ref[i]
    fast_subs = MOE_FAST // MOE_SUB
    fast = n_sub >= fast_subs

    def round_weights():
        wgb_ref[...] = wg_ref[...].astype(bf16)
        wub_ref[...] = wu_ref[...].astype(bf16)
        wdb_ref[...] = wd_ref[...].astype(bf16)

    def swiglu(x):
        a = _silu(_dot(x, wgb_ref[...])) * _dot(x, wub_ref[...])
        return _dot(a.astype(bf16), wdb_ref[...])

    sparse = valid & jnp.logical_not(fast)

    @pl.when((sparse | (i == 0)) & (j == 0))
    def _():
        acc_ref[...] = jnp.zeros_like(acc_ref)

    def dense_block(n_pieces):
        round_weights()
        parts = [pl.ds(q * MOE_SUB, MOE_SUB) for q in range(n_pieces)]

        def gate_up(rows):
            x = xbuf_ref[slot, rows, :]
            return _dot(x, wgb_ref[...]), _dot(x, wub_ref[...])

        nxt_gu = gate_up(parts[0])
        for q, rows in enumerate(parts):
            g, u = nxt_gu
            if q + 1 < len(parts):
                nxt_gu = gate_up(parts[q + 1])
            y = _dot((_silu(g) * u).astype(bf16), wdb_ref[...])
            total = jnp.where(j == 0, y, acc_ref[rows, :] + y)
            acc_ref[rows, :] = total
            o_ref[rows, :] = total.astype(o_ref.dtype)
        if n_pieces * MOE_SUB < MOE_TM:
            o_ref[n_pieces * MOE_SUB:, :] = jnp.zeros((MOE_TM - n_pieces * MOE_SUB, D), o_ref.dtype)

    for n_dense in (MOE_TM // MOE_SUB, fast_subs):
        pl.when(n_sub == n_dense)(functools.partial(dense_block, n_dense))

    @pl.when(sparse)
    def _():
        round_weights()

        def body(sb, carry):
            rows = pl.ds(pl.multiple_of(sb * MOE_SUB, MOE_SUB), MOE_SUB)
            acc_ref[rows, :] += swiglu(xbuf_ref[slot, rows, :])
            return carry

        lax.fori_loop(0, n_sub, body, 0)

    @pl.when(j == last)
    def _():
        @pl.when(sparse)
        def _():
            o_ref[...] = acc_ref[...].astype(o_ref.dtype)

        @pl.when(jnp.logical_not(valid))
        def _():
            o_ref[...] = jnp.zeros_like(o_ref)


def _experts(tile_expert, tile_valid, src_of_dst, hs2d, w_gate, w_up, w_down):
    n_tiles = tile_expert.shape[0]
    dff = w_gate.shape[2]
    nff = dff // MOE_TF

    def jj(j, tv, i):
        return jnp.where(tv[i] > 0, j, nff - 1)

    grid_spec = pltpu.PrefetchScalarGridSpec(
        num_scalar_prefetch=3,
        grid=(n_tiles, nff),
        in_specs=[
            pl.BlockSpec(memory_space=pl.ANY),
            pl.BlockSpec((None, D, MOE_TF), lambda i, j, te, tv, sd: (te[i], 0, jj(j, tv, i))),
            pl.BlockSpec((None, D, MOE_TF), lambda i, j, te, tv, sd: (te[i], 0, jj(j, tv, i))),
            pl.BlockSpec((None, MOE_TF, D), lambda i, j, te, tv, sd: (te[i], jj(j, tv, i), 0)),
        ],
        out_specs=pl.BlockSpec((MOE_TM, D), lambda i, j, te, tv, sd: (i, 0)),
        scratch_shapes=[
            pltpu.VMEM((2, MOE_TM, D), bf16),
            pltpu.SemaphoreType.DMA((2,)),
            pltpu.VMEM((MOE_TM, D), f32),
            pltpu.VMEM((D, MOE_TF), bf16),
            pltpu.VMEM((D, MOE_TF), bf16),
            pltpu.VMEM((MOE_TF, D), bf16),
        ],
    )
    return pl.pallas_call(
        _expert_kernel,
        grid_spec=grid_spec,
        out_shape=jax.ShapeDtypeStruct((n_tiles * MOE_TM, D), bf16),
        compiler_params=pltpu.CompilerParams(vmem_limit_bytes=VMEM_LIMIT_BIG),
        name="experts",
    )(tile_expert, tile_valid, src_of_dst, hs2d, w_gate, w_up, w_down)


def _combine_kernel(ds_ref, x_ref, mod_ref, o_hbm_ref, rt_ref, lng_ref, lnb_ref, out_ref, obuf_ref, sem):
    W = ROUTE_W
    c = pl.program_id(0)
    nc = pl.num_programs(0)
    slot = c % 2
    nxt = jnp.minimum(c + 1, nc - 1)

    @pl.when(c == 0)
    def _():
        _gather_start(o_hbm_ref, obuf_ref.at[0], sem.at[0], ds_ref, 0, CHUNK_GRANS)

    _gather_wait(o_hbm_ref, obuf_ref.at[slot], sem.at[slot], ds_ref, c * CHUNK_GRANS, CHUNK_GRANS)

    _gather_start(o_hbm_ref, obuf_ref.at[1 - slot], sem.at[1 - slot], ds_ref, nxt * CHUNK_GRANS, CHUNK_GRANS)

    osv = obuf_ref[slot]
    scol = lax.broadcasted_iota(jnp.int32, (W // 2, CHUNK_SLOTS), 1)
    for r in range(2):
        rr = slice(r * (W // 2), (r + 1) * (W // 2))
        rt = rt_ref[rr, :]
        p1 = (scol == rt[:, 0:1].astype(jnp.int32)).astype(f32).astype(bf16)
        p2 = (scol == rt[:, 1:2].astype(jnp.int32)).astype(f32).astype(bf16)
        y = rt[:, 2:3] * _dot(p1, osv) + rt[:, 3:4] * _dot(p2, osv)
        xr = ALPHA * x_ref[rr, :] + (1.0 + mod_ref[5:6, :]) * y
        out_ref[rr, :] = _layer_norm(xr, lng_ref[...], lnb_ref[...])

    @pl.when(c == nc - 1)
    def _():
        _gather_wait(o_hbm_ref, obuf_ref.at[1 - slot], sem.at[1 - slot], ds_ref, nxt * CHUNK_GRANS, CHUNK_GRANS)


def _combine(dst_of_src, x2d, mod_l, o2d, route_t, ln_g, ln_b, S):
    T = x2d.shape[0]
    W = ROUTE_W
    tpb = S // W
    grid_spec = pltpu.PrefetchScalarGridSpec(
        num_scalar_prefetch=1,
        grid=(T // W,),
        in_specs=[
            pl.BlockSpec((W, D), lambda c, ds: (c, 0)),
            pl.BlockSpec((None, 6, D), lambda c, ds: (c // tpb, 0, 0)),
            pl.BlockSpec(memory_space=pl.ANY),
            pl.BlockSpec((None, W, 8), lambda c, ds: (c, 0, 0)),
            pl.BlockSpec((1, D), lambda c, ds: (0, 0)),
            pl.BlockSpec((1, D), lambda c, ds: (0, 0)),
        ],
        out_specs=pl.BlockSpec((W, D), lambda c, ds: (c, 0)),
        scratch_shapes=[
            pltpu.VMEM((2, CHUNK_SLOTS, D), bf16),
            pltpu.SemaphoreType.DMA((2,)),
        ],
    )
    return pl.pallas_call(
        _combine_kernel,
        grid_spec=grid_spec,
        out_shape=jax.ShapeDtypeStruct((T, D), f32),
        compiler_params=pltpu.CompilerParams(vmem_limit_bytes=VMEM_LIMIT),
        name="combine",
    )(dst_of_src, x2d, mod_l, o2d, route_t, ln_g, ln_b)


def _routing_tables(counts, n_tiles):
    nc = counts.shape[0]
    gr = (counts + GRAN - 1) // GRAN
    seg_start = jnp.cumsum(gr, axis=1) - gr
    chunk_total = jnp.sum(gr, axis=1)
    prefix = jnp.cumsum(gr, axis=0) - gr
    g_e = jnp.sum(gr, axis=0)
    tiles_e = (g_e + TILE_GRANS - 1) // TILE_GRANS
    tile_end = jnp.cumsum(tiles_e)
    tile_start = tile_end - tiles_e
    total_tiles = tile_end[-1]

    i32 = jnp.int32
    er = jnp.arange(N_EXPERTS, dtype=i32)
    t = jnp.arange(n_tiles, dtype=i32)
    te = jnp.sum((t[:, None] >= tile_end[None, :]).astype(i32), axis=1)
    tile_valid = (t < total_tiles).astype(i32)
    last_e = jnp.sum((total_tiles - 1 >= tile_end).astype(i32))
    tile_expert = jnp.where(tile_valid > 0, jnp.minimum(te, N_EXPERTS - 1), last_e).astype(i32)
    oh_t = (tile_expert[:, None] == er).astype(i32)
    grans_left = jnp.sum(oh_t * (g_e - (t[:, None] - tile_start[None, :]) * TILE_GRANS), axis=1)
    tile_subs = tile_valid * jnp.clip((grans_left + SUB_GRANS - 1) // SUB_GRANS, 0, TILE_GRANS // SUB_GRANS)

    k = jnp.arange(CHUNK_GRANS, dtype=i32)
    seg_end = seg_start + gr
    e_of = jnp.sum((k[None, :, None] >= seg_end[:, None, :]).astype(i32), axis=2)
    oh_e = (jnp.minimum(e_of, N_EXPERTS - 1)[:, :, None] == er).astype(i32)
    base = tile_start[None, :] * TILE_GRANS + prefix - seg_start
    dst = jnp.sum(oh_e * base[:, None, :], axis=2) + k[None, :]
    valid_src = k[None, :] < chunk_total[:, None]
    dst_of_src = jnp.where(valid_src, dst, 0).astype(i32).reshape(-1)

    d = jnp.arange(n_tiles * TILE_GRANS, dtype=i32)
    oh_d = (jnp.repeat(tile_expert, TILE_GRANS)[:, None] == er).astype(i32)
    q = d - jnp.sum(oh_d * tile_start[None, :], axis=1) * TILE_GRANS
    incl_d = jnp.sum(oh_d[:, :, None] * (prefix + gr).T[None], axis=1)
    c_d = jnp.sum((q[:, None] >= incl_d).astype(i32), axis=1)
    oh_c = (jnp.minimum(c_d, nc - 1)[:, None] == jnp.arange(nc, dtype=i32)).astype(i32)
    cbase = jnp.arange(nc, dtype=i32)[:, None] * CHUNK_GRANS + seg_start - prefix
    sel = jnp.sum(oh_c[:, :, None] * oh_d[:, None, :] * cbase[None], axis=(1, 2))
    valid_dst = (jnp.repeat(tile_valid, TILE_GRANS) > 0) & (q >= 0) & (q < jnp.sum(oh_d * g_e[None, :], axis=1))
    src_of_dst = jnp.where(valid_dst, sel + q, 0).astype(i32)
    return tile_expert, tile_subs.astype(i32), src_of_dst, dst_of_src


def kernel(x, c, ada_w, ada_b, ln_g, ln_b, ev_w_in, ev_pool_w, ev_pool_scale, ev_sink, ev_w_out, od_w_in, od_sg_ln_g, od_sg_ln_b, od_w_s, od_b_s, od_w_out, ffn_w_gate, ffn_w_up, ffn_w_down, moe_w_router, moe_w_gate, moe_w_up, moe_w_down):
    B, S, _ = x.shape
    T = B * S
    assert x.shape[-1] == D and ada_w.shape == (DEPTH, D, 6 * D) and B <= 8
    assert ev_w_in.shape == (1, D, POOL_WIDTH + ATTN_WIDTH + 2 * KV_WIDTH) and moe_w_gate.shape[:3] == (1, N_EXPERTS, D)
    assert S % (ROUTE_PER_STEP * ROUTE_W) == 0 and moe_w_gate.shape[3] % MOE_TF == 0
    x2d = x.reshape(T, D)
    mod = _adaln(c, ada_w, ada_b)

    w_in = ev_w_in[0].astype(bf16)
    q0, k0, v0 = POOL_WIDTH, POOL_WIDTH + ATTN_WIDTH, POOL_WIDTH + ATTN_WIDTH + KV_WIDTH
    w_pk = jnp.concatenate([w_in[:, :q0], w_in[:, k0:v0]], axis=1)
    w_qv_t = jnp.concatenate([w_in[:, q0:k0], w_in[:, v0:]], axis=1).T
    p, k, qt, vt = _ev_in(x2d, mod[0], w_pk, w_qv_t, S)
    x2d = _ev_mix(x2d, mod[0], p, qt, k, vt, ev_pool_w[0].astype(bf16), ev_pool_scale[0][None, :],
                  ev_sink[0], ev_w_out[0], ln_g[0, 0][None, :], ln_b[0, 0][None, :], B, S)
    x2d = _ffn(x2d, mod[0], ffn_w_gate[0], ffn_w_up[0], ffn_w_down[0],
               ln_g[0, 1][None, :], ln_b[0, 1][None, :], S)

    x2d = _sg_mix(x2d, mod[1], od_w_in[0], od_sg_ln_g[0][None, :], od_sg_ln_b[0][None, :],
                  od_w_s[0].astype(bf16), od_b_s[0].T, od_w_out[0],
                  ln_g[1, 0][None, :], ln_b[1, 0][None, :], S)

    hs, route, cnt = _route(x2d, mod[1], moe_w_router[0].T, S)
    nc = T // ROUTE_W
    n_tiles = (nc * CHUNK_GRANS) // TILE_GRANS + N_EXPERTS
    tile_expert, tile_subs, src_of_dst, dst_of_src = _routing_tables(cnt[:, :, 0], n_tiles)
    o = _experts(tile_expert, tile_subs, src_of_dst, hs.reshape(nc * CHUNK_SLOTS, D),
                 moe_w_gate[0], moe_w_up[0], moe_w_down[0])
    x2d = _combine(dst_of_src, x2d, mod[1], o, jnp.swapaxes(route, 1, 2),
                   ln_g[1, 1][None, :], ln_b[1, 1][None, :], S)
    return x2d.reshape(B, S, D)
```

```python
import functools
import math

import jax
import jax.numpy as jnp
import numpy as np
from jax import lax
from jax.experimental import pallas as pl
from jax.experimental.pallas import tpu as pltpu

D = 1024
DEPTH = 2
ALPHA = (2.0 * DEPTH) ** 0.25
LN_EPS = 1e-5

POOL_WINDOWS = (2, 4, 8, 16)
POOL_CH = 128
POOL_WIDTH = 512
HEAD_DIM = 64
N_Q_HEADS = 8
N_KV_HEADS = 2
Q_GROUP = 4
ATTN_WIDTH = 512
KV_WIDTH = 128
BLOCK = 128
POOL_HALO = 8

CHUNK = 128
SG_GROUPS = 8
SG_CH = 128

N_EXPERTS = 8

ROUTE_W = 512
ROUTE_PER_STEP = 2
GRAN = 16
CHUNK_SLOTS = 2 * ROUTE_W + N_EXPERTS * GRAN
CHUNK_GRANS = CHUNK_SLOTS // GRAN
MOE_SUB = 256
MOE_FAST = 2048
MOE_TM = MOE_FAST + MOE_SUB
TILE_GRANS = MOE_TM // GRAN
SUB_GRANS = MOE_SUB // GRAN
MOE_TF = 512

VMEM_LIMIT = 48 * 1024 * 1024
VMEM_LIMIT_BIG = 56 * 1024 * 1024

bf16 = jnp.bfloat16
f32 = jnp.float32


def _dot(a, b):
    return jnp.dot(a, b, preferred_element_type=f32)


def _split_bf16(a):
    hi = a.astype(bf16)
    lo = (a - hi.astype(f32)).astype(bf16)
    return hi, lo


def _layer_norm(x, g, b):
    mu = jnp.mean(x, axis=-1, keepdims=True)
    xc = x - mu
    var = jnp.mean(xc * xc, axis=-1, keepdims=True)
    return xc * lax.rsqrt(var + LN_EPS) * g + b


def _silu(x):
    return x * jax.nn.sigmoid(x)


def _gelu_tanh(x):
    c = math.sqrt(2.0 / math.pi)
    return x * (0.5 * (1.0 + jnp.tanh(c * (x + 0.044715 * (x * x * x)))))


def _adaln_kernel(c_ref, w_ref, b_ref, o_ref):
    cond = _silu(c_ref[...])
    c_hi, c_lo = _split_bf16(cond)
    w_hi, w_lo = _split_bf16(w_ref[...])
    acc = _dot(c_hi, w_hi) + (_dot(c_lo, w_hi) + _dot(c_hi, w_lo))
    o_ref[...] = acc + b_ref[...]


def _adaln(c, ada_w, ada_b):
    B = c.shape[0]
    tn = 2048
    c_pad = jnp.zeros((8, D), f32).at[:B].set(c)
    out = pl.pallas_call(
        _adaln_kernel,
        grid=(DEPTH, 6 * D // tn),
        in_specs=[
            pl.BlockSpec((8, D), lambda l, j: (0, 0)),
            pl.BlockSpec((None, D, tn), lambda l, j: (l, 0, j)),
            pl.BlockSpec((None, 1, tn), lambda l, j: (l, 0, j)),
        ],
        out_specs=pl.BlockSpec((None, 8, tn), lambda l, j: (l, 0, j)),
        out_shape=jax.ShapeDtypeStruct((DEPTH, 8, 6 * D), f32),
        compiler_params=pltpu.CompilerParams(vmem_limit_bytes=VMEM_LIMIT),
        name="adaln",
    )(c_pad, ada_w, ada_b.reshape(DEPTH, 1, 6 * D))
    return out[:, :B].reshape(DEPTH, B, 6, D)


def _ev_in_kernel(x_ref, mod_ref, wpk_ref, wqvt_ref, p_ref, k_ref, qt_ref, vt_ref):
    h = (x_ref[...] * (1.0 + mod_ref[1:2, :]) + mod_ref[0:1, :]).astype(bf16)
    zpk = _dot(h, wpk_ref[...])
    p_ref[...] = zpk[:, :POOL_WIDTH]
    k_ref[...] = zpk[:, POOL_WIDTH:].astype(bf16)
    zt = lax.dot_general(wqvt_ref[...], h, (((1,), (1,)), ((), ())), preferred_element_type=f32)
    qt_ref[...] = (zt[:ATTN_WIDTH] * (HEAD_DIM ** -0.5)).astype(bf16)
    vt_ref[...] = zt[ATTN_WIDTH:].astype(bf16)


def _ev_in(x2d, mod_l, w_pk, w_qv_t, S):
    T = x2d.shape[0]
    tm = 1024
    tpb = S // tm
    return pl.pallas_call(
        _ev_in_kernel,
        grid=(T // tm,),
        in_specs=[
            pl.BlockSpec((tm, D), lambda i: (i, 0)),
            pl.BlockSpec((None, 6, D), lambda i: (i // tpb, 0, 0)),
            pl.BlockSpec((D, POOL_WIDTH + KV_WIDTH), lambda i: (0, 0)),
            pl.BlockSpec((ATTN_WIDTH + KV_WIDTH, D), lambda i: (0, 0)),
        ],
        out_specs=[
            pl.BlockSpec((tm, POOL_WIDTH), lambda i: (i, 0)),
            pl.BlockSpec((tm, KV_WIDTH), lambda i: (i, 0)),
            pl.BlockSpec((ATTN_WIDTH, tm), lambda i: (0, i)),
            pl.BlockSpec((KV_WIDTH, tm), lambda i: (0, i)),
        ],
        out_shape=[
            jax.ShapeDtypeStruct((T, POOL_WIDTH), f32),
            jax.ShapeDtypeStruct((T, KV_WIDTH), bf16),
            jax.ShapeDtypeStruct((ATTN_WIDTH, T), bf16),
            jax.ShapeDtypeStruct((KV_WIDTH, T), bf16),
        ],
        compiler_params=pltpu.CompilerParams(vmem_limit_bytes=VMEM_LIMIT),
        name="ev_in",
    )(x2d, mod_l, w_pk, w_qv_t)


def _ev_mix_kernel(x_ref, mod_ref, p_ref, pp_ref, pn_ref, qt_ref,
                   k_ref, kp_ref, kn_ref, vt_ref, vtp_ref, vtn_ref,
                   bias_ref, sink_ref, wpool_ref, pscale_ref, wout_hbm_ref, lng_ref, lnb_ref,
                   o_ref, pext_ref, lvl_ref, pooled_ref, kext_ref, vext_ref, ybt_ref, mix_ref,
                   wout_ref, stage_ref, sem, *, S, tq):
    i = pl.program_id(1)

    @pl.when((pl.program_id(0) == 0) & (i == 0))
    def _():
        _load_rounded(wout_hbm_ref, wout_ref, stage_ref, sem)

    n_tiles = S // tq
    is_first = i == 0
    is_last = i == n_tiles - 1
    H = POOL_HALO

    p = p_ref[...]
    pext_ref[0:H, :] = jnp.where(is_first, 0.0, pp_ref[...])
    pext_ref[H:H + tq, :] = p
    pext_ref[H + tq:2 * H + tq, :] = jnp.where(is_last, 0.0, pn_ref[...])
    pext_ref[2 * H + tq:, :] = jnp.zeros((pext_ref.shape[0] - 2 * H - tq, POOL_WIDTH), f32)
    near = lax.broadcasted_iota(jnp.int32, (H, 1), 0)

    def src_rows(src, k, off, n, cs):
        return src[off:off + n, cs] if src is pext_ref else src[k - 1, off:off + n, cs]

    def pool_group(g):
        w = POOL_WINDOWS[g]
        cs = slice(g * POOL_CH, (g + 1) * POOL_CH)
        r = w // 2
        src, length, k = pext_ref, pext_ref.shape[0] - H, 0
        while 2 ** k < r:
            step = 2 ** k
            lvl_ref[k, 0:length, cs] = src_rows(src, k, 0, length, cs) + src_rows(src, k, step, length, cs)
            src, length, k = lvl_ref, length - H, k + 1
        half_run = 2 ** k
        wsum = (src_rows(src, k, H - r, tq, cs) + src_rows(src, k, H - r + half_run, tq, cs)
                + pext_ref[H + r:H + r + tq, cs])
        pooled_ref[:, cs] = wsum / float(w + 1) - p[:, cs]
        cnt_head = (jnp.minimum(near, r) + (r + 1)).astype(f32)
        cnt_tail = (jnp.minimum(H - 1 - near, r) + (r + 1)).astype(f32)
        cnt_head = jnp.where(is_first, cnt_head, float(w + 1))
        cnt_tail = jnp.where(is_last, cnt_tail, float(w + 1))
        pooled_ref[0:H, cs] = wsum[0:H] / cnt_head - p[0:H, cs]
        pooled_ref[tq - H:tq, cs] = wsum[tq - H:tq] / cnt_tail - p[tq - H:tq, cs]
        ya = _dot(pooled_ref[:, cs].astype(bf16), wpool_ref[g])
        mix_ref[:, cs] = (ya * pscale_ref[:, cs]).astype(bf16)

    for g in range(len(POOL_WINDOWS)):
        pool_group(g)

    kext_ref[0:BLOCK, :] = kp_ref[...]
    kext_ref[BLOCK:BLOCK + tq, :] = k_ref[...]
    kext_ref[BLOCK + tq:, :] = kn_ref[...]
    vext_ref[:, 0:BLOCK] = vtp_ref[...]
    vext_ref[:, BLOCK:BLOCK + tq] = vt_ref[...]
    vext_ref[:, BLOCK + tq:] = vtn_ref[...]

    n_blocks = S // BLOCK
    zeros_q = jnp.zeros((HEAD_DIM, Q_GROUP * BLOCK), bf16)
    items = [(n, kvh) for n in range(tq // BLOCK) for kvh in range(N_KV_HEADS)]

    def scores(n, kvh):
        gb = i * (tq // BLOCK) + n
        variant = jnp.where(gb == 0, 1, jnp.where(gb == n_blocks - 1, 2, 0))
        cols = slice(n * BLOCK, (n + 1) * BLOCK)
        kw = kext_ref[n * BLOCK:n * BLOCK + 3 * BLOCK, :]
        qst = jnp.concatenate(
            [qt_ref[(kvh * Q_GROUP + gq) * HEAD_DIM:(kvh * Q_GROUP + gq + 1) * HEAD_DIM, cols]
             for gq in range(Q_GROUP)], axis=1)
        qst = jnp.concatenate([qst, zeros_q] if kvh == 0 else [zeros_q, qst], axis=0)
        return _dot(kw, qst) + bias_ref[variant, kvh]

    def softmax(n, kvh, s):
        sink = sink_ref[kvh]
        m = jnp.maximum(jnp.max(s, axis=0, keepdims=True), sink)
        e = jnp.exp(s - m)
        denom = jnp.sum(e, axis=0, keepdims=True) + jnp.exp(sink - m)
        return e.astype(bf16), denom

    def weighted_values(n, kvh, e, denom):
        cols = slice(n * BLOCK, (n + 1) * BLOCK)
        vwt = vext_ref[kvh * HEAD_DIM:(kvh + 1) * HEAD_DIM, n * BLOCK:n * BLOCK + 3 * BLOCK]
        out = _dot(vwt, e) / denom
        for gq in range(Q_GROUP):
            hq = kvh * Q_GROUP + gq
            ybt_ref[hq * HEAD_DIM:(hq + 1) * HEAD_DIM, cols] = out[:, gq * BLOCK:(gq + 1) * BLOCK]

    s_next = scores(*items[0])
    prev = None
    for idx, item in enumerate(items):
        s_cur = s_next
        if idx + 1 < len(items):
            s_next = scores(*items[idx + 1])
        cur = softmax(*item, s_cur)
        if prev is not None:
            weighted_values(*items[idx - 1], *prev)
        prev = cur
    weighted_values(*items[-1], *prev)
    mix_ref[:, POOL_WIDTH:] = ybt_ref[...].T.astype(bf16)

    halves = [slice(r * (tq // 2), (r + 1) * (tq // 2)) for r in range(2)]
    ys = [_dot(mix_ref[rr, :], wout_ref[...]) for rr in halves]
    for rr, y in zip(halves, ys):
        xr = ALPHA * x_ref[rr, :] + (1.0 + mod_ref[2:3, :]) * y
        o_ref[rr, :] = _layer_norm(xr, lng_ref[...], lnb_ref[...])


def _ev_mix(x2d, mod_l, p, qt, k, vt, w_pool, pool_scale, sink, w_out, ln_g, ln_b, B, S):
    T = x2d.shape[0]
    tq = 512
    nt = S // tq
    kb = tq // BLOCK
    pb = tq // POOL_HALO
    n_kblocks = T // BLOCK
    n_pblocks = T // POOL_HALO

    def main(b, i): return (b * nt + i, 0)
    def kprev(b, i): return (jnp.maximum((b * nt + i) * kb - 1, 0), 0)
    def knext(b, i): return (jnp.minimum((b * nt + i + 1) * kb, n_kblocks - 1), 0)
    def pprev(b, i): return (jnp.maximum((b * nt + i) * pb - 1, 0), 0)
    def pnext(b, i): return (jnp.minimum((b * nt + i + 1) * pb, n_pblocks - 1), 0)
    def const2(b, i): return (0, 0)

    def tmain(b, i): return (0, b * nt + i)
    def tprev(b, i): return (0, jnp.maximum((b * nt + i) * kb - 1, 0))
    def tnext(b, i): return (0, jnp.minimum((b * nt + i + 1) * kb, n_kblocks - 1))

    assert S // BLOCK >= 2
    kj = np.arange(3 * BLOCK)[:, None]
    qi = np.arange(BLOCK)[None, :]
    dist = np.abs(kj - BLOCK - qi)
    slopes = np.float32(2.0) ** (np.float32(-8.0) * np.arange(1, N_Q_HEADS + 1, dtype=np.float32) / N_Q_HEADS)
    alibi = -slopes[:, None, None] * dist.astype(np.float32)[None]
    in_window = dist <= BLOCK
    key_ok = np.stack([kj >= 0, kj >= BLOCK, kj < 2 * BLOCK])
    bias = np.where((in_window[None] & key_ok)[:, None], alibi[None], np.float32(-1e30))
    bias = bias.reshape(3, N_KV_HEADS, Q_GROUP, 3 * BLOCK, BLOCK).transpose(0, 1, 3, 2, 4)
    bias = jnp.asarray(bias.reshape(3, N_KV_HEADS, 3 * BLOCK, Q_GROUP * BLOCK), f32)
    sink_row = jnp.repeat(sink.astype(f32).reshape(N_KV_HEADS, Q_GROUP), BLOCK, axis=1)[:, None, :]

    assert tq >= 2 * POOL_HALO and max(POOL_WINDOWS) // 2 <= POOL_HALO and nt >= 2

    kernel = functools.partial(_ev_mix_kernel, S=S, tq=tq)
    return pl.pallas_call(
        kernel,
        grid=(B, nt),
        in_specs=[
            pl.BlockSpec((tq, D), main),
            pl.BlockSpec((None, 6, D), lambda b, i: (b, 0, 0)),
            pl.BlockSpec((tq, POOL_WIDTH), main),
            pl.BlockSpec((POOL_HALO, POOL_WIDTH), pprev),
            pl.BlockSpec((POOL_HALO, POOL_WIDTH), pnext),
            pl.BlockSpec((ATTN_WIDTH, tq), tmain),
            pl.BlockSpec((tq, KV_WIDTH), main),
            pl.BlockSpec((BLOCK, KV_WIDTH), kprev),
            pl.BlockSpec((BLOCK, KV_WIDTH), knext),
            pl.BlockSpec((KV_WIDTH, tq), tmain),
            pl.BlockSpec((KV_WIDTH, BLOCK), tprev),
            pl.BlockSpec((KV_WIDTH, BLOCK), tnext),
            pl.BlockSpec((3, N_KV_HEADS, 3 * BLOCK, Q_GROUP * BLOCK), lambda b, i: (0, 0, 0, 0)),
            pl.BlockSpec((N_KV_HEADS, 1, Q_GROUP * BLOCK), lambda b, i: (0, 0, 0)),
            pl.BlockSpec((len(POOL_WINDOWS), POOL_CH, POOL_CH), lambda b, i: (0, 0, 0)),
            pl.BlockSpec((1, POOL_WIDTH), const2),
            pl.BlockSpec(memory_space=pl.ANY),
            pl.BlockSpec((1, D), const2),
            pl.BlockSpec((1, D), const2),
        ],
        out_specs=pl.BlockSpec((tq, D), main),
        out_shape=jax.ShapeDtypeStruct((T, D), f32),
        scratch_shapes=[
            pltpu.VMEM((tq + 4 * POOL_HALO, POOL_WIDTH), f32),
            pltpu.VMEM((3, tq + 3 * POOL_HALO, POOL_WIDTH), f32),
            pltpu.VMEM((tq, POOL_WIDTH), f32),
            pltpu.VMEM((tq + 2 * BLOCK, KV_WIDTH), bf16),
            pltpu.VMEM((KV_WIDTH, tq + 2 * BLOCK), bf16),
            pltpu.VMEM((ATTN_WIDTH, tq), f32),
            pltpu.VMEM((tq, D), bf16),
            pltpu.VMEM((D, D), bf16),
            pltpu.VMEM((2, D // 8, D), f32),
            pltpu.SemaphoreType.DMA((2,)),
        ],
        compiler_params=pltpu.CompilerParams(vmem_limit_bytes=VMEM_LIMIT),
        name="ev_mix",
    )(x2d, mod_l, p, p, p, qt, k, k, k, vt, vt, vt, bias, sink_row, w_pool, pool_scale, w_out, ln_g, ln_b)


def _load_rounded(w_hbm_ref, w_ref, stage_ref, sem):
    rows = stage_ref.shape[1]
    n = w_hbm_ref.shape[0] // rows

    def copy(c):
        return pltpu.make_async_copy(w_hbm_ref.at[pl.ds(c * rows, rows), :], stage_ref.at[c % 2], sem.at[c % 2])

    copy(0).start()
    for c in range(n):
        if c + 1 < n:
            copy(c + 1).start()
        copy(c).wait()
        w_ref[pl.ds(c * rows, rows), :] = stage_ref[c % 2].astype(bf16)


def _ffn_kernel(x_ref, mod_ref, wg_hbm_ref, wu_hbm_ref, wd_hbm_ref, lng_ref, lnb_ref, o_ref,
                wg_ref, wu_ref, wd_ref, stage_in_ref, stage_out_ref, sem):
    @pl.when(pl.program_id(0) == 0)
    def _():
        _load_rounded(wg_hbm_ref, wg_ref, stage_in_ref, sem)
        _load_rounded(wu_hbm_ref, wu_ref, stage_in_ref, sem)
        _load_rounded(wd_hbm_ref, wd_ref, stage_out_ref, sem)

    piece = 256
    n_pieces = x_ref.shape[0] // piece
    parts = [pl.ds(q * piece, piece) for q in range(n_pieces)]

    def gate_up(rows):
        x = x_ref[rows, :]
        h = (x * (1.0 + mod_ref[4:5, :]) + mod_ref[3:4, :]).astype(bf16)
        return x, _dot(h, wg_ref[...]), _dot(h, wu_ref[...])

    nxt = gate_up(parts[0])
    for q, rows in enumerate(parts):
        x, g, u = nxt
        if q + 1 < n_pieces:
            nxt = gate_up(parts[q + 1])
        y = _dot((_silu(g) * u).astype(bf16), wd_ref[...])
        xr = ALPHA * x + (1.0 + mod_ref[5:6, :]) * y
        o_ref[rows, :] = _layer_norm(xr, lng_ref[...], lnb_ref[...])


def _ffn(x2d, mod_l, w_gate, w_up, w_down, ln_g, ln_b, S):
    T = x2d.shape[0]
    tm = 1024
    dff = w_gate.shape[1]
    tpb = S // tm
    n_stage = 8
    return pl.pallas_call(
        _ffn_kernel,
        grid=(T // tm,),
        in_specs=[
            pl.BlockSpec((tm, D), lambda i: (i, 0)),
            pl.BlockSpec((None, 6, D), lambda i: (i // tpb, 0, 0)),
            pl.BlockSpec(memory_space=pl.ANY),
            pl.BlockSpec(memory_space=pl.ANY),
            pl.BlockSpec(memory_space=pl.ANY),
            pl.BlockSpec((1, D), lambda i: (0, 0)),
            pl.BlockSpec((1, D), lambda i: (0, 0)),
        ],
        out_specs=pl.BlockSpec((tm, D), lambda i: (i, 0)),
        out_shape=jax.ShapeDtypeStruct((T, D), f32),
        scratch_shapes=[
            pltpu.VMEM((D, dff), bf16),
            pltpu.VMEM((D, dff), bf16),
            pltpu.VMEM((dff, D), bf16),
            pltpu.VMEM((2, D // n_stage, dff), f32),
            pltpu.VMEM((2, dff // n_stage, D), f32),
            pltpu.SemaphoreType.DMA((2,)),
        ],
        compiler_params=pltpu.CompilerParams(vmem_limit_bytes=VMEM_LIMIT_BIG),
        name="ffn",
    )(x2d, mod_l, w_gate, w_up, w_down, ln_g, ln_b)


def _sg_kernel(x_ref, mod_ref, win_hbm_ref, sgg_ref, sgb_ref, ws_ref, bst_ref, wout_hbm_ref, lng_ref, lnb_ref,
               o_ref, gate_ref, win_ref, wout_ref, stage_in_ref, stage_out_ref, sem, *, tm):
    @pl.when(pl.program_id(0) == 0)
    def _():
        _load_rounded(win_hbm_ref, win_ref, stage_in_ref, sem)
        _load_rounded(wout_hbm_ref, wout_ref, stage_out_ref, sem)

    piece = 256
    n_pieces = tm // piece

    def project(q):
        x = x_ref[q * piece:(q + 1) * piece, :]
        h = (x * (1.0 + mod_ref[1:2, :]) + mod_ref[0:1, :]).astype(bf16)
        return x, _dot(h, win_ref[:, D:]), _dot(h, win_ref[:, :D])

    nxt = project(0)
    for q in range(n_pieces):
        x, zv, zu = nxt
        if q + 1 < n_pieces:
            nxt = project(q + 1)
        v = _layer_norm(_gelu_tanh(zv), sgg_ref[...], sgb_ref[...]).astype(bf16)
        u = _gelu_tanh(zu)
        for n in range(piece // CHUNK):
            rows = slice(n * CHUNK, (n + 1) * CHUNK)
            grows = slice(q * piece + n * CHUNK, q * piece + (n + 1) * CHUNK)
            for g in range(SG_GROUPS):
                cols = slice(g * SG_CH, (g + 1) * SG_CH)
                sv = _dot(ws_ref[g], v[rows, cols]) + bst_ref[:, g:g + 1]
                gate_ref[grows, cols] = (u[rows, cols] * sv).astype(bf16)
        y = _dot(gate_ref[q * piece:(q + 1) * piece, :], wout_ref[...])
        xr = ALPHA * x + (1.0 + mod_ref[2:3, :]) * y
        o_ref[q * piece:(q + 1) * piece, :] = _layer_norm(xr, lng_ref[...], lnb_ref[...])


def _sg_mix(x2d, mod_l, w_in, sg_g, sg_b, w_s, b_s_t, w_out, ln_g, ln_b, S):
    T = x2d.shape[0]
    tm = 512
    tpb = S // tm
    c2 = lambda i: (0, 0)
    n_stage = 8
    return pl.pallas_call(
        functools.partial(_sg_kernel, tm=tm),
        grid=(T // tm,),
        in_specs=[
            pl.BlockSpec((tm, D), lambda i: (i, 0)),
            pl.BlockSpec((None, 6, D), lambda i: (i // tpb, 0, 0)),
            pl.BlockSpec(memory_space=pl.ANY),
            pl.BlockSpec((1, D), c2),
            pl.BlockSpec((1, D), c2),
            pl.BlockSpec((SG_GROUPS, CHUNK, CHUNK), lambda i: (0, 0, 0)),
            pl.BlockSpec((CHUNK, SG_GROUPS), c2),
            pl.BlockSpec(memory_space=pl.ANY),
            pl.BlockSpec((1, D), c2),
            pl.BlockSpec((1, D), c2),
        ],
        out_specs=pl.BlockSpec((tm, D), lambda i: (i, 0)),
        out_shape=jax.ShapeDtypeStruct((T, D), f32),
        scratch_shapes=[
            pltpu.VMEM((tm, D), bf16),
            pltpu.VMEM((D, 2 * D), bf16),
            pltpu.VMEM((D, D), bf16),
            pltpu.VMEM((2, D // n_stage, 2 * D), f32),
            pltpu.VMEM((2, D // n_stage, D), f32),
            pltpu.SemaphoreType.DMA((2,)),
        ],
        compiler_params=pltpu.CompilerParams(vmem_limit_bytes=VMEM_LIMIT),
        name="sg_mix",
    )(x2d, mod_l, w_in, sg_g, sg_b, w_s, b_s_t, w_out, ln_g, ln_b)


def _route_kernel(x_ref, mod_ref, wrt_ref, hs_ref, route_ref, cnt_ref):
    W = ROUTE_W
    w_hi, w_lo = _split_bf16(wrt_ref[...])
    nt = (((1,), (1,)), ((), ()))
    eidx = lax.broadcasted_iota(jnp.int32, (N_EXPERTS, W), 0)
    sub = lax.broadcasted_iota(jnp.int32, (N_EXPERTS, 1), 0)
    tr = lax.broadcasted_iota(jnp.int32, (W, W), 0)
    tc = lax.broadcasted_iota(jnp.int32, (W, W), 1)
    upper = (tr < tc).astype(bf16)
    srow = lax.broadcasted_iota(jnp.int32, (CHUNK_SLOTS, W), 0)
    ridx = lax.broadcasted_iota(jnp.int32, (8, W), 0)

    def assign_slots(k):
        h = x_ref[k * W:(k + 1) * W, :] * (1.0 + mod_ref[4:5, :]) + mod_ref[3:4, :]
        h_hi, h_lo = _split_bf16(h)
        logits = (lax.dot_general(w_hi, h_hi, nt, preferred_element_type=f32)
                  + (lax.dot_general(w_hi, h_lo, nt, preferred_element_type=f32)
                     + lax.dot_general(w_lo, h_hi, nt, preferred_element_type=f32)))
        m1 = jnp.max(logits, axis=0, keepdims=True)
        i1 = jnp.min(jnp.where(logits == m1, eidx, N_EXPERTS), axis=0, keepdims=True)
        sel1 = eidx == i1
        rest = jnp.where(sel1, -jnp.inf, logits)
        m2 = jnp.max(rest, axis=0, keepdims=True)
        i2 = jnp.min(jnp.where(rest == m2, eidx, N_EXPERTS), axis=0, keepdims=True)
        sel2 = eidx == i2
        e2 = jnp.exp(m2 - m1)
        g1 = 1.0 / (1.0 + e2)
        g2 = e2 / (1.0 + e2)

        a1 = sel1.astype(f32)
        a2 = sel2.astype(f32)
        assign = a1 + a2
        counts = jnp.sum(assign, axis=1, keepdims=True)
        grans = jnp.ceil(counts * (1.0 / GRAN))
        seg = jnp.zeros((N_EXPERTS, 1), f32)
        for e in range(N_EXPERTS - 1):
            seg = seg + jnp.where(sub > e, grans[e:e + 1, :] * GRAN, 0.0)
        rank = _dot(assign.astype(bf16), upper)
        slot = seg + rank
        pos1 = jnp.sum(a1 * slot, axis=0, keepdims=True)
        pos2 = jnp.sum(a2 * slot, axis=0, keepdims=True)
        route_ref[k] = jnp.where(ridx == 0, pos1, jnp.where(ridx == 1, pos2,
                                 jnp.where(ridx == 2, g1, jnp.where(ridx == 3, g2, 0.0))))
        cnt_ref[k] = jnp.broadcast_to(counts, (N_EXPERTS, 128)).astype(jnp.int32)
        return h_hi, pos1, pos2

    def sort_rows(k, h_hi, pos1, pos2):
        perm = ((srow == pos1.astype(jnp.int32)) | (srow == pos2.astype(jnp.int32)))
        hs_ref[k] = _dot(perm.astype(f32).astype(bf16), h_hi).astype(bf16)

    nxt = assign_slots(0)
    for k in range(ROUTE_PER_STEP):
        cur = nxt
        if k + 1 < ROUTE_PER_STEP:
            nxt = assign_slots(k + 1)
        sort_rows(k, *cur)


def _route(x2d, mod_l, w_router_t, S):
    T = x2d.shape[0]
    W = ROUTE_W
    R = ROUTE_PER_STEP
    nc = T // W
    tpb = S // (R * W)
    return pl.pallas_call(
        _route_kernel,
        grid=(nc // R,),
        in_specs=[
            pl.BlockSpec((R * W, D), lambda c: (c, 0)),
            pl.BlockSpec((None, 6, D), lambda c: (c // tpb, 0, 0)),
            pl.BlockSpec((N_EXPERTS, D), lambda c: (0, 0)),
        ],
        out_specs=[
            pl.BlockSpec((R, CHUNK_SLOTS, D), lambda c: (c, 0, 0)),
            pl.BlockSpec((R, 8, W), lambda c: (c, 0, 0)),
            pl.BlockSpec((R, N_EXPERTS, 128), lambda c: (c, 0, 0)),
        ],
        out_shape=[
            jax.ShapeDtypeStruct((nc, CHUNK_SLOTS, D), bf16),
            jax.ShapeDtypeStruct((nc, 8, W), f32),
            jax.ShapeDtypeStruct((nc, N_EXPERTS, 128), jnp.int32),
        ],
        compiler_params=pltpu.CompilerParams(vmem_limit_bytes=VMEM_LIMIT),
        name="route",
    )(x2d, mod_l, w_router_t)


def _granule_copy(src_ref, buf_ref, sem, idx_ref, base, g):
    row = pl.multiple_of(idx_ref[base + g] * GRAN, GRAN)
    return pltpu.make_async_copy(src_ref.at[pl.ds(row, GRAN), :], buf_ref.at[pl.ds(g * GRAN, GRAN), :], sem)


def _gather_start(src_ref, buf_ref, sem, idx_ref, base, n):
    for g in range(n):
        _granule_copy(src_ref, buf_ref, sem, idx_ref, base, g).start()


def _gather_wait(src_ref, buf_ref, sem, idx_ref, base, n):
    for g in range(n):
        _granule_copy(src_ref, buf_ref, sem, idx_ref, base, g).wait()


def _expert_kernel(te_ref, tv_ref, src_ref, hs_ref, wg_ref, wu_ref, wd_ref, o_ref,
                   xbuf_ref, sem, acc_ref, wgb_ref, wub_ref, wdb_ref):
    i = pl.program_id(0)
    j = pl.program_id(1)
    n_tiles = pl.num_programs(0)
    last = pl.num_programs(1) - 1
    valid = tv_ref[i] > 0
    slot = i % 2
    nxt = jnp.minimum(i + 1, n_tiles - 1)

    @pl.when(j == 0)
    def _():
        @pl.when(i == 0)
        def _():
            _gather_start(hs_ref, xbuf_ref.at[0], sem.at[0], src_ref, 0, TILE_GRANS)

        @pl.when(valid)
        def _():
            _gather_wait(hs_ref, xbuf_ref.at[slot], sem.at[slot], src_ref, i * TILE_GRANS, TILE_GRANS)

        @pl.when((i + 1 < n_tiles) & (tv_ref[nxt] > 0))
        def _():
            _gather_start(hs_ref, xbuf_ref.at[1 - slot], sem.at[1 - slot], src_ref, nxt * TILE_GRANS, TILE_GRANS)

    n_sub = tv_ref[i]
    fast_subs = MOE_FAST // MOE_SUB
    fast = n_sub >= fast_subs

    def round_weights():
        wgb_ref[...] = wg_ref[...].astype(bf16)
        wub_ref[...] = wu_ref[...].astype(bf16)
        wdb_ref[...] = wd_ref[...].astype(bf16)

    def swiglu(x):
        a = _silu(_dot(x, wgb_ref[...])) * _dot(x, wub_ref[...])
        return _dot(a.astype(bf16), wdb_ref[...])

    sparse = valid & jnp.logical_not(fast)

    @pl.when((sparse | (i == 0)) & (j == 0))
    def _():
        acc_ref[...] = jnp.zeros_like(acc_ref)

    def dense_block(n_pieces):
        round_weights()
        parts = [pl.ds(q * MOE_SUB, MOE_SUB) for q in range(n_pieces)]

        def gate_up(rows):
            x = xbuf_ref[slot, rows, :]
            return _dot(x, wgb_ref[...]), _dot(x, wub_ref[...])

        nxt_gu = gate_up(parts[0])
        for q, rows in enumerate(parts):
            g, u = nxt_gu
            if q + 1 < len(parts):
                nxt_gu = gate_up(parts[q + 1])
            y = _dot((_silu(g) * u).astype(bf16), wdb_ref[...])
            total = jnp.where(j == 0, y, acc_ref[rows, :] + y)
            acc_ref[rows, :] = total
            o_ref[rows, :] = total.astype(o_ref.dtype)
        if n_pieces * MOE_SUB < MOE_TM:
            o_ref[n_pieces * MOE_SUB:, :] = jnp.zeros((MOE_TM - n_pieces * MOE_SUB, D), o_ref.dtype)

    for n_dense in (MOE_TM // MOE_SUB, fast_subs):
        pl.when(n_sub == n_dense)(functools.partial(dense_block, n_dense))

    @pl.when(sparse)
    def _():
        round_weights()

        def body(sb, carry):
            rows = pl.ds(pl.multiple_of(sb * MOE_SUB, MOE_SUB), MOE_SUB)
            acc_ref[rows, :] += swiglu(xbuf_ref[slot, rows, :])
            return carry

        lax.fori_loop(0, n_sub, body, 0)

    @pl.when(j == last)
    def _():
        @pl.when(sparse)
        def _():
            o_ref[...] = acc_ref[...].astype(o_ref.dtype)

        @pl.when(jnp.logical_not(valid))
        def _():
            o_ref[...] = jnp.zeros_like(o_ref)


def _experts(tile_expert, tile_valid, src_of_dst, hs2d, w_gate, w_up, w_down):
    n_tiles = tile_expert.shape[0]
    dff = w_gate.shape[2]
    nff = dff // MOE_TF

    def jj(j, tv, i):
        return jnp.where(tv[i] > 0, j, nff - 1)

    grid_spec = pltpu.PrefetchScalarGridSpec(
        num_scalar_prefetch=3,
        grid=(n_tiles, nff),
        in_specs=[
            pl.BlockSpec(memory_space=pl.ANY),
            pl.BlockSpec((None, D, MOE_TF), lambda i, j, te, tv, sd: (te[i], 0, jj(j, tv, i))),
            pl.BlockSpec((None, D, MOE_TF), lambda i, j, te, tv, sd: (te[i], 0, jj(j, tv, i))),
            pl.BlockSpec((None, MOE_TF, D), lambda i, j, te, tv, sd: (te[i], jj(j, tv, i), 0)),
        ],
        out_specs=pl.BlockSpec((MOE_TM, D), lambda i, j, te, tv, sd: (i, 0)),
        scratch_shapes=[
            pltpu.VMEM((2, MOE_TM, D), bf16),
            pltpu.SemaphoreType.DMA((2,)),
            pltpu.VMEM((MOE_TM, D), f32),
            pltpu.VMEM((D, MOE_TF), bf16),
            pltpu.VMEM((D, MOE_TF), bf16),
            pltpu.VMEM((MOE_TF, D), bf16),
        ],
    )
    return pl.pallas_call(
        _expert_kernel,
        grid_spec=grid_spec,
        out_shape=jax.ShapeDtypeStruct((n_tiles * MOE_TM, D), bf16),
        compiler_params=pltpu.CompilerParams(vmem_limit_bytes=VMEM_LIMIT_BIG),
        name="experts",
    )(tile_expert, tile_valid, src_of_dst, hs2d, w_gate, w_up, w_down)


def _combine_kernel(ds_ref, x_ref, mod_ref, o_hbm_ref, rt_ref, lng_ref, lnb_ref, out_ref, obuf_ref, sem):
    W = ROUTE_W
    c = pl.program_id(0)
    nc = pl.num_programs(0)
    slot = c % 2
    nxt = jnp.minimum(c + 1, nc - 1)

    @pl.when(c == 0)
    def _():
        _gather_start(o_hbm_ref, obuf_ref.at[0], sem.at[0], ds_ref, 0, CHUNK_GRANS)

    _gather_wait(o_hbm_ref, obuf_ref.at[slot], sem.at[slot], ds_ref, c * CHUNK_GRANS, CHUNK_GRANS)

    _gather_start(o_hbm_ref, obuf_ref.at[1 - slot], sem.at[1 - slot], ds_ref, nxt * CHUNK_GRANS, CHUNK_GRANS)

    osv = obuf_ref[slot]
    scol = lax.broadcasted_iota(jnp.int32, (W // 2, CHUNK_SLOTS), 1)
    for r in range(2):
        rr = slice(r * (W // 2), (r + 1) * (W // 2))
        rt = rt_ref[rr, :]
        p1 = (scol == rt[:, 0:1].astype(jnp.int32)).astype(f32).astype(bf16)
        p2 = (scol == rt[:, 1:2].astype(jnp.int32)).astype(f32).astype(bf16)
        y = rt[:, 2:3] * _dot(p1, osv) + rt[:, 3:4] * _dot(p2, osv)
        xr = ALPHA * x_ref[rr, :] + (1.0 + mod_ref[5:6, :]) * y
        out_ref[rr, :] = _layer_norm(xr, lng_ref[...], lnb_ref[...])

    @pl.when(c == nc - 1)
    def _():
        _gather_wait(o_hbm_ref, obuf_ref.at[1 - slot], sem.at[1 - slot], ds_ref, nxt * CHUNK_GRANS, CHUNK_GRANS)


def _combine(dst_of_src, x2d, mod_l, o2d, route_t, ln_g, ln_b, S):
    T = x2d.shape[0]
    W = ROUTE_W
    tpb = S // W
    grid_spec = pltpu.PrefetchScalarGridSpec(
        num_scalar_prefetch=1,
        grid=(T // W,),
        in_specs=[
            pl.BlockSpec((W, D), lambda c, ds: (c, 0)),
            pl.BlockSpec((None, 6, D), lambda c, ds: (c // tpb, 0, 0)),
            pl.BlockSpec(memory_space=pl.ANY),
            pl.BlockSpec((None, W, 8), lambda c, ds: (c, 0, 0)),
            pl.BlockSpec((1, D), lambda c, ds: (0, 0)),
            pl.BlockSpec((1, D), lambda c, ds: (0, 0)),
        ],
        out_specs=pl.BlockSpec((W, D), lambda c, ds: (c, 0)),
        scratch_shapes=[
            pltpu.VMEM((2, CHUNK_SLOTS, D), bf16),
            pltpu.SemaphoreType.DMA((2,)),
        ],
    )
    return pl.pallas_call(
        _combine_kernel,
        grid_spec=grid_spec,
        out_shape=jax.ShapeDtypeStruct((T, D), f32),
        compiler_params=pltpu.CompilerParams(vmem_limit_bytes=VMEM_LIMIT),
        name="combine",
    )(dst_of_src, x2d, mod_l, o2d, route_t, ln_g, ln_b)


def _routing_tables(counts, n_tiles):
    nc = counts.shape[0]
    gr = (counts + GRAN - 1) // GRAN
    seg_start = jnp.cumsum(gr, axis=1) - gr
    chunk_total = jnp.sum(gr, axis=1)
    prefix = jnp.cumsum(gr, axis=0) - gr
    g_e = jnp.sum(gr, axis=0)
    tiles_e = (g_e + TILE_GRANS - 1) // TILE_GRANS
    tile_end = jnp.cumsum(tiles_e)
    tile_start = tile_end - tiles_e
    total_tiles = tile_end[-1]

    i32 = jnp.int32
    er = jnp.arange(N_EXPERTS, dtype=i32)
    t = jnp.arange(n_tiles, dtype=i32)
    te = jnp.sum((t[:, None] >= tile_end[None, :]).astype(i32), axis=1)
    tile_valid = (t < total_tiles).astype(i32)
    last_e = jnp.sum((total_tiles - 1 >= tile_end).astype(i32))
    tile_expert = jnp.where(tile_valid > 0, jnp.minimum(te, N_EXPERTS - 1), last_e).astype(i32)
    oh_t = (tile_expert[:, None] == er).astype(i32)
    grans_left = jnp.sum(oh_t * (g_e - (t[:, None] - tile_start[None, :]) * TILE_GRANS), axis=1)
    tile_subs = tile_valid * jnp.clip((grans_left + SUB_GRANS - 1) // SUB_GRANS, 0, TILE_GRANS // SUB_GRANS)

    k = jnp.arange(CHUNK_GRANS, dtype=i32)
    seg_end = seg_start + gr
    e_of = jnp.sum((k[None, :, None] >= seg_end[:, None, :]).astype(i32), axis=2)
    oh_e = (jnp.minimum(e_of, N_EXPERTS - 1)[:, :, None] == er).astype(i32)
    base = tile_start[None, :] * TILE_GRANS + prefix - seg_start
    dst = jnp.sum(oh_e * base[:, None, :], axis=2) + k[None, :]
    valid_src = k[None, :] < chunk_total[:, None]
    dst_of_src = jnp.where(valid_src, dst, 0).astype(i32).reshape(-1)

    d = jnp.arange(n_tiles * TILE_GRANS, dtype=i32)
    oh_d = (jnp.repeat(tile_expert, TILE_GRANS)[:, None] == er).astype(i32)
    q = d - jnp.sum(oh_d * tile_start[None, :], axis=1) * TILE_GRANS
    incl_d = jnp.sum(oh_d[:, :, None] * (prefix + gr).T[None], axis=1)
    c_d = jnp.sum((q[:, None] >= incl_d).astype(i32), axis=1)
    oh_c = (jnp.minimum(c_d, nc - 1)[:, None] == jnp.arange(nc, dtype=i32)).astype(i32)
    cbase = jnp.arange(nc, dtype=i32)[:, None] * CHUNK_GRANS + seg_start - prefix
    sel = jnp.sum(oh_c[:, :, None] * oh_d[:, None, :] * cbase[None], axis=(1, 2))
    valid_dst = (jnp.repeat(tile_valid, TILE_GRANS) > 0) & (q >= 0) & (q < jnp.sum(oh_d * g_e[None, :], axis=1))
    src_of_dst = jnp.where(valid_dst, sel + q, 0).astype(i32)
    return tile_expert, tile_subs.astype(i32), src_of_dst, dst_of_src


def kernel(x, c, ada_w, ada_b, ln_g, ln_b, ev_w_in, ev_pool_w, ev_pool_scale, ev_sink, ev_w_out, od_w_in, od_sg_ln_g, od_sg_ln_b, od_w_s, od_b_s, od_w_out, ffn_w_gate, ffn_w_up, ffn_w_down, moe_w_router, moe_w_gate, moe_w_up, moe_w_down):
    B, S, _ = x.shape
    T = B * S
    assert x.shape[-1] == D and ada_w.shape == (DEPTH, D, 6 * D) and B <= 8
    assert ev_w_in.shape == (1, D, POOL_WIDTH + ATTN_WIDTH + 2 * KV_WIDTH) and moe_w_gate.shape[:3] == (1, N_EXPERTS, D)
    assert S % (ROUTE_PER_STEP * ROUTE_W) == 0 and moe_w_gate.shape[3] % MOE_TF == 0
    x2d = x.reshape(T, D)
    mod = _adaln(c, ada_w, ada_b)

    w_in = ev_w_in[0].astype(bf16)
    q0, k0, v0 = POOL_WIDTH, POOL_WIDTH + ATTN_WIDTH, POOL_WIDTH + ATTN_WIDTH + KV_WIDTH
    w_pk = jnp.concatenate([w_in[:, :q0], w_in[:, k0:v0]], axis=1)
    w_qv_t = jnp.concatenate([w_in[:, q0:k0], w_in[:, v0:]], axis=1).T
    p, k, qt, vt = _ev_in(x2d, mod[0], w_pk, w_qv_t, S)
    x2d = _ev_mix(x2d, mod[0], p, qt, k, vt, ev_pool_w[0].astype(bf16), ev_pool_scale[0][None, :],
                  ev_sink[0], ev_w_out[0], ln_g[0, 0][None, :], ln_b[0, 0][None, :], B, S)
    x2d = _ffn(x2d, mod[0], ffn_w_gate[0], ffn_w_up[0], ffn_w_down[0],
               ln_g[0, 1][None, :], ln_b[0, 1][None, :], S)

    x2d = _sg_mix(x2d, mod[1], od_w_in[0], od_sg_ln_g[0][None, :], od_sg_ln_b[0][None, :],
                  od_w_s[0].astype(bf16), od_b_s[0].T, od_w_out[0],
                  ln_g[1, 0][None, :], ln_b[1, 0][None, :], S)

    hs, route, cnt = _route(x2d, mod[1], moe_w_router[0].T, S)
    nc = T // ROUTE_W
    n_tiles = (nc * CHUNK_GRANS) // TILE_GRANS + N_EXPERTS
    tile_expert, tile_subs, src_of_dst, dst_of_src = _routing_tables(cnt[:, :, 0], n_tiles)
    o = _experts(tile_expert, tile_subs, src_of_dst, hs.reshape(nc * CHUNK_SLOTS, D),
                 moe_w_gate[0], moe_w_up[0], moe_w_down[0])
    x2d = _combine(dst_of_src, x2d, mod[1], o, jnp.swapaxes(route, 1, 2),
                   ln_g[1, 1][None, :], ln_b[1, 1][None, :], S)
    return x2d.reshape(B, S, D)
```

```python
import functools
import math

import jax
import jax.numpy as jnp
import numpy as np
from jax import lax
from jax.experimental import pallas as pl
from jax.experimental.pallas import tpu as pltpu

D = 1024
DEPTH = 2
ALPHA = (2.0 * DEPTH) ** 0.25
LN_EPS = 1e-5

POOL_WINDOWS = (2, 4, 8, 16)
POOL_CH = 128
POOL_WIDTH = 512
HEAD_DIM = 64
N_Q_HEADS = 8
N_KV_HEADS = 2
Q_GROUP = 4
ATTN_WIDTH = 512
KV_WIDTH = 128
BLOCK = 128
POOL_HALO = 8

CHUNK = 128
SG_GROUPS = 8
SG_CH = 128

N_EXPERTS = 8

ROUTE_W = 512
ROUTE_PER_STEP = 2
GRAN = 16
CHUNK_SLOTS = 2 * ROUTE_W + N_EXPERTS * GRAN
CHUNK_GRANS = CHUNK_SLOTS // GRAN
MOE_SUB = 256
MOE_FAST = 2048
MOE_TM = MOE_FAST + MOE_SUB
TILE_GRANS = MOE_TM // GRAN
SUB_GRANS = MOE_SUB // GRAN
MOE_TF = 512

VMEM_LIMIT = 48 * 1024 * 1024
VMEM_LIMIT_BIG = 56 * 1024 * 1024

bf16 = jnp.bfloat16
f32 = jnp.float32


def _dot(a, b):
    return jnp.dot(a, b, preferred_element_type=f32)


def _split_bf16(a):
    hi = a.astype(bf16)
    lo = (a - hi.astype(f32)).astype(bf16)
    return hi, lo


def _layer_norm(x, g, b):
    mu = jnp.mean(x, axis=-1, keepdims=True)
    xc = x - mu
    var = jnp.mean(xc * xc, axis=-1, keepdims=True)
    return xc * lax.rsqrt(var + LN_EPS) * g + b


def _silu(x):
    return x * jax.nn.sigmoid(x)


def _gelu_tanh(x):
    c = math.sqrt(2.0 / math.pi)
    return x * (0.5 * (1.0 + jnp.tanh(c * (x + 0.044715 * (x * x * x)))))


def _adaln_kernel(c_ref, w_ref, b_ref, o_ref):
    cond = _silu(c_ref[...])
    c_hi, c_lo = _split_bf16(cond)
    w_hi, w_lo = _split_bf16(w_ref[...])
    acc = _dot(c_hi, w_hi) + (_dot(c_lo, w_hi) + _dot(c_hi, w_lo))
    o_ref[...] = acc + b_ref[...]


def _adaln(c, ada_w, ada_b):
    B = c.shape[0]
    tn = 2048
    c_pad = jnp.zeros((8, D), f32).at[:B].set(c)
    out = pl.pallas_call(
        _adaln_kernel,
        grid=(DEPTH, 6 * D // tn),
        in_specs=[
            pl.BlockSpec((8, D), lambda l, j: (0, 0)),
            pl.BlockSpec((None, D, tn), lambda l, j: (l, 0, j)),
            pl.BlockSpec((None, 1, tn), lambda l, j: (l, 0, j)),
        ],
        out_specs=pl.BlockSpec((None, 8, tn), lambda l, j: (l, 0, j)),
        out_shape=jax.ShapeDtypeStruct((DEPTH, 8, 6 * D), f32),
        compiler_params=pltpu.CompilerParams(vmem_limit_bytes=VMEM_LIMIT),
        name="adaln",
    )(c_pad, ada_w, ada_b.reshape(DEPTH, 1, 6 * D))
    return out[:, :B].reshape(DEPTH, B, 6, D)


def _ev_in_kernel(x_ref, mod_ref, wpk_ref, wqvt_ref, p_ref, k_ref, qt_ref, vt_ref):
    h = (x_ref[...] * (1.0 + mod_ref[1:2, :]) + mod_ref[0:1, :]).astype(bf16)
    zpk = _dot(h, wpk_ref[...])
    p_ref[...] = zpk[:, :POOL_WIDTH]
    k_ref[...] = zpk[:, POOL_WIDTH:].astype(bf16)
    zt = lax.dot_general(wqvt_ref[...], h, (((1,), (1,)), ((), ())), preferred_element_type=f32)
    qt_ref[...] = (zt[:ATTN_WIDTH] * (HEAD_DIM ** -0.5)).astype(bf16)
    vt_ref[...] = zt[ATTN_WIDTH:].astype(bf16)


def _ev_in(x2d, mod_l, w_pk, w_qv_t, S):
    T = x2d.shape[0]
    tm = 1024
    tpb = S // tm
    return pl.pallas_call(
        _ev_in_kernel,
        grid=(T // tm,),
        in_specs=[
            pl.BlockSpec((tm, D), lambda i: (i, 0)),
            pl.BlockSpec((None, 6, D), lambda i: (i // tpb, 0, 0)),
            pl.BlockSpec((D, POOL_WIDTH + KV_WIDTH), lambda i: (0, 0)),
            pl.BlockSpec((ATTN_WIDTH + KV_WIDTH, D), lambda i: (0, 0)),
        ],
        out_specs=[
            pl.BlockSpec((tm, POOL_WIDTH), lambda i: (i, 0)),
            pl.BlockSpec((tm, KV_WIDTH), lambda i: (i, 0)),
            pl.BlockSpec((ATTN_WIDTH, tm), lambda i: (0, i)),
            pl.BlockSpec((KV_WIDTH, tm), lambda i: (0, i)),
        ],
        out_shape=[
            jax.ShapeDtypeStruct((T, POOL_WIDTH), f32),
            jax.ShapeDtypeStruct((T, KV_WIDTH), bf16),
            jax.ShapeDtypeStruct((ATTN_WIDTH, T), bf16),
            jax.ShapeDtypeStruct((KV_WIDTH, T), bf16),
        ],
        compiler_params=pltpu.CompilerParams(vmem_limit_bytes=VMEM_LIMIT),
        name="ev_in",
    )(x2d, mod_l, w_pk, w_qv_t)


def _ev_mix_kernel(x_ref, mod_ref, p_ref, pp_ref, pn_ref, qt_ref,
                   k_ref, kp_ref, kn_ref, vt_ref, vtp_ref, vtn_ref,
                   bias_ref, sink_ref, wpool_ref, pscale_ref, wout_hbm_ref, lng_ref, lnb_ref,
                   o_ref, pext_ref, lvl_ref, pooled_ref, kext_ref, vext_ref, ybt_ref, mix_ref,
                   wout_ref, stage_ref, sem, *, S, tq):
    i = pl.program_id(1)

    @pl.when((pl.program_id(0) == 0) & (i == 0))
    def _():
        _load_rounded(wout_hbm_ref, wout_ref, stage_ref, sem)

    n_tiles = S // tq
    is_first = i == 0
    is_last = i == n_tiles - 1
    H = POOL_HALO

    p = p_ref[...]
    pext_ref[0:H, :] = jnp.where(is_first, 0.0, pp_ref[...])
    pext_ref[H:H + tq, :] = p
    pext_ref[H + tq:2 * H + tq, :] = jnp.where(is_last, 0.0, pn_ref[...])
    pext_ref[2 * H + tq:, :] = jnp.zeros((pext_ref.shape[0] - 2 * H - tq, POOL_WIDTH), f32)
    near = lax.broadcasted_iota(jnp.int32, (H, 1), 0)

    def src_rows(src, k, off, n, cs):
        return src[off:off + n, cs] if src is pext_ref else src[k - 1, off:off + n, cs]

    def pool_group(g):
        w = POOL_WINDOWS[g]
        cs = slice(g * POOL_CH, (g + 1) * POOL_CH)
        r = w // 2
        src, length, k = pext_ref, pext_ref.shape[0] - H, 0
        while 2 ** k < r:
            step = 2 ** k
            lvl_ref[k, 0:length, cs] = src_rows(src, k, 0, length, cs) + src_rows(src, k, step, length, cs)
            src, length, k = lvl_ref, length - H, k + 1
        half_run = 2 ** k
        wsum = (src_rows(src, k, H - r, tq, cs) + src_rows(src, k, H - r + half_run, tq, cs)
                + pext_ref[H + r:H + r + tq, cs])
        pooled_ref[:, cs] = wsum / float(w + 1) - p[:, cs]
        cnt_head = (jnp.minimum(near, r) + (r + 1)).astype(f32)
        cnt_tail = (jnp.minimum(H - 1 - near, r) + (r + 1)).astype(f32)
        cnt_head = jnp.where(is_first, cnt_head, float(w + 1))
        cnt_tail = jnp.where(is_last, cnt_tail, float(w + 1))
        pooled_ref[0:H, cs] = wsum[0:H] / cnt_head - p[0:H, cs]
        pooled_ref[tq - H:tq, cs] = wsum[tq - H:tq] / cnt_tail - p[tq - H:tq, cs]
        ya = _dot(pooled_ref[:, cs].astype(bf16), wpool_ref[g])
        mix_ref[:, cs] = (ya * pscale_ref[:, cs]).astype(bf16)

    for g in range(len(POOL_WINDOWS)):
        pool_group(g)

    kext_ref[0:BLOCK, :] = kp_ref[...]
    kext_ref[BLOCK:BLOCK + tq, :] = k_ref[...]
    kext_ref[BLOCK + tq:, :] = kn_ref[...]
    vext_ref[:, 0:BLOCK] = vtp_ref[...]
    vext_ref[:, BLOCK:BLOCK + tq] = vt_ref[...]
    vext_ref[:, BLOCK + tq:] = vtn_ref[...]

    n_blocks = S // BLOCK
    zeros_q = jnp.zeros((HEAD_DIM, Q_GROUP * BLOCK), bf16)
    items = [(n, kvh) for n in range(tq // BLOCK) for kvh in range(N_KV_HEADS)]

    def scores(n, kvh):
        gb = i * (tq // BLOCK) + n
        variant = jnp.where(gb == 0, 1, jnp.where(gb == n_blocks - 1, 2, 0))
        cols = slice(n * BLOCK, (n + 1) * BLOCK)
        kw = kext_ref[n * BLOCK:n * BLOCK + 3 * BLOCK, :]
        qst = jnp.concatenate(
            [qt_ref[(kvh * Q_GROUP + gq) * HEAD_DIM:(kvh * Q_GROUP + gq + 1) * HEAD_DIM, cols]
             for gq in range(Q_GROUP)], axis=1)
        qst = jnp.concatenate([qst, zeros_q] if kvh == 0 else [zeros_q, qst], axis=0)
        return _dot(kw, qst) + bias_ref[variant, kvh]

    def softmax(n, kvh, s):
        sink = sink_ref[kvh]
        m = jnp.maximum(jnp.max(s, axis=0, keepdims=True), sink)
        e = jnp.exp(s - m)
        denom = jnp.sum(e, axis=0, keepdims=True) + jnp.exp(sink - m)
        return e.astype(bf16), denom

    def weighted_values(n, kvh, e, denom):
        cols = slice(n * BLOCK, (n + 1) * BLOCK)
        vwt = vext_ref[kvh * HEAD_DIM:(kvh + 1) * HEAD_DIM, n * BLOCK:n * BLOCK + 3 * BLOCK]
        out = _dot(vwt, e) / denom
        for gq in range(Q_GROUP):
            hq = kvh * Q_GROUP + gq
            ybt_ref[hq * HEAD_DIM:(hq + 1) * HEAD_DIM, cols] = out[:, gq * BLOCK:(gq + 1) * BLOCK]

    s_next = scores(*items[0])
    prev = None
    for idx, item in enumerate(items):
        s_cur = s_next
        if idx + 1 < len(items):
            s_next = scores(*items[idx + 1])
        cur = softmax(*item, s_cur)
        if prev is not None:
            weighted_values(*items[idx - 1], *prev)
        prev = cur
    weighted_values(*items[-1], *prev)
    mix_ref[:, POOL_WIDTH:] = ybt_ref[...].T.astype(bf16)

    halves = [slice(r * (tq // 2), (r + 1) * (tq // 2)) for r in range(2)]
    ys = [_dot(mix_ref[rr, :], wout_ref[...]) for rr in halves]
    for rr, y in zip(halves, ys):
        xr = ALPHA * x_ref[rr, :] + (1.0 + mod_ref[2:3, :]) * y
        o_ref[rr, :] = _layer_norm(xr, lng_ref[...], lnb_ref[...])


def _ev_mix(x2d, mod_l, p, qt, k, vt, w_pool, pool_scale, sink, w_out, ln_g, ln_b, B, S):
    T = x2d.shape[0]
    tq = 512
    nt = S // tq
    kb = tq // BLOCK
    pb = tq // POOL_HALO
    n_kblocks = T // BLOCK
    n_pblocks = T // POOL_HALO

    def main(b, i): return (b * nt + i, 0)
    def kprev(b, i): return (jnp.maximum((b * nt + i) * kb - 1, 0), 0)
    def knext(b, i): return (jnp.minimum((b * nt + i + 1) * kb, n_kblocks - 1), 0)
    def pprev(b, i): return (jnp.maximum((b * nt + i) * pb - 1, 0), 0)
    def pnext(b, i): return (jnp.minimum((b * nt + i + 1) * pb, n_pblocks - 1), 0)
    def const2(b, i): return (0, 0)

    def tmain(b, i): return (0, b * nt + i)
    def tprev(b, i): return (0, jnp.maximum((b * nt + i) * kb - 1, 0))
    def tnext(b, i): return (0, jnp.minimum((b * nt + i + 1) * kb, n_kblocks - 1))

    assert S // BLOCK >= 2
    kj = np.arange(3 * BLOCK)[:, None]
    qi = np.arange(BLOCK)[None, :]
    dist = np.abs(kj - BLOCK - qi)
    slopes = np.float32(2.0) ** (np.float32(-8.0) * np.arange(1, N_Q_HEADS + 1, dtype=np.float32) / N_Q_HEADS)
    alibi = -slopes[:, None, None] * dist.astype(np.float32)[None]
    in_window = dist <= BLOCK
    key_ok = np.stack([kj >= 0, kj >= BLOCK, kj < 2 * BLOCK])
    bias = np.where((in_window[None] & key_ok)[:, None], alibi[None], np.float32(-1e30))
    bias = bias.reshape(3, N_KV_HEADS, Q_GROUP, 3 * BLOCK, BLOCK).transpose(0, 1, 3, 2, 4)
    bias = jnp.asarray(bias.reshape(3, N_KV_HEADS, 3 * BLOCK, Q_GROUP * BLOCK), f32)
    sink_row = jnp.repeat(sink.astype(f32).reshape(N_KV_HEADS, Q_GROUP), BLOCK, axis=1)[:, None, :]

    assert tq >= 2 * POOL_HALO and max(POOL_WINDOWS) // 2 <= POOL_HALO and nt >= 2

    kernel = functools.partial(_ev_mix_kernel, S=S, tq=tq)
    return pl.pallas_call(
        kernel,
        grid=(B, nt),
        in_specs=[
            pl.BlockSpec((tq, D), main),
            pl.BlockSpec((None, 6, D), lambda b, i: (b, 0, 0)),
            pl.BlockSpec((tq, POOL_WIDTH), main),
            pl.BlockSpec((POOL_HALO, POOL_WIDTH), pprev),
            pl.BlockSpec((POOL_HALO, POOL_WIDTH), pnext),
            pl.BlockSpec((ATTN_WIDTH, tq), tmain),
            pl.BlockSpec((tq, KV_WIDTH), main),
            pl.BlockSpec((BLOCK, KV_WIDTH), kprev),
            pl.BlockSpec((BLOCK, KV_WIDTH), knext),
            pl.BlockSpec((KV_WIDTH, tq), tmain),
            pl.BlockSpec((KV_WIDTH, BLOCK), tprev),
            pl.BlockSpec((KV_WIDTH, BLOCK), tnext),
            pl.BlockSpec((3, N_KV_HEADS, 3 * BLOCK, Q_GROUP * BLOCK), lambda b, i: (0, 0, 0, 0)),
            pl.BlockSpec((N_KV_HEADS, 1, Q_GROUP * BLOCK), lambda b, i: (0, 0, 0)),
            pl.BlockSpec((len(POOL_WINDOWS), POOL_CH, POOL_CH), lambda b, i: (0, 0, 0)),
            pl.BlockSpec((1, POOL_WIDTH), const2),
            pl.BlockSpec(memory_space=pl.ANY),
            pl.BlockSpec((1, D), const2),
            pl.BlockSpec((1, D), const2),
        ],
        out_specs=pl.BlockSpec((tq, D), main),
        out_shape=jax.ShapeDtypeStruct((T, D), f32),
        scratch_shapes=[
            pltpu.VMEM((tq + 4 * POOL_HALO, POOL_WIDTH), f32),
            pltpu.VMEM((3, tq + 3 * POOL_HALO, POOL_WIDTH), f32),
            pltpu.VMEM((tq, POOL_WIDTH), f32),
            pltpu.VMEM((tq + 2 * BLOCK, KV_WIDTH), bf16),
            pltpu.VMEM((KV_WIDTH, tq + 2 * BLOCK), bf16),
            pltpu.VMEM((ATTN_WIDTH, tq), f32),
            pltpu.VMEM((tq, D), bf16),
            pltpu.VMEM((D, D), bf16),
            pltpu.VMEM((2, D // 8, D), f32),
            pltpu.SemaphoreType.DMA((2,)),
        ],
        compiler_params=pltpu.CompilerParams(vmem_limit_bytes=VMEM_LIMIT),
        name="ev_mix",
    )(x2d, mod_l, p, p, p, qt, k, k, k, vt, vt, vt, bias, sink_row, w_pool, pool_scale, w_out, ln_g, ln_b)


def _load_rounded(w_hbm_ref, w_ref, stage_ref, sem):
    rows = stage_ref.shape[1]
    n = w_hbm_ref.shape[0] // rows

    def copy(c):
        return pltpu.make_async_copy(w_hbm_ref.at[pl.ds(c * rows, rows), :], stage_ref.at[c % 2], sem.at[c % 2])

    copy(0).start()
    for c in range(n):
        if c + 1 < n:
            copy(c + 1).start()
        copy(c).wait()
        w_ref[pl.ds(c * rows, rows), :] = stage_ref[c % 2].astype(bf16)


def _ffn_kernel(x_ref, mod_ref, wg_hbm_ref, wu_hbm_ref, wd_hbm_ref, lng_ref, lnb_ref, o_ref,
                wg_ref, wu_ref, wd_ref, stage_in_ref, stage_out_ref, sem):
    @pl.when(pl.program_id(0) == 0)
    def _():
        _load_rounded(wg_hbm_ref, wg_ref, stage_in_ref, sem)
        _load_rounded(wu_hbm_ref, wu_ref, stage_in_ref, sem)
        _load_rounded(wd_hbm_ref, wd_ref, stage_out_ref, sem)

    piece = 256
    n_pieces = x_ref.shape[0] // piece
    parts = [pl.ds(q * piece, piece) for q in range(n_pieces)]

    def gate_up(rows):
        x = x_ref[rows, :]
        h = (x * (1.0 + mod_ref[4:5, :]) + mod_ref[3:4, :]).astype(bf16)
        return x, _dot(h, wg_ref[...]), _dot(h, wu_ref[...])

    nxt = gate_up(parts[0])
    for q, rows in enumerate(parts):
        x, g, u = nxt
        if q + 1 < n_pieces:
            nxt = gate_up(parts[q + 1])
        y = _dot((_silu(g) * u).astype(bf16), wd_ref[...])
        xr = ALPHA * x + (1.0 + mod_ref[5:6, :]) * y
        o_ref[rows, :] = _layer_norm(xr, lng_ref[...], lnb_ref[...])


def _ffn(x2d, mod_l, w_gate, w_up, w_down, ln_g, ln_b, S):
    T = x2d.shape[0]
    tm = 1024
    dff = w_gate.shape[1]
    tpb = S // tm
    n_stage = 8
    return pl.pallas_call(
        _ffn_kernel,
        grid=(T // tm,),
        in_specs=[
            pl.BlockSpec((tm, D), lambda i: (i, 0)),
            pl.BlockSpec((None, 6, D), lambda i: (i // tpb, 0, 0)),
            pl.BlockSpec(memory_space=pl.ANY),
            pl.BlockSpec(memory_space=pl.ANY),
            pl.BlockSpec(memory_space=pl.ANY),
            pl.BlockSpec((1, D), lambda i: (0, 0)),
            pl.BlockSpec((1, D), lambda i: (0, 0)),
        ],
        out_specs=pl.BlockSpec((tm, D), lambda i: (i, 0)),
        out_shape=jax.ShapeDtypeStruct((T, D), f32),
        scratch_shapes=[
            pltpu.VMEM((D, dff), bf16),
            pltpu.VMEM((D, dff), bf16),
            pltpu.VMEM((dff, D), bf16),
            pltpu.VMEM((2, D // n_stage, dff), f32),
            pltpu.VMEM((2, dff // n_stage, D), f32),
            pltpu.SemaphoreType.DMA((2,)),
        ],
        compiler_params=pltpu.CompilerParams(vmem_limit_bytes=VMEM_LIMIT_BIG),
        name="ffn",
    )(x2d, mod_l, w_gate, w_up, w_down, ln_g, ln_b)


def _sg_kernel(x_ref, mod_ref, win_hbm_ref, sgg_ref, sgb_ref, ws_ref, bst_ref, wout_hbm_ref, lng_ref, lnb_ref,
               o_ref, gate_ref, win_ref, wout_ref, stage_in_ref, stage_out_ref, sem, *, tm):
    @pl.when(pl.program_id(0) == 0)
    def _():
        _load_rounded(win_hbm_ref, win_ref, stage_in_ref, sem)
        _load_rounded(wout_hbm_ref, wout_ref, stage_out_ref, sem)

    piece = 256
    n_pieces = tm // piece

    def project(q):
        x = x_ref[q * piece:(q + 1) * piece, :]
        h = (x * (1.0 + mod_ref[1:2, :]) + mod_ref[0:1, :]).astype(bf16)
        return x, _dot(h, win_ref[:, D:]), _dot(h, win_ref[:, :D])

    nxt = project(0)
    for q in range(n_pieces):
        x, zv, zu = nxt
        if q + 1 < n_pieces:
            nxt = project(q + 1)
        v = _layer_norm(_gelu_tanh(zv), sgg_ref[...], sgb_ref[...]).astype(bf16)
        u = _gelu_tanh(zu)
        for n in range(piece // CHUNK):
            rows = slice(n * CHUNK, (n + 1) * CHUNK)
            grows = slice(q * piece + n * CHUNK, q * piece + (n + 1) * CHUNK)
            for g in range(SG_GROUPS):
                cols = slice(g * SG_CH, (g + 1) * SG_CH)
                sv = _dot(ws_ref[g], v[rows, cols]) + bst_ref[:, g:g + 1]
                gate_ref[grows, cols] = (u[rows, cols] * sv).astype(bf16)
        y = _dot(gate_ref[q * piece:(q + 1) * piece, :], wout_ref[...])
        xr = ALPHA * x + (1.0 + mod_ref[2:3, :]) * y
        o_ref[q * piece:(q + 1) * piece, :] = _layer_norm(xr, lng_ref[...], lnb_ref[...])


def _sg_mix(x2d, mod_l, w_in, sg_g, sg_b, w_s, b_s_t, w_out, ln_g, ln_b, S):
    T = x2d.shape[0]
    tm = 512
    tpb = S // tm
    c2 = lambda i: (0, 0)
    n_stage = 8
    return pl.pallas_call(
        functools.partial(_sg_kernel, tm=tm),
        grid=(T // tm,),
        in_specs=[
            pl.BlockSpec((tm, D), lambda i: (i, 0)),
            pl.BlockSpec((None, 6, D), lambda i: (i // tpb, 0, 0)),
            pl.BlockSpec(memory_space=pl.ANY),
            pl.BlockSpec((1, D), c2),
            pl.BlockSpec((1, D), c2),
            pl.BlockSpec((SG_GROUPS, CHUNK, CHUNK), lambda i: (0, 0, 0)),
            pl.BlockSpec((CHUNK, SG_GROUPS), c2),
            pl.BlockSpec(memory_space=pl.ANY),
            pl.BlockSpec((1, D), c2),
            pl.BlockSpec((1, D), c2),
        ],
        out_specs=pl.BlockSpec((tm, D), lambda i: (i, 0)),
        out_shape=jax.ShapeDtypeStruct((T, D), f32),
        scratch_shapes=[
            pltpu.VMEM((tm, D), bf16),
            pltpu.VMEM((D, 2 * D), bf16),
            pltpu.VMEM((D, D), bf16),
            pltpu.VMEM((2, D // n_stage, 2 * D), f32),
            pltpu.VMEM((2, D // n_stage, D), f32),
            pltpu.SemaphoreType.DMA((2,)),
        ],
        compiler_params=pltpu.CompilerParams(vmem_limit_bytes=VMEM_LIMIT),
        name="sg_mix",
    )(x2d, mod_l, w_in, sg_g, sg_b, w_s, b_s_t, w_out, ln_g, ln_b)


def _route_kernel(x_ref, mod_ref, wrt_ref, hs_ref, route_ref, cnt_ref):
    W = ROUTE_W
    w_hi, w_lo = _split_bf16(wrt_ref[...])
    nt = (((1,), (1,)), ((), ()))
    eidx = lax.broadcasted_iota(jnp.int32, (N_EXPERTS, W), 0)
    sub = lax.broadcasted_iota(jnp.int32, (N_EXPERTS, 1), 0)
    tr = lax.broadcasted_iota(jnp.int32, (W, W), 0)
    tc = lax.broadcasted_iota(jnp.int32, (W, W), 1)
    upper = (tr < tc).astype(bf16)
    srow = lax.broadcasted_iota(jnp.int32, (CHUNK_SLOTS, W), 0)
    ridx = lax.broadcasted_iota(jnp.int32, (8, W), 0)

    def assign_slots(k):
        h = x_ref[k * W:(k + 1) * W, :] * (1.0 + mod_ref[4:5, :]) + mod_ref[3:4, :]
        h_hi, h_lo = _split_bf16(h)
        logits = (lax.dot_general(w_hi, h_hi, nt, preferred_element_type=f32)
                  + (lax.dot_general(w_hi, h_lo, nt, preferred_element_type=f32)
                     + lax.dot_general(w_lo, h_hi, nt, preferred_element_type=f32)))
        m1 = jnp.max(logits, axis=0, keepdims=True)
        i1 = jnp.min(jnp.where(logits == m1, eidx, N_EXPERTS), axis=0, keepdims=True)
        sel1 = eidx == i1
        rest = jnp.where(sel1, -jnp.inf, logits)
        m2 = jnp.max(rest, axis=0, keepdims=True)
        i2 = jnp.min(jnp.where(rest == m2, eidx, N_EXPERTS), axis=0, keepdims=True)
        sel2 = eidx == i2
        e2 = jnp.exp(m2 - m1)
        g1 = 1.0 / (1.0 + e2)
        g2 = e2 / (1.0 + e2)

        a1 = sel1.astype(f32)
        a2 = sel2.astype(f32)
        assign = a1 + a2
        counts = jnp.sum(assign, axis=1, keepdims=True)
        grans = jnp.ceil(counts * (1.0 / GRAN))
        seg = jnp.zeros((N_EXPERTS, 1), f32)
        for e in range(N_EXPERTS - 1):
            seg = seg + jnp.where(sub > e, grans[e:e + 1, :] * GRAN, 0.0)
        rank = _dot(assign.astype(bf16), upper)
        slot = seg + rank
        pos1 = jnp.sum(a1 * slot, axis=0, keepdims=True)
        pos2 = jnp.sum(a2 * slot, axis=0, keepdims=True)
        route_ref[k] = jnp.where(ridx == 0, pos1, jnp.where(ridx == 1, pos2,
                                 jnp.where(ridx == 2, g1, jnp.where(ridx == 3, g2, 0.0)))).T
        cnt_ref[k] = jnp.broadcast_to(counts, (N_EXPERTS, 128)).astype(jnp.int32)
        return h_hi, pos1, pos2

    def sort_rows(k, h_hi, pos1, pos2):
        perm = ((srow == pos1.astype(jnp.int32)) | (srow == pos2.astype(jnp.int32)))
        hs_ref[k] = _dot(perm.astype(f32).astype(bf16), h_hi).astype(bf16)

    nxt = assign_slots(0)
    for k in range(ROUTE_PER_STEP):
        cur = nxt
        if k + 1 < ROUTE_PER_STEP:
            nxt = assign_slots(k + 1)
        sort_rows(k, *cur)


def _route(x2d, mod_l, w_router_t, S):
    T = x2d.shape[0]
    W = ROUTE_W
    R = ROUTE_PER_STEP
    nc = T // W
    tpb = S // (R * W)
    return pl.pallas_call(
        _route_kernel,
        grid=(nc // R,),
        in_specs=[
            pl.BlockSpec((R * W, D), lambda c: (c, 0)),
            pl.BlockSpec((None, 6, D), lambda c: (c // tpb, 0, 0)),
            pl.BlockSpec((N_EXPERTS, D), lambda c: (0, 0)),
        ],
        out_specs=[
            pl.BlockSpec((R, CHUNK_SLOTS, D), lambda c: (c, 0, 0)),
            pl.BlockSpec((R, W, 8), lambda c: (c, 0, 0)),
            pl.BlockSpec((R, N_EXPERTS, 128), lambda c: (c, 0, 0)),
        ],
        out_shape=[
            jax.ShapeDtypeStruct((nc, CHUNK_SLOTS, D), bf16),
            jax.ShapeDtypeStruct((nc, W, 8), f32),
            jax.ShapeDtypeStruct((nc, N_EXPERTS, 128), jnp.int32),
        ],
        compiler_params=pltpu.CompilerParams(vmem_limit_bytes=VMEM_LIMIT),
        name="route",
    )(x2d, mod_l, w_router_t)


def _granule_copy(src_ref, buf_ref, sem, idx_ref, base, g):
    row = pl.multiple_of(idx_ref[base + g] * GRAN, GRAN)
    return pltpu.make_async_copy(src_ref.at[pl.ds(row, GRAN), :], buf_ref.at[pl.ds(g * GRAN, GRAN), :], sem)


def _gather_start(src_ref, buf_ref, sem, idx_ref, base, n):
    for g in range(n):
        _granule_copy(src_ref, buf_ref, sem, idx_ref, base, g).start()


def _gather_wait(src_ref, buf_ref, sem, idx_ref, base, n):
    for g in range(n):
        _granule_copy(src_ref, buf_ref, sem, idx_ref, base, g).wait()


def _expert_kernel(te_ref, tv_ref, src_ref, hs_ref, wg_ref, wu_ref, wd_ref, o_ref,
                   xbuf_ref, sem, acc_ref, wgb_ref, wub_ref, wdb_ref):
    i = pl.program_id(0)
    j = pl.program_id(1)
    n_tiles = pl.num_programs(0)
    last = pl.num_programs(1) - 1
    valid = tv_ref[i] > 0
    slot = i % 2
    nxt = jnp.minimum(i + 1, n_tiles - 1)

    @pl.when(j == 0)
    def _():
        @pl.when(i == 0)
        def _():
            _gather_start(hs_ref, xbuf_ref.at[0], sem.at[0], src_ref, 0, TILE_GRANS)

        @pl.when(valid)
        def _():
            _gather_wait(hs_ref, xbuf_ref.at[slot], sem.at[slot], src_ref, i * TILE_GRANS, TILE_GRANS)

        @pl.when((i + 1 < n_tiles) & (tv_ref[nxt] > 0))
        def _():
            _gather_start(hs_ref, xbuf_ref.at[1 - slot], sem.at[1 - slot], src_ref, nxt * TILE_GRANS, TILE_GRANS)

    n_sub = tv_ref[i]
    fast_subs = MOE_FAST // MOE_SUB
    fast = n_sub >= fast_subs

    def round_weights():
        wgb_ref[...] = wg_ref[...].astype(bf16)
        wub_ref[...] = wu_ref[...].astype(bf16)
        wdb_ref[...] = wd_ref[...].astype(bf16)

    def swiglu(x):
        a = _silu(_dot(x, wgb_ref[...])) * _dot(x, wub_ref[...])
        return _dot(a.astype(bf16), wdb_ref[...])

    sparse = valid & jnp.logical_not(fast)

    @pl.when((sparse | (i == 0)) & (j == 0))
    def _():
        acc_ref[...] = jnp.zeros_like(acc_ref)

    def dense_block(n_pieces):
        round_weights()
        parts = [pl.ds(q * MOE_SUB, MOE_SUB) for q in range(n_pieces)]

        def gate_up(rows):
            x = xbuf_ref[slot, rows, :]
            return _dot(x, wgb_ref[...]), _dot(x, wub_ref[...])

        nxt_gu = gate_up(parts[0])
        for q, rows in enumerate(parts):
            g, u = nxt_gu
            if q + 1 < len(parts):
                nxt_gu = gate_up(parts[q + 1])
            y = _dot((_silu(g) * u).astype(bf16), wdb_ref[...])
            total = jnp.where(j == 0, y, acc_ref[rows, :] + y)
            acc_ref[rows, :] = total
            o_ref[rows, :] = total.astype(o_ref.dtype)
        if n_pieces * MOE_SUB < MOE_TM:
            o_ref[n_pieces * MOE_SUB:, :] = jnp.zeros((MOE_TM - n_pieces * MOE_SUB, D), o_ref.dtype)

    for n_dense in (MOE_TM // MOE_SUB, fast_subs):
        pl.when(n_sub == n_dense)(functools.partial(dense_block, n_dense))

    @pl.when(sparse)
    def _():
        round_weights()

        def body(sb, carry):
            rows = pl.ds(pl.multiple_of(sb * MOE_SUB, MOE_SUB), MOE_SUB)
            acc_ref[rows, :] += swiglu(xbuf_ref[slot, rows, :])
            return carry

        lax.fori_loop(0, n_sub, body, 0)

    @pl.when(j == last)
    def _():
        @pl.when(sparse)
        def _():
            o_ref[...] = acc_ref[...].astype(o_ref.dtype)

        @pl.when(jnp.logical_not(valid))
        def _():
            o_ref[...] = jnp.zeros_like(o_ref)


def _experts(tile_expert, tile_valid, src_of_dst, hs2d, w_gate, w_up, w_down):
    n_tiles = tile_expert.shape[0]
    dff = w_gate.shape[2]
    nff = dff // MOE_TF

    def jj(j, tv, i):
        return jnp.where(tv[i] > 0, j, nff - 1)

    grid_spec = pltpu.PrefetchScalarGridSpec(
        num_scalar_prefetch=3,
        grid=(n_tiles, nff),
        in_specs=[
            pl.BlockSpec(memory_space=pl.ANY),
            pl.BlockSpec((None, D, MOE_TF), lambda i, j, te, tv, sd: (te[i], 0, jj(j, tv, i))),
            pl.BlockSpec((None, D, MOE_TF), lambda i, j, te, tv, sd: (te[i], 0, jj(j, tv, i))),
            pl.BlockSpec((None, MOE_TF, D), lambda i, j, te, tv, sd: (te[i], jj(j, tv, i), 0)),
        ],
        out_specs=pl.BlockSpec((MOE_TM, D), lambda i, j, te, tv, sd: (i, 0)),
        scratch_shapes=[
            pltpu.VMEM((2, MOE_TM, D), bf16),
            pltpu.SemaphoreType.DMA((2,)),
            pltpu.VMEM((MOE_TM, D), f32),
            pltpu.VMEM((D, MOE_TF), bf16),
            pltpu.VMEM((D, MOE_TF), bf16),
            pltpu.VMEM((MOE_TF, D), bf16),
        ],
    )
    return pl.pallas_call(
        _expert_kernel,
        grid_spec=grid_spec,
        out_shape=jax.ShapeDtypeStruct((n_tiles * MOE_TM, D), bf16),
        compiler_params=pltpu.CompilerParams(vmem_limit_bytes=VMEM_LIMIT_BIG),
        name="experts",
    )(tile_expert, tile_valid, src_of_dst, hs2d, w_gate, w_up, w_down)


def _combine_kernel(ds_ref, x_ref, mod_ref, o_hbm_ref, rt_ref, lng_ref, lnb_ref, out_ref, obuf_ref, sem):
    W = ROUTE_W
    c = pl.program_id(0)
    nc = pl.num_programs(0)
    slot = c % 2
    nxt = jnp.minimum(c + 1, nc - 1)

    @pl.when(c == 0)
    def _():
        _gather_start(o_hbm_ref, obuf_ref.at[0], sem.at[0], ds_ref, 0, CHUNK_GRANS)

    _gather_wait(o_hbm_ref, obuf_ref.at[slot], sem.at[slot], ds_ref, c * CHUNK_GRANS, CHUNK_GRANS)

    _gather_start(o_hbm_ref, obuf_ref.at[1 - slot], sem.at[1 - slot], ds_ref, nxt * CHUNK_GRANS, CHUNK_GRANS)

    osv = obuf_ref[slot]
    scol = lax.broadcasted_iota(jnp.int32, (W // 2, CHUNK_SLOTS), 1)
    for r in range(2):
        rr = slice(r * (W // 2), (r + 1) * (W // 2))
        rt = rt_ref[rr, :]
        p1 = (scol == rt[:, 0:1].astype(jnp.int32)).astype(f32).astype(bf16)
        p2 = (scol == rt[:, 1:2].astype(jnp.int32)).astype(f32).astype(bf16)
        y = rt[:, 2:3] * _dot(p1, osv) + rt[:, 3:4] * _dot(p2, osv)
        xr = ALPHA * x_ref[rr, :] + (1.0 + mod_ref[5:6, :]) * y
        out_ref[rr, :] = _layer_norm(xr, lng_ref[...], lnb_ref[...])

    @pl.when(c == nc - 1)
    def _():
        _gather_wait(o_hbm_ref, obuf_ref.at[1 - slot], sem.at[1 - slot], ds_ref, nxt * CHUNK_GRANS, CHUNK_GRANS)


def _combine(dst_of_src, x2d, mod_l, o2d, route_t, ln_g, ln_b, S):
    T = x2d.shape[0]
    W = ROUTE_W
    tpb = S // W
    grid_spec = pltpu.PrefetchScalarGridSpec(
        num_scalar_prefetch=1,
        grid=(T // W,),
        in_specs=[
            pl.BlockSpec((W, D), lambda c, ds: (c, 0)),
            pl.BlockSpec((None, 6, D), lambda c, ds: (c // tpb, 0, 0)),
            pl.BlockSpec(memory_space=pl.ANY),
            pl.BlockSpec((None, W, 8), lambda c, ds: (c, 0, 0)),
            pl.BlockSpec((1, D), lambda c, ds: (0, 0)),
            pl.BlockSpec((1, D), lambda c, ds: (0, 0)),
        ],
        out_specs=pl.BlockSpec((W, D), lambda c, ds: (c, 0)),
        scratch_shapes=[
            pltpu.VMEM((2, CHUNK_SLOTS, D), bf16),
            pltpu.SemaphoreType.DMA((2,)),
        ],
    )
    return pl.pallas_call(
        _combine_kernel,
        grid_spec=grid_spec,
        out_shape=jax.ShapeDtypeStruct((T, D), f32),
        compiler_params=pltpu.CompilerParams(vmem_limit_bytes=VMEM_LIMIT),
        name="combine",
    )(dst_of_src, x2d, mod_l, o2d, route_t, ln_g, ln_b)


def _routing_tables(counts, n_tiles):
    nc = counts.shape[0]
    gr = (counts + GRAN - 1) // GRAN
    seg_start = jnp.cumsum(gr, axis=1) - gr
    chunk_total = jnp.sum(gr, axis=1)
    prefix = jnp.cumsum(gr, axis=0) - gr
    g_e = jnp.sum(gr, axis=0)
    tiles_e = (g_e + TILE_GRANS - 1) // TILE_GRANS
    tile_end = jnp.cumsum(tiles_e)
    tile_start = tile_end - tiles_e
    total_tiles = tile_end[-1]

    i32 = jnp.int32
    er = jnp.arange(N_EXPERTS, dtype=i32)
    t = jnp.arange(n_tiles, dtype=i32)
    te = jnp.sum((t[:, None] >= tile_end[None, :]).astype(i32), axis=1)
    tile_valid = (t < total_tiles).astype(i32)
    last_e = jnp.sum((total_tiles - 1 >= tile_end).astype(i32))
    tile_expert = jnp.where(tile_valid > 0, jnp.minimum(te, N_EXPERTS - 1), last_e).astype(i32)
    oh_t = (tile_expert[:, None] == er).astype(i32)
    grans_left = jnp.sum(oh_t * (g_e - (t[:, None] - tile_start[None, :]) * TILE_GRANS), axis=1)
    tile_subs = tile_valid * jnp.clip((grans_left + SUB_GRANS - 1) // SUB_GRANS, 0, TILE_GRANS // SUB_GRANS)

    k = jnp.arange(CHUNK_GRANS, dtype=i32)
    seg_end = seg_start + gr
    e_of = jnp.sum((k[None, :, None] >= seg_end[:, None, :]).astype(i32), axis=2)
    oh_e = (jnp.minimum(e_of, N_EXPERTS - 1)[:, :, None] == er).astype(i32)
    base = tile_start[None, :] * TILE_GRANS + prefix - seg_start
    dst = jnp.sum(oh_e * base[:, None, :], axis=2) + k[None, :]
    valid_src = k[None, :] < chunk_total[:, None]
    dst_of_src = jnp.where(valid_src, dst, 0).astype(i32).reshape(-1)

    d = jnp.arange(n_tiles * TILE_GRANS, dtype=i32)
    oh_d = (jnp.repeat(tile_expert, TILE_GRANS)[:, None] == er).astype(i32)
    q = d - jnp.sum(oh_d * tile_start[None, :], axis=1) * TILE_GRANS
    incl_d = jnp.sum(oh_d[:, :, None] * (prefix + gr).T[None], axis=1)
    c_d = jnp.sum((q[:, None] >= incl_d).astype(i32), axis=1)
    oh_c = (jnp.minimum(c_d, nc - 1)[:, None] == jnp.arange(nc, dtype=i32)).astype(i32)
    cbase = jnp.arange(nc, dtype=i32)[:, None] * CHUNK_GRANS + seg_start - prefix
    sel = jnp.sum(oh_c[:, :, None] * oh_d[:, None, :] * cbase[None], axis=(1, 2))
    valid_dst = (jnp.repeat(tile_valid, TILE_GRANS) > 0) & (q >= 0) & (q < jnp.sum(oh_d * g_e[None, :], axis=1))
    src_of_dst = jnp.where(valid_dst, sel + q, 0).astype(i32)
    return tile_expert, tile_subs.astype(i32), src_of_dst, dst_of_src


def kernel(x, c, ada_w, ada_b, ln_g, ln_b, ev_w_in, ev_pool_w, ev_pool_scale, ev_sink, ev_w_out, od_w_in, od_sg_ln_g, od_sg_ln_b, od_w_s, od_b_s, od_w_out, ffn_w_gate, ffn_w_up, ffn_w_down, moe_w_router, moe_w_gate, moe_w_up, moe_w_down):
    B, S, _ = x.shape
    T = B * S
    assert x.shape[-1] == D and ada_w.shape == (DEPTH, D, 6 * D) and B <= 8
    assert ev_w_in.shape == (1, D, POOL_WIDTH + ATTN_WIDTH + 2 * KV_WIDTH) and moe_w_gate.shape[:3] == (1, N_EXPERTS, D)
    assert S % (ROUTE_PER_STEP * ROUTE_W) == 0 and moe_w_gate.shape[3] % MOE_TF == 0
    x2d = x.reshape(T, D)
    mod = _adaln(c, ada_w, ada_b)

    w_in = ev_w_in[0].astype(bf16)
    q0, k0, v0 = POOL_WIDTH, POOL_WIDTH + ATTN_WIDTH, POOL_WIDTH + ATTN_WIDTH + KV_WIDTH
    w_pk = jnp.concatenate([w_in[:, :q0], w_in[:, k0:v0]], axis=1)
    w_qv_t = jnp.concatenate([w_in[:, q0:k0], w_in[:, v0:]], axis=1).T
    p, k, qt, vt = _ev_in(x2d, mod[0], w_pk, w_qv_t, S)
    x2d = _ev_mix(x2d, mod[0], p, qt, k, vt, ev_pool_w[0].astype(bf16), ev_pool_scale[0][None, :],
                  ev_sink[0], ev_w_out[0], ln_g[0, 0][None, :], ln_b[0, 0][None, :], B, S)
    x2d = _ffn(x2d, mod[0], ffn_w_gate[0], ffn_w_up[0], ffn_w_down[0],
               ln_g[0, 1][None, :], ln_b[0, 1][None, :], S)

    x2d = _sg_mix(x2d, mod[1], od_w_in[0], od_sg_ln_g[0][None, :], od_sg_ln_b[0][None, :],
                  od_w_s[0].astype(bf16), od_b_s[0].T, od_w_out[0],
                  ln_g[1, 0][None, :], ln_b[1, 0][None, :], S)

    hs, route, cnt = _route(x2d, mod[1], moe_w_router[0].T, S)
    nc = T // ROUTE_W
    n_tiles = (nc * CHUNK_GRANS) // TILE_GRANS + N_EXPERTS
    tile_expert, tile_subs, src_of_dst, dst_of_src = _routing_tables(cnt[:, :, 0], n_tiles)
    o = _experts(tile_expert, tile_subs, src_of_dst, hs.reshape(nc * CHUNK_SLOTS, D),
                 moe_w_gate[0], moe_w_up[0], moe_w_down[0])
    x2d = _combine(dst_of_src, x2d, mod[1], o, route,
                   ln_g[1, 1][None, :], ln_b[1, 1][None, :], S)
    return x2d.reshape(B, S, D)
```

```python
import functools
import math

import jax
import jax.numpy as jnp
import numpy as np
from jax import lax
from jax.experimental import pallas as pl
from jax.experimental.pallas import tpu as pltpu

D = 1024
DEPTH = 2
ALPHA = (2.0 * DEPTH) ** 0.25
LN_EPS = 1e-5

POOL_WINDOWS = (2, 4, 8, 16)
POOL_CH = 128
POOL_WIDTH = 512
HEAD_DIM = 64
N_Q_HEADS = 8
N_KV_HEADS = 2
Q_GROUP = 4
ATTN_WIDTH = 512
KV_WIDTH = 128
BLOCK = 128
POOL_HALO = 8

CHUNK = 128
SG_GROUPS = 8
SG_CH = 128

N_EXPERTS = 8

ROUTE_W = 512
ROUTE_PER_STEP = 2
GRAN = 16
CHUNK_SLOTS = 2 * ROUTE_W + N_EXPERTS * GRAN
CHUNK_GRANS = CHUNK_SLOTS // GRAN
MOE_SUB = 256
MOE_FAST = 2048
MOE_TM = MOE_FAST + MOE_SUB
TILE_GRANS = MOE_TM // GRAN
SUB_GRANS = MOE_SUB // GRAN
MOE_TF = 512

VMEM_LIMIT = 48 * 1024 * 1024
VMEM_LIMIT_BIG = 56 * 1024 * 1024

bf16 = jnp.bfloat16
f32 = jnp.float32


def _dot(a, b):
    return jnp.dot(a, b, preferred_element_type=f32)


def _split_bf16(a):
    hi = a.astype(bf16)
    lo = (a - hi.astype(f32)).astype(bf16)
    return hi, lo


def _layer_norm(x, g, b):
    mu = jnp.mean(x, axis=-1, keepdims=True)
    xc = x - mu
    var = jnp.mean(xc * xc, axis=-1, keepdims=True)
    return xc * lax.rsqrt(var + LN_EPS) * g + b


def _silu(x):
    return x * jax.nn.sigmoid(x)


def _gelu_tanh(x):
    c = math.sqrt(2.0 / math.pi)
    return x * (0.5 * (1.0 + jnp.tanh(c * (x + 0.044715 * (x * x * x)))))


def _adaln_kernel(c_ref, w_ref, b_ref, o_ref):
    cond = _silu(c_ref[...])
    c_hi, c_lo = _split_bf16(cond)
    w_hi, w_lo = _split_bf16(w_ref[...])
    acc = _dot(c_hi, w_hi) + (_dot(c_lo, w_hi) + _dot(c_hi, w_lo))
    o_ref[...] = acc + b_ref[...]


def _adaln(c, ada_w, ada_b):
    B = c.shape[0]
    tn = 2048
    c_pad = jnp.zeros((8, D), f32).at[:B].set(c)
    out = pl.pallas_call(
        _adaln_kernel,
        grid=(DEPTH, 6 * D // tn),
        in_specs=[
            pl.BlockSpec((8, D), lambda l, j: (0, 0)),
            pl.BlockSpec((None, D, tn), lambda l, j: (l, 0, j)),
            pl.BlockSpec((None, 1, tn), lambda l, j: (l, 0, j)),
        ],
        out_specs=pl.BlockSpec((None, 8, tn), lambda l, j: (l, 0, j)),
        out_shape=jax.ShapeDtypeStruct((DEPTH, 8, 6 * D), f32),
        compiler_params=pltpu.CompilerParams(vmem_limit_bytes=VMEM_LIMIT),
        name="adaln",
    )(c_pad, ada_w, ada_b.reshape(DEPTH, 1, 6 * D))
    return out[:, :B].reshape(DEPTH, B, 6, D)


def _ev_in_kernel(x_ref, mod_ref, wpk_ref, wqvt_ref, p_ref, k_ref, qt_ref, vt_ref):
    h = (x_ref[...] * (1.0 + mod_ref[1:2, :]) + mod_ref[0:1, :]).astype(bf16)
    zpk = _dot(h, wpk_ref[...])
    p_ref[...] = zpk[:, :POOL_WIDTH]
    k_ref[...] = zpk[:, POOL_WIDTH:].astype(bf16)
    zt = lax.dot_general(wqvt_ref[...], h, (((1,), (1,)), ((), ())), preferred_element_type=f32)
    qt_ref[...] = (zt[:ATTN_WIDTH] * (HEAD_DIM ** -0.5)).astype(bf16)
    vt_ref[...] = zt[ATTN_WIDTH:].astype(bf16)


def _ev_in(x2d, mod_l, w_pk, w_qv_t, S):
    T = x2d.shape[0]
    tm = 1024
    tpb = S // tm
    return pl.pallas_call(
        _ev_in_kernel,
        grid=(T // tm,),
        in_specs=[
            pl.BlockSpec((tm, D), lambda i: (i, 0)),
            pl.BlockSpec((None, 6, D), lambda i: (i // tpb, 0, 0)),
            pl.BlockSpec((D, POOL_WIDTH + KV_WIDTH), lambda i: (0, 0)),
            pl.BlockSpec((ATTN_WIDTH + KV_WIDTH, D), lambda i: (0, 0)),
        ],
        out_specs=[
            pl.BlockSpec((tm, POOL_WIDTH), lambda i: (i, 0)),
            pl.BlockSpec((tm, KV_WIDTH), lambda i: (i, 0)),
            pl.BlockSpec((ATTN_WIDTH, tm), lambda i: (0, i)),
            pl.BlockSpec((KV_WIDTH, tm), lambda i: (0, i)),
        ],
        out_shape=[
            jax.ShapeDtypeStruct((T, POOL_WIDTH), f32),
            jax.ShapeDtypeStruct((T, KV_WIDTH), bf16),
            jax.ShapeDtypeStruct((ATTN_WIDTH, T), bf16),
            jax.ShapeDtypeStruct((KV_WIDTH, T), bf16),
        ],
        compiler_params=pltpu.CompilerParams(vmem_limit_bytes=VMEM_LIMIT),
        name="ev_in",
    )(x2d, mod_l, w_pk, w_qv_t)


def _ev_mix_kernel(x_ref, mod_ref, p_ref, pp_ref, pn_ref, qt_ref,
                   k_ref, kp_ref, kn_ref, vt_ref, vtp_ref, vtn_ref,
                   bias_ref, sink_ref, wpool_ref, pscale_ref, wout_hbm_ref, lng_ref, lnb_ref,
                   o_ref, pext_ref, lvl_ref, pooled_ref, kext_ref, vext_ref, ybt_ref, mix_ref,
                   wout_ref, stage_ref, sem, *, S, tq):
    i = pl.program_id(1)

    @pl.when((pl.program_id(0) == 0) & (i == 0))
    def _():
        _load_rounded(wout_hbm_ref, wout_ref, stage_ref, sem)

    n_tiles = S // tq
    is_first = i == 0
    is_last = i == n_tiles - 1
    H = POOL_HALO

    p = p_ref[...]
    pext_ref[0:H, :] = jnp.where(is_first, 0.0, pp_ref[...])
    pext_ref[H:H + tq, :] = p
    pext_ref[H + tq:2 * H + tq, :] = jnp.where(is_last, 0.0, pn_ref[...])
    pext_ref[2 * H + tq:, :] = jnp.zeros((pext_ref.shape[0] - 2 * H - tq, POOL_WIDTH), f32)
    near = lax.broadcasted_iota(jnp.int32, (H, 1), 0)

    def src_rows(src, k, off, n, cs):
        return src[off:off + n, cs] if src is pext_ref else src[k - 1, off:off + n, cs]

    def pool_group(g):
        w = POOL_WINDOWS[g]
        cs = slice(g * POOL_CH, (g + 1) * POOL_CH)
        r = w // 2
        src, length, k = pext_ref, pext_ref.shape[0] - H, 0
        while 2 ** k < r:
            step = 2 ** k
            lvl_ref[k, 0:length, cs] = src_rows(src, k, 0, length, cs) + src_rows(src, k, step, length, cs)
            src, length, k = lvl_ref, length - H, k + 1
        half_run = 2 ** k
        wsum = (src_rows(src, k, H - r, tq, cs) + src_rows(src, k, H - r + half_run, tq, cs)
                + pext_ref[H + r:H + r + tq, cs])
        pooled_ref[:, cs] = wsum / float(w + 1) - p[:, cs]
        cnt_head = (jnp.minimum(near, r) + (r + 1)).astype(f32)
        cnt_tail = (jnp.minimum(H - 1 - near, r) + (r + 1)).astype(f32)
        cnt_head = jnp.where(is_first, cnt_head, float(w + 1))
        cnt_tail = jnp.where(is_last, cnt_tail, float(w + 1))
        pooled_ref[0:H, cs] = wsum[0:H] / cnt_head - p[0:H, cs]
        pooled_ref[tq - H:tq, cs] = wsum[tq - H:tq] / cnt_tail - p[tq - H:tq, cs]
        ya = _dot(pooled_ref[:, cs].astype(bf16), wpool_ref[g])
        mix_ref[:, cs] = (ya * pscale_ref[:, cs]).astype(bf16)

    for g in range(len(POOL_WINDOWS)):
        pool_group(g)

    kext_ref[0:BLOCK, :] = kp_ref[...]
    kext_ref[BLOCK:BLOCK + tq, :] = k_ref[...]
    kext_ref[BLOCK + tq:, :] = kn_ref[...]
    vext_ref[:, 0:BLOCK] = vtp_ref[...]
    vext_ref[:, BLOCK:BLOCK + tq] = vt_ref[...]
    vext_ref[:, BLOCK + tq:] = vtn_ref[...]

    n_blocks = S // BLOCK
    zeros_q = jnp.zeros((HEAD_DIM, Q_GROUP * BLOCK), bf16)
    items = [(n, kvh) for n in range(tq // BLOCK) for kvh in range(N_KV_HEADS)]

    def scores(n, kvh):
        gb = i * (tq // BLOCK) + n
        variant = jnp.where(gb == 0, 1, jnp.where(gb == n_blocks - 1, 2, 0))
        cols = slice(n * BLOCK, (n + 1) * BLOCK)
        kw = kext_ref[n * BLOCK:n * BLOCK + 3 * BLOCK, :]
        qst = jnp.concatenate(
            [qt_ref[(kvh * Q_GROUP + gq) * HEAD_DIM:(kvh * Q_GROUP + gq + 1) * HEAD_DIM, cols]
             for gq in range(Q_GROUP)], axis=1)
        qst = jnp.concatenate([qst, zeros_q] if kvh == 0 else [zeros_q, qst], axis=0)
        return _dot(kw, qst) + bias_ref[variant, kvh]

    def softmax(n, kvh, s):
        sink = sink_ref[kvh]
        m = jnp.maximum(jnp.max(s, axis=0, keepdims=True), sink)
        e = jnp.exp(s - m)
        denom = jnp.sum(e, axis=0, keepdims=True) + jnp.exp(sink - m)
        return e.astype(bf16), denom

    def weighted_values(n, kvh, e, denom):
        cols = slice(n * BLOCK, (n + 1) * BLOCK)
        vwt = vext_ref[kvh * HEAD_DIM:(kvh + 1) * HEAD_DIM, n * BLOCK:n * BLOCK + 3 * BLOCK]
        out = _dot(vwt, e) / denom
        for gq in range(Q_GROUP):
            hq = kvh * Q_GROUP + gq
            ybt_ref[hq * HEAD_DIM:(hq + 1) * HEAD_DIM, cols] = out[:, gq * BLOCK:(gq + 1) * BLOCK]

    s_next = scores(*items[0])
    prev = None
    for idx, item in enumerate(items):
        s_cur = s_next
        if idx + 1 < len(items):
            s_next = scores(*items[idx + 1])
        cur = softmax(*item, s_cur)
        if prev is not None:
            weighted_values(*items[idx - 1], *prev)
        prev = cur
    weighted_values(*items[-1], *prev)
    mix_ref[:, POOL_WIDTH:] = ybt_ref[...].T.astype(bf16)

    halves = [slice(r * (tq // 2), (r + 1) * (tq // 2)) for r in range(2)]
    ys = [_dot(mix_ref[rr, :], wout_ref[...]) for rr in halves]
    for rr, y in zip(halves, ys):
        xr = ALPHA * x_ref[rr, :] + (1.0 + mod_ref[2:3, :]) * y
        o_ref[rr, :] = _layer_norm(xr, lng_ref[...], lnb_ref[...])


def _ev_mix(x2d, mod_l, p, qt, k, vt, w_pool, pool_scale, sink, w_out, ln_g, ln_b, B, S):
    T = x2d.shape[0]
    tq = 512
    nt = S // tq
    kb = tq // BLOCK
    pb = tq // POOL_HALO
    n_kblocks = T // BLOCK
    n_pblocks = T // POOL_HALO

    def main(b, i): return (b * nt + i, 0)
    def kprev(b, i): return (jnp.maximum((b * nt + i) * kb - 1, 0), 0)
    def knext(b, i): return (jnp.minimum((b * nt + i + 1) * kb, n_kblocks - 1), 0)
    def pprev(b, i): return (jnp.maximum((b * nt + i) * pb - 1, 0), 0)
    def pnext(b, i): return (jnp.minimum((b * nt + i + 1) * pb, n_pblocks - 1), 0)
    def const2(b, i): return (0, 0)

    def tmain(b, i): return (0, b * nt + i)
    def tprev(b, i): return (0, jnp.maximum((b * nt + i) * kb - 1, 0))
    def tnext(b, i): return (0, jnp.minimum((b * nt + i + 1) * kb, n_kblocks - 1))

    assert S // BLOCK >= 2
    kj = np.arange(3 * BLOCK)[:, None]
    qi = np.arange(BLOCK)[None, :]
    dist = np.abs(kj - BLOCK - qi)
    slopes = np.float32(2.0) ** (np.float32(-8.0) * np.arange(1, N_Q_HEADS + 1, dtype=np.float32) / N_Q_HEADS)
    alibi = -slopes[:, None, None] * dist.astype(np.float32)[None]
    in_window = dist <= BLOCK
    key_ok = np.stack([kj >= 0, kj >= BLOCK, kj < 2 * BLOCK])
    bias = np.where((in_window[None] & key_ok)[:, None], alibi[None], np.float32(-1e30))
    bias = bias.reshape(3, N_KV_HEADS, Q_GROUP, 3 * BLOCK, BLOCK).transpose(0, 1, 3, 2, 4)
    bias = jnp.asarray(bias.reshape(3, N_KV_HEADS, 3 * BLOCK, Q_GROUP * BLOCK), f32)
    sink_row = jnp.repeat(sink.astype(f32).reshape(N_KV_HEADS, Q_GROUP), BLOCK, axis=1)[:, None, :]

    assert tq >= 2 * POOL_HALO and max(POOL_WINDOWS) // 2 <= POOL_HALO and nt >= 2

    kernel = functools.partial(_ev_mix_kernel, S=S, tq=tq)
    return pl.pallas_call(
        kernel,
        grid=(B, nt),
        in_specs=[
            pl.BlockSpec((tq, D), main),
            pl.BlockSpec((None, 6, D), lambda b, i: (b, 0, 0)),
            pl.BlockSpec((tq, POOL_WIDTH), main),
            pl.BlockSpec((POOL_HALO, POOL_WIDTH), pprev),
            pl.BlockSpec((POOL_HALO, POOL_WIDTH), pnext),
            pl.BlockSpec((ATTN_WIDTH, tq), tmain),
            pl.BlockSpec((tq, KV_WIDTH), main),
            pl.BlockSpec((BLOCK, KV_WIDTH), kprev),
            pl.BlockSpec((BLOCK, KV_WIDTH), knext),
            pl.BlockSpec((KV_WIDTH, tq), tmain),
            pl.BlockSpec((KV_WIDTH, BLOCK), tprev),
            pl.BlockSpec((KV_WIDTH, BLOCK), tnext),
            pl.BlockSpec((3, N_KV_HEADS, 3 * BLOCK, Q_GROUP * BLOCK), lambda b, i: (0, 0, 0, 0)),
            pl.BlockSpec((N_KV_HEADS, 1, Q_GROUP * BLOCK), lambda b, i: (0, 0, 0)),
            pl.BlockSpec((len(POOL_WINDOWS), POOL_CH, POOL_CH), lambda b, i: (0, 0, 0)),
            pl.BlockSpec((1, POOL_WIDTH), const2),
            pl.BlockSpec(memory_space=pl.ANY),
            pl.BlockSpec((1, D), const2),
            pl.BlockSpec((1, D), const2),
        ],
        out_specs=pl.BlockSpec((tq, D), main),
        out_shape=jax.ShapeDtypeStruct((T, D), f32),
        scratch_shapes=[
            pltpu.VMEM((tq + 4 * POOL_HALO, POOL_WIDTH), f32),
            pltpu.VMEM((3, tq + 3 * POOL_HALO, POOL_WIDTH), f32),
            pltpu.VMEM((tq, POOL_WIDTH), f32),
            pltpu.VMEM((tq + 2 * BLOCK, KV_WIDTH), bf16),
            pltpu.VMEM((KV_WIDTH, tq + 2 * BLOCK), bf16),
            pltpu.VMEM((ATTN_WIDTH, tq), f32),
            pltpu.VMEM((tq, D), bf16),
            pltpu.VMEM((D, D), bf16),
            pltpu.VMEM((2, D // 8, D), f32),
            pltpu.SemaphoreType.DMA((2,)),
        ],
        compiler_params=pltpu.CompilerParams(vmem_limit_bytes=VMEM_LIMIT),
        name="ev_mix",
    )(x2d, mod_l, p, p, p, qt, k, k, k, vt, vt, vt, bias, sink_row, w_pool, pool_scale, w_out, ln_g, ln_b)


def _load_rounded(w_hbm_ref, w_ref, stage_ref, sem):
    rows = stage_ref.shape[1]
    n = w_hbm_ref.shape[0] // rows

    def copy(c):
        return pltpu.make_async_copy(w_hbm_ref.at[pl.ds(c * rows, rows), :], stage_ref.at[c % 2], sem.at[c % 2])

    copy(0).start()
    for c in range(n):
        if c + 1 < n:
            copy(c + 1).start()
        copy(c).wait()
        w_ref[pl.ds(c * rows, rows), :] = stage_ref[c % 2].astype(bf16)


def _ffn_kernel(x_ref, mod_ref, wg_hbm_ref, wu_hbm_ref, wd_hbm_ref, lng_ref, lnb_ref, o_ref,
                wg_ref, wu_ref, wd_ref, stage_in_ref, stage_out_ref, sem):
    @pl.when(pl.program_id(0) == 0)
    def _():
        _load_rounded(wg_hbm_ref, wg_ref, stage_in_ref, sem)
        _load_rounded(wu_hbm_ref, wu_ref, stage_in_ref, sem)
        _load_rounded(wd_hbm_ref, wd_ref, stage_out_ref, sem)

    piece = 256
    n_pieces = x_ref.shape[0] // piece
    parts = [pl.ds(q * piece, piece) for q in range(n_pieces)]

    def gate_up(rows):
        x = x_ref[rows, :]
        h = (x * (1.0 + mod_ref[4:5, :]) + mod_ref[3:4, :]).astype(bf16)
        return x, _dot(h, wg_ref[...]), _dot(h, wu_ref[...])

    nxt = gate_up(parts[0])
    for q, rows in enumerate(parts):
        x, g, u = nxt
        if q + 1 < n_pieces:
            nxt = gate_up(parts[q + 1])
        y = _dot((_silu(g) * u).astype(bf16), wd_ref[...])
        xr = ALPHA * x + (1.0 + mod_ref[5:6, :]) * y
        o_ref[rows, :] = _layer_norm(xr, lng_ref[...], lnb_ref[...])


def _ffn(x2d, mod_l, w_gate, w_up, w_down, ln_g, ln_b, S):
    T = x2d.shape[0]
    tm = 512
    dff = w_gate.shape[1]
    tpb = S // tm
    n_stage = 8
    return pl.pallas_call(
        _ffn_kernel,
        grid=(T // tm,),
        in_specs=[
            pl.BlockSpec((tm, D), lambda i: (i, 0)),
            pl.BlockSpec((None, 6, D), lambda i: (i // tpb, 0, 0)),
            pl.BlockSpec(memory_space=pl.ANY),
            pl.BlockSpec(memory_space=pl.ANY),
            pl.BlockSpec(memory_space=pl.ANY),
            pl.BlockSpec((1, D), lambda i: (0, 0)),
            pl.BlockSpec((1, D), lambda i: (0, 0)),
        ],
        out_specs=pl.BlockSpec((tm, D), lambda i: (i, 0)),
        out_shape=jax.ShapeDtypeStruct((T, D), f32),
        scratch_shapes=[
            pltpu.VMEM((D, dff), bf16),
            pltpu.VMEM((D, dff), bf16),
            pltpu.VMEM((dff, D), bf16),
            pltpu.VMEM((2, D // n_stage, dff), f32),
            pltpu.VMEM((2, dff // n_stage, D), f32),
            pltpu.SemaphoreType.DMA((2,)),
        ],
        compiler_params=pltpu.CompilerParams(vmem_limit_bytes=VMEM_LIMIT_BIG),
        name="ffn",
    )(x2d, mod_l, w_gate, w_up, w_down, ln_g, ln_b)


def _sg_kernel(x_ref, mod_ref, win_hbm_ref, sgg_ref, sgb_ref, ws_ref, bst_ref, wout_hbm_ref, lng_ref, lnb_ref,
               o_ref, gate_ref, win_ref, wout_ref, stage_in_ref, stage_out_ref, sem, *, tm):
    @pl.when(pl.program_id(0) == 0)
    def _():
        _load_rounded(win_hbm_ref, win_ref, stage_in_ref, sem)
        _load_rounded(wout_hbm_ref, wout_ref, stage_out_ref, sem)

    piece = 256
    n_pieces = tm // piece

    def project(q):
        x = x_ref[q * piece:(q + 1) * piece, :]
        h = (x * (1.0 + mod_ref[1:2, :]) + mod_ref[0:1, :]).astype(bf16)
        return x, _dot(h, win_ref[:, D:]), _dot(h, win_ref[:, :D])

    nxt = project(0)
    for q in range(n_pieces):
        x, zv, zu = nxt
        if q + 1 < n_pieces:
            nxt = project(q + 1)
        v = _layer_norm(_gelu_tanh(zv), sgg_ref[...], sgb_ref[...]).astype(bf16)
        u = _gelu_tanh(zu)
        for n in range(piece // CHUNK):
            rows = slice(n * CHUNK, (n + 1) * CHUNK)
            grows = slice(q * piece + n * CHUNK, q * piece + (n + 1) * CHUNK)
            for g in range(SG_GROUPS):
                cols = slice(g * SG_CH, (g + 1) * SG_CH)
                sv = _dot(ws_ref[g], v[rows, cols]) + bst_ref[:, g:g + 1]
                gate_ref[grows, cols] = (u[rows, cols] * sv).astype(bf16)
        y = _dot(gate_ref[q * piece:(q + 1) * piece, :], wout_ref[...])
        xr = ALPHA * x + (1.0 + mod_ref[2:3, :]) * y
        o_ref[q * piece:(q + 1) * piece, :] = _layer_norm(xr, lng_ref[...], lnb_ref[...])


def _sg_mix(x2d, mod_l, w_in, sg_g, sg_b, w_s, b_s_t, w_out, ln_g, ln_b, S):
    T = x2d.shape[0]
    tm = 512
    tpb = S // tm
    c2 = lambda i: (0, 0)
    n_stage = 8
    return pl.pallas_call(
        functools.partial(_sg_kernel, tm=tm),
        grid=(T // tm,),
        in_specs=[
            pl.BlockSpec((tm, D), lambda i: (i, 0)),
            pl.BlockSpec((None, 6, D), lambda i: (i // tpb, 0, 0)),
            pl.BlockSpec(memory_space=pl.ANY),
            pl.BlockSpec((1, D), c2),
            pl.BlockSpec((1, D), c2),
            pl.BlockSpec((SG_GROUPS, CHUNK, CHUNK), lambda i: (0, 0, 0)),
            pl.BlockSpec((CHUNK, SG_GROUPS), c2),
            pl.BlockSpec(memory_space=pl.ANY),
            pl.BlockSpec((1, D), c2),
            pl.BlockSpec((1, D), c2),
        ],
        out_specs=pl.BlockSpec((tm, D), lambda i: (i, 0)),
        out_shape=jax.ShapeDtypeStruct((T, D), f32),
        scratch_shapes=[
            pltpu.VMEM((tm, D), bf16),
            pltpu.VMEM((D, 2 * D), bf16),
            pltpu.VMEM((D, D), bf16),
            pltpu.VMEM((2, D // n_stage, 2 * D), f32),
            pltpu.VMEM((2, D // n_stage, D), f32),
            pltpu.SemaphoreType.DMA((2,)),
        ],
        compiler_params=pltpu.CompilerParams(vmem_limit_bytes=VMEM_LIMIT),
        name="sg_mix",
    )(x2d, mod_l, w_in, sg_g, sg_b, w_s, b_s_t, w_out, ln_g, ln_b)


def _route_kernel(x_ref, mod_ref, wrt_ref, hs_ref, route_ref, cnt_ref):
    W = ROUTE_W
    w_hi, w_lo = _split_bf16(wrt_ref[...])
    nt = (((1,), (1,)), ((), ()))
    eidx = lax.broadcasted_iota(jnp.int32, (N_EXPERTS, W), 0)
    sub = lax.broadcasted_iota(jnp.int32, (N_EXPERTS, 1), 0)
    tr = lax.broadcasted_iota(jnp.int32, (W, W), 0)
    tc = lax.broadcasted_iota(jnp.int32, (W, W), 1)
    upper = (tr < tc).astype(bf16)
    srow = lax.broadcasted_iota(jnp.int32, (CHUNK_SLOTS, W), 0)
    ridx = lax.broadcasted_iota(jnp.int32, (8, W), 0)

    def assign_slots(k):
        h = x_ref[k * W:(k + 1) * W, :] * (1.0 + mod_ref[4:5, :]) + mod_ref[3:4, :]
        h_hi, h_lo = _split_bf16(h)
        logits = (lax.dot_general(w_hi, h_hi, nt, preferred_element_type=f32)
                  + (lax.dot_general(w_hi, h_lo, nt, preferred_element_type=f32)
                     + lax.dot_general(w_lo, h_hi, nt, preferred_element_type=f32)))
        m1 = jnp.max(logits, axis=0, keepdims=True)
        i1 = jnp.min(jnp.where(logits == m1, eidx, N_EXPERTS), axis=0, keepdims=True)
        sel1 = eidx == i1
        rest = jnp.where(sel1, -jnp.inf, logits)
        m2 = jnp.max(rest, axis=0, keepdims=True)
        i2 = jnp.min(jnp.where(rest == m2, eidx, N_EXPERTS), axis=0, keepdims=True)
        sel2 = eidx == i2
        e2 = jnp.exp(m2 - m1)
        g1 = 1.0 / (1.0 + e2)
        g2 = e2 / (1.0 + e2)

        a1 = sel1.astype(f32)
        a2 = sel2.astype(f32)
        assign = a1 + a2
        counts = jnp.sum(assign, axis=1, keepdims=True)
        grans = jnp.ceil(counts * (1.0 / GRAN))
        seg = jnp.zeros((N_EXPERTS, 1), f32)
        for e in range(N_EXPERTS - 1):
            seg = seg + jnp.where(sub > e, grans[e:e + 1, :] * GRAN, 0.0)
        rank = _dot(assign.astype(bf16), upper)
        slot = seg + rank
        pos1 = jnp.sum(a1 * slot, axis=0, keepdims=True)
        pos2 = jnp.sum(a2 * slot, axis=0, keepdims=True)
        route_ref[k] = jnp.where(ridx == 0, pos1, jnp.where(ridx == 1, pos2,
                                 jnp.where(ridx == 2, g1, jnp.where(ridx == 3, g2, 0.0)))).T
        cnt_ref[k] = jnp.broadcast_to(counts, (N_EXPERTS, 128)).astype(jnp.int32)
        return h_hi, pos1, pos2

    def sort_rows(k, h_hi, pos1, pos2):
        perm = ((srow == pos1.astype(jnp.int32)) | (srow == pos2.astype(jnp.int32)))
        hs_ref[k] = _dot(perm.astype(f32).astype(bf16), h_hi).astype(bf16)

    nxt = assign_slots(0)
    for k in range(ROUTE_PER_STEP):
        cur = nxt
        if k + 1 < ROUTE_PER_STEP:
            nxt = assign_slots(k + 1)
        sort_rows(k, *cur)


def _route(x2d, mod_l, w_router_t, S):
    T = x2d.shape[0]
    W = ROUTE_W
    R = ROUTE_PER_STEP
    nc = T // W
    tpb = S // (R * W)
    return pl.pallas_call(
        _route_kernel,
        grid=(nc // R,),
        in_specs=[
            pl.BlockSpec((R * W, D), lambda c: (c, 0)),
            pl.BlockSpec((None, 6, D), lambda c: (c // tpb, 0, 0)),
            pl.BlockSpec((N_EXPERTS, D), lambda c: (0, 0)),
        ],
        out_specs=[
            pl.BlockSpec((R, CHUNK_SLOTS, D), lambda c: (c, 0, 0)),
            pl.BlockSpec((R, W, 8), lambda c: (c, 0, 0)),
            pl.BlockSpec((R, N_EXPERTS, 128), lambda c: (c, 0, 0)),
        ],
        out_shape=[
            jax.ShapeDtypeStruct((nc, CHUNK_SLOTS, D), bf16),
            jax.ShapeDtypeStruct((nc, W, 8), f32),
            jax.ShapeDtypeStruct((nc, N_EXPERTS, 128), jnp.int32),
        ],
        compiler_params=pltpu.CompilerParams(vmem_limit_bytes=VMEM_LIMIT),
        name="route",
    )(x2d, mod_l, w_router_t)


def _granule_copy(src_ref, buf_ref, sem, idx_ref, base, g):
    row = pl.multiple_of(idx_ref[base + g] * GRAN, GRAN)
    return pltpu.make_async_copy(src_ref.at[pl.ds(row, GRAN), :], buf_ref.at[pl.ds(g * GRAN, GRAN), :], sem)


def _gather_start(src_ref, buf_ref, sem, idx_ref, base, n):
    for g in range(n):
        _granule_copy(src_ref, buf_ref, sem, idx_ref, base, g).start()


def _gather_wait(src_ref, buf_ref, sem, idx_ref, base, n):
    for g in range(n):
        _granule_copy(src_ref, buf_ref, sem, idx_ref, base, g).wait()


def _expert_kernel(te_ref, tv_ref, src_ref, hs_ref, wg_ref, wu_ref, wd_ref, o_ref,
                   xbuf_ref, sem, acc_ref, wgb_ref, wub_ref, wdb_ref):
    i = pl.program_id(0)
    j = pl.program_id(1)
    n_tiles = pl.num_programs(0)
    last = pl.num_programs(1) - 1
    valid = tv_ref[i] > 0
    slot = i % 2
    nxt = jnp.minimum(i + 1, n_tiles - 1)

    @pl.when(j == 0)
    def _():
        @pl.when(i == 0)
        def _():
            _gather_start(hs_ref, xbuf_ref.at[0], sem.at[0], src_ref, 0, TILE_GRANS)

        @pl.when(valid)
        def _():
            _gather_wait(hs_ref, xbuf_ref.at[slot], sem.at[slot], src_ref, i * TILE_GRANS, TILE_GRANS)

        @pl.when((i + 1 < n_tiles) & (tv_ref[nxt] > 0))
        def _():
            _gather_start(hs_ref, xbuf_ref.at[1 - slot], sem.at[1 - slot], src_ref, nxt * TILE_GRANS, TILE_GRANS)

    n_sub = tv_ref[i]
    fast_subs = MOE_FAST // MOE_SUB
    fast = n_sub >= fast_subs

    def round_weights():
        wgb_ref[...] = wg_ref[...].astype(bf16)
        wub_ref[...] = wu_ref[...].astype(bf16)
        wdb_ref[...] = wd_ref[...].astype(bf16)

    def swiglu(x):
        a = _silu(_dot(x, wgb_ref[...])) * _dot(x, wub_ref[...])
        return _dot(a.astype(bf16), wdb_ref[...])

    sparse = valid & jnp.logical_not(fast)

    @pl.when((sparse | (i == 0)) & (j == 0))
    def _():
        acc_ref[...] = jnp.zeros_like(acc_ref)

    def dense_block(n_pieces):
        round_weights()
        parts = [pl.ds(q * MOE_SUB, MOE_SUB) for q in range(n_pieces)]

        def gate_up(rows):
            x = xbuf_ref[slot, rows, :]
            return _dot(x, wgb_ref[...]), _dot(x, wub_ref[...])

        nxt_gu = gate_up(parts[0])
        for q, rows in enumerate(parts):
            g, u = nxt_gu
            if q + 1 < len(parts):
                nxt_gu = gate_up(parts[q + 1])
            y = _dot((_silu(g) * u).astype(bf16), wdb_ref[...])
            total = jnp.where(j == 0, y, acc_ref[rows, :] + y)
            acc_ref[rows, :] = total
            o_ref[rows, :] = total.astype(o_ref.dtype)
        if n_pieces * MOE_SUB < MOE_TM:
            o_ref[n_pieces * MOE_SUB:, :] = jnp.zeros((MOE_TM - n_pieces * MOE_SUB, D), o_ref.dtype)

    for n_dense in (MOE_TM // MOE_SUB, fast_subs):
        pl.when(n_sub == n_dense)(functools.partial(dense_block, n_dense))

    @pl.when(sparse)
    def _():
        round_weights()

        def body(sb, carry):
            rows = pl.ds(pl.multiple_of(sb * MOE_SUB, MOE_SUB), MOE_SUB)
            acc_ref[rows, :] += swiglu(xbuf_ref[slot, rows, :])
            return carry

        lax.fori_loop(0, n_sub, body, 0)

    @pl.when(j == last)
    def _():
        @pl.when(sparse)
        def _():
            o_ref[...] = acc_ref[...].astype(o_ref.dtype)

        @pl.when(jnp.logical_not(valid))
        def _():
            o_ref[...] = jnp.zeros_like(o_ref)


def _experts(tile_expert, tile_valid, src_of_dst, hs2d, w_gate, w_up, w_down):
    n_tiles = tile_expert.shape[0]
    dff = w_gate.shape[2]
    nff = dff // MOE_TF

    def jj(j, tv, i):
        return jnp.where(tv[i] > 0, j, nff - 1)

    grid_spec = pltpu.PrefetchScalarGridSpec(
        num_scalar_prefetch=3,
        grid=(n_tiles, nff),
        in_specs=[
            pl.BlockSpec(memory_space=pl.ANY),
            pl.BlockSpec((None, D, MOE_TF), lambda i, j, te, tv, sd: (te[i], 0, jj(j, tv, i))),
            pl.BlockSpec((None, D, MOE_TF), lambda i, j, te, tv, sd: (te[i], 0, jj(j, tv, i))),
            pl.BlockSpec((None, MOE_TF, D), lambda i, j, te, tv, sd: (te[i], jj(j, tv, i), 0)),
        ],
        out_specs=pl.BlockSpec((MOE_TM, D), lambda i, j, te, tv, sd: (i, 0)),
        scratch_shapes=[
            pltpu.VMEM((2, MOE_TM, D), bf16),
            pltpu.SemaphoreType.DMA((2,)),
            pltpu.VMEM((MOE_TM, D), f32),
            pltpu.VMEM((D, MOE_TF), bf16),
            pltpu.VMEM((D, MOE_TF), bf16),
            pltpu.VMEM((MOE_TF, D), bf16),
        ],
    )
    return pl.pallas_call(
        _expert_kernel,
        grid_spec=grid_spec,
        out_shape=jax.ShapeDtypeStruct((n_tiles * MOE_TM, D), bf16),
        compiler_params=pltpu.CompilerParams(vmem_limit_bytes=VMEM_LIMIT_BIG),
        name="experts",
    )(tile_expert, tile_valid, src_of_dst, hs2d, w_gate, w_up, w_down)


def _combine_kernel(ds_ref, x_ref, mod_ref, o_hbm_ref, rt_ref, lng_ref, lnb_ref, out_ref, obuf_ref, sem):
    W = ROUTE_W
    c = pl.program_id(0)
    nc = pl.num_programs(0)
    slot = c % 2
    nxt = jnp.minimum(c + 1, nc - 1)

    @pl.when(c == 0)
    def _():
        _gather_start(o_hbm_ref, obuf_ref.at[0], sem.at[0], ds_ref, 0, CHUNK_GRANS)

    _gather_wait(o_hbm_ref, obuf_ref.at[slot], sem.at[slot], ds_ref, c * CHUNK_GRANS, CHUNK_GRANS)

    _gather_start(o_hbm_ref, obuf_ref.at[1 - slot], sem.at[1 - slot], ds_ref, nxt * CHUNK_GRANS, CHUNK_GRANS)

    osv = obuf_ref[slot]
    scol = lax.broadcasted_iota(jnp.int32, (W // 2, CHUNK_SLOTS), 1)
    for r in range(2):
        rr = slice(r * (W // 2), (r + 1) * (W // 2))
        rt = rt_ref[rr, :]
        p1 = (scol == rt[:, 0:1].astype(jnp.int32)).astype(f32).astype(bf16)
        p2 = (scol == rt[:, 1:2].astype(jnp.int32)).astype(f32).astype(bf16)
        y = rt[:, 2:3] * _dot(p1, osv) + rt[:, 3:4] * _dot(p2, osv)
        xr = ALPHA * x_ref[rr, :] + (1.0 + mod_ref[5:6, :]) * y
        out_ref[rr, :] = _layer_norm(xr, lng_ref[...], lnb_ref[...])

    @pl.when(c == nc - 1)
    def _():
        _gather_wait(o_hbm_ref, obuf_ref.at[1 - slot], sem.at[1 - slot], ds_ref, nxt * CHUNK_GRANS, CHUNK_GRANS)


def _combine(dst_of_src, x2d, mod_l, o2d, route_t, ln_g, ln_b, S):
    T = x2d.shape[0]
    W = ROUTE_W
    tpb = S // W
    grid_spec = pltpu.PrefetchScalarGridSpec(
        num_scalar_prefetch=1,
        grid=(T // W,),
        in_specs=[
            pl.BlockSpec((W, D), lambda c, ds: (c, 0)),
            pl.BlockSpec((None, 6, D), lambda c, ds: (c // tpb, 0, 0)),
            pl.BlockSpec(memory_space=pl.ANY),
            pl.BlockSpec((None, W, 8), lambda c, ds: (c, 0, 0)),
            pl.BlockSpec((1, D), lambda c, ds: (0, 0)),
            pl.BlockSpec((1, D), lambda c, ds: (0, 0)),
        ],
        out_specs=pl.BlockSpec((W, D), lambda c, ds: (c, 0)),
        scratch_shapes=[
            pltpu.VMEM((2, CHUNK_SLOTS, D), bf16),
            pltpu.SemaphoreType.DMA((2,)),
        ],
    )
    return pl.pallas_call(
        _combine_kernel,
        grid_spec=grid_spec,
        out_shape=jax.ShapeDtypeStruct((T, D), f32),
        compiler_params=pltpu.CompilerParams(vmem_limit_bytes=VMEM_LIMIT),
        name="combine",
    )(dst_of_src, x2d, mod_l, o2d, route_t, ln_g, ln_b)


def _routing_tables(counts, n_tiles):
    nc = counts.shape[0]
    gr = (counts + GRAN - 1) // GRAN
    seg_start = jnp.cumsum(gr, axis=1) - gr
    chunk_total = jnp.sum(gr, axis=1)
    prefix = jnp.cumsum(gr, axis=0) - gr
    g_e = jnp.sum(gr, axis=0)
    tiles_e = (g_e + TILE_GRANS - 1) // TILE_GRANS
    tile_end = jnp.cumsum(tiles_e)
    tile_start = tile_end - tiles_e
    total_tiles = tile_end[-1]

    i32 = jnp.int32
    er = jnp.arange(N_EXPERTS, dtype=i32)
    t = jnp.arange(n_tiles, dtype=i32)
    te = jnp.sum((t[:, None] >= tile_end[None, :]).astype(i32), axis=1)
    tile_valid = (t < total_tiles).astype(i32)
    last_e = jnp.sum((total_tiles - 1 >= tile_end).astype(i32))
    tile_expert = jnp.where(tile_valid > 0, jnp.minimum(te, N_EXPERTS - 1), last_e).astype(i32)
    oh_t = (tile_expert[:, None] == er).astype(i32)
    grans_left = jnp.sum(oh_t * (g_e - (t[:, None] - tile_start[None, :]) * TILE_GRANS), axis=1)
    tile_subs = tile_valid * jnp.clip((grans_left + SUB_GRANS - 1) // SUB_GRANS, 0, TILE_GRANS // SUB_GRANS)

    k = jnp.arange(CHUNK_GRANS, dtype=i32)
    seg_end = seg_start + gr
    e_of = jnp.sum((k[None, :, None] >= seg_end[:, None, :]).astype(i32), axis=2)
    oh_e = (jnp.minimum(e_of, N_EXPERTS - 1)[:, :, None] == er).astype(i32)
    base = tile_start[None, :] * TILE_GRANS + prefix - seg_start
    dst = jnp.sum(oh_e * base[:, None, :], axis=2) + k[None, :]
    valid_src = k[None, :] < chunk_total[:, None]
    dst_of_src = jnp.where(valid_src, dst, 0).astype(i32).reshape(-1)

    d = jnp.arange(n_tiles * TILE_GRANS, dtype=i32)
    oh_d = (jnp.repeat(tile_expert, TILE_GRANS)[:, None] == er).astype(i32)
    q = d - jnp.sum(oh_d * tile_start[None, :], axis=1) * TILE_GRANS
    incl_d = jnp.sum(oh_d[:, :, None] * (prefix + gr).T[None], axis=1)
    c_d = jnp.sum((q[:, None] >= incl_d).astype(i32), axis=1)
    oh_c = (jnp.minimum(c_d, nc - 1)[:, None] == jnp.arange(nc, dtype=i32)).astype(i32)
    cbase = jnp.arange(nc, dtype=i32)[:, None] * CHUNK_GRANS + seg_start - prefix
    sel = jnp.sum(oh_c[:, :, None] * oh_d[:, None, :] * cbase[None], axis=(1, 2))
    valid_dst = (jnp.repeat(tile_valid, TILE_GRANS) > 0) & (q >= 0) & (q < jnp.sum(oh_d * g_e[None, :], axis=1))
    src_of_dst = jnp.where(valid_dst, sel + q, 0).astype(i32)
    return tile_expert, tile_subs.astype(i32), src_of_dst, dst_of_src


def kernel(x, c, ada_w, ada_b, ln_g, ln_b, ev_w_in, ev_pool_w, ev_pool_scale, ev_sink, ev_w_out, od_w_in, od_sg_ln_g, od_sg_ln_b, od_w_s, od_b_s, od_w_out, ffn_w_gate, ffn_w_up, ffn_w_down, moe_w_router, moe_w_gate, moe_w_up, moe_w_down):
    B, S, _ = x.shape
    T = B * S
    assert x.shape[-1] == D and ada_w.shape == (DEPTH, D, 6 * D) and B <= 8
    assert ev_w_in.shape == (1, D, POOL_WIDTH + ATTN_WIDTH + 2 * KV_WIDTH) and moe_w_gate.shape[:3] == (1, N_EXPERTS, D)
    assert S % (ROUTE_PER_STEP * ROUTE_W) == 0 and moe_w_gate.shape[3] % MOE_TF == 0
    x2d = x.reshape(T, D)
    mod = _adaln(c, ada_w, ada_b)

    w_in = ev_w_in[0].astype(bf16)
    q0, k0, v0 = POOL_WIDTH, POOL_WIDTH + ATTN_WIDTH, POOL_WIDTH + ATTN_WIDTH + KV_WIDTH
    w_pk = jnp.concatenate([w_in[:, :q0], w_in[:, k0:v0]], axis=1)
    w_qv_t = jnp.concatenate([w_in[:, q0:k0], w_in[:, v0:]], axis=1).T
    p, k, qt, vt = _ev_in(x2d, mod[0], w_pk, w_qv_t, S)
    x2d = _ev_mix(x2d, mod[0], p, qt, k, vt, ev_pool_w[0].astype(bf16), ev_pool_scale[0][None, :],
                  ev_sink[0], ev_w_out[0], ln_g[0, 0][None, :], ln_b[0, 0][None, :], B, S)
    x2d = _ffn(x2d, mod[0], ffn_w_gate[0], ffn_w_up[0], ffn_w_down[0],
               ln_g[0, 1][None, :], ln_b[0, 1][None, :], S)

    x2d = _sg_mix(x2d, mod[1], od_w_in[0], od_sg_ln_g[0][None, :], od_sg_ln_b[0][None, :],
                  od_w_s[0].astype(bf16), od_b_s[0].T, od_w_out[0],
                  ln_g[1, 0][None, :], ln_b[1, 0][None, :], S)

    hs, route, cnt = _route(x2d, mod[1], moe_w_router[0].T, S)
    nc = T // ROUTE_W
    n_tiles = (nc * CHUNK_GRANS) // TILE_GRANS + N_EXPERTS
    tile_expert, tile_subs, src_of_dst, dst_of_src = _routing_tables(cnt[:, :, 0], n_tiles)
    o = _experts(tile_expert, tile_subs, src_of_dst, hs.reshape(nc * CHUNK_SLOTS, D),
                 moe_w_gate[0], moe_w_up[0], moe_w_down[0])
    x2d = _combine(dst_of_src, x2d, mod[1], o, route,
                   ln_g[1, 1][None, :], ln_b[1, 1][None, :], S)
    return x2d.reshape(B, S, D)
```

```python
import functools
import math

import jax
import jax.numpy as jnp
import numpy as np
from jax import lax
from jax.experimental import pallas as pl
from jax.experimental.pallas import tpu as pltpu

D = 1024
DEPTH = 2
ALPHA = (2.0 * DEPTH) ** 0.25
LN_EPS = 1e-5

POOL_WINDOWS = (2, 4, 8, 16)
POOL_CH = 128
POOL_WIDTH = 512
HEAD_DIM = 64
N_Q_HEADS = 8
N_KV_HEADS = 2
Q_GROUP = 4
ATTN_WIDTH = 512
KV_WIDTH = 128
BLOCK = 128
POOL_HALO = 8

CHUNK = 128
SG_GROUPS = 8
SG_CH = 128

N_EXPERTS = 8

ROUTE_W = 512
ROUTE_PER_STEP = 2
GRAN = 16
CHUNK_SLOTS = 2 * ROUTE_W + N_EXPERTS * GRAN
CHUNK_GRANS = CHUNK_SLOTS // GRAN
MOE_SUB = 256
MOE_FAST = 2048
MOE_TM = MOE_FAST + MOE_SUB
TILE_GRANS = MOE_TM // GRAN
SUB_GRANS = MOE_SUB // GRAN
MOE_TF = 512

VMEM_LIMIT = 48 * 1024 * 1024
VMEM_LIMIT_BIG = 56 * 1024 * 1024

bf16 = jnp.bfloat16
f32 = jnp.float32


def _dot(a, b):
    return jnp.dot(a, b, preferred_element_type=f32)


def _split_bf16(a):
    hi = a.astype(bf16)
    lo = (a - hi.astype(f32)).astype(bf16)
    return hi, lo


def _layer_norm(x, g, b):
    mu = jnp.mean(x, axis=-1, keepdims=True)
    xc = x - mu
    var = jnp.mean(xc * xc, axis=-1, keepdims=True)
    return xc * lax.rsqrt(var + LN_EPS) * g + b


def _silu(x):
    return x * jax.nn.sigmoid(x)


def _gelu_tanh(x):
    c = math.sqrt(2.0 / math.pi)
    return x * (0.5 * (1.0 + jnp.tanh(c * (x + 0.044715 * (x * x * x)))))


def _adaln_kernel(c_ref, w_ref, b_ref, o_ref):
    cond = _silu(c_ref[...])
    c_hi, c_lo = _split_bf16(cond)
    w_hi, w_lo = _split_bf16(w_ref[...])
    acc = _dot(c_hi, w_hi) + (_dot(c_lo, w_hi) + _dot(c_hi, w_lo))
    o_ref[...] = acc + b_ref[...]


def _adaln(c, ada_w, ada_b):
    B = c.shape[0]
    tn = 2048
    c_pad = jnp.zeros((8, D), f32).at[:B].set(c)
    out = pl.pallas_call(
        _adaln_kernel,
        grid=(DEPTH, 6 * D // tn),
        in_specs=[
            pl.BlockSpec((8, D), lambda l, j: (0, 0)),
            pl.BlockSpec((None, D, tn), lambda l, j: (l, 0, j)),
            pl.BlockSpec((None, 1, tn), lambda l, j: (l, 0, j)),
        ],
        out_specs=pl.BlockSpec((None, 8, tn), lambda l, j: (l, 0, j)),
        out_shape=jax.ShapeDtypeStruct((DEPTH, 8, 6 * D), f32),
        compiler_params=pltpu.CompilerParams(vmem_limit_bytes=VMEM_LIMIT),
        name="adaln",
    )(c_pad, ada_w, ada_b.reshape(DEPTH, 1, 6 * D))
    return out[:, :B].reshape(DEPTH, B, 6, D)


def _ev_in_kernel(x_ref, mod_ref, wpk_ref, wqvt_ref, p_ref, k_ref, qt_ref, vt_ref):
    h = (x_ref[...] * (1.0 + mod_ref[1:2, :]) + mod_ref[0:1, :]).astype(bf16)
    zpk = _dot(h, wpk_ref[...])
    p_ref[...] = zpk[:, :POOL_WIDTH]
    k_ref[...] = zpk[:, POOL_WIDTH:].astype(bf16)
    zt = lax.dot_general(wqvt_ref[...], h, (((1,), (1,)), ((), ())), preferred_element_type=f32)
    qt_ref[...] = (zt[:ATTN_WIDTH] * (HEAD_DIM ** -0.5)).astype(bf16)
    vt_ref[...] = zt[ATTN_WIDTH:].astype(bf16)


def _ev_in(x2d, mod_l, w_pk, w_qv_t, S):
    T = x2d.shape[0]
    tm = 1024
    tpb = S // tm
    return pl.pallas_call(
        _ev_in_kernel,
        grid=(T // tm,),
        in_specs=[
            pl.BlockSpec((tm, D), lambda i: (i, 0)),
            pl.BlockSpec((None, 6, D), lambda i: (i // tpb, 0, 0)),
            pl.BlockSpec((D, POOL_WIDTH + KV_WIDTH), lambda i: (0, 0)),
            pl.BlockSpec((ATTN_WIDTH + KV_WIDTH, D), lambda i: (0, 0)),
        ],
        out_specs=[
            pl.BlockSpec((tm, POOL_WIDTH), lambda i: (i, 0)),
            pl.BlockSpec((tm, KV_WIDTH), lambda i: (i, 0)),
            pl.BlockSpec((ATTN_WIDTH, tm), lambda i: (0, i)),
            pl.BlockSpec((KV_WIDTH, tm), lambda i: (0, i)),
        ],
        out_shape=[
            jax.ShapeDtypeStruct((T, POOL_WIDTH), f32),
            jax.ShapeDtypeStruct((T, KV_WIDTH), bf16),
            jax.ShapeDtypeStruct((ATTN_WIDTH, T), bf16),
            jax.ShapeDtypeStruct((KV_WIDTH, T), bf16),
        ],
        compiler_params=pltpu.CompilerParams(vmem_limit_bytes=VMEM_LIMIT),
        name="ev_in",
    )(x2d, mod_l, w_pk, w_qv_t)


def _ev_mix_kernel(x_ref, mod_ref, p_ref, pp_ref, pn_ref, qt_ref,
                   k_ref, kp_ref, kn_ref, vt_ref, vtp_ref, vtn_ref,
                   bias_ref, sink_ref, wpool_ref, pscale_ref, wout_hbm_ref, lng_ref, lnb_ref,
                   o_ref, pext_ref, lvl_ref, pooled_ref, kext_ref, vext_ref, ybt_ref, mix_ref,
                   wout_ref, stage_ref, sem, *, S, tq):
    i = pl.program_id(1)

    @pl.when((pl.program_id(0) == 0) & (i == 0))
    def _():
        _load_rounded(wout_hbm_ref, wout_ref, stage_ref, sem)

    n_tiles = S // tq
    is_first = i == 0
    is_last = i == n_tiles - 1
    H = POOL_HALO

    p = p_ref[...]
    pext_ref[0:H, :] = jnp.where(is_first, 0.0, pp_ref[...])
    pext_ref[H:H + tq, :] = p
    pext_ref[H + tq:2 * H + tq, :] = jnp.where(is_last, 0.0, pn_ref[...])
    pext_ref[2 * H + tq:, :] = jnp.zeros((pext_ref.shape[0] - 2 * H - tq, POOL_WIDTH), f32)
    near = lax.broadcasted_iota(jnp.int32, (H, 1), 0)

    def src_rows(src, k, off, n, cs):
        return src[off:off + n, cs] if src is pext_ref else src[k - 1, off:off + n, cs]

    def pool_group(g):
        w = POOL_WINDOWS[g]
        cs = slice(g * POOL_CH, (g + 1) * POOL_CH)
        r = w // 2
        src, length, k = pext_ref, pext_ref.shape[0] - H, 0
        while 2 ** k < r:
            step = 2 ** k
            lvl_ref[k, 0:length, cs] = src_rows(src, k, 0, length, cs) + src_rows(src, k, step, length, cs)
            src, length, k = lvl_ref, length - H, k + 1
        half_run = 2 ** k
        wsum = (src_rows(src, k, H - r, tq, cs) + src_rows(src, k, H - r + half_run, tq, cs)
                + pext_ref[H + r:H + r + tq, cs])
        pooled_ref[:, cs] = wsum / float(w + 1) - p[:, cs]
        cnt_head = (jnp.minimum(near, r) + (r + 1)).astype(f32)
        cnt_tail = (jnp.minimum(H - 1 - near, r) + (r + 1)).astype(f32)
        cnt_head = jnp.where(is_first, cnt_head, float(w + 1))
        cnt_tail = jnp.where(is_last, cnt_tail, float(w + 1))
        pooled_ref[0:H, cs] = wsum[0:H] / cnt_head - p[0:H, cs]
        pooled_ref[tq - H:tq, cs] = wsum[tq - H:tq] / cnt_tail - p[tq - H:tq, cs]
        ya = _dot(pooled_ref[:, cs].astype(bf16), wpool_ref[g])
        mix_ref[:, cs] = (ya * pscale_ref[:, cs]).astype(bf16)

    for g in range(len(POOL_WINDOWS)):
        pool_group(g)

    kext_ref[0:BLOCK, :] = kp_ref[...]
    kext_ref[BLOCK:BLOCK + tq, :] = k_ref[...]
    kext_ref[BLOCK + tq:, :] = kn_ref[...]
    vext_ref[:, 0:BLOCK] = vtp_ref[...]
    vext_ref[:, BLOCK:BLOCK + tq] = vt_ref[...]
    vext_ref[:, BLOCK + tq:] = vtn_ref[...]

    n_blocks = S // BLOCK
    zeros_q = jnp.zeros((HEAD_DIM, Q_GROUP * BLOCK), bf16)
    items = [(n, kvh) for n in range(tq // BLOCK) for kvh in range(N_KV_HEADS)]

    def scores(n, kvh):
        gb = i * (tq // BLOCK) + n
        variant = jnp.where(gb == 0, 1, jnp.where(gb == n_blocks - 1, 2, 0))
        cols = slice(n * BLOCK, (n + 1) * BLOCK)
        kw = kext_ref[n * BLOCK:n * BLOCK + 3 * BLOCK, :]
        qst = jnp.concatenate(
            [qt_ref[(kvh * Q_GROUP + gq) * HEAD_DIM:(kvh * Q_GROUP + gq + 1) * HEAD_DIM, cols]
             for gq in range(Q_GROUP)], axis=1)
        qst = jnp.concatenate([qst, zeros_q] if kvh == 0 else [zeros_q, qst], axis=0)
        return _dot(kw, qst) + bias_ref[variant, kvh]

    def softmax(n, kvh, s):
        sink = sink_ref[kvh]
        m = jnp.maximum(jnp.max(s, axis=0, keepdims=True), sink)
        e = jnp.exp(s - m)
        denom = jnp.sum(e, axis=0, keepdims=True) + jnp.exp(sink - m)
        return e.astype(bf16), denom

    def weighted_values(n, kvh, e, denom):
        cols = slice(n * BLOCK, (n + 1) * BLOCK)
        vwt = vext_ref[kvh * HEAD_DIM:(kvh + 1) * HEAD_DIM, n * BLOCK:n * BLOCK + 3 * BLOCK]
        out = _dot(vwt, e) / denom
        for gq in range(Q_GROUP):
            hq = kvh * Q_GROUP + gq
            ybt_ref[hq * HEAD_DIM:(hq + 1) * HEAD_DIM, cols] = out[:, gq * BLOCK:(gq + 1) * BLOCK]

    s_next = scores(*items[0])
    prev = None
    for idx, item in enumerate(items):
        s_cur = s_next
        if idx + 1 < len(items):
            s_next = scores(*items[idx + 1])
        cur = softmax(*item, s_cur)
        if prev is not None:
            weighted_values(*items[idx - 1], *prev)
        prev = cur
    weighted_values(*items[-1], *prev)
    mix_ref[:, POOL_WIDTH:] = ybt_ref[...].T.astype(bf16)

    halves = [slice(r * (tq // 2), (r + 1) * (tq // 2)) for r in range(2)]
    ys = [_dot(mix_ref[rr, :], wout_ref[...]) for rr in halves]
    for rr, y in zip(halves, ys):
        xr = ALPHA * x_ref[rr, :] + (1.0 + mod_ref[2:3, :]) * y
        o_ref[rr, :] = _layer_norm(xr, lng_ref[...], lnb_ref[...])


def _ev_mix(x2d, mod_l, p, qt, k, vt, w_pool, pool_scale, sink, w_out, ln_g, ln_b, B, S):
    T = x2d.shape[0]
    tq = 512
    nt = S // tq
    kb = tq // BLOCK
    pb = tq // POOL_HALO
    n_kblocks = T // BLOCK
    n_pblocks = T // POOL_HALO

    def main(b, i): return (b * nt + i, 0)
    def kprev(b, i): return (jnp.maximum((b * nt + i) * kb - 1, 0), 0)
    def knext(b, i): return (jnp.minimum((b * nt + i + 1) * kb, n_kblocks - 1), 0)
    def pprev(b, i): return (jnp.maximum((b * nt + i) * pb - 1, 0), 0)
    def pnext(b, i): return (jnp.minimum((b * nt + i + 1) * pb, n_pblocks - 1), 0)
    def const2(b, i): return (0, 0)

    def tmain(b, i): return (0, b * nt + i)
    def tprev(b, i): return (0, jnp.maximum((b * nt + i) * kb - 1, 0))
    def tnext(b, i): return (0, jnp.minimum((b * nt + i + 1) * kb, n_kblocks - 1))

    assert S // BLOCK >= 2
    kj = np.arange(3 * BLOCK)[:, None]
    qi = np.arange(BLOCK)[None, :]
    dist = np.abs(kj - BLOCK - qi)
    slopes = np.float32(2.0) ** (np.float32(-8.0) * np.arange(1, N_Q_HEADS + 1, dtype=np.float32) / N_Q_HEADS)
    alibi = -slopes[:, None, None] * dist.astype(np.float32)[None]
    in_window = dist <= BLOCK
    key_ok = np.stack([kj >= 0, kj >= BLOCK, kj < 2 * BLOCK])
    bias = np.where((in_window[None] & key_ok)[:, None], alibi[None], np.float32(-1e30))
    bias = bias.reshape(3, N_KV_HEADS, Q_GROUP, 3 * BLOCK, BLOCK).transpose(0, 1, 3, 2, 4)
    bias = jnp.asarray(bias.reshape(3, N_KV_HEADS, 3 * BLOCK, Q_GROUP * BLOCK), f32)
    sink_row = jnp.repeat(sink.astype(f32).reshape(N_KV_HEADS, Q_GROUP), BLOCK, axis=1)[:, None, :]

    assert tq >= 2 * POOL_HALO and max(POOL_WINDOWS) // 2 <= POOL_HALO and nt >= 2

    kernel = functools.partial(_ev_mix_kernel, S=S, tq=tq)
    return pl.pallas_call(
        kernel,
        grid=(B, nt),
        in_specs=[
            pl.BlockSpec((tq, D), main),
            pl.BlockSpec((None, 6, D), lambda b, i: (b, 0, 0)),
            pl.BlockSpec((tq, POOL_WIDTH), main),
            pl.BlockSpec((POOL_HALO, POOL_WIDTH), pprev),
            pl.BlockSpec((POOL_HALO, POOL_WIDTH), pnext),
            pl.BlockSpec((ATTN_WIDTH, tq), tmain),
            pl.BlockSpec((tq, KV_WIDTH), main),
            pl.BlockSpec((BLOCK, KV_WIDTH), kprev),
            pl.BlockSpec((BLOCK, KV_WIDTH), knext),
            pl.BlockSpec((KV_WIDTH, tq), tmain),
            pl.BlockSpec((KV_WIDTH, BLOCK), tprev),
            pl.BlockSpec((KV_WIDTH, BLOCK), tnext),
            pl.BlockSpec((3, N_KV_HEADS, 3 * BLOCK, Q_GROUP * BLOCK), lambda b, i: (0, 0, 0, 0)),
            pl.BlockSpec((N_KV_HEADS, 1, Q_GROUP * BLOCK), lambda b, i: (0, 0, 0)),
            pl.BlockSpec((len(POOL_WINDOWS), POOL_CH, POOL_CH), lambda b, i: (0, 0, 0)),
            pl.BlockSpec((1, POOL_WIDTH), const2),
            pl.BlockSpec(memory_space=pl.ANY),
            pl.BlockSpec((1, D), const2),
            pl.BlockSpec((1, D), const2),
        ],
        out_specs=pl.BlockSpec((tq, D), main),
        out_shape=jax.ShapeDtypeStruct((T, D), f32),
        scratch_shapes=[
            pltpu.VMEM((tq + 4 * POOL_HALO, POOL_WIDTH), f32),
            pltpu.VMEM((3, tq + 3 * POOL_HALO, POOL_WIDTH), f32),
            pltpu.VMEM((tq, POOL_WIDTH), f32),
            pltpu.VMEM((tq + 2 * BLOCK, KV_WIDTH), bf16),
            pltpu.VMEM((KV_WIDTH, tq + 2 * BLOCK), bf16),
            pltpu.VMEM((ATTN_WIDTH, tq), f32),
            pltpu.VMEM((tq, D), bf16),
            pltpu.VMEM((D, D), bf16),
            pltpu.VMEM((2, D // 8, D), f32),
            pltpu.SemaphoreType.DMA((2,)),
        ],
        compiler_params=pltpu.CompilerParams(vmem_limit_bytes=VMEM_LIMIT),
        name="ev_mix",
    )(x2d, mod_l, p, p, p, qt, k, k, k, vt, vt, vt, bias, sink_row, w_pool, pool_scale, w_out, ln_g, ln_b)


def _load_rounded(w_hbm_ref, w_ref, stage_ref, sem):
    rows = stage_ref.shape[1]
    n = w_hbm_ref.shape[0] // rows

    def copy(c):
        return pltpu.make_async_copy(w_hbm_ref.at[pl.ds(c * rows, rows), :], stage_ref.at[c % 2], sem.at[c % 2])

    copy(0).start()
    for c in range(n):
        if c + 1 < n:
            copy(c + 1).start()
        copy(c).wait()
        w_ref[pl.ds(c * rows, rows), :] = stage_ref[c % 2].astype(bf16)


def _ffn_kernel(x_ref, mod_ref, wg_hbm_ref, wu_hbm_ref, wd_hbm_ref, lng_ref, lnb_ref, o_ref,
                wg_ref, wu_ref, wd_ref, stage_in_ref, stage_out_ref, sem):
    @pl.when(pl.program_id(0) == 0)
    def _():
        _load_rounded(wg_hbm_ref, wg_ref, stage_in_ref, sem)
        _load_rounded(wu_hbm_ref, wu_ref, stage_in_ref, sem)
        _load_rounded(wd_hbm_ref, wd_ref, stage_out_ref, sem)

    piece = 256
    n_pieces = x_ref.shape[0] // piece
    parts = [pl.ds(q * piece, piece) for q in range(n_pieces)]

    def gate_up(rows):
        x = x_ref[rows, :]
        h = (x * (1.0 + mod_ref[4:5, :]) + mod_ref[3:4, :]).astype(bf16)
        return x, _dot(h, wg_ref[...]), _dot(h, wu_ref[...])

    nxt = gate_up(parts[0])
    for q, rows in enumerate(parts):
        x, g, u = nxt
        if q + 1 < n_pieces:
            nxt = gate_up(parts[q + 1])
        y = _dot((_silu(g) * u).astype(bf16), wd_ref[...])
        xr = ALPHA * x + (1.0 + mod_ref[5:6, :]) * y
        o_ref[rows, :] = _layer_norm(xr, lng_ref[...], lnb_ref[...])


def _ffn(x2d, mod_l, w_gate, w_up, w_down, ln_g, ln_b, S):
    T = x2d.shape[0]
    tm = 512
    dff = w_gate.shape[1]
    tpb = S // tm
    n_stage = 8
    return pl.pallas_call(
        _ffn_kernel,
        grid=(T // tm,),
        in_specs=[
            pl.BlockSpec((tm, D), lambda i: (i, 0)),
            pl.BlockSpec((None, 6, D), lambda i: (i // tpb, 0, 0)),
            pl.BlockSpec(memory_space=pl.ANY),
            pl.BlockSpec(memory_space=pl.ANY),
            pl.BlockSpec(memory_space=pl.ANY),
            pl.BlockSpec((1, D), lambda i: (0, 0)),
            pl.BlockSpec((1, D), lambda i: (0, 0)),
        ],
        out_specs=pl.BlockSpec((tm, D), lambda i: (i, 0)),
        out_shape=jax.ShapeDtypeStruct((T, D), f32),
        scratch_shapes=[
            pltpu.VMEM((D, dff), bf16),
            pltpu.VMEM((D, dff), bf16),
            pltpu.VMEM((dff, D), bf16),
            pltpu.VMEM((2, D // n_stage, dff), f32),
            pltpu.VMEM((2, dff // n_stage, D), f32),
            pltpu.SemaphoreType.DMA((2,)),
        ],
        compiler_params=pltpu.CompilerParams(vmem_limit_bytes=VMEM_LIMIT_BIG),
        name="ffn",
    )(x2d, mod_l, w_gate, w_up, w_down, ln_g, ln_b)


def _sg_kernel(x_ref, mod_ref, win_hbm_ref, sgg_ref, sgb_ref, ws_ref, bst_ref, wout_hbm_ref, lng_ref, lnb_ref,
               o_ref, gate_ref, win_ref, wout_ref, stage_in_ref, stage_out_ref, sem, *, tm):
    @pl.when(pl.program_id(0) == 0)
    def _():
        _load_rounded(win_hbm_ref, win_ref, stage_in_ref, sem)
        _load_rounded(wout_hbm_ref, wout_ref, stage_out_ref, sem)

    piece = 256
    n_pieces = tm // piece

    def project(q):
        x = x_ref[q * piece:(q + 1) * piece, :]
        h = (x * (1.0 + mod_ref[1:2, :]) + mod_ref[0:1, :]).astype(bf16)
        return x, _dot(h, win_ref[:, D:]), _dot(h, win_ref[:, :D])

    nxt = project(0)
    for q in range(n_pieces):
        x, zv, zu = nxt
        if q + 1 < n_pieces:
            nxt = project(q + 1)
        v = _layer_norm(_gelu_tanh(zv), sgg_ref[...], sgb_ref[...]).astype(bf16)
        u = _gelu_tanh(zu)
        for n in range(piece // CHUNK):
            rows = slice(n * CHUNK, (n + 1) * CHUNK)
            grows = slice(q * piece + n * CHUNK, q * piece + (n + 1) * CHUNK)
            for g in range(SG_GROUPS):
                cols = slice(g * SG_CH, (g + 1) * SG_CH)
                sv = _dot(ws_ref[g], v[rows, cols]) + bst_ref[:, g:g + 1]
                gate_ref[grows, cols] = (u[rows, cols] * sv).astype(bf16)
        y = _dot(gate_ref[q * piece:(q + 1) * piece, :], wout_ref[...])
        xr = ALPHA * x + (1.0 + mod_ref[2:3, :]) * y
        o_ref[q * piece:(q + 1) * piece, :] = _layer_norm(xr, lng_ref[...], lnb_ref[...])


def _sg_mix(x2d, mod_l, w_in, sg_g, sg_b, w_s, b_s_t, w_out, ln_g, ln_b, S):
    T = x2d.shape[0]
    tm = 512
    tpb = S // tm
    c2 = lambda i: (0, 0)
    n_stage = 8
    return pl.pallas_call(
        functools.partial(_sg_kernel, tm=tm),
        grid=(T // tm,),
        in_specs=[
            pl.BlockSpec((tm, D), lambda i: (i, 0)),
            pl.BlockSpec((None, 6, D), lambda i: (i // tpb, 0, 0)),
            pl.BlockSpec(memory_space=pl.ANY),
            pl.BlockSpec((1, D), c2),
            pl.BlockSpec((1, D), c2),
            pl.BlockSpec((SG_GROUPS, CHUNK, CHUNK), lambda i: (0, 0, 0)),
            pl.BlockSpec((CHUNK, SG_GROUPS), c2),
            pl.BlockSpec(memory_space=pl.ANY),
            pl.BlockSpec((1, D), c2),
            pl.BlockSpec((1, D), c2),
        ],
        out_specs=pl.BlockSpec((tm, D), lambda i: (i, 0)),
        out_shape=jax.ShapeDtypeStruct((T, D), f32),
        scratch_shapes=[
            pltpu.VMEM((tm, D), bf16),
            pltpu.VMEM((D, 2 * D), bf16),
            pltpu.VMEM((D, D), bf16),
            pltpu.VMEM((2, D // n_stage, 2 * D), f32),
            pltpu.VMEM((2, D // n_stage, D), f32),
            pltpu.SemaphoreType.DMA((2,)),
        ],
        compiler_params=pltpu.CompilerParams(vmem_limit_bytes=VMEM_LIMIT),
        name="sg_mix",
    )(x2d, mod_l, w_in, sg_g, sg_b, w_s, b_s_t, w_out, ln_g, ln_b)


def _route_kernel(x_ref, mod_ref, wrt_ref, hs_ref, route_ref, cnt_ref):
    W = ROUTE_W
    w_hi, w_lo = _split_bf16(wrt_ref[...])
    nt = (((1,), (1,)), ((), ()))
    eidx = lax.broadcasted_iota(jnp.int32, (N_EXPERTS, W), 0)
    sub = lax.broadcasted_iota(jnp.int32, (N_EXPERTS, 1), 0)
    tr = lax.broadcasted_iota(jnp.int32, (W, W), 0)
    tc = lax.broadcasted_iota(jnp.int32, (W, W), 1)
    upper = (tr < tc).astype(bf16)
    srow = lax.broadcasted_iota(jnp.int32, (CHUNK_SLOTS, W), 0)
    ridx = lax.broadcasted_iota(jnp.int32, (8, W), 0)

    def assign_slots(k):
        h = x_ref[k * W:(k + 1) * W, :] * (1.0 + mod_ref[4:5, :]) + mod_ref[3:4, :]
        h_hi, h_lo = _split_bf16(h)
        logits = (lax.dot_general(w_hi, h_hi, nt, preferred_element_type=f32)
                  + (lax.dot_general(w_hi, h_lo, nt, preferred_element_type=f32)
                     + lax.dot_general(w_lo, h_hi, nt, preferred_element_type=f32)))
        m1 = jnp.max(logits, axis=0, keepdims=True)
        i1 = jnp.min(jnp.where(logits == m1, eidx, N_EXPERTS), axis=0, keepdims=True)
        sel1 = eidx == i1
        rest = jnp.where(sel1, -jnp.inf, logits)
        m2 = jnp.max(rest, axis=0, keepdims=True)
        i2 = jnp.min(jnp.where(rest == m2, eidx, N_EXPERTS), axis=0, keepdims=True)
        sel2 = eidx == i2
        e2 = jnp.exp(m2 - m1)
        g1 = 1.0 / (1.0 + e2)
        g2 = e2 / (1.0 + e2)

        a1 = sel1.astype(f32)
        a2 = sel2.astype(f32)
        assign = a1 + a2
        counts = jnp.sum(assign, axis=1, keepdims=True)
        grans = jnp.ceil(counts * (1.0 / GRAN))
        seg = jnp.zeros((N_EXPERTS, 1), f32)
        for e in range(N_EXPERTS - 1):
            seg = seg + jnp.where(sub > e, grans[e:e + 1, :] * GRAN, 0.0)
        rank = _dot(assign.astype(bf16), upper)
        slot = seg + rank
        pos1 = jnp.sum(a1 * slot, axis=0, keepdims=True)
        pos2 = jnp.sum(a2 * slot, axis=0, keepdims=True)
        route_ref[k] = jnp.where(ridx == 0, pos1, jnp.where(ridx == 1, pos2,
                                 jnp.where(ridx == 2, g1, jnp.where(ridx == 3, g2, 0.0)))).T
        cnt_ref[k] = jnp.broadcast_to(counts, (N_EXPERTS, 128)).astype(jnp.int32)
        return h_hi, pos1, pos2

    def sort_rows(k, h_hi, pos1, pos2):
        perm = ((srow == pos1.astype(jnp.int32)) | (srow == pos2.astype(jnp.int32)))
        hs_ref[k] = _dot(perm.astype(f32).astype(bf16), h_hi).astype(bf16)

    nxt = assign_slots(0)
    for k in range(ROUTE_PER_STEP):
        cur = nxt
        if k + 1 < ROUTE_PER_STEP:
            nxt = assign_slots(k + 1)
        sort_rows(k, *cur)


def _route(x2d, mod_l, w_router_t, S):
    T = x2d.shape[0]
    W = ROUTE_W
    R = ROUTE_PER_STEP
    nc = T // W
    tpb = S // (R * W)
    return pl.pallas_call(
        _route_kernel,
        grid=(nc // R,),
        in_specs=[
            pl.BlockSpec((R * W, D), lambda c: (c, 0)),
            pl.BlockSpec((None, 6, D), lambda c: (c // tpb, 0, 0)),
            pl.BlockSpec((N_EXPERTS, D), lambda c: (0, 0)),
        ],
        out_specs=[
            pl.BlockSpec((R, CHUNK_SLOTS, D), lambda c: (c, 0, 0)),
            pl.BlockSpec((R, W, 8), lambda c: (c, 0, 0)),
            pl.BlockSpec((R, N_EXPERTS, 128), lambda c: (c, 0, 0)),
        ],
        out_shape=[
            jax.ShapeDtypeStruct((nc, CHUNK_SLOTS, D), bf16),
            jax.ShapeDtypeStruct((nc, W, 8), f32),
            jax.ShapeDtypeStruct((nc, N_EXPERTS, 128), jnp.int32),
        ],
        compiler_params=pltpu.CompilerParams(vmem_limit_bytes=VMEM_LIMIT),
        name="route",
    )(x2d, mod_l, w_router_t)


def _granule_copy(src_ref, buf_ref, sem, idx_ref, base, g):
    return pltpu.make_async_copy(src_ref.at[idx_ref[base + g]], buf_ref.at[pl.ds(g * GRAN, GRAN), :], sem)


def _gather_start(src_ref, buf_ref, sem, idx_ref, base, n):
    for g in range(n):
        _granule_copy(src_ref, buf_ref, sem, idx_ref, base, g).start()


def _gather_wait(src_ref, buf_ref, sem, idx_ref, base, n):
    for g in range(n):
        _granule_copy(src_ref, buf_ref, sem, idx_ref, base, g).wait()


def _expert_kernel(te_ref, tv_ref, src_ref, hs_ref, wg_ref, wu_ref, wd_ref, o_ref,
                   xbuf_ref, sem, acc_ref, wgb_ref, wub_ref, wdb_ref):
    i = pl.program_id(0)
    j = pl.program_id(1)
    n_tiles = pl.num_programs(0)
    last = pl.num_programs(1) - 1
    valid = tv_ref[i] > 0
    slot = i % 2
    nxt = jnp.minimum(i + 1, n_tiles - 1)

    @pl.when(j == 0)
    def _():
        @pl.when(i == 0)
        def _():
            _gather_start(hs_ref, xbuf_ref.at[0], sem.at[0], src_ref, 0, TILE_GRANS)

        @pl.when(valid)
        def _():
            _gather_wait(hs_ref, xbuf_ref.at[slot], sem.at[slot], src_ref, i * TILE_GRANS, TILE_GRANS)

        @pl.when((i + 1 < n_tiles) & (tv_ref[nxt] > 0))
        def _():
            _gather_start(hs_ref, xbuf_ref.at[1 - slot], sem.at[1 - slot], src_ref, nxt * TILE_GRANS, TILE_GRANS)

    n_sub = tv_ref[i]
    fast_subs = MOE_FAST // MOE_SUB
    fast = n_sub >= fast_subs

    def round_weights():
        wgb_ref[...] = wg_ref[...].astype(bf16)
        wub_ref[...] = wu_ref[...].astype(bf16)
        wdb_ref[...] = wd_ref[...].astype(bf16)

    def swiglu(x):
        a = _silu(_dot(x, wgb_ref[...])) * _dot(x, wub_ref[...])
        return _dot(a.astype(bf16), wdb_ref[...])

    sparse = valid & jnp.logical_not(fast)

    @pl.when((sparse | (i == 0)) & (j == 0))
    def _():
        acc_ref[...] = jnp.zeros_like(acc_ref)

    def dense_block(n_pieces):
        round_weights()
        parts = [pl.ds(q * MOE_SUB, MOE_SUB) for q in range(n_pieces)]

        def gate_up(rows):
            x = xbuf_ref[slot, rows, :]
            return _dot(x, wgb_ref[...]), _dot(x, wub_ref[...])

        nxt_gu = gate_up(parts[0])
        for q, rows in enumerate(parts):
            g, u = nxt_gu
            if q + 1 < len(parts):
                nxt_gu = gate_up(parts[q + 1])
            y = _dot((_silu(g) * u).astype(bf16), wdb_ref[...])
            total = jnp.where(j == 0, y, acc_ref[rows, :] + y)
            acc_ref[rows, :] = total
            o_ref[rows, :] = total.astype(o_ref.dtype)
        if n_pieces * MOE_SUB < MOE_TM:
            o_ref[n_pieces * MOE_SUB:, :] = jnp.zeros((MOE_TM - n_pieces * MOE_SUB, D), o_ref.dtype)

    for n_dense in (MOE_TM // MOE_SUB, fast_subs):
        pl.when(n_sub == n_dense)(functools.partial(dense_block, n_dense))

    @pl.when(sparse)
    def _():
        round_weights()

        def body(sb, carry):
            rows = pl.ds(pl.multiple_of(sb * MOE_SUB, MOE_SUB), MOE_SUB)
            acc_ref[rows, :] += swiglu(xbuf_ref[slot, rows, :])
            return carry

        lax.fori_loop(0, n_sub, body, 0)

    @pl.when(j == last)
    def _():
        @pl.when(sparse)
        def _():
            o_ref[...] = acc_ref[...].astype(o_ref.dtype)

        @pl.when(jnp.logical_not(valid))
        def _():
            o_ref[...] = jnp.zeros_like(o_ref)


def _experts(tile_expert, tile_valid, src_of_dst, hs2d, w_gate, w_up, w_down):
    n_tiles = tile_expert.shape[0]
    dff = w_gate.shape[2]
    nff = dff // MOE_TF

    def jj(j, tv, i):
        return jnp.where(tv[i] > 0, j, nff - 1)

    grid_spec = pltpu.PrefetchScalarGridSpec(
        num_scalar_prefetch=3,
        grid=(n_tiles, nff),
        in_specs=[
            pl.BlockSpec(memory_space=pl.ANY),
            pl.BlockSpec((None, D, MOE_TF), lambda i, j, te, tv, sd: (te[i], 0, jj(j, tv, i))),
            pl.BlockSpec((None, D, MOE_TF), lambda i, j, te, tv, sd: (te[i], 0, jj(j, tv, i))),
            pl.BlockSpec((None, MOE_TF, D), lambda i, j, te, tv, sd: (te[i], jj(j, tv, i), 0)),
        ],
        out_specs=pl.BlockSpec((MOE_TM, D), lambda i, j, te, tv, sd: (i, 0)),
        scratch_shapes=[
            pltpu.VMEM((2, MOE_TM, D), bf16),
            pltpu.SemaphoreType.DMA((2,)),
            pltpu.VMEM((MOE_TM, D), f32),
            pltpu.VMEM((D, MOE_TF), bf16),
            pltpu.VMEM((D, MOE_TF), bf16),
            pltpu.VMEM((MOE_TF, D), bf16),
        ],
    )
    return pl.pallas_call(
        _expert_kernel,
        grid_spec=grid_spec,
        out_shape=jax.ShapeDtypeStruct((n_tiles * MOE_TM, D), bf16),
        compiler_params=pltpu.CompilerParams(vmem_limit_bytes=VMEM_LIMIT_BIG),
        name="experts",
    )(tile_expert, tile_valid, src_of_dst, hs2d, w_gate, w_up, w_down)


def _combine_kernel(ds_ref, x_ref, mod_ref, o_hbm_ref, rt_ref, lng_ref, lnb_ref, out_ref, obuf_ref, sem):
    W = ROUTE_W
    c = pl.program_id(0)
    nc = pl.num_programs(0)
    slot = c % 2
    nxt = jnp.minimum(c + 1, nc - 1)

    @pl.when(c == 0)
    def _():
        _gather_start(o_hbm_ref, obuf_ref.at[0], sem.at[0], ds_ref, 0, CHUNK_GRANS)

    _gather_wait(o_hbm_ref, obuf_ref.at[slot], sem.at[slot], ds_ref, c * CHUNK_GRANS, CHUNK_GRANS)

    _gather_start(o_hbm_ref, obuf_ref.at[1 - slot], sem.at[1 - slot], ds_ref, nxt * CHUNK_GRANS, CHUNK_GRANS)

    osv = obuf_ref[slot]
    scol = lax.broadcasted_iota(jnp.int32, (W // 2, CHUNK_SLOTS), 1)
    for r in range(2):
        rr = slice(r * (W // 2), (r + 1) * (W // 2))
        rt = rt_ref[rr, :]
        p1 = (scol == rt[:, 0:1].astype(jnp.int32)).astype(f32).astype(bf16)
        p2 = (scol == rt[:, 1:2].astype(jnp.int32)).astype(f32).astype(bf16)
        y = rt[:, 2:3] * _dot(p1, osv) + rt[:, 3:4] * _dot(p2, osv)
        xr = ALPHA * x_ref[rr, :] + (1.0 + mod_ref[5:6, :]) * y
        out_ref[rr, :] = _layer_norm(xr, lng_ref[...], lnb_ref[...])

    @pl.when(c == nc - 1)
    def _():
        _gather_wait(o_hbm_ref, obuf_ref.at[1 - slot], sem.at[1 - slot], ds_ref, nxt * CHUNK_GRANS, CHUNK_GRANS)


def _combine(dst_of_src, x2d, mod_l, o2d, route_t, ln_g, ln_b, S):
    T = x2d.shape[0]
    W = ROUTE_W
    tpb = S // W
    grid_spec = pltpu.PrefetchScalarGridSpec(
        num_scalar_prefetch=1,
        grid=(T // W,),
        in_specs=[
            pl.BlockSpec((W, D), lambda c, ds: (c, 0)),
            pl.BlockSpec((None, 6, D), lambda c, ds: (c // tpb, 0, 0)),
            pl.BlockSpec(memory_space=pl.ANY),
            pl.BlockSpec((None, W, 8), lambda c, ds: (c, 0, 0)),
            pl.BlockSpec((1, D), lambda c, ds: (0, 0)),
            pl.BlockSpec((1, D), lambda c, ds: (0, 0)),
        ],
        out_specs=pl.BlockSpec((W, D), lambda c, ds: (c, 0)),
        scratch_shapes=[
            pltpu.VMEM((2, CHUNK_SLOTS, D), bf16),
            pltpu.SemaphoreType.DMA((2,)),
        ],
    )
    return pl.pallas_call(
        _combine_kernel,
        grid_spec=grid_spec,
        out_shape=jax.ShapeDtypeStruct((T, D), f32),
        compiler_params=pltpu.CompilerParams(vmem_limit_bytes=VMEM_LIMIT),
        name="combine",
    )(dst_of_src, x2d, mod_l, o2d, route_t, ln_g, ln_b)


def _routing_tables(counts, n_tiles):
    nc = counts.shape[0]
    gr = (counts + GRAN - 1) // GRAN
    seg_start = jnp.cumsum(gr, axis=1) - gr
    chunk_total = jnp.sum(gr, axis=1)
    prefix = jnp.cumsum(gr, axis=0) - gr
    g_e = jnp.sum(gr, axis=0)
    tiles_e = (g_e + TILE_GRANS - 1) // TILE_GRANS
    tile_end = jnp.cumsum(tiles_e)
    tile_start = tile_end - tiles_e
    total_tiles = tile_end[-1]

    i32 = jnp.int32
    er = jnp.arange(N_EXPERTS, dtype=i32)
    t = jnp.arange(n_tiles, dtype=i32)
    te = jnp.sum((t[:, None] >= tile_end[None, :]).astype(i32), axis=1)
    tile_valid = (t < total_tiles).astype(i32)
    last_e = jnp.sum((total_tiles - 1 >= tile_end).astype(i32))
    tile_expert = jnp.where(tile_valid > 0, jnp.minimum(te, N_EXPERTS - 1), last_e).astype(i32)
    oh_t = (tile_expert[:, None] == er).astype(i32)
    grans_left = jnp.sum(oh_t * (g_e - (t[:, None] - tile_start[None, :]) * TILE_GRANS), axis=1)
    tile_subs = tile_valid * jnp.clip((grans_left + SUB_GRANS - 1) // SUB_GRANS, 0, TILE_GRANS // SUB_GRANS)

    k = jnp.arange(CHUNK_GRANS, dtype=i32)
    seg_end = seg_start + gr
    e_of = jnp.sum((k[None, :, None] >= seg_end[:, None, :]).astype(i32), axis=2)
    oh_e = (jnp.minimum(e_of, N_EXPERTS - 1)[:, :, None] == er).astype(i32)
    base = tile_start[None, :] * TILE_GRANS + prefix - seg_start
    dst = jnp.sum(oh_e * base[:, None, :], axis=2) + k[None, :]
    valid_src = k[None, :] < chunk_total[:, None]
    dst_of_src = jnp.where(valid_src, dst, 0).astype(i32).reshape(-1)

    d = jnp.arange(n_tiles * TILE_GRANS, dtype=i32)
    oh_d = (jnp.repeat(tile_expert, TILE_GRANS)[:, None] == er).astype(i32)
    q = d - jnp.sum(oh_d * tile_start[None, :], axis=1) * TILE_GRANS
    incl_d = jnp.sum(oh_d[:, :, None] * (prefix + gr).T[None], axis=1)
    c_d = jnp.sum((q[:, None] >= incl_d).astype(i32), axis=1)
    oh_c = (jnp.minimum(c_d, nc - 1)[:, None] == jnp.arange(nc, dtype=i32)).astype(i32)
    cbase = jnp.arange(nc, dtype=i32)[:, None] * CHUNK_GRANS + seg_start - prefix
    sel = jnp.sum(oh_c[:, :, None] * oh_d[:, None, :] * cbase[None], axis=(1, 2))
    valid_dst = (jnp.repeat(tile_valid, TILE_GRANS) > 0) & (q >= 0) & (q < jnp.sum(oh_d * g_e[None, :], axis=1))
    src_of_dst = jnp.where(valid_dst, sel + q, 0).astype(i32)
    return tile_expert, tile_subs.astype(i32), src_of_dst, dst_of_src


def kernel(x, c, ada_w, ada_b, ln_g, ln_b, ev_w_in, ev_pool_w, ev_pool_scale, ev_sink, ev_w_out, od_w_in, od_sg_ln_g, od_sg_ln_b, od_w_s, od_b_s, od_w_out, ffn_w_gate, ffn_w_up, ffn_w_down, moe_w_router, moe_w_gate, moe_w_up, moe_w_down):
    B, S, _ = x.shape
    T = B * S
    assert x.shape[-1] == D and ada_w.shape == (DEPTH, D, 6 * D) and B <= 8
    assert ev_w_in.shape == (1, D, POOL_WIDTH + ATTN_WIDTH + 2 * KV_WIDTH) and moe_w_gate.shape[:3] == (1, N_EXPERTS, D)
    assert S % (ROUTE_PER_STEP * ROUTE_W) == 0 and moe_w_gate.shape[3] % MOE_TF == 0
    x2d = x.reshape(T, D)
    mod = _adaln(c, ada_w, ada_b)

    w_in = ev_w_in[0].astype(bf16)
    q0, k0, v0 = POOL_WIDTH, POOL_WIDTH + ATTN_WIDTH, POOL_WIDTH + ATTN_WIDTH + KV_WIDTH
    w_pk = jnp.concatenate([w_in[:, :q0], w_in[:, k0:v0]], axis=1)
    w_qv_t = jnp.concatenate([w_in[:, q0:k0], w_in[:, v0:]], axis=1).T
    p, k, qt, vt = _ev_in(x2d, mod[0], w_pk, w_qv_t, S)
    x2d = _ev_mix(x2d, mod[0], p, qt, k, vt, ev_pool_w[0].astype(bf16), ev_pool_scale[0][None, :],
                  ev_sink[0], ev_w_out[0], ln_g[0, 0][None, :], ln_b[0, 0][None, :], B, S)
    x2d = _ffn(x2d, mod[0], ffn_w_gate[0], ffn_w_up[0], ffn_w_down[0],
               ln_g[0, 1][None, :], ln_b[0, 1][None, :], S)

    x2d = _sg_mix(x2d, mod[1], od_w_in[0], od_sg_ln_g[0][None, :], od_sg_ln_b[0][None, :],
                  od_w_s[0].astype(bf16), od_b_s[0].T, od_w_out[0],
                  ln_g[1, 0][None, :], ln_b[1, 0][None, :], S)

    hs, route, cnt = _route(x2d, mod[1], moe_w_router[0].T, S)
    nc = T // ROUTE_W
    n_tiles = (nc * CHUNK_GRANS) // TILE_GRANS + N_EXPERTS
    tile_expert, tile_subs, src_of_dst, dst_of_src = _routing_tables(cnt[:, :, 0], n_tiles)
    o = _experts(tile_expert, tile_subs, src_of_dst, hs.reshape(nc * CHUNK_GRANS, GRAN, D),
                 moe_w_gate[0], moe_w_up[0], moe_w_down[0])
    x2d = _combine(dst_of_src, x2d, mod[1], o.reshape(n_tiles * TILE_GRANS, GRAN, D), route,
                   ln_g[1, 1][None, :], ln_b[1, 1][None, :], S)
    return x2d.reshape(B, S, D)
```
